```python
import jax, jax.numpy as jnp
from jax import lax
import numpy as np

D_MODEL = 2048
BATCH = 4
SEQ = 2048
DEPTH = 1
DEC_BATCH = 128
DEC_SEQ = 8
PAST_LEN = 16384
PAGE_SIZE = 128

D_MIX = D_MODEL
D_POOL = D_MIX // 2
D_LRU = D_MIX - D_POOL
POOL_WINDOWS = (2, 4, 8, 16)
N_POOL_GROUPS = len(POOL_WINDOWS)
POOL_GROUP_DIM = D_POOL // N_POOL_GROUPS
POOL_STATE = max(POOL_WINDOWS) - 1
LRU_HEADS = 16
LRU_HEAD_DIM = D_LRU // LRU_HEADS
CONV_WIDTH = 4
LRU_C = 8.0
N_EXPERT_GROUPS = 4
EXPERTS_PER_GROUP = 8
N_EXPERTS = N_EXPERT_GROUPS * EXPERTS_PER_GROUP
TOP_K_INNER = 2
D_EXPERT = D_MODEL // 4
ALPHA = (2.0 * DEPTH) ** 0.25
BETA = (8.0 * DEPTH) ** -0.25
LN_EPS = 1e-5

kernel_name = "hymba_pool_rglru_hiermoe_step"


def layer_norm(x, g, b):
    xf = x.astype(jnp.float32)
    mu = jnp.mean(xf, axis=-1, keepdims=True)
    var = jnp.mean(jnp.square(xf - mu), axis=-1, keepdims=True)
    y = (xf - mu) * lax.rsqrt(var + LN_EPS) * g.astype(jnp.float32) + b.astype(jnp.float32)
    return y.astype(x.dtype)


def pool_mixer(u, prev, pos0, pool_w, pool_b, pool_scale):
    B, T, _ = u.shape
    ext = jnp.concatenate([prev.astype(u.dtype), u], axis=1)
    cs = jnp.cumsum(ext.astype(jnp.float32), axis=1)
    cs = jnp.pad(cs, ((0, 0), (1, 0), (0, 0)))
    pos = pos0 + jnp.arange(T, dtype=jnp.int32)
    outs = []
    for g, w in enumerate(POOL_WINDOWS):
        lo, hi = g * POOL_GROUP_DIM, (g + 1) * POOL_GROUP_DIM
        win_sum = cs[:, POOL_STATE + 1:, lo:hi] - cs[:, POOL_STATE + 1 - w:POOL_STATE + 1 - w + T, lo:hi]
        count = jnp.minimum(pos + 1, w).astype(jnp.float32)
        outs.append(win_sum / count[None, :, None])
    pooled = jnp.concatenate(outs, axis=-1)
    d = (pooled - u.astype(jnp.float32)).reshape(B, T, N_POOL_GROUPS, POOL_GROUP_DIM)
    z = jnp.einsum("btgi,gio->btgo", d, pool_w.astype(jnp.float32)) + pool_b.astype(jnp.float32)
    z = z.reshape(B, T, D_POOL) * pool_scale.astype(jnp.float32)
    return z.astype(u.dtype), ext[:, -POOL_STATE:]


def block_diag(x, w, b):
    B, T, _ = x.shape
    xh = x.reshape(B, T, LRU_HEADS, LRU_HEAD_DIM)
    y = jnp.einsum("bthi,hij->bthj", xh, w.astype(jnp.float32)) + b.astype(jnp.float32)
    return y.reshape(B, T, D_LRU)


def rglru_mixer(xb, gate, conv_prev, h_prev, pos0, conv_w, conv_b, rg_w, rg_b, ig_w, ig_b, lru_lambda):
    B, T, _ = xb.shape
    ext = jnp.concatenate([conv_prev.astype(xb.dtype), xb], axis=1)
    extf = ext.astype(jnp.float32)
    cw = conv_w.astype(jnp.float32)
    conv = conv_b.astype(jnp.float32) + sum(extf[:, k:k + T] * cw[k] for k in range(CONV_WIDTH))
    r = jax.nn.sigmoid(block_diag(conv, rg_w, rg_b))
    i = jax.nn.sigmoid(block_diag(conv, ig_w, ig_b))
    log_a = -LRU_C * r * jax.nn.softplus(-lru_lambda.astype(jnp.float32))
    a = jnp.exp(log_a)
    pos = pos0 + jnp.arange(T, dtype=jnp.int32)
    mult = jnp.where((pos == 0)[None, :, None], 1.0, jnp.sqrt(-jnp.expm1(2.0 * log_a)))
    u = mult * (i * conv)

    def step(h, inp):
        a_t, u_t = inp
        h = a_t * h + u_t
        return h, h

    h_last, hs = lax.scan(step, h_prev.astype(jnp.float32),
                          (jnp.swapaxes(a, 0, 1), jnp.swapaxes(u, 0, 1)))
    y = jnp.swapaxes(hs, 0, 1) * jax.nn.gelu(gate.astype(jnp.float32))
    return y.astype(xb.dtype), ext[:, -(CONV_WIDTH - 1):], h_last.astype(h_prev.dtype)


def hier_moe(x, rgw, rgb, rew, reb, w1, w3, w2):
    B, T, D = x.shape
    N = B * T
    xt = x.reshape(N, D)
    xf = xt.astype(jnp.float32)
    lg = xf @ rgw.astype(jnp.float32) + rgb.astype(jnp.float32)
    pg = jax.nn.softmax(lg, axis=-1)
    gi = jnp.argmax(lg, axis=-1)
    g_oh = jax.nn.one_hot(gi, N_EXPERT_GROUPS, dtype=jnp.float32)
    pg_sel = jnp.max(pg, axis=-1)
    le = jnp.einsum("nd,gde->nge", xf, rew.astype(jnp.float32)) + reb.astype(jnp.float32)
    le_sel = jnp.einsum("nge,ng->ne", le, g_oh)
    tv, ti = lax.top_k(le_sel, TOP_K_INNER)
    tw = jax.nn.softmax(tv, axis=-1) * pg_sel[:, None]
    inner = jnp.sum(jax.nn.one_hot(ti, EXPERTS_PER_GROUP, dtype=jnp.float32) * tw[..., None], axis=1)
    gate = (g_oh[:, :, None] * inner[:, None, :]).reshape(N, N_EXPERTS)
    h = jax.nn.silu(jnp.einsum("nd,edf->nef", xt, w1)) * jnp.einsum("nd,edf->nef", xt, w3)
    y = jnp.einsum("nef,efd->nd", h * gate[:, :, None].astype(h.dtype), w2)
    return y.reshape(B, T, D)


def hybrid_layer(x, pool_prev, conv_prev, h_prev, pos0,
                 w_in, pool_w, pool_b, pool_scale, conv_w, conv_b, rg_w, rg_b, ig_w, ig_b, lru_lambda,
                 w_out, ln1_g, ln1_b, router_group_w, router_group_b, router_expert_w, router_expert_b,
                 expert_w1, expert_w3, expert_w2, ln2_g, ln2_b):
    proj = jnp.einsum("btd,dn->btn", x, w_in)
    u_pool = proj[..., :D_POOL]
    x_lru = proj[..., D_POOL:D_POOL + D_LRU]
    g_lru = proj[..., D_POOL + D_LRU:]
    y_pool, new_pool = pool_mixer(u_pool, pool_prev, pos0, pool_w, pool_b, pool_scale)
    y_lru, new_conv, new_h = rglru_mixer(x_lru, g_lru, conv_prev, h_prev, pos0,
                                         conv_w, conv_b, rg_w, rg_b, ig_w, ig_b, lru_lambda)
    mix = jnp.einsum("btn,nd->btd", jnp.concatenate([y_pool, y_lru], axis=-1), w_out)
    x1 = layer_norm(ALPHA * x + mix, ln1_g, ln1_b)
    moe = hier_moe(x1, router_group_w, router_group_b, router_expert_w, router_expert_b,
                   expert_w1, expert_w3, expert_w2)
    x2 = layer_norm(ALPHA * x1 + moe, ln2_g, ln2_b)
    return x2, new_pool, new_conv, new_h


def setup_inputs(seed: int = 0) -> dict:
    key = jax.random.key(seed)
    ks = jax.random.split(key, 32)
    f32 = jnp.float32
    nrm = lambda k, shape, s: jax.random.normal(k, shape, f32) * s
    p_a = jax.random.uniform(ks[15], (DEPTH, D_LRU), f32, 0.9, 0.999)
    return {
        "x_prompt": nrm(ks[0], (BATCH, SEQ, D_MODEL), 1.0),
        "x_sample": nrm(ks[1], (DEC_BATCH, DEC_SEQ, D_MODEL), 1.0),
        "state_pool": nrm(ks[2], (DEPTH, DEC_BATCH, POOL_STATE, D_POOL), 1.0),
        "state_conv": nrm(ks[3], (DEPTH, DEC_BATCH, CONV_WIDTH - 1, D_LRU), 1.0),
        "state_h": nrm(ks[4], (DEPTH, DEC_BATCH, D_LRU), 0.5),
        "w_in": nrm(ks[5], (DEPTH, D_MODEL, D_POOL + 2 * D_LRU), D_MODEL ** -0.5),
        "pool_w": nrm(ks[6], (DEPTH, N_POOL_GROUPS, POOL_GROUP_DIM, POOL_GROUP_DIM), POOL_GROUP_DIM ** -0.5),
        "pool_b": nrm(ks[7], (DEPTH, N_POOL_GROUPS, POOL_GROUP_DIM), 0.01),
        "pool_scale": 1.0 + nrm(ks[8], (DEPTH, D_POOL), 0.02),
        "conv_w": nrm(ks[9], (DEPTH, CONV_WIDTH, D_LRU), CONV_WIDTH ** -0.5),
        "conv_b": nrm(ks[10], (DEPTH, D_LRU), 0.01),
        "rg_w": nrm(ks[11], (DEPTH, LRU_HEADS, LRU_HEAD_DIM, LRU_HEAD_DIM), LRU_HEAD_DIM ** -0.5),
        "rg_b": nrm(ks[12], (DEPTH, LRU_HEADS, LRU_HEAD_DIM), 0.01),
        "ig_w": nrm(ks[13], (DEPTH, LRU_HEADS, LRU_HEAD_DIM, LRU_HEAD_DIM), LRU_HEAD_DIM ** -0.5),
        "ig_b": nrm(ks[14], (DEPTH, LRU_HEADS, LRU_HEAD_DIM), 0.01),
        "lru_lambda": jnp.log(p_a) - jnp.log1p(-p_a),
        "w_out": nrm(ks[16], (DEPTH, D_MIX, D_MODEL), BETA * D_MIX ** -0.5),
        "ln1_g": 1.0 + nrm(ks[17], (DEPTH, D_MODEL), 0.02),
        "ln1_b": nrm(ks[18], (DEPTH, D_MODEL), 0.01),
        "router_group_w": nrm(ks[19], (DEPTH, D_MODEL, N_EXPERT_GROUPS), D_MODEL ** -0.5),
        "router_group_b": nrm(ks[20], (DEPTH, N_EXPERT_GROUPS), 0.01),
        "router_expert_w": nrm(ks[21], (DEPTH, N_EXPERT_GROUPS, D_MODEL, EXPERTS_PER_GROUP), D_MODEL ** -0.5),
        "router_expert_b": nrm(ks[22], (DEPTH, N_EXPERT_GROUPS, EXPERTS_PER_GROUP), 0.01),
        "expert_w1": nrm(ks[23], (DEPTH, N_EXPERTS, D_MODEL, D_EXPERT), D_MODEL ** -0.5),
        "expert_w3": nrm(ks[24], (DEPTH, N_EXPERTS, D_MODEL, D_EXPERT), D_MODEL ** -0.5),
        "expert_w2": nrm(ks[25], (DEPTH, N_EXPERTS, D_EXPERT, D_MODEL), BETA * D_EXPERT ** -0.5),
        "ln2_g": 1.0 + nrm(ks[26], (DEPTH, D_MODEL), 0.02),
        "ln2_b": nrm(ks[27], (DEPTH, D_MODEL), 0.01),
    }


def reference(x_prompt, x_sample, state_pool, state_conv, state_h,
              w_in, pool_w, pool_b, pool_scale, conv_w, conv_b, rg_w, rg_b, ig_w, ig_b, lru_lambda,
              w_out, ln1_g, ln1_b, router_group_w, router_group_b, router_expert_w, router_expert_b,
              expert_w1, expert_w3, expert_w2, ln2_g, ln2_b):
    yp, ys = x_prompt, x_sample
    B = x_prompt.shape[0]
    pool_p, conv_p, h_p, pool_s, conv_s, h_s = [], [], [], [], [], []
    for l in range(DEPTH):
        lw = (w_in[l], pool_w[l], pool_b[l], pool_scale[l], conv_w[l], conv_b[l], rg_w[l], rg_b[l],
              ig_w[l], ig_b[l], lru_lambda[l], w_out[l], ln1_g[l], ln1_b[l], router_group_w[l],
              router_group_b[l], router_expert_w[l], router_expert_b[l], expert_w1[l], expert_w3[l],
              expert_w2[l], ln2_g[l], ln2_b[l])
        zp = jnp.zeros((B, POOL_STATE, D_POOL), yp.dtype)
        zc = jnp.zeros((B, CONV_WIDTH - 1, D_LRU), yp.dtype)
        zh = jnp.zeros((B, D_LRU), state_h.dtype)
        yp, npool, nconv, nh = hybrid_layer(yp, zp, zc, zh, 0, *lw)
        pool_p.append(npool); conv_p.append(nconv); h_p.append(nh)
        ys, npool, nconv, nh = hybrid_layer(ys, state_pool[l], state_conv[l], state_h[l], PAST_LEN, *lw)
        pool_s.append(npool); conv_s.append(nconv); h_s.append(nh)
    return (yp, ys,
            jnp.stack(pool_p, axis=0), jnp.stack(conv_p, axis=0), jnp.stack(h_p, axis=0),
            jnp.stack(pool_s, axis=0), jnp.stack(conv_s, axis=0), jnp.stack(h_s, axis=0))
```

```python
import functools

import jax
import jax.numpy as jnp
from jax import lax
from jax.experimental import pallas as pl
from jax.experimental.pallas import tpu as pltpu

F32 = jnp.float32
BF16 = jnp.bfloat16
I32 = jnp.int32

D_MODEL = 2048
D_POOL = 1024
D_LRU = 1024
POOL_WINDOWS = (2, 4, 8, 16)
POOL_GROUP = D_POOL // len(POOL_WINDOWS)
POOL_STATE = max(POOL_WINDOWS) - 1
CONV_WIDTH = 4
LRU_HEADS = 16
LRU_HEAD_DIM = D_LRU // LRU_HEADS
LRU_C = 8.0
N_GROUPS = 4
PER_GROUP = 8
N_EXPERTS = N_GROUPS * PER_GROUP
D_EXPERT = D_MODEL // 4
LN_EPS = 1e-5

SUBLANES = 8
LANE_CHUNK = 256
POOL_HALO = 16
CONV_HALO = 8
ROWS = 256
TM = 256
ROUTER_ROWS = 8 + N_EXPERTS
META_LANES = 128
VMEM_LIMIT = 56 * 1024 * 1024

_NT = (((1,), (1,)), ((), ()))


def _const_spec(shape):
    nd = len(shape)
    return pl.BlockSpec(shape, lambda *_: (0,) * nd, pipeline_mode=pl.Buffered(1))


def _mixer_kernel(x_ref, pool_in, conv_in, h_in, w_in, pool_w, pool_b, pool_scale, conv_w, conv_b,
                  gate_w, gate_b, lam, w_out, ln_g, ln_b, rt_hi, rt_lo, *rest, ns, l, pos0, alpha, aliased):
    if aliased:
        rest = rest[2:]
    (x1_ref, lg_ref, pool_o, conv_o, h_o,
     pool_ext, conv_ext, h_prev, gate_s, a_s, u_s, y_s) = rest
    t = pl.program_id(1)
    rows = ns * l

    @pl.when(t == 0)
    def _():
        pool_ext[:, 0:POOL_HALO, :] = pool_in[...]
        conv_ext[:, 0:CONV_HALO, :] = conv_in[...]
        h_prev[...] = h_in[...]

    xb = x_ref[...].reshape(rows, D_MODEL).astype(BF16)
    pool_ext[:, POOL_HALO:, :] = jnp.dot(
        xb, w_in[:, 0:D_POOL], preferred_element_type=F32).reshape(ns, l, D_POOL)
    conv_ext[:, CONV_HALO:, :] = jnp.dot(
        xb, w_in[:, D_POOL:D_POOL + D_LRU], preferred_element_type=F32).reshape(ns, l, D_LRU)
    gate_s[...] = jnp.dot(xb, w_in[:, D_POOL + D_LRU:], preferred_element_type=F32)

    pos = pos0 + t * l + lax.broadcasted_iota(I32, (ns, l, LANE_CHUNK), 1).reshape(rows, LANE_CHUNK)

    for g, w in enumerate(POOL_WINDOWS):
        cols = slice(g * POOL_GROUP, (g + 1) * POOL_GROUP)
        s = pool_ext[:, :, cols].reshape(ns * (POOL_HALO + l), POOL_GROUP)
        step = 1
        while step < w:
            s = s + pltpu.roll(s, step, 0)
            step *= 2
        win = s.reshape(ns, POOL_HALO + l, POOL_GROUP)[:, POOL_HALO:, :].reshape(rows, POOL_GROUP)
        u = pool_ext[:, POOL_HALO:, cols].reshape(rows, POOL_GROUP)
        if pos0 >= w - 1:
            inv = 1.0 / w
        else:
            inv = 1.0 / jnp.minimum(pos + 1, w).astype(F32)
        d = win * inv - u
        z = jnp.dot(d.astype(BF16), pool_w[g], preferred_element_type=F32) + pool_b[:, cols]
        y_s[:, cols] = (z * pool_scale[:, cols]).astype(BF16)

    lam_v = lam[...]
    softplus_neg = jnp.maximum(-lam_v, 0.0) + jnp.log1p(jnp.exp(-jnp.abs(lam_v)))
    log_a_scale = -LRU_C * softplus_neg
    sub = lax.broadcasted_iota(I32, (rows // SUBLANES, SUBLANES, LANE_CHUNK), 1)
    for c in range(D_LRU // LANE_CHUNK):
        cols = slice(c * LANE_CHUNK, (c + 1) * LANE_CHUNK)
        ce = conv_ext[:, :, cols].reshape(ns * (CONV_HALO + l), LANE_CHUNK)
        conv = conv_b[:, cols] + conv_w[CONV_WIDTH - 1:CONV_WIDTH, cols] * ce
        for k in range(1, CONV_WIDTH):
            conv = conv + conv_w[CONV_WIDTH - 1 - k:CONV_WIDTH - k, cols] * pltpu.roll(ce, k, 0)
        conv = conv.reshape(ns, CONV_HALO + l, LANE_CHUNK)[:, CONV_HALO:, :].reshape(rows, LANE_CHUNK)
        gz = jnp.dot(conv.astype(BF16), gate_w[c], preferred_element_type=F32)
        gz = gz + gate_b[:, 2 * c * LANE_CHUNK:2 * (c + 1) * LANE_CHUNK]
        r = jax.nn.sigmoid(gz[:, :LANE_CHUNK])
        i = jax.nn.sigmoid(gz[:, LANE_CHUNK:])
        log_a = r * log_a_scale[:, cols]
        a = jnp.exp(log_a)
        th = jnp.tanh(log_a)
        mult = jnp.sqrt(-2.0 * th / (1.0 - th))
        if pos0 == 0:
            mult = jnp.where(pos == 0, 1.0, mult)
        u = mult * (i * conv)
        a3 = a.reshape(rows // SUBLANES, SUBLANES, LANE_CHUNK)
        u3 = u.reshape(rows // SUBLANES, SUBLANES, LANE_CHUNK)
        for sh in (1, 2, 4):
            keep = sub >= sh
            a_sh = jnp.where(keep, pltpu.roll(a3, sh, 1), 1.0)
            u_sh = jnp.where(keep, pltpu.roll(u3, sh, 1), 0.0)
            u3 = a3 * u_sh + u3
            a3 = a3 * a_sh
        a_s[:, :, cols] = a3.reshape(ns, l, LANE_CHUNK)
        u_s[:, :, cols] = u3.reshape(ns, l, LANE_CHUNK)

    hp = h_prev[...]
    for gi in range(l // SUBLANES):
        sl = slice(gi * SUBLANES, (gi + 1) * SUBLANES)
        h = a_s[:, sl, :] * hp + u_s[:, sl, :]
        u_s[:, sl, :] = h
        hp = h[:, SUBLANES - 1:SUBLANES, :]
    h_prev[...] = hp

    for c in range(D_LRU // LANE_CHUNK):
        cols = slice(c * LANE_CHUNK, (c + 1) * LANE_CHUNK)
        h = u_s[:, :, cols].reshape(rows, LANE_CHUNK)
        y_s[:, D_POOL + c * LANE_CHUNK:D_POOL + (c + 1) * LANE_CHUNK] = (
            h * jax.nn.gelu(gate_s[:, cols])).astype(BF16)

    mix = jnp.dot(y_s[...], w_out[...], preferred_element_type=F32)
    xin = alpha * x_ref[...].reshape(rows, D_MODEL) + mix
    mu = jnp.mean(xin, axis=-1, keepdims=True)
    xc = xin - mu
    var = jnp.mean(xc * xc, axis=-1, keepdims=True)
    x1 = xc * lax.rsqrt(var + LN_EPS) * ln_g[...] + ln_b[...]
    x1_ref[...] = x1

    hi = x1.astype(BF16)
    lo = (x1 - hi.astype(F32)).astype(BF16)
    lg = lax.dot_general(rt_hi[...], hi, _NT, preferred_element_type=F32)
    lg = lg + lax.dot_general(rt_lo[...], hi, _NT, preferred_element_type=F32)
    lg = lg + lax.dot_general(rt_hi[...], lo, _NT, preferred_element_type=F32)
    lg_ref[...] = lg

    pool_tail = pool_ext[:, l:l + POOL_HALO, :]
    conv_tail = conv_ext[:, l:l + CONV_HALO, :]
    pool_o[...] = pool_tail
    conv_o[...] = conv_tail
    h_o[...] = u_s[:, l - SUBLANES:l, :]
    pool_ext[:, 0:POOL_HALO, :] = pool_tail
    conv_ext[:, 0:CONV_HALO, :] = conv_tail


def _mixer_call(x, pool_in, conv_in, h_in, wts, x1_buf, lg_buf, *, ns, l, pos0, alpha, n_total, blk0):
    bsz, tlen, _ = x.shape
    nb, nt = bsz // ns, tlen // l
    rows = ns * l
    assert rows == ROWS and bsz % ns == 0 and tlen % l == 0 and l % SUBLANES == 0
    aliased = x1_buf is not None

    def row_blk(b, t):
        return (blk0 + b * nt + t, 0)

    in_specs = [
        pl.BlockSpec((ns, l, D_MODEL), lambda b, t: (b, t, 0)),
        pl.BlockSpec((ns, POOL_HALO, D_POOL), lambda b, t: (b, 0, 0)),
        pl.BlockSpec((ns, CONV_HALO, D_LRU), lambda b, t: (b, 0, 0)),
        pl.BlockSpec((ns, 1, D_LRU), lambda b, t: (b, 0, 0)),
    ] + [_const_spec(w.shape) for w in wts]
    args = [x, pool_in, conv_in, h_in, *wts]
    aliases = {}
    if aliased:
        in_specs += [pl.BlockSpec(memory_space=pl.ANY), pl.BlockSpec(memory_space=pl.ANY)]
        aliases = {len(args): 0, len(args) + 1: 1}
        args += [x1_buf, lg_buf]
    out_shape = (
        jax.ShapeDtypeStruct((n_total, D_MODEL), F32),
        jax.ShapeDtypeStruct((ROUTER_ROWS, n_total), F32),
        jax.ShapeDtypeStruct((bsz, POOL_HALO, D_POOL), F32),
        jax.ShapeDtypeStruct((bsz, CONV_HALO, D_LRU), F32),
        jax.ShapeDtypeStruct((bsz, SUBLANES, D_LRU), F32),
    )
    out_specs = (
        pl.BlockSpec((rows, D_MODEL), row_blk),
        pl.BlockSpec((ROUTER_ROWS, rows), lambda b, t: (0, blk0 + b * nt + t)),
        pl.BlockSpec((ns, POOL_HALO, D_POOL), lambda b, t: (b, 0, 0)),
        pl.BlockSpec((ns, CONV_HALO, D_LRU), lambda b, t: (b, 0, 0)),
        pl.BlockSpec((ns, SUBLANES, D_LRU), lambda b, t: (b, 0, 0)),
    )
    scratch = [
        pltpu.VMEM((ns, POOL_HALO + l, D_POOL), F32),
        pltpu.VMEM((ns, CONV_HALO + l, D_LRU), F32),
        pltpu.VMEM((ns, 1, D_LRU), F32),
        pltpu.VMEM((rows, D_LRU), F32),
        pltpu.VMEM((ns, l, D_LRU), F32),
        pltpu.VMEM((ns, l, D_LRU), F32),
        pltpu.VMEM((rows, D_MODEL), BF16),
    ]
    return pl.pallas_call(
        functools.partial(_mixer_kernel, ns=ns, l=l, pos0=pos0, alpha=alpha, aliased=aliased),
        grid=(nb, nt),
        in_specs=in_specs,
        out_specs=out_specs,
        out_shape=out_shape,
        scratch_shapes=scratch,
        input_output_aliases=aliases,
        compiler_params=pltpu.CompilerParams(
            dimension_semantics=("arbitrary", "arbitrary"), vmem_limit_bytes=VMEM_LIMIT),
        name="mixer_seq" if nt > 1 else "mixer_step",
    )(*args)


def _route_kernel(lg_ref, bias_ref, dest_ref, tw_ref, meta_ref, e_s, r_s, *, n, max_tiles):
    blk = 256
    nblk = n // blk
    neg_inf = -jnp.inf
    ridx8 = lax.broadcasted_iota(I32, (SUBLANES, blk), 0).astype(F32)
    eidx = lax.broadcasted_iota(I32, (N_EXPERTS, blk), 0).astype(F32)
    tri = (lax.broadcasted_iota(I32, (blk, blk), 0) < lax.broadcasted_iota(I32, (blk, blk), 1))
    tri = jnp.where(tri, 1.0, 0.0).astype(BF16)

    def first_idx(vals, m):
        return jnp.min(jnp.where(vals == m, ridx8, float(SUBLANES)), axis=0, keepdims=True)

    def pass1(j, base):
        ls = pl.ds(pl.multiple_of(j * blk, blk), blk)
        lg = lg_ref[0:SUBLANES, ls] + bias_ref[0:SUBLANES, :]
        lg = jnp.where(ridx8 < N_GROUPS, lg, neg_inf)
        m = jnp.max(lg, axis=0, keepdims=True)
        gi = first_idx(lg, m)
        pg_sel = 1.0 / jnp.sum(jnp.exp(lg - m), axis=0, keepdims=True)
        le = jnp.zeros((PER_GROUP, blk), F32)
        for g in range(N_GROUPS):
            rows = slice(SUBLANES + g * PER_GROUP, SUBLANES + (g + 1) * PER_GROUP)
            le = jnp.where(gi == float(g), lg_ref[rows, ls] + bias_ref[rows, :], le)
        m1 = jnp.max(le, axis=0, keepdims=True)
        i1 = first_idx(le, m1)
        le2 = jnp.where(ridx8 == i1, neg_inf, le)
        m2 = jnp.max(le2, axis=0, keepdims=True)
        i2 = first_idx(le2, m2)
        e21 = jnp.exp(m2 - m1)
        denom = 1.0 / (1.0 + e21)
        tw_ref[0:1, ls] = pg_sel * denom
        tw_ref[1:2, ls] = pg_sel * (e21 * denom)
        e1 = gi * float(PER_GROUP) + i1
        e2 = gi * float(PER_GROUP) + i2
        e_s[0:1, ls] = e1
        e_s[1:2, ls] = e2
        oh1 = jnp.where(eidx == e1, 1.0, 0.0)
        oh2 = jnp.where(eidx == e2, 1.0, 0.0)
        oh = oh1 + oh2
        before = base + jnp.dot(oh.astype(BF16), tri, preferred_element_type=F32)
        r_s[0:1, ls] = jnp.sum(oh1 * before, axis=0, keepdims=True)
        r_s[1:2, ls] = jnp.sum(oh2 * before, axis=0, keepdims=True)
        return base + jnp.sum(oh, axis=1, keepdims=True)

    cnt = lax.fori_loop(0, nblk, pass1, jnp.zeros((N_EXPERTS, blk), F32))

    ntile = jnp.floor((cnt + float(TM - 1)) * (1.0 / TM))
    lt = (lax.broadcasted_iota(I32, (N_EXPERTS, N_EXPERTS), 1) < lax.broadcasted_iota(I32, (N_EXPERTS, N_EXPERTS), 0))
    lt = jnp.where(lt, 1.0, 0.0).astype(BF16)
    tile0 = jnp.dot(lt, ntile.astype(BF16), preferred_element_type=F32)
    row0 = tile0 * float(TM)

    def pass2(j, carry):
        ls = pl.ds(pl.multiple_of(j * blk, blk), blk)
        for k in range(2):
            ohk = jnp.where(eidx == e_s[k:k + 1, ls], 1.0, 0.0)
            dest = r_s[k:k + 1, ls] + jnp.sum(ohk * row0, axis=0, keepdims=True)
            dest_ref[k:k + 1, ls] = dest.astype(I32)
        return carry

    lax.fori_loop(0, nblk, pass2, 0)

    ml = slice(0, META_LANES)
    total = jnp.sum(ntile[:, ml], axis=0, keepdims=True)
    tile_i = lax.broadcasted_iota(I32, (N_EXPERTS, META_LANES), 1).astype(F32)
    tile_c = jnp.minimum(tile_i, total - 1.0)
    tile_end = tile0[:, ml] + ntile[:, ml]
    te = jnp.sum(jnp.where(tile_end <= tile_c, 1.0, 0.0), axis=0, keepdims=True)
    mine = eidx[:, ml] == te
    left = jnp.sum(jnp.where(mine, cnt[:, ml] - (tile_i - tile0[:, ml]) * float(TM), 0.0), axis=0, keepdims=True)
    nvalid = jnp.where(tile_i[0:1] < total, jnp.clip(left, 0.0, float(TM)), 0.0)
    meta_ref[...] = jnp.zeros((SUBLANES, META_LANES), I32)
    meta_ref[0:1, :] = te.astype(I32)
    meta_ref[1:2, :] = nvalid.astype(I32)
    meta_ref[2:3, :] = total.astype(I32)


def _route_call(lg, bias, n, max_tiles):
    return pl.pallas_call(
        functools.partial(_route_kernel, n=n, max_tiles=max_tiles),
        out_shape=(
            jax.ShapeDtypeStruct((2, n), I32),
            jax.ShapeDtypeStruct((2, n), F32),
            jax.ShapeDtypeStruct((SUBLANES, META_LANES), I32),
        ),
        scratch_shapes=[pltpu.VMEM((2, n), F32), pltpu.VMEM((2, n), F32)],
        compiler_params=pltpu.CompilerParams(vmem_limit_bytes=VMEM_LIMIT),
        name="route",
    )(lg, bias)


def _inverse_kernel(dest_ref, src_ref, *, n, n_slots):
    def clear(p, c):
        src_ref[p] = 0
        return c

    lax.fori_loop(0, n_slots, clear, 0, unroll=8)

    def fill(t, c):
        src_ref[dest_ref[0, t]] = t
        src_ref[dest_ref[1, t]] = t
        return c

    lax.fori_loop(0, n, fill, 0, unroll=8)


def _inverse_call(dest, n, n_slots):
    return pl.pallas_call(
        functools.partial(_inverse_kernel, n=n, n_slots=n_slots),
        in_specs=[pl.BlockSpec(memory_space=pltpu.SMEM)],
        out_specs=pl.BlockSpec(memory_space=pltpu.SMEM),
        out_shape=jax.ShapeDtypeStruct((n_slots,), I32),
        name="inverse",
    )(dest)


def _row_copy(src_hbm, row, dst_vmem, slot, sem):
    return pltpu.make_async_copy(src_hbm.at[pl.ds(row, 1), :], dst_vmem.at[pl.ds(slot, 1), :], sem)


def _moe_kernel(te_ref, nv_ref, nt_ref, src_ref, x1_hbm, w1_ref, w3_ref, w2_ref, o_ref,
                xbuf, w1b, w3b, w2b, sem):
    i = pl.program_id(0)
    nv = nv_ref[i]

    @pl.when(i == 0)
    def _():
        xbuf[...] = jnp.zeros_like(xbuf)

    @pl.when(i >= nt_ref[0])
    def _():
        o_ref[...] = jnp.zeros_like(o_ref)

    @pl.when(i < nt_ref[0])
    def _():
        def issue(r, c):
            _row_copy(x1_hbm, src_ref[0, 0, r], xbuf, r, sem.at[0]).start()
            return c

        lax.fori_loop(0, nv, issue, 0)

        prev = te_ref[jnp.maximum(i - 1, 0)]

        @pl.when((i == 0) | (te_ref[i] != prev))
        def _():
            w1b[...] = w1_ref[0].astype(BF16)
            w3b[...] = w3_ref[0].astype(BF16)
            w2b[...] = w2_ref[0].astype(BF16)

        def wait(r, c):
            _row_copy(x1_hbm, 0, xbuf, r, sem.at[0]).wait()
            return c

        lax.fori_loop(0, nv, wait, 0)

        xb = xbuf[...].astype(BF16)
        h1 = jnp.dot(xb, w1b[...], preferred_element_type=F32)
        h3 = jnp.dot(xb, w3b[...], preferred_element_type=F32)
        h = (jax.nn.silu(h1) * h3).astype(BF16)
        o_ref[...] = jnp.dot(h, w2b[...], preferred_element_type=F32)


def _moe_call(te, nvalid, ntiles, src3, x1, w1, w3, w2, max_tiles):
    grid_spec = pltpu.PrefetchScalarGridSpec(
        num_scalar_prefetch=3,
        grid=(max_tiles,),
        in_specs=[
            pl.BlockSpec((1, 1, TM), lambda i, te, nv, nt: (i, 0, 0), memory_space=pltpu.SMEM),
            pl.BlockSpec(memory_space=pl.ANY),
            pl.BlockSpec((1, D_MODEL, D_EXPERT), lambda i, te, nv, nt: (te[i], 0, 0)),
            pl.BlockSpec((1, D_MODEL, D_EXPERT), lambda i, te, nv, nt: (te[i], 0, 0)),
            pl.BlockSpec((1, D_EXPERT, D_MODEL), lambda i, te, nv, nt: (te[i], 0, 0)),
        ],
        out_specs=pl.BlockSpec((TM, D_MODEL), lambda i, te, nv, nt: (jnp.minimum(i, nt[0]), 0)),
        scratch_shapes=[
            pltpu.VMEM((TM, D_MODEL), F32),
            pltpu.VMEM((D_MODEL, D_EXPERT), BF16),
            pltpu.VMEM((D_MODEL, D_EXPERT), BF16),
            pltpu.VMEM((D_EXPERT, D_MODEL), BF16),
            pltpu.SemaphoreType.DMA((1,)),
        ],
    )
    return pl.pallas_call(
        _moe_kernel,
        grid_spec=grid_spec,
        out_shape=jax.ShapeDtypeStruct(((max_tiles + 1) * TM, D_MODEL), F32),
        compiler_params=pltpu.CompilerParams(dimension_semantics=("arbitrary",), vmem_limit_bytes=VMEM_LIMIT),
        name="moe",
    )(te, nvalid, ntiles, src3, x1, w1, w3, w2)


def _combine_kernel(dest_ref, x1_ref, tw_ref, ys_hbm, ln_g, ln_b, o_ref, buf0, buf1, sem, *, alpha):
    def issue(r, c):
        _row_copy(ys_hbm, dest_ref[0, 0, r], buf0, r, sem.at[0]).start()
        _row_copy(ys_hbm, dest_ref[0, 1, r], buf1, r, sem.at[1]).start()
        return c

    lax.fori_loop(0, ROWS, issue, 0)

    def wait(r, c):
        _row_copy(ys_hbm, 0, buf0, r, sem.at[0]).wait()
        _row_copy(ys_hbm, 0, buf1, r, sem.at[1]).wait()
        return c

    lax.fori_loop(0, ROWS, wait, 0)

    tw = tw_ref[...]
    moe = tw[:, 0:1] * buf0[...] + tw[:, 1:2] * buf1[...]
    xin = alpha * x1_ref[...] + moe
    mu = jnp.mean(xin, axis=-1, keepdims=True)
    xc = xin - mu
    var = jnp.mean(xc * xc, axis=-1, keepdims=True)
    y = xc * lax.rsqrt(var + LN_EPS) * ln_g[...] + ln_b[...]
    o_ref[...] = y.reshape(o_ref.shape)


def _combine_call(dest3, x1, tw, ys, ln_g, ln_b, out_shape, blk, blk0, alpha):
    bsz, tlen, _ = out_shape
    nt = tlen // blk[1]
    nb = bsz // blk[0]
    return pl.pallas_call(
        functools.partial(_combine_kernel, alpha=alpha),
        grid=(nb * nt,),
        in_specs=[
            pl.BlockSpec((1, 2, ROWS), lambda i: (blk0 + i, 0, 0), memory_space=pltpu.SMEM),
            pl.BlockSpec((ROWS, D_MODEL), lambda i: (blk0 + i, 0)),
            pl.BlockSpec((ROWS, 2), lambda i: (blk0 + i, 0)),
            pl.BlockSpec(memory_space=pl.ANY),
            _const_spec(ln_g.shape),
            _const_spec(ln_b.shape),
        ],
        out_specs=pl.BlockSpec(blk, lambda i: (i // nt, i % nt, 0)),
        out_shape=jax.ShapeDtypeStruct(out_shape, F32),
        scratch_shapes=[
            pltpu.VMEM((ROWS, D_MODEL), F32),
            pltpu.VMEM((ROWS, D_MODEL), F32),
            pltpu.SemaphoreType.DMA((2,)),
        ],
        compiler_params=pltpu.CompilerParams(dimension_semantics=("arbitrary",), vmem_limit_bytes=VMEM_LIMIT),
        name="combine",
    )(dest3, x1, tw, ys, ln_g, ln_b)


def _block_diag_chunks(w):
    per = LANE_CHUNK // LRU_HEAD_DIM
    nchunk = LRU_HEADS // per
    w4 = w.reshape(nchunk, per, LRU_HEAD_DIM, LRU_HEAD_DIM)
    bd = jnp.einsum("cjio,jk->cjiko", w4, jnp.eye(per, dtype=w.dtype))
    return bd.reshape(nchunk, LANE_CHUNK, LANE_CHUNK)


def _layer(yp, ys, pool_s, conv_s, h_s, lw, alpha, past_len):
    (w_in, pool_w, pool_b, pool_scale, conv_w, conv_b, rg_w, rg_b, ig_w, ig_b, lru_lambda, w_out,
     ln1_g, ln1_b, rgw, rgb, rew, reb, w1, w3, w2, ln2_g, ln2_b) = lw
    bp, tp, _ = yp.shape
    bs, ts, _ = ys.shape
    n_p, n_s = bp * tp, bs * ts
    n = n_p + n_s
    assert n % ROWS == 0 and n_p % ROWS == 0
    max_tiles = (2 * n) // TM + N_EXPERTS
    assert max_tiles <= META_LANES

    nchunk = D_LRU // LANE_CHUNK
    gate_w = jnp.concatenate([_block_diag_chunks(rg_w), _block_diag_chunks(ig_w)], axis=-1).astype(BF16)
    gate_b = jnp.concatenate(
        [rg_b.reshape(nchunk, LANE_CHUNK), ig_b.reshape(nchunk, LANE_CHUNK)], axis=-1).reshape(1, 2 * D_LRU)
    rt = jnp.concatenate(
        [rgw.T, jnp.zeros((SUBLANES - N_GROUPS, D_MODEL), F32),
         jnp.transpose(rew, (0, 2, 1)).reshape(N_EXPERTS, D_MODEL)], axis=0)
    rt_hi = rt.astype(BF16)
    rt_lo = (rt - rt_hi.astype(F32)).astype(BF16)
    r_bias = jnp.concatenate([rgb, jnp.zeros((SUBLANES - N_GROUPS,), F32), reb.reshape(N_EXPERTS)]).reshape(ROUTER_ROWS, 1)
    wts = (
        w_in.astype(BF16), pool_w.astype(BF16), pool_b.reshape(1, D_POOL), pool_scale.reshape(1, D_POOL),
        conv_w, conv_b.reshape(1, D_LRU), gate_w, gate_b, lru_lambda.reshape(1, D_LRU), w_out.astype(BF16),
        ln1_g.reshape(1, D_MODEL), ln1_b.reshape(1, D_MODEL), rt_hi, rt_lo,
    )

    x1, lg, pool_p, conv_p, h_p = _mixer_call(
        yp, jnp.zeros((bp, POOL_HALO, D_POOL), F32), jnp.zeros((bp, CONV_HALO, D_LRU), F32),
        jnp.zeros((bp, 1, D_LRU), F32), wts, None, None,
        ns=1, l=ROWS, pos0=0, alpha=alpha, n_total=n, blk0=0)
    pool_in = jnp.pad(pool_s, ((0, 0), (POOL_HALO - POOL_STATE, 0), (0, 0)))
    conv_in = jnp.pad(conv_s, ((0, 0), (CONV_HALO - (CONV_WIDTH - 1), 0), (0, 0)))
    x1, lg, pool_n, conv_n, h_n = _mixer_call(
        ys, pool_in, conv_in, h_s.reshape(bs, 1, D_LRU), wts, x1, lg,
        ns=ROWS // ts, l=ts, pos0=past_len, alpha=alpha, n_total=n, blk0=n_p // ROWS)

    dest, tw, meta = _route_call(lg, r_bias, n, max_tiles)
    src = _inverse_call(dest, n, max_tiles * TM)
    ysort = _moe_call(meta[0], meta[1], meta[2], src.reshape(max_tiles, 1, TM), x1, w1, w3, w2, max_tiles)

    dest3 = jnp.transpose(dest.reshape(2, n // ROWS, ROWS), (1, 0, 2))
    tw_t = tw.T
    g2, b2 = ln2_g.reshape(1, D_MODEL), ln2_b.reshape(1, D_MODEL)
    out_p = _combine_call(dest3, x1, tw_t, ysort, g2, b2, (bp, tp, D_MODEL), (1, ROWS, D_MODEL), 0, alpha)
    out_s = _combine_call(dest3, x1, tw_t, ysort, g2, b2, (bs, ts, D_MODEL), (ROWS // ts, ts, D_MODEL),
                          n_p // ROWS, alpha)
    states = (pool_p[:, 1:], conv_p[:, CONV_HALO - (CONV_WIDTH - 1):], h_p[:, SUBLANES - 1],
              pool_n[:, 1:], conv_n[:, CONV_HALO - (CONV_WIDTH - 1):], h_n[:, SUBLANES - 1])
    return out_p, out_s, states


def kernel(x_prompt, x_sample, state_pool, state_conv, state_h, w_in, pool_w, pool_b, pool_scale, conv_w, conv_b, rg_w, rg_b, ig_w, ig_b, lru_lambda, w_out, ln1_g, ln1_b, router_group_w, router_group_b, router_expert_w, router_expert_b, expert_w1, expert_w3, expert_w2, ln2_g, ln2_b):
    depth = w_in.shape[0]
    alpha = (2.0 * depth) ** 0.25
    past_len = 16384
    layer_weights = (w_in, pool_w, pool_b, pool_scale, conv_w, conv_b, rg_w, rg_b, ig_w, ig_b, lru_lambda, w_out,
                     ln1_g, ln1_b, router_group_w, router_group_b, router_expert_w, router_expert_b,
                     expert_w1, expert_w3, expert_w2, ln2_g, ln2_b)
    yp, ys = x_prompt, x_sample
    outs = [[] for _ in range(6)]
    for layer in range(depth):
        lw = tuple(w[layer] for w in layer_weights)
        yp, ys, states = _layer(yp, ys, state_pool[layer], state_conv[layer], state_h[layer], lw, alpha, past_len)
        for acc, s in zip(outs, states):
            acc.append(s)
    return (yp, ys) + tuple(jnp.stack(o, axis=0) for o in outs)
```

```python
import functools

import jax
import jax.numpy as jnp
from jax import lax
from jax.experimental import pallas as pl
from jax.experimental.pallas import tpu as pltpu

F32 = jnp.float32
BF16 = jnp.bfloat16
I32 = jnp.int32

D_MODEL = 2048
D_POOL = 1024
D_LRU = 1024
POOL_WINDOWS = (2, 4, 8, 16)
POOL_GROUP = D_POOL // len(POOL_WINDOWS)
POOL_STATE = max(POOL_WINDOWS) - 1
CONV_WIDTH = 4
LRU_HEADS = 16
LRU_HEAD_DIM = D_LRU // LRU_HEADS
LRU_C = 8.0
N_GROUPS = 4
PER_GROUP = 8
N_EXPERTS = N_GROUPS * PER_GROUP
D_EXPERT = D_MODEL // 4
LN_EPS = 1e-5

SUBLANES = 8
LANE_CHUNK = 256
POOL_HALO = 16
CONV_HALO = 8
ROWS = 256
TM = 256
ROUTER_ROWS = 8 + N_EXPERTS
META_LANES = 128
VMEM_LIMIT = 56 * 1024 * 1024

_NT = (((1,), (1,)), ((), ()))


def _const_spec(shape):
    nd = len(shape)
    return pl.BlockSpec(shape, lambda *_: (0,) * nd, pipeline_mode=pl.Buffered(1))


def _mixer_kernel(x_ref, pool_in, conv_in, h_in, w_in, pool_w, pool_b, pool_scale, conv_w, conv_b,
                  gate_w, gate_b, lam, w_out, ln_g, ln_b, rt_hi, rt_lo, *rest, ns, l, pos0, alpha, aliased):
    if aliased:
        rest = rest[2:]
    (x1_ref, lg_ref, pool_o, conv_o, h_o,
     pool_ext, conv_ext, h_prev, gate_s, a_s, u_s, y_s) = rest
    t = pl.program_id(1)
    rows = ns * l

    @pl.when(t == 0)
    def _():
        pool_ext[:, 0:POOL_HALO, :] = pool_in[...]
        conv_ext[:, 0:CONV_HALO, :] = conv_in[...]
        h_prev[...] = h_in[...]

    xb = x_ref[...].reshape(rows, D_MODEL).astype(BF16)
    pool_ext[:, POOL_HALO:, :] = jnp.dot(
        xb, w_in[:, 0:D_POOL], preferred_element_type=F32).reshape(ns, l, D_POOL)
    conv_ext[:, CONV_HALO:, :] = jnp.dot(
        xb, w_in[:, D_POOL:D_POOL + D_LRU], preferred_element_type=F32).reshape(ns, l, D_LRU)
    gate_s[...] = jnp.dot(xb, w_in[:, D_POOL + D_LRU:], preferred_element_type=F32)

    pos = pos0 + t * l + lax.broadcasted_iota(I32, (ns, l, LANE_CHUNK), 1).reshape(rows, LANE_CHUNK)

    for g, w in enumerate(POOL_WINDOWS):
        cols = slice(g * POOL_GROUP, (g + 1) * POOL_GROUP)
        s = pool_ext[:, :, cols].reshape(ns * (POOL_HALO + l), POOL_GROUP)
        step = 1
        while step < w:
            s = s + pltpu.roll(s, step, 0)
            step *= 2
        win = s.reshape(ns, POOL_HALO + l, POOL_GROUP)[:, POOL_HALO:, :].reshape(rows, POOL_GROUP)
        u = pool_ext[:, POOL_HALO:, cols].reshape(rows, POOL_GROUP)
        if pos0 >= w - 1:
            inv = 1.0 / w
        else:
            inv = 1.0 / jnp.minimum(pos + 1, w).astype(F32)
        d = win * inv - u
        z = jnp.dot(d.astype(BF16), pool_w[g], preferred_element_type=F32) + pool_b[:, cols]
        y_s[:, cols] = (z * pool_scale[:, cols]).astype(BF16)

    lam_v = lam[...]
    softplus_neg = jnp.maximum(-lam_v, 0.0) + jnp.log1p(jnp.exp(-jnp.abs(lam_v)))
    log_a_scale = -LRU_C * softplus_neg
    sub = lax.broadcasted_iota(I32, (rows // SUBLANES, SUBLANES, LANE_CHUNK), 1)
    for c in range(D_LRU // LANE_CHUNK):
        cols = slice(c * LANE_CHUNK, (c + 1) * LANE_CHUNK)
        ce = conv_ext[:, :, cols].reshape(ns * (CONV_HALO + l), LANE_CHUNK)
        conv = conv_b[:, cols] + conv_w[CONV_WIDTH - 1:CONV_WIDTH, cols] * ce
        for k in range(1, CONV_WIDTH):
            conv = conv + conv_w[CONV_WIDTH - 1 - k:CONV_WIDTH - k, cols] * pltpu.roll(ce, k, 0)
        conv = conv.reshape(ns, CONV_HALO + l, LANE_CHUNK)[:, CONV_HALO:, :].reshape(rows, LANE_CHUNK)
        gz = jnp.dot(conv.astype(BF16), gate_w[c], preferred_element_type=F32)
        gz = gz + gate_b[:, 2 * c * LANE_CHUNK:2 * (c + 1) * LANE_CHUNK]
        r = jax.nn.sigmoid(gz[:, :LANE_CHUNK])
        i = jax.nn.sigmoid(gz[:, LANE_CHUNK:])
        log_a = r * log_a_scale[:, cols]
        a = jnp.exp(log_a)
        th = jnp.tanh(log_a)
        mult = jnp.sqrt(-2.0 * th / (1.0 - th))
        if pos0 == 0:
            mult = jnp.where(pos == 0, 1.0, mult)
        u = mult * (i * conv)
        a3 = a.reshape(rows // SUBLANES, SUBLANES, LANE_CHUNK)
        u3 = u.reshape(rows // SUBLANES, SUBLANES, LANE_CHUNK)
        for sh in (1, 2, 4):
            keep = sub >= sh
            a_sh = jnp.where(keep, pltpu.roll(a3, sh, 1), 1.0)
            u_sh = jnp.where(keep, pltpu.roll(u3, sh, 1), 0.0)
            u3 = a3 * u_sh + u3
            a3 = a3 * a_sh
        a_s[:, :, cols] = a3.reshape(ns, l, LANE_CHUNK)
        u_s[:, :, cols] = u3.reshape(ns, l, LANE_CHUNK)

    hp = h_prev[...]
    for gi in range(l // SUBLANES):
        sl = slice(gi * SUBLANES, (gi + 1) * SUBLANES)
        h = a_s[:, sl, :] * hp + u_s[:, sl, :]
        u_s[:, sl, :] = h
        hp = h[:, SUBLANES - 1:SUBLANES, :]
    h_prev[...] = hp

    for c in range(D_LRU // LANE_CHUNK):
        cols = slice(c * LANE_CHUNK, (c + 1) * LANE_CHUNK)
        h = u_s[:, :, cols].reshape(rows, LANE_CHUNK)
        y_s[:, D_POOL + c * LANE_CHUNK:D_POOL + (c + 1) * LANE_CHUNK] = (
            h * jax.nn.gelu(gate_s[:, cols])).astype(BF16)

    mix = jnp.dot(y_s[...], w_out[...], preferred_element_type=F32)
    xin = alpha * x_ref[...].reshape(rows, D_MODEL) + mix
    mu = jnp.mean(xin, axis=-1, keepdims=True)
    xc = xin - mu
    var = jnp.mean(xc * xc, axis=-1, keepdims=True)
    x1 = xc * lax.rsqrt(var + LN_EPS) * ln_g[...] + ln_b[...]
    x1_ref[...] = x1

    hi = x1.astype(BF16)
    lo = (x1 - hi.astype(F32)).astype(BF16)
    lg = lax.dot_general(rt_hi[...], hi, _NT, preferred_element_type=F32)
    lg = lg + lax.dot_general(rt_lo[...], hi, _NT, preferred_element_type=F32)
    lg = lg + lax.dot_general(rt_hi[...], lo, _NT, preferred_element_type=F32)
    lg_ref[...] = lg

    pool_tail = pool_ext[:, l:l + POOL_HALO, :]
    conv_tail = conv_ext[:, l:l + CONV_HALO, :]
    pool_o[...] = pool_tail
    conv_o[...] = conv_tail
    h_o[...] = u_s[:, l - SUBLANES:l, :]
    pool_ext[:, 0:POOL_HALO, :] = pool_tail
    conv_ext[:, 0:CONV_HALO, :] = conv_tail


def _mixer_call(x, pool_in, conv_in, h_in, wts, x1_buf, lg_buf, *, ns, l, pos0, alpha, n_total, blk0):
    bsz, tlen, _ = x.shape
    nb, nt = bsz // ns, tlen // l
    rows = ns * l
    assert rows == ROWS and bsz % ns == 0 and tlen % l == 0 and l % SUBLANES == 0
    aliased = x1_buf is not None

    def row_blk(b, t):
        return (blk0 + b * nt + t, 0)

    in_specs = [
        pl.BlockSpec((ns, l, D_MODEL), lambda b, t: (b, t, 0)),
        pl.BlockSpec((ns, POOL_HALO, D_POOL), lambda b, t: (b, 0, 0)),
        pl.BlockSpec((ns, CONV_HALO, D_LRU), lambda b, t: (b, 0, 0)),
        pl.BlockSpec((ns, 1, D_LRU), lambda b, t: (b, 0, 0)),
    ] + [_const_spec(w.shape) for w in wts]
    args = [x, pool_in, conv_in, h_in, *wts]
    aliases = {}
    if aliased:
        in_specs += [pl.BlockSpec(memory_space=pl.ANY), pl.BlockSpec(memory_space=pl.ANY)]
        aliases = {len(args): 0, len(args) + 1: 1}
        args += [x1_buf, lg_buf]
    out_shape = (
        jax.ShapeDtypeStruct((n_total, D_MODEL), F32),
        jax.ShapeDtypeStruct((ROUTER_ROWS, n_total), F32),
        jax.ShapeDtypeStruct((bsz, POOL_HALO, D_POOL), F32),
        jax.ShapeDtypeStruct((bsz, CONV_HALO, D_LRU), F32),
        jax.ShapeDtypeStruct((bsz, SUBLANES, D_LRU), F32),
    )
    out_specs = (
        pl.BlockSpec((rows, D_MODEL), row_blk),
        pl.BlockSpec((ROUTER_ROWS, rows), lambda b, t: (0, blk0 + b * nt + t)),
        pl.BlockSpec((ns, POOL_HALO, D_POOL), lambda b, t: (b, 0, 0)),
        pl.BlockSpec((ns, CONV_HALO, D_LRU), lambda b, t: (b, 0, 0)),
        pl.BlockSpec((ns, SUBLANES, D_LRU), lambda b, t: (b, 0, 0)),
    )
    scratch = [
        pltpu.VMEM((ns, POOL_HALO + l, D_POOL), F32),
        pltpu.VMEM((ns, CONV_HALO + l, D_LRU), F32),
        pltpu.VMEM((ns, 1, D_LRU), F32),
        pltpu.VMEM((rows, D_LRU), F32),
        pltpu.VMEM((ns, l, D_LRU), F32),
        pltpu.VMEM((ns, l, D_LRU), F32),
        pltpu.VMEM((rows, D_MODEL), BF16),
    ]
    return pl.pallas_call(
        functools.partial(_mixer_kernel, ns=ns, l=l, pos0=pos0, alpha=alpha, aliased=aliased),
        grid=(nb, nt),
        in_specs=in_specs,
        out_specs=out_specs,
        out_shape=out_shape,
        scratch_shapes=scratch,
        input_output_aliases=aliases,
        compiler_params=pltpu.CompilerParams(
            dimension_semantics=("arbitrary", "arbitrary"), vmem_limit_bytes=VMEM_LIMIT),
        name="mixer_seq" if nt > 1 else "mixer_step",
    )(*args)


def _route_kernel(lg_ref, bias_ref, dest_ref, tw_ref, meta_ref, e_s, r_s, *, n, max_tiles):
    blk = 256
    nblk = n // blk
    neg_inf = -jnp.inf
    ridx8 = lax.broadcasted_iota(I32, (SUBLANES, blk), 0).astype(F32)
    eidx = lax.broadcasted_iota(I32, (N_EXPERTS, blk), 0).astype(F32)
    tri = (lax.broadcasted_iota(I32, (blk, blk), 0) < lax.broadcasted_iota(I32, (blk, blk), 1))
    tri = jnp.where(tri, 1.0, 0.0).astype(BF16)

    def first_idx(vals, m):
        return jnp.min(jnp.where(vals == m, ridx8, float(SUBLANES)), axis=0, keepdims=True)

    def pass1(j, base):
        ls = pl.ds(pl.multiple_of(j * blk, blk), blk)
        lg = lg_ref[0:SUBLANES, ls] + bias_ref[0:SUBLANES, :]
        lg = jnp.where(ridx8 < N_GROUPS, lg, neg_inf)
        m = jnp.max(lg, axis=0, keepdims=True)
        gi = first_idx(lg, m)
        pg_sel = 1.0 / jnp.sum(jnp.exp(lg - m), axis=0, keepdims=True)
        le = jnp.zeros((PER_GROUP, blk), F32)
        for g in range(N_GROUPS):
            rows = slice(SUBLANES + g * PER_GROUP, SUBLANES + (g + 1) * PER_GROUP)
            le = jnp.where(gi == float(g), lg_ref[rows, ls] + bias_ref[rows, :], le)
        m1 = jnp.max(le, axis=0, keepdims=True)
        i1 = first_idx(le, m1)
        le2 = jnp.where(ridx8 == i1, neg_inf, le)
        m2 = jnp.max(le2, axis=0, keepdims=True)
        i2 = first_idx(le2, m2)
        e21 = jnp.exp(m2 - m1)
        denom = 1.0 / (1.0 + e21)
        tw_ref[0:1, ls] = pg_sel * denom
        tw_ref[1:2, ls] = pg_sel * (e21 * denom)
        e1 = gi * float(PER_GROUP) + i1
        e2 = gi * float(PER_GROUP) + i2
        e_s[0:1, ls] = e1
        e_s[1:2, ls] = e2
        oh1 = jnp.where(eidx == e1, 1.0, 0.0)
        oh2 = jnp.where(eidx == e2, 1.0, 0.0)
        oh = oh1 + oh2
        before = base + jnp.dot(oh.astype(BF16), tri, preferred_element_type=F32)
        r_s[0:1, ls] = jnp.sum(oh1 * before, axis=0, keepdims=True)
        r_s[1:2, ls] = jnp.sum(oh2 * before, axis=0, keepdims=True)
        return base + jnp.sum(oh, axis=1, keepdims=True)

    cnt = lax.fori_loop(0, nblk, pass1, jnp.zeros((N_EXPERTS, blk), F32))

    ntile = jnp.floor((cnt + float(TM - 1)) * (1.0 / TM))
    lt = (lax.broadcasted_iota(I32, (N_EXPERTS, N_EXPERTS), 1) < lax.broadcasted_iota(I32, (N_EXPERTS, N_EXPERTS), 0))
    lt = jnp.where(lt, 1.0, 0.0).astype(BF16)
    tile0 = jnp.dot(lt, ntile.astype(BF16), preferred_element_type=F32)
    row0 = tile0 * float(TM)

    def pass2(j, carry):
        ls = pl.ds(pl.multiple_of(j * blk, blk), blk)
        for k in range(2):
            ohk = jnp.where(eidx == e_s[k:k + 1, ls], 1.0, 0.0)
            dest = r_s[k:k + 1, ls] + jnp.sum(ohk * row0, axis=0, keepdims=True)
            dest_ref[k:k + 1, ls] = dest.astype(I32)
        return carry

    lax.fori_loop(0, nblk, pass2, 0)

    ml = slice(0, META_LANES)
    total = jnp.sum(ntile[:, ml], axis=0, keepdims=True)
    tile_i = lax.broadcasted_iota(I32, (N_EXPERTS, META_LANES), 1).astype(F32)
    tile_c = jnp.minimum(tile_i, total - 1.0)
    tile_end = tile0[:, ml] + ntile[:, ml]
    te = jnp.sum(jnp.where(tile_end <= tile_c, 1.0, 0.0), axis=0, keepdims=True)
    meta_ref[...] = jnp.zeros((SUBLANES, META_LANES), I32)
    meta_ref[0:1, :] = te.astype(I32)
    meta_ref[1:2, :] = total.astype(I32)
    later = (eidx[:, ml] > te) & (ntile[:, ml] > 0.0)
    nxt = jnp.min(jnp.where(later, eidx[:, ml], float(N_EXPERTS)), axis=0, keepdims=True)
    meta_ref[2:3, :] = nxt.astype(I32)


def _route_call(lg, bias, n, max_tiles):
    return pl.pallas_call(
        functools.partial(_route_kernel, n=n, max_tiles=max_tiles),
        out_shape=(
            jax.ShapeDtypeStruct((2, n), I32),
            jax.ShapeDtypeStruct((2, n), F32),
            jax.ShapeDtypeStruct((SUBLANES, META_LANES), I32),
        ),
        scratch_shapes=[pltpu.VMEM((2, n), F32), pltpu.VMEM((2, n), F32)],
        compiler_params=pltpu.CompilerParams(vmem_limit_bytes=VMEM_LIMIT),
        name="route",
    )(lg, bias)


def _inverse_kernel(dest_ref, zeros_hbm, src_ref, sem, *, n):
    init = pltpu.make_async_copy(zeros_hbm, src_ref, sem)
    init.start()
    init.wait()

    def fill(t, c):
        src_ref[dest_ref[t]] = t
        src_ref[dest_ref[n + t]] = t
        return c

    lax.fori_loop(0, n, fill, 0, unroll=8)


def _inverse_call(dest_flat, n, n_slots):
    return pl.pallas_call(
        functools.partial(_inverse_kernel, n=n),
        in_specs=[pl.BlockSpec(memory_space=pltpu.SMEM), pl.BlockSpec(memory_space=pl.ANY)],
        out_specs=pl.BlockSpec(memory_space=pltpu.SMEM),
        out_shape=jax.ShapeDtypeStruct((n_slots,), I32),
        scratch_shapes=[pltpu.SemaphoreType.DMA(())],
        name="inverse",
    )(dest_flat, jnp.zeros((n_slots,), I32))


def _row_copy(src_hbm, row, dst_vmem, slot, sem):
    return pltpu.make_async_copy(src_hbm.at[pl.ds(row, 1), :], dst_vmem.at[pl.ds(slot, 1), :], sem)


def _tile_copy(src_hbm, dst_vmem, sem):
    return pltpu.make_async_copy(src_hbm.at[pl.ds(0, dst_vmem.shape[0]), :], dst_vmem, sem)


def _moe_kernel(te_ref, ne_ref, nt_ref, src_cur, src_nxt, x1_hbm, w1_hbm, w3_hbm, w2_hbm, o_ref,
                xbuf, wf1, wf3, wf2, w1b, w3b, w2b, wslot, gsem, wsem):
    i = pl.program_id(0)
    nt = nt_ref[0]

    def gather_start(src_ref, slot):
        for r in range(TM):
            _row_copy(x1_hbm, src_ref[0, 0, r], xbuf.at[slot], r, gsem.at[slot]).start()

    def weight_copies(e, slot):
        return (pltpu.make_async_copy(w1_hbm.at[e], wf1.at[slot], wsem.at[slot, 0]),
                pltpu.make_async_copy(w3_hbm.at[e], wf3.at[slot], wsem.at[slot, 1]),
                pltpu.make_async_copy(w2_hbm.at[e], wf2.at[slot], wsem.at[slot, 2]))

    @pl.when(i == 0)
    def _():
        for cp in weight_copies(te_ref[0], 0):
            cp.start()
        wslot[0] = 0
        gather_start(src_cur, 0)

    @pl.when(i >= nt)
    def _():
        o_ref[...] = jnp.zeros_like(o_ref)

    @pl.when(i < nt)
    def _():
        slot = lax.rem(i, 2)
        _tile_copy(x1_hbm, xbuf.at[slot], gsem.at[slot]).wait()

        @pl.when((i == 0) | (te_ref[i] != te_ref[jnp.maximum(i - 1, 0)]))
        def _():
            cur = wslot[0]
            for cp in weight_copies(te_ref[i], cur):
                cp.wait()
            w1b[...] = wf1[cur].astype(BF16)
            w3b[...] = wf3[cur].astype(BF16)
            w2b[...] = wf2[cur].astype(BF16)

            @pl.when(ne_ref[i] < N_EXPERTS)
            def _():
                for cp in weight_copies(ne_ref[i], 1 - cur):
                    cp.start()

            wslot[0] = 1 - cur

        xb = xbuf[slot].astype(BF16)
        gather_start(src_nxt, 1 - slot)
        h1 = jnp.dot(xb, w1b[...], preferred_element_type=F32)
        h3 = jnp.dot(xb, w3b[...], preferred_element_type=F32)
        h = (jax.nn.silu(h1) * h3).astype(BF16)
        o_ref[...] = jnp.dot(h, w2b[...], preferred_element_type=F32)

        @pl.when(i + 1 >= nt)
        def _():
            _tile_copy(x1_hbm, xbuf.at[1 - slot], gsem.at[1 - slot]).wait()


def _moe_call(te, ne, ntiles, src3, x1, w1, w3, w2, max_tiles):
    grid_spec = pltpu.PrefetchScalarGridSpec(
        num_scalar_prefetch=3,
        grid=(max_tiles,),
        in_specs=[
            pl.BlockSpec((1, 1, TM), lambda i, te, ne, nt: (i, 0, 0), memory_space=pltpu.SMEM),
            pl.BlockSpec((1, 1, TM), lambda i, te, ne, nt: (jnp.minimum(i + 1, max_tiles - 1), 0, 0),
                         memory_space=pltpu.SMEM),
            pl.BlockSpec(memory_space=pl.ANY),
            pl.BlockSpec(memory_space=pl.ANY),
            pl.BlockSpec(memory_space=pl.ANY),
            pl.BlockSpec(memory_space=pl.ANY),
        ],
        out_specs=pl.BlockSpec((TM, D_MODEL), lambda i, te, ne, nt: (jnp.minimum(i, nt[0]), 0)),
        scratch_shapes=[
            pltpu.VMEM((2, TM, D_MODEL), F32),
            pltpu.VMEM((2, D_MODEL, D_EXPERT), F32),
            pltpu.VMEM((2, D_MODEL, D_EXPERT), F32),
            pltpu.VMEM((2, D_EXPERT, D_MODEL), F32),
            pltpu.VMEM((D_MODEL, D_EXPERT), BF16),
            pltpu.VMEM((D_MODEL, D_EXPERT), BF16),
            pltpu.VMEM((D_EXPERT, D_MODEL), BF16),
            pltpu.SMEM((1,), I32),
            pltpu.SemaphoreType.DMA((2,)),
            pltpu.SemaphoreType.DMA((2, 3)),
        ],
    )
    return pl.pallas_call(
        _moe_kernel,
        grid_spec=grid_spec,
        out_shape=jax.ShapeDtypeStruct(((max_tiles + 1) * TM, D_MODEL), F32),
        compiler_params=pltpu.CompilerParams(dimension_semantics=("arbitrary",), vmem_limit_bytes=VMEM_LIMIT),
        name="moe",
    )(te, ne, ntiles, src3, src3, x1, w1, w3, w2)


def _combine_kernel(dest_cur, dest_nxt, x1_ref, tw_ref, ys_hbm, ln_g, ln_b, o_ref, buf, sem, *, alpha, nsteps):
    i = pl.program_id(0)
    slot = lax.rem(i, 2)

    def gather_start(dest_ref, s):
        for r in range(ROWS):
            for k in range(2):
                _row_copy(ys_hbm, dest_ref[0, k, r], buf.at[s, k], r, sem.at[s, k]).start()

    def gather_wait(s):
        for k in range(2):
            _tile_copy(ys_hbm, buf.at[s, k], sem.at[s, k]).wait()

    @pl.when(i == 0)
    def _():
        gather_start(dest_cur, 0)

    gather_wait(slot)
    tw = tw_ref[...]
    moe = tw[:, 0:1] * buf[slot, 0] + tw[:, 1:2] * buf[slot, 1]
    gather_start(dest_nxt, 1 - slot)
    xin = alpha * x1_ref[...] + moe
    mu = jnp.mean(xin, axis=-1, keepdims=True)
    xc = xin - mu
    var = jnp.mean(xc * xc, axis=-1, keepdims=True)
    y = xc * lax.rsqrt(var + LN_EPS) * ln_g[...] + ln_b[...]
    o_ref[...] = y.reshape(o_ref.shape)

    @pl.when(i == nsteps - 1)
    def _():
        gather_wait(1 - slot)


def _combine_call(dest3, x1, tw, ys, ln_g, ln_b, out_shape, blk, blk0, alpha):
    bsz, tlen, _ = out_shape
    nt = tlen // blk[1]
    nsteps = (bsz // blk[0]) * nt
    return pl.pallas_call(
        functools.partial(_combine_kernel, alpha=alpha, nsteps=nsteps),
        grid=(nsteps,),
        in_specs=[
            pl.BlockSpec((1, 2, ROWS), lambda i: (blk0 + i, 0, 0), memory_space=pltpu.SMEM),
            pl.BlockSpec((1, 2, ROWS), lambda i: (blk0 + jnp.minimum(i + 1, nsteps - 1), 0, 0),
                         memory_space=pltpu.SMEM),
            pl.BlockSpec((ROWS, D_MODEL), lambda i: (blk0 + i, 0)),
            pl.BlockSpec((ROWS, 2), lambda i: (blk0 + i, 0)),
            pl.BlockSpec(memory_space=pl.ANY),
            _const_spec(ln_g.shape),
            _const_spec(ln_b.shape),
        ],
        out_specs=pl.BlockSpec(blk, lambda i: (i // nt, i % nt, 0)),
        out_shape=jax.ShapeDtypeStruct(out_shape, F32),
        scratch_shapes=[
            pltpu.VMEM((2, 2, ROWS, D_MODEL), F32),
            pltpu.SemaphoreType.DMA((2, 2)),
        ],
        compiler_params=pltpu.CompilerParams(dimension_semantics=("arbitrary",), vmem_limit_bytes=VMEM_LIMIT),
        name="combine",
    )(dest3, dest3, x1, tw, ys, ln_g, ln_b)


def _block_diag_chunks(w):
    per = LANE_CHUNK // LRU_HEAD_DIM
    nchunk = LRU_HEADS // per
    w4 = w.reshape(nchunk, per, LRU_HEAD_DIM, LRU_HEAD_DIM)
    bd = jnp.einsum("cjio,jk->cjiko", w4, jnp.eye(per, dtype=w.dtype))
    return bd.reshape(nchunk, LANE_CHUNK, LANE_CHUNK)


def _layer(yp, ys, pool_s, conv_s, h_s, lw, alpha, past_len):
    (w_in, pool_w, pool_b, pool_scale, conv_w, conv_b, rg_w, rg_b, ig_w, ig_b, lru_lambda, w_out,
     ln1_g, ln1_b, rgw, rgb, rew, reb, w1, w3, w2, ln2_g, ln2_b) = lw
    bp, tp, _ = yp.shape
    bs, ts, _ = ys.shape
    n_p, n_s = bp * tp, bs * ts
    n = n_p + n_s
    assert n % ROWS == 0 and n_p % ROWS == 0
    max_tiles = (2 * n) // TM + N_EXPERTS
    assert max_tiles <= META_LANES

    nchunk = D_LRU // LANE_CHUNK
    gate_w = jnp.concatenate([_block_diag_chunks(rg_w), _block_diag_chunks(ig_w)], axis=-1).astype(BF16)
    gate_b = jnp.concatenate(
        [rg_b.reshape(nchunk, LANE_CHUNK), ig_b.reshape(nchunk, LANE_CHUNK)], axis=-1).reshape(1, 2 * D_LRU)
    rt = jnp.concatenate(
        [rgw.T, jnp.zeros((SUBLANES - N_GROUPS, D_MODEL), F32),
         jnp.transpose(rew, (0, 2, 1)).reshape(N_EXPERTS, D_MODEL)], axis=0)
    rt_hi = rt.astype(BF16)
    rt_lo = (rt - rt_hi.astype(F32)).astype(BF16)
    r_bias = jnp.concatenate([rgb, jnp.zeros((SUBLANES - N_GROUPS,), F32), reb.reshape(N_EXPERTS)]).reshape(ROUTER_ROWS, 1)
    wts = (
        w_in.astype(BF16), pool_w.astype(BF16), pool_b.reshape(1, D_POOL), pool_scale.reshape(1, D_POOL),
        conv_w, conv_b.reshape(1, D_LRU), gate_w, gate_b, lru_lambda.reshape(1, D_LRU), w_out.astype(BF16),
        ln1_g.reshape(1, D_MODEL), ln1_b.reshape(1, D_MODEL), rt_hi, rt_lo,
    )

    x1, lg, pool_p, conv_p, h_p = _mixer_call(
        yp, jnp.zeros((bp, POOL_HALO, D_POOL), F32), jnp.zeros((bp, CONV_HALO, D_LRU), F32),
        jnp.zeros((bp, 1, D_LRU), F32), wts, None, None,
        ns=1, l=ROWS, pos0=0, alpha=alpha, n_total=n, blk0=0)
    pool_in = jnp.pad(pool_s, ((0, 0), (POOL_HALO - POOL_STATE, 0), (0, 0)))
    conv_in = jnp.pad(conv_s, ((0, 0), (CONV_HALO - (CONV_WIDTH - 1), 0), (0, 0)))
    x1, lg, pool_n, conv_n, h_n = _mixer_call(
        ys, pool_in, conv_in, h_s.reshape(bs, 1, D_LRU), wts, x1, lg,
        ns=ROWS // ts, l=ts, pos0=past_len, alpha=alpha, n_total=n, blk0=n_p // ROWS)

    dest, tw, meta = _route_call(lg, r_bias, n, max_tiles)
    src = _inverse_call(dest.reshape(2 * n), n, max_tiles * TM)
    ysort = _moe_call(meta[0], meta[2], meta[1, 0:1], src.reshape(max_tiles, 1, TM), x1, w1, w3, w2, max_tiles)

    dest3 = jnp.transpose(dest.reshape(2, n // ROWS, ROWS), (1, 0, 2))
    tw_t = tw.T
    g2, b2 = ln2_g.reshape(1, D_MODEL), ln2_b.reshape(1, D_MODEL)
    out_p = _combine_call(dest3, x1, tw_t, ysort, g2, b2, (bp, tp, D_MODEL), (1, ROWS, D_MODEL), 0, alpha)
    out_s = _combine_call(dest3, x1, tw_t, ysort, g2, b2, (bs, ts, D_MODEL), (ROWS // ts, ts, D_MODEL),
                          n_p // ROWS, alpha)
    states = (pool_p[:, 1:], conv_p[:, CONV_HALO - (CONV_WIDTH - 1):], h_p[:, SUBLANES - 1],
              pool_n[:, 1:], conv_n[:, CONV_HALO - (CONV_WIDTH - 1):], h_n[:, SUBLANES - 1])
    return out_p, out_s, states


def kernel(x_prompt, x_sample, state_pool, state_conv, state_h, w_in, pool_w, pool_b, pool_scale, conv_w, conv_b, rg_w, rg_b, ig_w, ig_b, lru_lambda, w_out, ln1_g, ln1_b, router_group_w, router_group_b, router_expert_w, router_expert_b, expert_w1, expert_w3, expert_w2, ln2_g, ln2_b):
    depth = w_in.shape[0]
    alpha = (2.0 * depth) ** 0.25
    past_len = 16384
    layer_weights = (w_in, pool_w, pool_b, pool_scale, conv_w, conv_b, rg_w, rg_b, ig_w, ig_b, lru_lambda, w_out,
                     ln1_g, ln1_b, router_group_w, router_group_b, router_expert_w, router_expert_b,
                     expert_w1, expert_w3, expert_w2, ln2_g, ln2_b)
    yp, ys = x_prompt, x_sample
    outs = [[] for _ in range(6)]
    for layer in range(depth):
        lw = tuple(w[layer] for w in layer_weights)
        yp, ys, states = _layer(yp, ys, state_pool[layer], state_conv[layer], state_h[layer], lw, alpha, past_len)
        for acc, s in zip(outs, states):
            acc.append(s)
    return (yp, ys) + tuple(jnp.stack(o, axis=0) for o in outs)
```

```python
import functools

import jax
import jax.numpy as jnp
from jax import lax
from jax.experimental import pallas as pl
from jax.experimental.pallas import tpu as pltpu

F32 = jnp.float32
BF16 = jnp.bfloat16
I32 = jnp.int32

D_MODEL = 2048
D_POOL = 1024
D_LRU = 1024
POOL_WINDOWS = (2, 4, 8, 16)
POOL_GROUP = D_POOL // len(POOL_WINDOWS)
POOL_STATE = max(POOL_WINDOWS) - 1
CONV_WIDTH = 4
LRU_HEADS = 16
LRU_HEAD_DIM = D_LRU // LRU_HEADS
LRU_C = 8.0
N_GROUPS = 4
PER_GROUP = 8
N_EXPERTS = N_GROUPS * PER_GROUP
D_EXPERT = D_MODEL // 4
LN_EPS = 1e-5

SUBLANES = 8
LANE_CHUNK = 256
POOL_HALO = 16
CONV_HALO = 8
ROWS = 256
TM = 256
ROUTER_ROWS = 8 + N_EXPERTS
META_LANES = 128
VMEM_LIMIT = 56 * 1024 * 1024

_NT = (((1,), (1,)), ((), ()))


def _const_spec(shape):
    nd = len(shape)
    return pl.BlockSpec(shape, lambda *_: (0,) * nd, pipeline_mode=pl.Buffered(1))


def _mixer_kernel(x_ref, pool_in, conv_in, h_in, w_in, pool_w, pool_b, pool_scale, conv_w, conv_b,
                  gate_w, gate_b, lam, w_out, ln_g, ln_b, rt_hi, rt_lo, *rest, ns, l, pos0, alpha, aliased):
    if aliased:
        rest = rest[2:]
    (x1_ref, lg_ref, pool_o, conv_o, h_o,
     pool_ext, conv_ext, h_prev, gate_s, a_s, u_s, y_s) = rest
    t = pl.program_id(1)
    rows = ns * l

    @pl.when(t == 0)
    def _():
        pool_ext[:, 0:POOL_HALO, :] = pool_in[...]
        conv_ext[:, 0:CONV_HALO, :] = conv_in[...]
        h_prev[...] = h_in[...]

    xb = x_ref[...].reshape(rows, D_MODEL).astype(BF16)
    pool_ext[:, POOL_HALO:, :] = jnp.dot(
        xb, w_in[:, 0:D_POOL], preferred_element_type=F32).reshape(ns, l, D_POOL)
    conv_ext[:, CONV_HALO:, :] = jnp.dot(
        xb, w_in[:, D_POOL:D_POOL + D_LRU], preferred_element_type=F32).reshape(ns, l, D_LRU)
    gate_s[...] = jnp.dot(xb, w_in[:, D_POOL + D_LRU:], preferred_element_type=F32)

    pos = pos0 + t * l + lax.broadcasted_iota(I32, (ns, l, LANE_CHUNK), 1).reshape(rows, LANE_CHUNK)

    for g, w in enumerate(POOL_WINDOWS):
        cols = slice(g * POOL_GROUP, (g + 1) * POOL_GROUP)
        s = pool_ext[:, :, cols].reshape(ns * (POOL_HALO + l), POOL_GROUP)
        step = 1
        while step < w:
            s = s + pltpu.roll(s, step, 0)
            step *= 2
        win = s.reshape(ns, POOL_HALO + l, POOL_GROUP)[:, POOL_HALO:, :].reshape(rows, POOL_GROUP)
        u = pool_ext[:, POOL_HALO:, cols].reshape(rows, POOL_GROUP)
        if pos0 >= w - 1:
            inv = 1.0 / w
        else:
            inv = 1.0 / jnp.minimum(pos + 1, w).astype(F32)
        d = win * inv - u
        z = jnp.dot(d.astype(BF16), pool_w[g], preferred_element_type=F32) + pool_b[:, cols]
        y_s[:, cols] = (z * pool_scale[:, cols]).astype(BF16)

    lam_v = lam[...]
    softplus_neg = jnp.maximum(-lam_v, 0.0) + jnp.log1p(jnp.exp(-jnp.abs(lam_v)))
    log_a_scale = -LRU_C * softplus_neg
    sub = lax.broadcasted_iota(I32, (rows // SUBLANES, SUBLANES, LANE_CHUNK), 1)
    for c in range(D_LRU // LANE_CHUNK):
        cols = slice(c * LANE_CHUNK, (c + 1) * LANE_CHUNK)
        ce = conv_ext[:, :, cols].reshape(ns * (CONV_HALO + l), LANE_CHUNK)
        conv = conv_b[:, cols] + conv_w[CONV_WIDTH - 1:CONV_WIDTH, cols] * ce
        for k in range(1, CONV_WIDTH):
            conv = conv + conv_w[CONV_WIDTH - 1 - k:CONV_WIDTH - k, cols] * pltpu.roll(ce, k, 0)
        conv = conv.reshape(ns, CONV_HALO + l, LANE_CHUNK)[:, CONV_HALO:, :].reshape(rows, LANE_CHUNK)
        gz = jnp.dot(conv.astype(BF16), gate_w[c], preferred_element_type=F32)
        gz = gz + gate_b[:, 2 * c * LANE_CHUNK:2 * (c + 1) * LANE_CHUNK]
        r = jax.nn.sigmoid(gz[:, :LANE_CHUNK])
        i = jax.nn.sigmoid(gz[:, LANE_CHUNK:])
        log_a = r * log_a_scale[:, cols]
        a = jnp.exp(log_a)
        th = jnp.tanh(log_a)
        mult = jnp.sqrt(-2.0 * th / (1.0 - th))
        if pos0 == 0:
            mult = jnp.where(pos == 0, 1.0, mult)
        u = mult * (i * conv)
        a3 = a.reshape(rows // SUBLANES, SUBLANES, LANE_CHUNK)
        u3 = u.reshape(rows // SUBLANES, SUBLANES, LANE_CHUNK)
        for sh in (1, 2, 4):
            keep = sub >= sh
            a_sh = jnp.where(keep, pltpu.roll(a3, sh, 1), 1.0)
            u_sh = jnp.where(keep, pltpu.roll(u3, sh, 1), 0.0)
            u3 = a3 * u_sh + u3
            a3 = a3 * a_sh
        a_s[:, :, cols] = a3.reshape(ns, l, LANE_CHUNK)
        u_s[:, :, cols] = u3.reshape(ns, l, LANE_CHUNK)

    hp = h_prev[...]
    for gi in range(l // SUBLANES):
        sl = slice(gi * SUBLANES, (gi + 1) * SUBLANES)
        h = a_s[:, sl, :] * hp + u_s[:, sl, :]
        u_s[:, sl, :] = h
        hp = h[:, SUBLANES - 1:SUBLANES, :]
    h_prev[...] = hp

    for c in range(D_LRU // LANE_CHUNK):
        cols = slice(c * LANE_CHUNK, (c + 1) * LANE_CHUNK)
        h = u_s[:, :, cols].reshape(rows, LANE_CHUNK)
        y_s[:, D_POOL + c * LANE_CHUNK:D_POOL + (c + 1) * LANE_CHUNK] = (
            h * jax.nn.gelu(gate_s[:, cols])).astype(BF16)

    mix = jnp.dot(y_s[...], w_out[...], preferred_element_type=F32)
    xin = alpha * x_ref[...].reshape(rows, D_MODEL) + mix
    mu = jnp.mean(xin, axis=-1, keepdims=True)
    xc = xin - mu
    var = jnp.mean(xc * xc, axis=-1, keepdims=True)
    x1 = xc * lax.rsqrt(var + LN_EPS) * ln_g[...] + ln_b[...]
    x1_ref[...] = x1

    hi = x1.astype(BF16)
    lo = (x1 - hi.astype(F32)).astype(BF16)
    lg = lax.dot_general(rt_hi[...], hi, _NT, preferred_element_type=F32)
    lg = lg + lax.dot_general(rt_lo[...], hi, _NT, preferred_element_type=F32)
    lg = lg + lax.dot_general(rt_hi[...], lo, _NT, preferred_element_type=F32)
    lg_ref[...] = lg

    pool_tail = pool_ext[:, l:l + POOL_HALO, :]
    conv_tail = conv_ext[:, l:l + CONV_HALO, :]
    pool_o[...] = pool_tail
    conv_o[...] = conv_tail
    h_o[...] = u_s[:, l - SUBLANES:l, :]
    pool_ext[:, 0:POOL_HALO, :] = pool_tail
    conv_ext[:, 0:CONV_HALO, :] = conv_tail


def _mixer_call(x, pool_in, conv_in, h_in, wts, x1_buf, lg_buf, *, ns, l, pos0, alpha, n_total, blk0):
    bsz, tlen, _ = x.shape
    nb, nt = bsz // ns, tlen // l
    rows = ns * l
    assert rows == ROWS and bsz % ns == 0 and tlen % l == 0 and l % SUBLANES == 0
    aliased = x1_buf is not None

    def row_blk(b, t):
        return (blk0 + b * nt + t, 0)

    in_specs = [
        pl.BlockSpec((ns, l, D_MODEL), lambda b, t: (b, t, 0)),
        pl.BlockSpec((ns, POOL_HALO, D_POOL), lambda b, t: (b, 0, 0)),
        pl.BlockSpec((ns, CONV_HALO, D_LRU), lambda b, t: (b, 0, 0)),
        pl.BlockSpec((ns, 1, D_LRU), lambda b, t: (b, 0, 0)),
    ] + [_const_spec(w.shape) for w in wts]
    args = [x, pool_in, conv_in, h_in, *wts]
    aliases = {}
    if aliased:
        in_specs += [pl.BlockSpec(memory_space=pl.ANY), pl.BlockSpec(memory_space=pl.ANY)]
        aliases = {len(args): 0, len(args) + 1: 1}
        args += [x1_buf, lg_buf]
    out_shape = (
        jax.ShapeDtypeStruct((n_total, D_MODEL), F32),
        jax.ShapeDtypeStruct((ROUTER_ROWS, n_total), F32),
        jax.ShapeDtypeStruct((bsz, POOL_HALO, D_POOL), F32),
        jax.ShapeDtypeStruct((bsz, CONV_HALO, D_LRU), F32),
        jax.ShapeDtypeStruct((bsz, SUBLANES, D_LRU), F32),
    )
    out_specs = (
        pl.BlockSpec((rows, D_MODEL), row_blk),
        pl.BlockSpec((ROUTER_ROWS, rows), lambda b, t: (0, blk0 + b * nt + t)),
        pl.BlockSpec((ns, POOL_HALO, D_POOL), lambda b, t: (b, 0, 0)),
        pl.BlockSpec((ns, CONV_HALO, D_LRU), lambda b, t: (b, 0, 0)),
        pl.BlockSpec((ns, SUBLANES, D_LRU), lambda b, t: (b, 0, 0)),
    )
    scratch = [
        pltpu.VMEM((ns, POOL_HALO + l, D_POOL), F32),
        pltpu.VMEM((ns, CONV_HALO + l, D_LRU), F32),
        pltpu.VMEM((ns, 1, D_LRU), F32),
        pltpu.VMEM((rows, D_LRU), F32),
        pltpu.VMEM((ns, l, D_LRU), F32),
        pltpu.VMEM((ns, l, D_LRU), F32),
        pltpu.VMEM((rows, D_MODEL), BF16),
    ]
    return pl.pallas_call(
        functools.partial(_mixer_kernel, ns=ns, l=l, pos0=pos0, alpha=alpha, aliased=aliased),
        grid=(nb, nt),
        in_specs=in_specs,
        out_specs=out_specs,
        out_shape=out_shape,
        scratch_shapes=scratch,
        input_output_aliases=aliases,
        compiler_params=pltpu.CompilerParams(
            dimension_semantics=("arbitrary", "arbitrary"), vmem_limit_bytes=VMEM_LIMIT),
        name="mixer_seq" if nt > 1 else "mixer_step",
    )(*args)


def _route_kernel(lg_ref, bias_ref, dest_ref, tw_ref, meta_ref, e_s, r_s, *, n, max_tiles):
    blk = 256
    nblk = n // blk
    neg_inf = -jnp.inf
    ridx8 = lax.broadcasted_iota(I32, (SUBLANES, blk), 0).astype(F32)
    eidx = lax.broadcasted_iota(I32, (N_EXPERTS, blk), 0).astype(F32)
    tri = (lax.broadcasted_iota(I32, (blk, blk), 0) < lax.broadcasted_iota(I32, (blk, blk), 1))
    tri = jnp.where(tri, 1.0, 0.0).astype(BF16)

    def first_idx(vals, m):
        return jnp.min(jnp.where(vals == m, ridx8, float(SUBLANES)), axis=0, keepdims=True)

    def pass1(j, base):
        ls = pl.ds(pl.multiple_of(j * blk, blk), blk)
        lg = lg_ref[0:SUBLANES, ls] + bias_ref[0:SUBLANES, :]
        lg = jnp.where(ridx8 < N_GROUPS, lg, neg_inf)
        m = jnp.max(lg, axis=0, keepdims=True)
        gi = first_idx(lg, m)
        pg_sel = 1.0 / jnp.sum(jnp.exp(lg - m), axis=0, keepdims=True)
        le = jnp.zeros((PER_GROUP, blk), F32)
        for g in range(N_GROUPS):
            rows = slice(SUBLANES + g * PER_GROUP, SUBLANES + (g + 1) * PER_GROUP)
            le = jnp.where(gi == float(g), lg_ref[rows, ls] + bias_ref[rows, :], le)
        m1 = jnp.max(le, axis=0, keepdims=True)
        i1 = first_idx(le, m1)
        le2 = jnp.where(ridx8 == i1, neg_inf, le)
        m2 = jnp.max(le2, axis=0, keepdims=True)
        i2 = first_idx(le2, m2)
        e21 = jnp.exp(m2 - m1)
        denom = 1.0 / (1.0 + e21)
        tw_ref[0:1, ls] = pg_sel * denom
        tw_ref[1:2, ls] = pg_sel * (e21 * denom)
        e1 = gi * float(PER_GROUP) + i1
        e2 = gi * float(PER_GROUP) + i2
        e_s[0:1, ls] = e1
        e_s[1:2, ls] = e2
        oh1 = jnp.where(eidx == e1, 1.0, 0.0)
        oh2 = jnp.where(eidx == e2, 1.0, 0.0)
        oh = oh1 + oh2
        before = base + jnp.dot(oh.astype(BF16), tri, preferred_element_type=F32)
        r_s[0:1, ls] = jnp.sum(oh1 * before, axis=0, keepdims=True)
        r_s[1:2, ls] = jnp.sum(oh2 * before, axis=0, keepdims=True)
        return base + jnp.sum(oh, axis=1, keepdims=True)

    cnt = lax.fori_loop(0, nblk, pass1, jnp.zeros((N_EXPERTS, blk), F32))

    ntile = jnp.floor((cnt + float(TM - 1)) * (1.0 / TM))
    lt = (lax.broadcasted_iota(I32, (N_EXPERTS, N_EXPERTS), 1) < lax.broadcasted_iota(I32, (N_EXPERTS, N_EXPERTS), 0))
    lt = jnp.where(lt, 1.0, 0.0).astype(BF16)
    tile0 = jnp.dot(lt, ntile.astype(BF16), preferred_element_type=F32)
    row0 = tile0 * float(TM)

    def pass2(j, carry):
        ls = pl.ds(pl.multiple_of(j * blk, blk), blk)
        for k in range(2):
            ohk = jnp.where(eidx == e_s[k:k + 1, ls], 1.0, 0.0)
            dest = r_s[k:k + 1, ls] + jnp.sum(ohk * row0, axis=0, keepdims=True)
            dest_ref[k:k + 1, ls] = dest.astype(I32)
        return carry

    lax.fori_loop(0, nblk, pass2, 0)

    ml = slice(0, META_LANES)
    total = jnp.sum(ntile[:, ml], axis=0, keepdims=True)
    tile_i = lax.broadcasted_iota(I32, (N_EXPERTS, META_LANES), 1).astype(F32)
    tile_c = jnp.minimum(tile_i, total - 1.0)
    tile_end = tile0[:, ml] + ntile[:, ml]
    te = jnp.sum(jnp.where(tile_end <= tile_c, 1.0, 0.0), axis=0, keepdims=True)
    meta_ref[...] = jnp.zeros((SUBLANES, META_LANES), I32)
    meta_ref[0:1, :] = te.astype(I32)
    meta_ref[1:2, :] = total.astype(I32)
    later = (eidx[:, ml] > te) & (ntile[:, ml] > 0.0)
    nxt = jnp.min(jnp.where(later, eidx[:, ml], float(N_EXPERTS)), axis=0, keepdims=True)
    meta_ref[2:3, :] = nxt.astype(I32)


def _route_call(lg, bias, n, max_tiles):
    return pl.pallas_call(
        functools.partial(_route_kernel, n=n, max_tiles=max_tiles),
        out_shape=(
            jax.ShapeDtypeStruct((2, n), I32),
            jax.ShapeDtypeStruct((2, n), F32),
            jax.ShapeDtypeStruct((SUBLANES, META_LANES), I32),
        ),
        scratch_shapes=[pltpu.VMEM((2, n), F32), pltpu.VMEM((2, n), F32)],
        compiler_params=pltpu.CompilerParams(vmem_limit_bytes=VMEM_LIMIT),
        name="route",
    )(lg, bias)


def _inverse_kernel(dest_ref, init_hbm, src_ref, sem, *, n):
    init = pltpu.make_async_copy(init_hbm, src_ref, sem)
    init.start()
    init.wait()

    def fill(t, c):
        src_ref[dest_ref[t]] = t
        src_ref[dest_ref[n + t]] = t
        return c

    lax.fori_loop(0, n, fill, 0, unroll=8)


def _inverse_call(dest_flat, n, n_slots):
    return pl.pallas_call(
        functools.partial(_inverse_kernel, n=n),
        in_specs=[pl.BlockSpec(memory_space=pltpu.SMEM), pl.BlockSpec(memory_space=pl.ANY)],
        out_specs=pl.BlockSpec(memory_space=pltpu.SMEM),
        out_shape=jax.ShapeDtypeStruct((n_slots,), I32),
        scratch_shapes=[pltpu.SemaphoreType.DMA(())],
        name="inverse",
    )(dest_flat, jnp.arange(n_slots, dtype=I32) % n)


def _row_copy(src_hbm, row, dst_vmem, slot, sem):
    return pltpu.make_async_copy(src_hbm.at[pl.ds(row, 1), :], dst_vmem.at[pl.ds(slot, 1), :], sem)


def _tile_copy(src_hbm, dst_vmem, sem):
    return pltpu.make_async_copy(src_hbm.at[pl.ds(0, dst_vmem.shape[0]), :], dst_vmem, sem)


def _moe_kernel(te_ref, ne_ref, nt_ref, src_cur, src_nxt, x1_hbm, w1_hbm, w3_hbm, w2_hbm, o_ref,
                xbuf, wf1, wf3, wf2, w1b, w3b, w2b, wslot, gsem, wsem):
    i = pl.program_id(0)
    nt = nt_ref[0]

    def gather_start(src_ref, slot):
        for r in range(TM):
            _row_copy(x1_hbm, src_ref[0, 0, r], xbuf.at[slot], r, gsem.at[slot]).start()

    def weight_copies(e, slot):
        return (pltpu.make_async_copy(w1_hbm.at[e], wf1.at[slot], wsem.at[slot, 0]),
                pltpu.make_async_copy(w3_hbm.at[e], wf3.at[slot], wsem.at[slot, 1]),
                pltpu.make_async_copy(w2_hbm.at[e], wf2.at[slot], wsem.at[slot, 2]))

    @pl.when(i == 0)
    def _():
        for cp in weight_copies(te_ref[0], 0):
            cp.start()
        wslot[0] = 0
        gather_start(src_cur, 0)

    @pl.when(i >= nt)
    def _():
        o_ref[...] = jnp.zeros_like(o_ref)

    @pl.when(i < nt)
    def _():
        slot = lax.rem(i, 2)
        _tile_copy(x1_hbm, xbuf.at[slot], gsem.at[slot]).wait()

        @pl.when((i == 0) | (te_ref[i] != te_ref[jnp.maximum(i - 1, 0)]))
        def _():
            cur = wslot[0]
            for cp in weight_copies(te_ref[i], cur):
                cp.wait()
            w1b[...] = wf1[cur].astype(BF16)
            w3b[...] = wf3[cur].astype(BF16)
            w2b[...] = wf2[cur].astype(BF16)

            @pl.when(ne_ref[i] < N_EXPERTS)
            def _():
                for cp in weight_copies(ne_ref[i], 1 - cur):
                    cp.start()

            wslot[0] = 1 - cur

        xb = xbuf[slot].astype(BF16)
        gather_start(src_nxt, 1 - slot)
        h1 = jnp.dot(xb, w1b[...], preferred_element_type=F32)
        h3 = jnp.dot(xb, w3b[...], preferred_element_type=F32)
        h = (jax.nn.silu(h1) * h3).astype(BF16)
        o_ref[...] = jnp.dot(h, w2b[...], preferred_element_type=F32)

        @pl.when(i + 1 >= nt)
        def _():
            _tile_copy(x1_hbm, xbuf.at[1 - slot], gsem.at[1 - slot]).wait()


def _moe_call(te, ne, ntiles, src3, x1, w1, w3, w2, max_tiles):
    grid_spec = pltpu.PrefetchScalarGridSpec(
        num_scalar_prefetch=3,
        grid=(max_tiles,),
        in_specs=[
            pl.BlockSpec((1, 1, TM), lambda i, te, ne, nt: (i, 0, 0), memory_space=pltpu.SMEM),
            pl.BlockSpec((1, 1, TM), lambda i, te, ne, nt: (jnp.minimum(i + 1, max_tiles - 1), 0, 0),
                         memory_space=pltpu.SMEM),
            pl.BlockSpec(memory_space=pl.ANY),
            pl.BlockSpec(memory_space=pl.ANY),
            pl.BlockSpec(memory_space=pl.ANY),
            pl.BlockSpec(memory_space=pl.ANY),
        ],
        out_specs=pl.BlockSpec((TM, D_MODEL), lambda i, te, ne, nt: (jnp.minimum(i, nt[0]), 0)),
        scratch_shapes=[
            pltpu.VMEM((2, TM, D_MODEL), F32),
            pltpu.VMEM((2, D_MODEL, D_EXPERT), F32),
            pltpu.VMEM((2, D_MODEL, D_EXPERT), F32),
            pltpu.VMEM((2, D_EXPERT, D_MODEL), F32),
            pltpu.VMEM((D_MODEL, D_EXPERT), BF16),
            pltpu.VMEM((D_MODEL, D_EXPERT), BF16),
            pltpu.VMEM((D_EXPERT, D_MODEL), BF16),
            pltpu.SMEM((1,), I32),
            pltpu.SemaphoreType.DMA((2,)),
            pltpu.SemaphoreType.DMA((2, 3)),
        ],
    )
    return pl.pallas_call(
        _moe_kernel,
        grid_spec=grid_spec,
        out_shape=jax.ShapeDtypeStruct(((max_tiles + 1) * TM, D_MODEL), F32),
        compiler_params=pltpu.CompilerParams(dimension_semantics=("arbitrary",), vmem_limit_bytes=VMEM_LIMIT),
        name="moe",
    )(te, ne, ntiles, src3, src3, x1, w1, w3, w2)


def _combine_kernel(dest_cur, dest_nxt, x1_ref, tw_ref, ys_hbm, ln_g, ln_b, o_ref, buf, sem, *, alpha, nsteps):
    i = pl.program_id(0)
    slot = lax.rem(i, 2)

    def gather_start(dest_ref, s):
        for r in range(ROWS):
            for k in range(2):
                _row_copy(ys_hbm, dest_ref[0, k, r], buf.at[s, k], r, sem.at[s, k]).start()

    def gather_wait(s):
        for k in range(2):
            _tile_copy(ys_hbm, buf.at[s, k], sem.at[s, k]).wait()

    @pl.when(i == 0)
    def _():
        gather_start(dest_cur, 0)

    gather_wait(slot)
    tw = tw_ref[...]
    moe = tw[:, 0:1] * buf[slot, 0] + tw[:, 1:2] * buf[slot, 1]
    gather_start(dest_nxt, 1 - slot)
    xin = alpha * x1_ref[...] + moe
    mu = jnp.mean(xin, axis=-1, keepdims=True)
    xc = xin - mu
    var = jnp.mean(xc * xc, axis=-1, keepdims=True)
    y = xc * lax.rsqrt(var + LN_EPS) * ln_g[...] + ln_b[...]
    o_ref[...] = y.reshape(o_ref.shape)

    @pl.when(i == nsteps - 1)
    def _():
        gather_wait(1 - slot)


def _combine_call(dest3, x1, tw, ys, ln_g, ln_b, out_shape, blk, blk0, alpha):
    bsz, tlen, _ = out_shape
    nt = tlen // blk[1]
    nsteps = (bsz // blk[0]) * nt
    return pl.pallas_call(
        functools.partial(_combine_kernel, alpha=alpha, nsteps=nsteps),
        grid=(nsteps,),
        in_specs=[
            pl.BlockSpec((1, 2, ROWS), lambda i: (blk0 + i, 0, 0), memory_space=pltpu.SMEM),
            pl.BlockSpec((1, 2, ROWS), lambda i: (blk0 + jnp.minimum(i + 1, nsteps - 1), 0, 0),
                         memory_space=pltpu.SMEM),
            pl.BlockSpec((ROWS, D_MODEL), lambda i: (blk0 + i, 0)),
            pl.BlockSpec((ROWS, 2), lambda i: (blk0 + i, 0)),
            pl.BlockSpec(memory_space=pl.ANY),
            _const_spec(ln_g.shape),
            _const_spec(ln_b.shape),
        ],
        out_specs=pl.BlockSpec(blk, lambda i: (i // nt, i % nt, 0)),
        out_shape=jax.ShapeDtypeStruct(out_shape, F32),
        scratch_shapes=[
            pltpu.VMEM((2, 2, ROWS, D_MODEL), F32),
            pltpu.SemaphoreType.DMA((2, 2)),
        ],
        compiler_params=pltpu.CompilerParams(dimension_semantics=("arbitrary",), vmem_limit_bytes=VMEM_LIMIT),
        name="combine",
    )(dest3, dest3, x1, tw, ys, ln_g, ln_b)


def _block_diag_chunks(w):
    per = LANE_CHUNK // LRU_HEAD_DIM
    nchunk = LRU_HEADS // per
    w4 = w.reshape(nchunk, per, LRU_HEAD_DIM, LRU_HEAD_DIM)
    bd = jnp.einsum("cjio,jk->cjiko", w4, jnp.eye(per, dtype=w.dtype))
    return bd.reshape(nchunk, LANE_CHUNK, LANE_CHUNK)


def _layer(yp, ys, pool_s, conv_s, h_s, lw, alpha, past_len):
    (w_in, pool_w, pool_b, pool_scale, conv_w, conv_b, rg_w, rg_b, ig_w, ig_b, lru_lambda, w_out,
     ln1_g, ln1_b, rgw, rgb, rew, reb, w1, w3, w2, ln2_g, ln2_b) = lw
    bp, tp, _ = yp.shape
    bs, ts, _ = ys.shape
    n_p, n_s = bp * tp, bs * ts
    n = n_p + n_s
    assert n % ROWS == 0 and n_p % ROWS == 0
    max_tiles = (2 * n) // TM + N_EXPERTS
    assert max_tiles <= META_LANES

    nchunk = D_LRU // LANE_CHUNK
    gate_w = jnp.concatenate([_block_diag_chunks(rg_w), _block_diag_chunks(ig_w)], axis=-1).astype(BF16)
    gate_b = jnp.concatenate(
        [rg_b.reshape(nchunk, LANE_CHUNK), ig_b.reshape(nchunk, LANE_CHUNK)], axis=-1).reshape(1, 2 * D_LRU)
    rt = jnp.concatenate(
        [rgw.T, jnp.zeros((SUBLANES - N_GROUPS, D_MODEL), F32),
         jnp.transpose(rew, (0, 2, 1)).reshape(N_EXPERTS, D_MODEL)], axis=0)
    rt_hi = rt.astype(BF16)
    rt_lo = (rt - rt_hi.astype(F32)).astype(BF16)
    r_bias = jnp.concatenate([rgb, jnp.zeros((SUBLANES - N_GROUPS,), F32), reb.reshape(N_EXPERTS)]).reshape(ROUTER_ROWS, 1)
    wts = (
        w_in.astype(BF16), pool_w.astype(BF16), pool_b.reshape(1, D_POOL), pool_scale.reshape(1, D_POOL),
        conv_w, conv_b.reshape(1, D_LRU), gate_w, gate_b, lru_lambda.reshape(1, D_LRU), w_out.astype(BF16),
        ln1_g.reshape(1, D_MODEL), ln1_b.reshape(1, D_MODEL), rt_hi, rt_lo,
    )

    x1, lg, pool_p, conv_p, h_p = _mixer_call(
        yp, jnp.zeros((bp, POOL_HALO, D_POOL), F32), jnp.zeros((bp, CONV_HALO, D_LRU), F32),
        jnp.zeros((bp, 1, D_LRU), F32), wts, None, None,
        ns=1, l=ROWS, pos0=0, alpha=alpha, n_total=n, blk0=0)
    pool_in = jnp.pad(pool_s, ((0, 0), (POOL_HALO - POOL_STATE, 0), (0, 0)))
    conv_in = jnp.pad(conv_s, ((0, 0), (CONV_HALO - (CONV_WIDTH - 1), 0), (0, 0)))
    x1, lg, pool_n, conv_n, h_n = _mixer_call(
        ys, pool_in, conv_in, h_s.reshape(bs, 1, D_LRU), wts, x1, lg,
        ns=ROWS // ts, l=ts, pos0=past_len, alpha=alpha, n_total=n, blk0=n_p // ROWS)

    dest, tw, meta = _route_call(lg, r_bias, n, max_tiles)
    src = _inverse_call(dest.reshape(2 * n), n, max_tiles * TM)
    ysort = _moe_call(meta[0], meta[2], meta[1, 0:1], src.reshape(max_tiles, 1, TM), x1, w1, w3, w2, max_tiles)

    dest3 = jnp.transpose(dest.reshape(2, n // ROWS, ROWS), (1, 0, 2))
    tw_t = tw.T
    g2, b2 = ln2_g.reshape(1, D_MODEL), ln2_b.reshape(1, D_MODEL)
    out_p = _combine_call(dest3, x1, tw_t, ysort, g2, b2, (bp, tp, D_MODEL), (1, ROWS, D_MODEL), 0, alpha)
    out_s = _combine_call(dest3, x1, tw_t, ysort, g2, b2, (bs, ts, D_MODEL), (ROWS // ts, ts, D_MODEL),
                          n_p // ROWS, alpha)
    states = (pool_p[:, 1:], conv_p[:, CONV_HALO - (CONV_WIDTH - 1):], h_p[:, SUBLANES - 1],
              pool_n[:, 1:], conv_n[:, CONV_HALO - (CONV_WIDTH - 1):], h_n[:, SUBLANES - 1])
    return out_p, out_s, states


def kernel(x_prompt, x_sample, state_pool, state_conv, state_h, w_in, pool_w, pool_b, pool_scale, conv_w, conv_b, rg_w, rg_b, ig_w, ig_b, lru_lambda, w_out, ln1_g, ln1_b, router_group_w, router_group_b, router_expert_w, router_expert_b, expert_w1, expert_w3, expert_w2, ln2_g, ln2_b):
    depth = w_in.shape[0]
    alpha = (2.0 * depth) ** 0.25
    past_len = 16384
    layer_weights = (w_in, pool_w, pool_b, pool_scale, conv_w, conv_b, rg_w, rg_b, ig_w, ig_b, lru_lambda, w_out,
                     ln1_g, ln1_b, router_group_w, router_group_b, router_expert_w, router_expert_b,
                     expert_w1, expert_w3, expert_w2, ln2_g, ln2_b)
    yp, ys = x_prompt, x_sample
    outs = [[] for _ in range(6)]
    for layer in range(depth):
        lw = tuple(w[layer] for w in layer_weights)
        yp, ys, states = _layer(yp, ys, state_pool[layer], state_conv[layer], state_h[layer], lw, alpha, past_len)
        for acc, s in zip(outs, states):
            acc.append(s)
    return (yp, ys) + tuple(jnp.stack(o, axis=0) for o in outs)
```

```python
import functools

import jax
import jax.numpy as jnp
from jax import lax
from jax.experimental import pallas as pl
from jax.experimental.pallas import tpu as pltpu

F32 = jnp.float32
BF16 = jnp.bfloat16
I32 = jnp.int32
U32 = jnp.uint32

D_MODEL = 2048
D_PACKED = D_MODEL // 2
D_POOL = 1024
D_LRU = 1024
POOL_WINDOWS = (2, 4, 8, 16)
POOL_GROUP = D_POOL // len(POOL_WINDOWS)
POOL_STATE = max(POOL_WINDOWS) - 1
CONV_WIDTH = 4
LRU_HEADS = 16
LRU_HEAD_DIM = D_LRU // LRU_HEADS
LRU_C = 8.0
N_GROUPS = 4
PER_GROUP = 8
N_EXPERTS = N_GROUPS * PER_GROUP
D_EXPERT = D_MODEL // 4
LN_EPS = 1e-5

SUBLANES = 8
LANES = 128
LANE_CHUNK = 256
POOL_HALO = 16
CONV_HALO = 8
ROWS = 256
TM = 256
ROUTER_ROWS = 8 + N_EXPERTS
META_LANES = LANES
VMEM_LIMIT = 56 * 1024 * 1024
HIGH_HALF = 0xFFFF0000

_NT = (((1,), (1,)), ((), ()))


def _const_spec(shape):
    nd = len(shape)
    return pl.BlockSpec(shape, lambda *_: (0,) * nd, pipeline_mode=pl.Buffered(1))


def _pack_halves(x):
    w = x.shape[-1] // 2
    lo = pltpu.bitcast(x[:, :w].astype(BF16).astype(F32), U32)
    hi = pltpu.bitcast(x[:, w:].astype(BF16).astype(F32), U32)
    return lax.shift_right_logical(lo, jnp.uint32(16)) | (hi & jnp.uint32(HIGH_HALF))


def _unpack_halves(words):
    lo = pltpu.bitcast(lax.shift_left(words, jnp.uint32(16)), F32)
    hi = pltpu.bitcast(words & jnp.uint32(HIGH_HALF), F32)
    return lo, hi


def _mixer_kernel(x_ref, pool_in, conv_in, h_in, w_in, pool_w, pool_b, pool_scale, conv_w, conv_b,
                  gate_w, gate_b, lam, w_out, ln_g, ln_b, rt_hi, rt_lo, *rest, ns, l, pos0, alpha, aliased):
    if aliased:
        rest = rest[3:]
    (x1_ref, x1p_ref, lg_ref, pool_o, conv_o, h_o,
     pool_ext, conv_ext, h_prev, gate_s, a_s, u_s, y_s) = rest
    t = pl.program_id(1)
    rows = ns * l

    @pl.when(t == 0)
    def _():
        pool_ext[:, 0:POOL_HALO, :] = pool_in[...]
        conv_ext[:, 0:CONV_HALO, :] = conv_in[...]
        h_prev[...] = h_in[...]

    xb = x_ref[...].reshape(rows, D_MODEL).astype(BF16)
    pool_ext[:, POOL_HALO:, :] = jnp.dot(
        xb, w_in[:, 0:D_POOL], preferred_element_type=F32).reshape(ns, l, D_POOL)
    conv_ext[:, CONV_HALO:, :] = jnp.dot(
        xb, w_in[:, D_POOL:D_POOL + D_LRU], preferred_element_type=F32).reshape(ns, l, D_LRU)
    gate_s[...] = jnp.dot(xb, w_in[:, D_POOL + D_LRU:], preferred_element_type=F32)

    pos = pos0 + t * l + lax.broadcasted_iota(I32, (ns, l, LANE_CHUNK), 1).reshape(rows, LANE_CHUNK)

    for g, w in enumerate(POOL_WINDOWS):
        cols = slice(g * POOL_GROUP, (g + 1) * POOL_GROUP)
        s = pool_ext[:, :, cols].reshape(ns * (POOL_HALO + l), POOL_GROUP)
        shift = 1
        while shift < w:
            s = s + pltpu.roll(s, shift, 0)
            shift *= 2
        win = s.reshape(ns, POOL_HALO + l, POOL_GROUP)[:, POOL_HALO:, :].reshape(rows, POOL_GROUP)
        u = pool_ext[:, POOL_HALO:, cols].reshape(rows, POOL_GROUP)
        if pos0 >= w - 1:
            inv = 1.0 / w
        else:
            inv = 1.0 / jnp.minimum(pos + 1, w).astype(F32)
        d = win * inv - u
        z = jnp.dot(d.astype(BF16), pool_w[g], preferred_element_type=F32) + pool_b[:, cols]
        y_s[:, cols] = (z * pool_scale[:, cols]).astype(BF16)

    lam_v = lam[...]
    softplus_neg = jnp.maximum(-lam_v, 0.0) + jnp.log1p(jnp.exp(-jnp.abs(lam_v)))
    log_a_scale = -LRU_C * softplus_neg
    sub = lax.broadcasted_iota(I32, (rows // SUBLANES, SUBLANES, LANE_CHUNK), 1)
    for c in range(D_LRU // LANE_CHUNK):
        cols = slice(c * LANE_CHUNK, (c + 1) * LANE_CHUNK)
        ce = conv_ext[:, :, cols].reshape(ns * (CONV_HALO + l), LANE_CHUNK)
        conv = conv_b[:, cols] + conv_w[CONV_WIDTH - 1:CONV_WIDTH, cols] * ce
        for k in range(1, CONV_WIDTH):
            conv = conv + conv_w[CONV_WIDTH - 1 - k:CONV_WIDTH - k, cols] * pltpu.roll(ce, k, 0)
        conv = conv.reshape(ns, CONV_HALO + l, LANE_CHUNK)[:, CONV_HALO:, :].reshape(rows, LANE_CHUNK)
        gz = jnp.dot(conv.astype(BF16), gate_w[c], preferred_element_type=F32)
        gz = gz + gate_b[:, 2 * c * LANE_CHUNK:2 * (c + 1) * LANE_CHUNK]
        r = jax.nn.sigmoid(gz[:, :LANE_CHUNK])
        i = jax.nn.sigmoid(gz[:, LANE_CHUNK:])
        log_a = r * log_a_scale[:, cols]
        a = jnp.exp(log_a)
        th = jnp.tanh(log_a)
        mult = jnp.sqrt(-2.0 * th / (1.0 - th))
        if pos0 == 0:
            mult = jnp.where(pos == 0, 1.0, mult)
        u = mult * (i * conv)
        a3 = a.reshape(rows // SUBLANES, SUBLANES, LANE_CHUNK)
        u3 = u.reshape(rows // SUBLANES, SUBLANES, LANE_CHUNK)
        for sh in (1, 2, 4):
            keep = sub >= sh
            a_sh = jnp.where(keep, pltpu.roll(a3, sh, 1), 1.0)
            u_sh = jnp.where(keep, pltpu.roll(u3, sh, 1), 0.0)
            u3 = a3 * u_sh + u3
            a3 = a3 * a_sh
        a_s[:, :, cols] = a3.reshape(ns, l, LANE_CHUNK)
        u_s[:, :, cols] = u3.reshape(ns, l, LANE_CHUNK)

    hp = h_prev[...]
    for gi in range(l // SUBLANES):
        sl = slice(gi * SUBLANES, (gi + 1) * SUBLANES)
        h = a_s[:, sl, :] * hp + u_s[:, sl, :]
        u_s[:, sl, :] = h
        hp = h[:, SUBLANES - 1:SUBLANES, :]
    h_prev[...] = hp

    for c in range(D_LRU // LANE_CHUNK):
        cols = slice(c * LANE_CHUNK, (c + 1) * LANE_CHUNK)
        h = u_s[:, :, cols].reshape(rows, LANE_CHUNK)
        y_s[:, D_POOL + c * LANE_CHUNK:D_POOL + (c + 1) * LANE_CHUNK] = (
            h * jax.nn.gelu(gate_s[:, cols])).astype(BF16)

    mix = jnp.dot(y_s[...], w_out[...], preferred_element_type=F32)
    xin = alpha * x_ref[...].reshape(rows, D_MODEL) + mix
    mu = jnp.mean(xin, axis=-1, keepdims=True)
    xc = xin - mu
    var = jnp.mean(xc * xc, axis=-1, keepdims=True)
    x1 = xc * lax.rsqrt(var + LN_EPS) * ln_g[...] + ln_b[...]
    x1_ref[...] = x1
    x1p_ref[...] = _pack_halves(x1)

    hi = x1.astype(BF16)
    lo = (x1 - hi.astype(F32)).astype(BF16)
    lg = lax.dot_general(rt_hi[...], hi, _NT, preferred_element_type=F32)
    lg = lg + lax.dot_general(rt_lo[...], hi, _NT, preferred_element_type=F32)
    lg = lg + lax.dot_general(rt_hi[...], lo, _NT, preferred_element_type=F32)
    lg_ref[...] = lg

    pool_tail = pool_ext[:, l:l + POOL_HALO, :]
    conv_tail = conv_ext[:, l:l + CONV_HALO, :]
    pool_o[...] = pool_tail
    conv_o[...] = conv_tail
    h_o[...] = u_s[:, l - SUBLANES:l, :]
    pool_ext[:, 0:POOL_HALO, :] = pool_tail
    conv_ext[:, 0:CONV_HALO, :] = conv_tail


def _mixer_call(x, pool_in, conv_in, h_in, wts, shared, *, ns, l, pos0, alpha, n_total, blk0):
    bsz, tlen, _ = x.shape
    nb, nt = bsz // ns, tlen // l
    rows = ns * l
    assert rows == ROWS and bsz % ns == 0 and tlen % l == 0 and l % SUBLANES == 0
    aliased = shared is not None

    def row_blk(b, t):
        return (blk0 + b * nt + t, 0)

    in_specs = [
        pl.BlockSpec((ns, l, D_MODEL), lambda b, t: (b, t, 0)),
        pl.BlockSpec((ns, POOL_HALO, D_POOL), lambda b, t: (b, 0, 0)),
        pl.BlockSpec((ns, CONV_HALO, D_LRU), lambda b, t: (b, 0, 0)),
        pl.BlockSpec((ns, 1, D_LRU), lambda b, t: (b, 0, 0)),
    ] + [_const_spec(w.shape) for w in wts]
    args = [x, pool_in, conv_in, h_in, *wts]
    aliases = {}
    if aliased:
        in_specs += [pl.BlockSpec(memory_space=pl.ANY)] * 3
        aliases = {len(args) + j: j for j in range(3)}
        args += list(shared)
    out_shape = (
        jax.ShapeDtypeStruct((n_total, D_MODEL), F32),
        jax.ShapeDtypeStruct((n_total, D_PACKED), U32),
        jax.ShapeDtypeStruct((ROUTER_ROWS, n_total), F32),
        jax.ShapeDtypeStruct((bsz, POOL_HALO, D_POOL), F32),
        jax.ShapeDtypeStruct((bsz, CONV_HALO, D_LRU), F32),
        jax.ShapeDtypeStruct((bsz, SUBLANES, D_LRU), F32),
    )
    out_specs = (
        pl.BlockSpec((rows, D_MODEL), row_blk),
        pl.BlockSpec((rows, D_PACKED), row_blk),
        pl.BlockSpec((ROUTER_ROWS, rows), lambda b, t: (0, blk0 + b * nt + t)),
        pl.BlockSpec((ns, POOL_HALO, D_POOL), lambda b, t: (b, 0, 0)),
        pl.BlockSpec((ns, CONV_HALO, D_LRU), lambda b, t: (b, 0, 0)),
        pl.BlockSpec((ns, SUBLANES, D_LRU), lambda b, t: (b, 0, 0)),
    )
    scratch = [
        pltpu.VMEM((ns, POOL_HALO + l, D_POOL), F32),
        pltpu.VMEM((ns, CONV_HALO + l, D_LRU), F32),
        pltpu.VMEM((ns, 1, D_LRU), F32),
        pltpu.VMEM((rows, D_LRU), F32),
        pltpu.VMEM((ns, l, D_LRU), F32),
        pltpu.VMEM((ns, l, D_LRU), F32),
        pltpu.VMEM((rows, D_MODEL), BF16),
    ]
    return pl.pallas_call(
        functools.partial(_mixer_kernel, ns=ns, l=l, pos0=pos0, alpha=alpha, aliased=aliased),
        grid=(nb, nt),
        in_specs=in_specs,
        out_specs=out_specs,
        out_shape=out_shape,
        scratch_shapes=scratch,
        input_output_aliases=aliases,
        compiler_params=pltpu.CompilerParams(
            dimension_semantics=("arbitrary", "arbitrary"), vmem_limit_bytes=VMEM_LIMIT),
        name="mixer_seq" if nt > 1 else "mixer_step",
    )(*args)


def _route_kernel(lg_ref, bias_ref, dest_ref, tw_ref, meta_ref, e_s, r_s, *, n):
    blk = 256
    nblk = n // blk
    neg_inf = -jnp.inf
    ridx8 = lax.broadcasted_iota(I32, (SUBLANES, blk), 0).astype(F32)
    eidx = lax.broadcasted_iota(I32, (N_EXPERTS, blk), 0).astype(F32)
    tri = (lax.broadcasted_iota(I32, (blk, blk), 0) < lax.broadcasted_iota(I32, (blk, blk), 1))
    tri = jnp.where(tri, 1.0, 0.0).astype(BF16)

    def first_idx(vals, m):
        return jnp.min(jnp.where(vals == m, ridx8, float(SUBLANES)), axis=0, keepdims=True)

    def pass1(j, base):
        ls = pl.ds(pl.multiple_of(j * blk, blk), blk)
        lg = lg_ref[0:SUBLANES, ls] + bias_ref[0:SUBLANES, :]
        lg = jnp.where(ridx8 < N_GROUPS, lg, neg_inf)
        m = jnp.max(lg, axis=0, keepdims=True)
        gi = first_idx(lg, m)
        pg_sel = 1.0 / jnp.sum(jnp.exp(lg - m), axis=0, keepdims=True)
        le = jnp.zeros((PER_GROUP, blk), F32)
        for g in range(N_GROUPS):
            rows = slice(SUBLANES + g * PER_GROUP, SUBLANES + (g + 1) * PER_GROUP)
            le = jnp.where(gi == float(g), lg_ref[rows, ls] + bias_ref[rows, :], le)
        m1 = jnp.max(le, axis=0, keepdims=True)
        i1 = first_idx(le, m1)
        le2 = jnp.where(ridx8 == i1, neg_inf, le)
        m2 = jnp.max(le2, axis=0, keepdims=True)
        i2 = first_idx(le2, m2)
        e21 = jnp.exp(m2 - m1)
        denom = 1.0 / (1.0 + e21)
        tw_ref[0:1, ls] = pg_sel * denom
        tw_ref[1:2, ls] = pg_sel * (e21 * denom)
        e1 = gi * float(PER_GROUP) + i1
        e2 = gi * float(PER_GROUP) + i2
        e_s[0:1, ls] = e1
        e_s[1:2, ls] = e2
        oh1 = jnp.where(eidx == e1, 1.0, 0.0)
        oh2 = jnp.where(eidx == e2, 1.0, 0.0)
        oh = oh1 + oh2
        before = base + jnp.dot(oh.astype(BF16), tri, preferred_element_type=F32)
        r_s[0:1, ls] = jnp.sum(oh1 * before, axis=0, keepdims=True)
        r_s[1:2, ls] = jnp.sum(oh2 * before, axis=0, keepdims=True)
        return base + jnp.sum(oh, axis=1, keepdims=True)

    cnt = lax.fori_loop(0, nblk, pass1, jnp.zeros((N_EXPERTS, blk), F32))

    ntile = jnp.floor((cnt + float(TM - 1)) * (1.0 / TM))
    lt = (lax.broadcasted_iota(I32, (N_EXPERTS, N_EXPERTS), 1) < lax.broadcasted_iota(I32, (N_EXPERTS, N_EXPERTS), 0))
    lt = jnp.where(lt, 1.0, 0.0).astype(BF16)
    tile0 = jnp.dot(lt, ntile.astype(BF16), preferred_element_type=F32)
    row0 = tile0 * float(TM)

    def pass2(j, carry):
        ls = pl.ds(pl.multiple_of(j * blk, blk), blk)
        for k in range(2):
            ohk = jnp.where(eidx == e_s[k:k + 1, ls], 1.0, 0.0)
            dest = r_s[k:k + 1, ls] + jnp.sum(ohk * row0, axis=0, keepdims=True)
            dest_ref[k:k + 1, ls] = dest.astype(I32)
        return carry

    lax.fori_loop(0, nblk, pass2, 0)

    ml = slice(0, META_LANES)
    total = jnp.sum(ntile[:, ml], axis=0, keepdims=True)
    tile_i = lax.broadcasted_iota(I32, (N_EXPERTS, META_LANES), 1).astype(F32)
    tile_c = jnp.minimum(tile_i, total - 1.0)
    tile_end = tile0[:, ml] + ntile[:, ml]
    te = jnp.sum(jnp.where(tile_end <= tile_c, 1.0, 0.0), axis=0, keepdims=True)
    meta_ref[...] = jnp.zeros((SUBLANES, META_LANES), I32)
    meta_ref[0:1, :] = te.astype(I32)
    meta_ref[1:2, :] = total.astype(I32)
    later = (eidx[:, ml] > te) & (ntile[:, ml] > 0.0)
    nxt = jnp.min(jnp.where(later, eidx[:, ml], float(N_EXPERTS)), axis=0, keepdims=True)
    meta_ref[2:3, :] = nxt.astype(I32)


def _route_call(lg, bias, n):
    return pl.pallas_call(
        functools.partial(_route_kernel, n=n),
        out_shape=(
            jax.ShapeDtypeStruct((2, n), I32),
            jax.ShapeDtypeStruct((2, n), F32),
            jax.ShapeDtypeStruct((SUBLANES, META_LANES), I32),
        ),
        scratch_shapes=[pltpu.VMEM((2, n), F32), pltpu.VMEM((2, n), F32)],
        compiler_params=pltpu.CompilerParams(vmem_limit_bytes=VMEM_LIMIT),
        name="route",
    )(lg, bias)


def _inverse_kernel(dest_ref, init_hbm, src_ref, sem, *, n):
    init = pltpu.make_async_copy(init_hbm, src_ref, sem)
    init.start()
    init.wait()

    def fill(t, c):
        src_ref[dest_ref[t]] = t
        src_ref[dest_ref[n + t]] = t
        return c

    lax.fori_loop(0, n, fill, 0, unroll=8)


def _inverse_call(dest_flat, n, n_slots):
    return pl.pallas_call(
        functools.partial(_inverse_kernel, n=n),
        in_specs=[pl.BlockSpec(memory_space=pltpu.SMEM), pl.BlockSpec(memory_space=pl.ANY)],
        out_specs=pl.BlockSpec(memory_space=pltpu.SMEM),
        out_shape=jax.ShapeDtypeStruct((n_slots,), I32),
        scratch_shapes=[pltpu.SemaphoreType.DMA(())],
        name="inverse",
    )(dest_flat, jnp.arange(n_slots, dtype=I32) % n)


def _row_copy(src_hbm, row, dst_vmem, slot, sem):
    return pltpu.make_async_copy(src_hbm.at[pl.ds(row, 1), :], dst_vmem.at[pl.ds(slot, 1), :], sem)


def _tile_copy(src_hbm, dst_vmem, sem):
    return pltpu.make_async_copy(src_hbm.at[pl.ds(0, dst_vmem.shape[0]), :], dst_vmem, sem)


def _moe_kernel(te_ref, ne_ref, nt_ref, src_cur, src_nxt, x1p_hbm, w1_hbm, w3_hbm, w2_hbm, o_ref,
                xbuf, wf1, wf3, wf2, w1b, w3b, w2b, wslot, gsem, wsem):
    i = pl.program_id(0)
    nt = nt_ref[0]

    def gather_start(src_ref, slot):
        for r in range(TM):
            _row_copy(x1p_hbm, src_ref[0, 0, r], xbuf.at[slot], r, gsem.at[slot]).start()

    def weight_copies(e, slot):
        return (pltpu.make_async_copy(w1_hbm.at[e], wf1.at[slot], wsem.at[slot, 0]),
                pltpu.make_async_copy(w3_hbm.at[e], wf3.at[slot], wsem.at[slot, 1]),
                pltpu.make_async_copy(w2_hbm.at[e], wf2.at[slot], wsem.at[slot, 2]))

    @pl.when(i == 0)
    def _():
        for cp in weight_copies(te_ref[0], 0):
            cp.start()
        wslot[0] = 0
        gather_start(src_cur, 0)

    @pl.when(i >= nt)
    def _():
        o_ref[...] = jnp.zeros_like(o_ref)

    @pl.when(i < nt)
    def _():
        slot = lax.rem(i, 2)
        _tile_copy(x1p_hbm, xbuf.at[slot], gsem.at[slot]).wait()

        @pl.when((i == 0) | (te_ref[i] != te_ref[jnp.maximum(i - 1, 0)]))
        def _():
            cur = wslot[0]
            for cp in weight_copies(te_ref[i], cur):
                cp.wait()
            w1b[...] = wf1[cur].astype(BF16)
            w3b[...] = wf3[cur].astype(BF16)
            w2b[...] = wf2[cur].astype(BF16)

            @pl.when(ne_ref[i] < N_EXPERTS)
            def _():
                for cp in weight_copies(ne_ref[i], 1 - cur):
                    cp.start()

            wslot[0] = 1 - cur

        x_lo, x_hi = _unpack_halves(xbuf[slot])
        xb = jnp.concatenate([x_lo.astype(BF16), x_hi.astype(BF16)], axis=1)
        gather_start(src_nxt, 1 - slot)
        h1 = jnp.dot(xb, w1b[...], preferred_element_type=F32)
        h3 = jnp.dot(xb, w3b[...], preferred_element_type=F32)
        h = (jax.nn.silu(h1) * h3).astype(BF16)
        o_ref[...] = _pack_halves(jnp.dot(h, w2b[...], preferred_element_type=F32))

        @pl.when(i + 1 >= nt)
        def _():
            _tile_copy(x1p_hbm, xbuf.at[1 - slot], gsem.at[1 - slot]).wait()


def _moe_call(te, ne, ntiles, src3, x1p, w1, w3, w2, max_tiles):
    grid_spec = pltpu.PrefetchScalarGridSpec(
        num_scalar_prefetch=3,
        grid=(max_tiles,),
        in_specs=[
            pl.BlockSpec((1, 1, TM), lambda i, te, ne, nt: (i, 0, 0), memory_space=pltpu.SMEM),
            pl.BlockSpec((1, 1, TM), lambda i, te, ne, nt: (jnp.minimum(i + 1, max_tiles - 1), 0, 0),
                         memory_space=pltpu.SMEM),
            pl.BlockSpec(memory_space=pl.ANY),
            pl.BlockSpec(memory_space=pl.ANY),
            pl.BlockSpec(memory_space=pl.ANY),
            pl.BlockSpec(memory_space=pl.ANY),
        ],
        out_specs=pl.BlockSpec((TM, D_PACKED), lambda i, te, ne, nt: (jnp.minimum(i, nt[0]), 0)),
        scratch_shapes=[
            pltpu.VMEM((2, TM, D_PACKED), U32),
            pltpu.VMEM((2, D_MODEL, D_EXPERT), F32),
            pltpu.VMEM((2, D_MODEL, D_EXPERT), F32),
            pltpu.VMEM((2, D_EXPERT, D_MODEL), F32),
            pltpu.VMEM((D_MODEL, D_EXPERT), BF16),
            pltpu.VMEM((D_MODEL, D_EXPERT), BF16),
            pltpu.VMEM((D_EXPERT, D_MODEL), BF16),
            pltpu.SMEM((1,), I32),
            pltpu.SemaphoreType.DMA((2,)),
            pltpu.SemaphoreType.DMA((2, 3)),
        ],
    )
    return pl.pallas_call(
        _moe_kernel,
        grid_spec=grid_spec,
        out_shape=jax.ShapeDtypeStruct(((max_tiles + 1) * TM, D_PACKED), U32),
        compiler_params=pltpu.CompilerParams(dimension_semantics=("arbitrary",), vmem_limit_bytes=VMEM_LIMIT),
        name="moe",
    )(te, ne, ntiles, src3, src3, x1p, w1, w3, w2)


def _combine_kernel(dest_cur, dest_nxt, x1_ref, tw_ref, ys_hbm, ln_g, ln_b, o_ref, buf, sem, *, alpha, nsteps):
    i = pl.program_id(0)
    slot = lax.rem(i, 2)

    def gather_start(dest_ref, s):
        for r in range(ROWS):
            for k in range(2):
                _row_copy(ys_hbm, dest_ref[0, k, r], buf.at[s, k], r, sem.at[s, k]).start()

    def gather_wait(s):
        for k in range(2):
            _tile_copy(ys_hbm, buf.at[s, k], sem.at[s, k]).wait()

    @pl.when(i == 0)
    def _():
        gather_start(dest_cur, 0)

    gather_wait(slot)
    tw = tw_ref[...]
    lo0, hi0 = _unpack_halves(buf[slot, 0])
    lo1, hi1 = _unpack_halves(buf[slot, 1])
    gather_start(dest_nxt, 1 - slot)
    moe = jnp.concatenate([tw[:, 0:1] * lo0 + tw[:, 1:2] * lo1, tw[:, 0:1] * hi0 + tw[:, 1:2] * hi1], axis=1)
    xin = alpha * x1_ref[...] + moe
    mu = jnp.mean(xin, axis=-1, keepdims=True)
    xc = xin - mu
    var = jnp.mean(xc * xc, axis=-1, keepdims=True)
    y = xc * lax.rsqrt(var + LN_EPS) * ln_g[...] + ln_b[...]
    o_ref[...] = y.reshape(o_ref.shape)

    @pl.when(i == nsteps - 1)
    def _():
        gather_wait(1 - slot)


def _combine_call(dest3, x1, tw, ys, ln_g, ln_b, out_shape, blk, blk0, alpha):
    bsz, tlen, _ = out_shape
    nt = tlen // blk[1]
    nsteps = (bsz // blk[0]) * nt
    return pl.pallas_call(
        functools.partial(_combine_kernel, alpha=alpha, nsteps=nsteps),
        grid=(nsteps,),
        in_specs=[
            pl.BlockSpec((1, 2, ROWS), lambda i: (blk0 + i, 0, 0), memory_space=pltpu.SMEM),
            pl.BlockSpec((1, 2, ROWS), lambda i: (blk0 + jnp.minimum(i + 1, nsteps - 1), 0, 0),
                         memory_space=pltpu.SMEM),
            pl.BlockSpec((ROWS, D_MODEL), lambda i: (blk0 + i, 0)),
            pl.BlockSpec((ROWS, 2), lambda i: (blk0 + i, 0)),
            pl.BlockSpec(memory_space=pl.ANY),
            _const_spec(ln_g.shape),
            _const_spec(ln_b.shape),
        ],
        out_specs=pl.BlockSpec(blk, lambda i: (i // nt, i % nt, 0)),
        out_shape=jax.ShapeDtypeStruct(out_shape, F32),
        scratch_shapes=[
            pltpu.VMEM((2, 2, ROWS, D_PACKED), U32),
            pltpu.SemaphoreType.DMA((2, 2)),
        ],
        compiler_params=pltpu.CompilerParams(dimension_semantics=("arbitrary",), vmem_limit_bytes=VMEM_LIMIT),
        name="combine",
    )(dest3, dest3, x1, tw, ys, ln_g, ln_b)


def _block_diag_chunks(w):
    per = LANE_CHUNK // LRU_HEAD_DIM
    nchunk = LRU_HEADS // per
    w4 = w.reshape(nchunk, per, LRU_HEAD_DIM, LRU_HEAD_DIM)
    bd = jnp.einsum("cjio,jk->cjiko", w4, jnp.eye(per, dtype=w.dtype))
    return bd.reshape(nchunk, LANE_CHUNK, LANE_CHUNK)


def _layer(yp, ys, pool_s, conv_s, h_s, lw, alpha, past_len):
    (w_in, pool_w, pool_b, pool_scale, conv_w, conv_b, rg_w, rg_b, ig_w, ig_b, lru_lambda, w_out,
     ln1_g, ln1_b, rgw, rgb, rew, reb, w1, w3, w2, ln2_g, ln2_b) = lw
    bp, tp, _ = yp.shape
    bs, ts, _ = ys.shape
    n_p, n_s = bp * tp, bs * ts
    n = n_p + n_s
    assert n % ROWS == 0 and n_p % ROWS == 0
    max_tiles = (2 * n) // TM + N_EXPERTS
    assert max_tiles <= META_LANES

    nchunk = D_LRU // LANE_CHUNK
    gate_w = jnp.concatenate([_block_diag_chunks(rg_w), _block_diag_chunks(ig_w)], axis=-1).astype(BF16)
    gate_b = jnp.concatenate(
        [rg_b.reshape(nchunk, LANE_CHUNK), ig_b.reshape(nchunk, LANE_CHUNK)], axis=-1).reshape(1, 2 * D_LRU)
    rt = jnp.concatenate(
        [rgw.T, jnp.zeros((SUBLANES - N_GROUPS, D_MODEL), F32),
         jnp.transpose(rew, (0, 2, 1)).reshape(N_EXPERTS, D_MODEL)], axis=0)
    rt_hi = rt.astype(BF16)
    rt_lo = (rt - rt_hi.astype(F32)).astype(BF16)
    r_bias = jnp.concatenate([rgb, jnp.zeros((SUBLANES - N_GROUPS,), F32), reb.reshape(N_EXPERTS)]).reshape(ROUTER_ROWS, 1)
    wts = (
        w_in.astype(BF16), pool_w.astype(BF16), pool_b.reshape(1, D_POOL), pool_scale.reshape(1, D_POOL),
        conv_w, conv_b.reshape(1, D_LRU), gate_w, gate_b, lru_lambda.reshape(1, D_LRU), w_out.astype(BF16),
        ln1_g.reshape(1, D_MODEL), ln1_b.reshape(1, D_MODEL), rt_hi, rt_lo,
    )

    x1, x1p, lg, pool_p, conv_p, h_p = _mixer_call(
        yp, jnp.zeros((bp, POOL_HALO, D_POOL), F32), jnp.zeros((bp, CONV_HALO, D_LRU), F32),
        jnp.zeros((bp, 1, D_LRU), F32), wts, None,
        ns=1, l=ROWS, pos0=0, alpha=alpha, n_total=n, blk0=0)
    pool_in = jnp.pad(pool_s, ((0, 0), (POOL_HALO - POOL_STATE, 0), (0, 0)))
    conv_in = jnp.pad(conv_s, ((0, 0), (CONV_HALO - (CONV_WIDTH - 1), 0), (0, 0)))
    x1, x1p, lg, pool_n, conv_n, h_n = _mixer_call(
        ys, pool_in, conv_in, h_s.reshape(bs, 1, D_LRU), wts, (x1, x1p, lg),
        ns=ROWS // ts, l=ts, pos0=past_len, alpha=alpha, n_total=n, blk0=n_p // ROWS)

    dest, tw, meta = _route_call(lg, r_bias, n)
    src = _inverse_call(dest.reshape(2 * n), n, max_tiles * TM)
    ysort = _moe_call(meta[0], meta[2], meta[1, 0:1], src.reshape(max_tiles, 1, TM), x1p, w1, w3, w2, max_tiles)

    dest3 = jnp.transpose(dest.reshape(2, n // ROWS, ROWS), (1, 0, 2))
    tw_t = tw.T
    g2, b2 = ln2_g.reshape(1, D_MODEL), ln2_b.reshape(1, D_MODEL)
    out_p = _combine_call(dest3, x1, tw_t, ysort, g2, b2, (bp, tp, D_MODEL), (1, ROWS, D_MODEL), 0, alpha)
    out_s = _combine_call(dest3, x1, tw_t, ysort, g2, b2, (bs, ts, D_MODEL), (ROWS // ts, ts, D_MODEL),
                          n_p // ROWS, alpha)
    states = (pool_p[:, 1:], conv_p[:, CONV_HALO - (CONV_WIDTH - 1):], h_p[:, SUBLANES - 1],
              pool_n[:, 1:], conv_n[:, CONV_HALO - (CONV_WIDTH - 1):], h_n[:, SUBLANES - 1])
    return out_p, out_s, states


def kernel(x_prompt, x_sample, state_pool, state_conv, state_h, w_in, pool_w, pool_b, pool_scale, conv_w, conv_b, rg_w, rg_b, ig_w, ig_b, lru_lambda, w_out, ln1_g, ln1_b, router_group_w, router_group_b, router_expert_w, router_expert_b, expert_w1, expert_w3, expert_w2, ln2_g, ln2_b):
    depth = w_in.shape[0]
    alpha = (2.0 * depth) ** 0.25
    past_len = 16384
    layer_weights = (w_in, pool_w, pool_b, pool_scale, conv_w, conv_b, rg_w, rg_b, ig_w, ig_b, lru_lambda, w_out,
                     ln1_g, ln1_b, router_group_w, router_group_b, router_expert_w, router_expert_b,
                     expert_w1, expert_w3, expert_w2, ln2_g, ln2_b)
    yp, ys = x_prompt, x_sample
    outs = [[] for _ in range(6)]
    for layer in range(depth):
        lw = tuple(w[layer] for w in layer_weights)
        yp, ys, states = _layer(yp, ys, state_pool[layer], state_conv[layer], state_h[layer], lw, alpha, past_len)
        for acc, s in zip(outs, states):
            acc.append(s)
    return (yp, ys) + tuple(jnp.stack(o, axis=0) for o in outs)
```

```python
import functools

import jax
import jax.numpy as jnp
from jax import lax
from jax.experimental import pallas as pl
from jax.experimental.pallas import tpu as pltpu

F32 = jnp.float32
BF16 = jnp.bfloat16
I32 = jnp.int32

D_MODEL = 2048
D_POOL = 1024
D_LRU = 1024
POOL_WINDOWS = (2, 4, 8, 16)
POOL_GROUP = D_POOL // len(POOL_WINDOWS)
POOL_STATE = max(POOL_WINDOWS) - 1
CONV_WIDTH = 4
LRU_HEADS = 16
LRU_HEAD_DIM = D_LRU // LRU_HEADS
LRU_C = 8.0
N_GROUPS = 4
PER_GROUP = 8
N_EXPERTS = N_GROUPS * PER_GROUP
D_EXPERT = D_MODEL // 4
LN_EPS = 1e-5

SUBLANES = 8
LANES = 128
LANE_CHUNK = 256
POOL_HALO = 16
CONV_HALO = 8
ROWS = 256
TM = 256
ROUTER_ROWS = 8 + N_EXPERTS
META_LANES = LANES
VMEM_LIMIT = 56 * 1024 * 1024
BULK_DMA_PRIORITY = 1

_NT = (((1,), (1,)), ((), ()))


def _const_spec(shape):
    nd = len(shape)
    return pl.BlockSpec(shape, lambda *_: (0,) * nd, pipeline_mode=pl.Buffered(1))


def _mixer_kernel(x_ref, pool_in, conv_in, h_in, w_in, pool_w, pool_b, pool_scale, conv_w, conv_b,
                  gate_w, gate_b, lam, w_out, ln_g, ln_b, rt_hi, rt_lo, *rest, ns, l, pos0, alpha, aliased):
    if aliased:
        rest = rest[2:]
    (x1_ref, lg_ref, pool_o, conv_o, h_o,
     pool_ext, conv_ext, h_prev, gate_s, a_s, u_s, y_s) = rest
    t = pl.program_id(1)
    rows = ns * l

    @pl.when(t == 0)
    def _():
        pool_ext[:, 0:POOL_HALO, :] = pool_in[...]
        conv_ext[:, 0:CONV_HALO, :] = conv_in[...]
        h_prev[...] = h_in[...]

    xb = x_ref[...].reshape(rows, D_MODEL).astype(BF16)
    pool_ext[:, POOL_HALO:, :] = jnp.dot(
        xb, w_in[:, 0:D_POOL], preferred_element_type=F32).reshape(ns, l, D_POOL)
    conv_ext[:, CONV_HALO:, :] = jnp.dot(
        xb, w_in[:, D_POOL:D_POOL + D_LRU], preferred_element_type=F32).reshape(ns, l, D_LRU)
    gate_s[...] = jnp.dot(xb, w_in[:, D_POOL + D_LRU:], preferred_element_type=F32)

    pos = pos0 + t * l + lax.broadcasted_iota(I32, (ns, l, LANE_CHUNK), 1).reshape(rows, LANE_CHUNK)

    for g, w in enumerate(POOL_WINDOWS):
        cols = slice(g * POOL_GROUP, (g + 1) * POOL_GROUP)
        s = pool_ext[:, :, cols].reshape(ns * (POOL_HALO + l), POOL_GROUP)
        shift = 1
        while shift < w:
            s = s + pltpu.roll(s, shift, 0)
            shift *= 2
        win = s.reshape(ns, POOL_HALO + l, POOL_GROUP)[:, POOL_HALO:, :].reshape(rows, POOL_GROUP)
        u = pool_ext[:, POOL_HALO:, cols].reshape(rows, POOL_GROUP)
        if pos0 >= w - 1:
            inv = 1.0 / w
        else:
            inv = 1.0 / jnp.minimum(pos + 1, w).astype(F32)
        d = win * inv - u
        z = jnp.dot(d.astype(BF16), pool_w[g], preferred_element_type=F32) + pool_b[:, cols]
        y_s[:, cols] = (z * pool_scale[:, cols]).astype(BF16)

    lam_v = lam[...]
    softplus_neg = jnp.maximum(-lam_v, 0.0) + jnp.log1p(jnp.exp(-jnp.abs(lam_v)))
    log_a_scale = -LRU_C * softplus_neg
    sub = lax.broadcasted_iota(I32, (rows // SUBLANES, SUBLANES, LANE_CHUNK), 1)
    for c in range(D_LRU // LANE_CHUNK):
        cols = slice(c * LANE_CHUNK, (c + 1) * LANE_CHUNK)
        ce = conv_ext[:, :, cols].reshape(ns * (CONV_HALO + l), LANE_CHUNK)
        conv = conv_b[:, cols] + conv_w[CONV_WIDTH - 1:CONV_WIDTH, cols] * ce
        for k in range(1, CONV_WIDTH):
            conv = conv + conv_w[CONV_WIDTH - 1 - k:CONV_WIDTH - k, cols] * pltpu.roll(ce, k, 0)
        conv = conv.reshape(ns, CONV_HALO + l, LANE_CHUNK)[:, CONV_HALO:, :].reshape(rows, LANE_CHUNK)
        gz = jnp.dot(conv.astype(BF16), gate_w[c], preferred_element_type=F32)
        gz = gz + gate_b[:, 2 * c * LANE_CHUNK:2 * (c + 1) * LANE_CHUNK]
        r = jax.nn.sigmoid(gz[:, :LANE_CHUNK])
        i = jax.nn.sigmoid(gz[:, LANE_CHUNK:])
        log_a = r * log_a_scale[:, cols]
        a = jnp.exp(log_a)
        th = jnp.tanh(log_a)
        mult = jnp.sqrt(-2.0 * th / (1.0 - th))
        if pos0 == 0:
            mult = jnp.where(pos == 0, 1.0, mult)
        u = mult * (i * conv)
        a3 = a.reshape(rows // SUBLANES, SUBLANES, LANE_CHUNK)
        u3 = u.reshape(rows // SUBLANES, SUBLANES, LANE_CHUNK)
        for sh in (1, 2, 4):
            keep = sub >= sh
            a_sh = jnp.where(keep, pltpu.roll(a3, sh, 1), 1.0)
            u_sh = jnp.where(keep, pltpu.roll(u3, sh, 1), 0.0)
            u3 = a3 * u_sh + u3
            a3 = a3 * a_sh
        a_s[:, :, cols] = a3.reshape(ns, l, LANE_CHUNK)
        u_s[:, :, cols] = u3.reshape(ns, l, LANE_CHUNK)

    hp = h_prev[...]
    for gi in range(l // SUBLANES):
        sl = slice(gi * SUBLANES, (gi + 1) * SUBLANES)
        h = a_s[:, sl, :] * hp + u_s[:, sl, :]
        u_s[:, sl, :] = h
        hp = h[:, SUBLANES - 1:SUBLANES, :]
    h_prev[...] = hp

    for c in range(D_LRU // LANE_CHUNK):
        cols = slice(c * LANE_CHUNK, (c + 1) * LANE_CHUNK)
        h = u_s[:, :, cols].reshape(rows, LANE_CHUNK)
        y_s[:, D_POOL + c * LANE_CHUNK:D_POOL + (c + 1) * LANE_CHUNK] = (
            h * jax.nn.gelu(gate_s[:, cols])).astype(BF16)

    mix = jnp.dot(y_s[...], w_out[...], preferred_element_type=F32)
    xin = alpha * x_ref[...].reshape(rows, D_MODEL) + mix
    mu = jnp.mean(xin, axis=-1, keepdims=True)
    xc = xin - mu
    var = jnp.mean(xc * xc, axis=-1, keepdims=True)
    x1 = xc * lax.rsqrt(var + LN_EPS) * ln_g[...] + ln_b[...]
    x1_ref[...] = x1

    hi = x1.astype(BF16)
    lo = (x1 - hi.astype(F32)).astype(BF16)
    lg = lax.dot_general(rt_hi[...], hi, _NT, preferred_element_type=F32)
    lg = lg + lax.dot_general(rt_lo[...], hi, _NT, preferred_element_type=F32)
    lg = lg + lax.dot_general(rt_hi[...], lo, _NT, preferred_element_type=F32)
    lg_ref[...] = lg

    pool_tail = pool_ext[:, l:l + POOL_HALO, :]
    conv_tail = conv_ext[:, l:l + CONV_HALO, :]
    pool_o[...] = pool_tail
    conv_o[...] = conv_tail
    h_o[...] = u_s[:, l - SUBLANES:l, :]
    pool_ext[:, 0:POOL_HALO, :] = pool_tail
    conv_ext[:, 0:CONV_HALO, :] = conv_tail


def _mixer_call(x, pool_in, conv_in, h_in, wts, x1_buf, lg_buf, *, ns, l, pos0, alpha, n_total, blk0):
    bsz, tlen, _ = x.shape
    nb, nt = bsz // ns, tlen // l
    rows = ns * l
    assert rows == ROWS and bsz % ns == 0 and tlen % l == 0 and l % SUBLANES == 0
    aliased = x1_buf is not None

    def row_blk(b, t):
        return (blk0 + b * nt + t, 0)

    in_specs = [
        pl.BlockSpec((ns, l, D_MODEL), lambda b, t: (b, t, 0)),
        pl.BlockSpec((ns, POOL_HALO, D_POOL), lambda b, t: (b, 0, 0)),
        pl.BlockSpec((ns, CONV_HALO, D_LRU), lambda b, t: (b, 0, 0)),
        pl.BlockSpec((ns, 1, D_LRU), lambda b, t: (b, 0, 0)),
    ] + [_const_spec(w.shape) for w in wts]
    args = [x, pool_in, conv_in, h_in, *wts]
    aliases = {}
    if aliased:
        in_specs += [pl.BlockSpec(memory_space=pl.ANY), pl.BlockSpec(memory_space=pl.ANY)]
        aliases = {len(args): 0, len(args) + 1: 1}
        args += [x1_buf, lg_buf]
    out_shape = (
        jax.ShapeDtypeStruct((n_total, D_MODEL), F32),
        jax.ShapeDtypeStruct((ROUTER_ROWS, n_total), F32),
        jax.ShapeDtypeStruct((bsz, POOL_HALO, D_POOL), F32),
        jax.ShapeDtypeStruct((bsz, CONV_HALO, D_LRU), F32),
        jax.ShapeDtypeStruct((bsz, SUBLANES, D_LRU), F32),
    )
    out_specs = (
        pl.BlockSpec((rows, D_MODEL), row_blk),
        pl.BlockSpec((ROUTER_ROWS, rows), lambda b, t: (0, blk0 + b * nt + t)),
        pl.BlockSpec((ns, POOL_HALO, D_POOL), lambda b, t: (b, 0, 0)),
        pl.BlockSpec((ns, CONV_HALO, D_LRU), lambda b, t: (b, 0, 0)),
        pl.BlockSpec((ns, SUBLANES, D_LRU), lambda b, t: (b, 0, 0)),
    )
    scratch = [
        pltpu.VMEM((ns, POOL_HALO + l, D_POOL), F32),
        pltpu.VMEM((ns, CONV_HALO + l, D_LRU), F32),
        pltpu.VMEM((ns, 1, D_LRU), F32),
        pltpu.VMEM((rows, D_LRU), F32),
        pltpu.VMEM((ns, l, D_LRU), F32),
        pltpu.VMEM((ns, l, D_LRU), F32),
        pltpu.VMEM((rows, D_MODEL), BF16),
    ]
    return pl.pallas_call(
        functools.partial(_mixer_kernel, ns=ns, l=l, pos0=pos0, alpha=alpha, aliased=aliased),
        grid=(nb, nt),
        in_specs=in_specs,
        out_specs=out_specs,
        out_shape=out_shape,
        scratch_shapes=scratch,
        input_output_aliases=aliases,
        compiler_params=pltpu.CompilerParams(
            dimension_semantics=("arbitrary", "arbitrary"), vmem_limit_bytes=VMEM_LIMIT),
        name="mixer_seq" if nt > 1 else "mixer_step",
    )(*args)


def _route_kernel(lg_ref, bias_ref, dest_ref, tw_ref, meta_ref, e_s, r_s, *, n):
    blk = 256
    nblk = n // blk
    neg_inf = -jnp.inf
    ridx8 = lax.broadcasted_iota(I32, (SUBLANES, blk), 0).astype(F32)
    eidx = lax.broadcasted_iota(I32, (N_EXPERTS, blk), 0).astype(F32)
    tri = (lax.broadcasted_iota(I32, (blk, blk), 0) < lax.broadcasted_iota(I32, (blk, blk), 1))
    tri = jnp.where(tri, 1.0, 0.0).astype(BF16)

    def first_idx(vals, m):
        return jnp.min(jnp.where(vals == m, ridx8, float(SUBLANES)), axis=0, keepdims=True)

    def pass1(j, base):
        ls = pl.ds(pl.multiple_of(j * blk, blk), blk)
        lg = lg_ref[0:SUBLANES, ls] + bias_ref[0:SUBLANES, :]
        lg = jnp.where(ridx8 < N_GROUPS, lg, neg_inf)
        m = jnp.max(lg, axis=0, keepdims=True)
        gi = first_idx(lg, m)
        pg_sel = 1.0 / jnp.sum(jnp.exp(lg - m), axis=0, keepdims=True)
        le = jnp.zeros((PER_GROUP, blk), F32)
        for g in range(N_GROUPS):
            rows = slice(SUBLANES + g * PER_GROUP, SUBLANES + (g + 1) * PER_GROUP)
            le = jnp.where(gi == float(g), lg_ref[rows, ls] + bias_ref[rows, :], le)
        m1 = jnp.max(le, axis=0, keepdims=True)
        i1 = first_idx(le, m1)
        le2 = jnp.where(ridx8 == i1, neg_inf, le)
        m2 = jnp.max(le2, axis=0, keepdims=True)
        i2 = first_idx(le2, m2)
        e21 = jnp.exp(m2 - m1)
        denom = 1.0 / (1.0 + e21)
        tw_ref[0:1, ls] = pg_sel * denom
        tw_ref[1:2, ls] = pg_sel * (e21 * denom)
        e1 = gi * float(PER_GROUP) + i1
        e2 = gi * float(PER_GROUP) + i2
        e_s[0:1, ls] = e1
        e_s[1:2, ls] = e2
        oh1 = jnp.where(eidx == e1, 1.0, 0.0)
        oh2 = jnp.where(eidx == e2, 1.0, 0.0)
        oh = oh1 + oh2
        before = base + jnp.dot(oh.astype(BF16), tri, preferred_element_type=F32)
        r_s[0:1, ls] = jnp.sum(oh1 * before, axis=0, keepdims=True)
        r_s[1:2, ls] = jnp.sum(oh2 * before, axis=0, keepdims=True)
        return base + jnp.sum(oh, axis=1, keepdims=True)

    cnt = lax.fori_loop(0, nblk, pass1, jnp.zeros((N_EXPERTS, blk), F32))

    ntile = jnp.floor((cnt + float(TM - 1)) * (1.0 / TM))
    lt = (lax.broadcasted_iota(I32, (N_EXPERTS, N_EXPERTS), 1) < lax.broadcasted_iota(I32, (N_EXPERTS, N_EXPERTS), 0))
    lt = jnp.where(lt, 1.0, 0.0).astype(BF16)
    tile0 = jnp.dot(lt, ntile.astype(BF16), preferred_element_type=F32)
    row0 = tile0 * float(TM)

    def pass2(j, carry):
        ls = pl.ds(pl.multiple_of(j * blk, blk), blk)
        for k in range(2):
            ohk = jnp.where(eidx == e_s[k:k + 1, ls], 1.0, 0.0)
            dest = r_s[k:k + 1, ls] + jnp.sum(ohk * row0, axis=0, keepdims=True)
            dest_ref[k:k + 1, ls] = dest.astype(I32)
        return carry

    lax.fori_loop(0, nblk, pass2, 0)

    ml = slice(0, META_LANES)
    total = jnp.sum(ntile[:, ml], axis=0, keepdims=True)
    tile_i = lax.broadcasted_iota(I32, (N_EXPERTS, META_LANES), 1).astype(F32)
    tile_c = jnp.minimum(tile_i, total - 1.0)
    tile_end = tile0[:, ml] + ntile[:, ml]
    te = jnp.sum(jnp.where(tile_end <= tile_c, 1.0, 0.0), axis=0, keepdims=True)
    meta_ref[...] = jnp.zeros((SUBLANES, META_LANES), I32)
    meta_ref[0:1, :] = te.astype(I32)
    meta_ref[1:2, :] = total.astype(I32)
    later = (eidx[:, ml] > te) & (ntile[:, ml] > 0.0)
    nxt = jnp.min(jnp.where(later, eidx[:, ml], float(N_EXPERTS)), axis=0, keepdims=True)
    meta_ref[2:3, :] = nxt.astype(I32)


def _route_call(lg, bias, n):
    return pl.pallas_call(
        functools.partial(_route_kernel, n=n),
        out_shape=(
            jax.ShapeDtypeStruct((2, n), I32),
            jax.ShapeDtypeStruct((2, n), F32),
            jax.ShapeDtypeStruct((SUBLANES, META_LANES), I32),
        ),
        scratch_shapes=[pltpu.VMEM((2, n), F32), pltpu.VMEM((2, n), F32)],
        compiler_params=pltpu.CompilerParams(vmem_limit_bytes=VMEM_LIMIT),
        name="route",
    )(lg, bias)


def _inverse_kernel(dest_ref, init_hbm, src_ref, sem, *, n):
    init = pltpu.make_async_copy(init_hbm, src_ref, sem)
    init.start()
    init.wait()

    def fill(t, c):
        src_ref[dest_ref[t]] = t
        src_ref[dest_ref[n + t]] = t
        return c

    lax.fori_loop(0, n, fill, 0, unroll=8)


def _inverse_call(dest_flat, n, n_slots):
    return pl.pallas_call(
        functools.partial(_inverse_kernel, n=n),
        in_specs=[pl.BlockSpec(memory_space=pltpu.SMEM), pl.BlockSpec(memory_space=pl.ANY)],
        out_specs=pl.BlockSpec(memory_space=pltpu.SMEM),
        out_shape=jax.ShapeDtypeStruct((n_slots,), I32),
        scratch_shapes=[pltpu.SemaphoreType.DMA(())],
        name="inverse",
    )(dest_flat, jnp.arange(n_slots, dtype=I32) % n)


def _row_copy(src_hbm, row, dst_vmem, slot, sem):
    return pltpu.make_async_copy(src_hbm.at[pl.ds(row, 1), :], dst_vmem.at[pl.ds(slot, 1), :], sem)


def _tile_copy(src_hbm, dst_vmem, sem):
    return pltpu.make_async_copy(src_hbm.at[pl.ds(0, dst_vmem.shape[0]), :], dst_vmem, sem)


def _moe_kernel(te_ref, ne_ref, nt_ref, src_cur, src_nxt, x1_hbm, w1_hbm, w3_hbm, w2_hbm, o_ref,
                xbuf, wf1, wf3, wf2, w1b, w3b, w2b, wslot, gsem, wsem):
    i = pl.program_id(0)
    nt = nt_ref[0]

    def gather_start(src_ref, slot):
        for r in range(TM):
            _row_copy(x1_hbm, src_ref[0, 0, r], xbuf.at[slot], r, gsem.at[slot]).start()

    def weight_copies(e, slot):
        return (pltpu.make_async_copy(w1_hbm.at[e], wf1.at[slot], wsem.at[slot, 0]),
                pltpu.make_async_copy(w3_hbm.at[e], wf3.at[slot], wsem.at[slot, 1]),
                pltpu.make_async_copy(w2_hbm.at[e], wf2.at[slot], wsem.at[slot, 2]))

    @pl.when(i == 0)
    def _():
        for cp in weight_copies(te_ref[0], 0):
            cp.start(priority=BULK_DMA_PRIORITY)
        wslot[0] = 0
        gather_start(src_cur, 0)

    @pl.when(i >= nt)
    def _():
        o_ref[...] = jnp.zeros_like(o_ref)

    @pl.when(i < nt)
    def _():
        slot = lax.rem(i, 2)
        _tile_copy(x1_hbm, xbuf.at[slot], gsem.at[slot]).wait()

        @pl.when((i == 0) | (te_ref[i] != te_ref[jnp.maximum(i - 1, 0)]))
        def _():
            cur = wslot[0]
            for cp in weight_copies(te_ref[i], cur):
                cp.wait()
            w1b[...] = wf1[cur].astype(BF16)
            w3b[...] = wf3[cur].astype(BF16)
            w2b[...] = wf2[cur].astype(BF16)

            @pl.when(ne_ref[i] < N_EXPERTS)
            def _():
                for cp in weight_copies(ne_ref[i], 1 - cur):
                    cp.start(priority=BULK_DMA_PRIORITY)

            wslot[0] = 1 - cur

        xb = xbuf[slot].astype(BF16)
        gather_start(src_nxt, 1 - slot)
        h1 = jnp.dot(xb, w1b[...], preferred_element_type=F32)
        h3 = jnp.dot(xb, w3b[...], preferred_element_type=F32)
        h = (jax.nn.silu(h1) * h3).astype(BF16)
        o_ref[...] = jnp.dot(h, w2b[...], preferred_element_type=F32)

        @pl.when(i + 1 >= nt)
        def _():
            _tile_copy(x1_hbm, xbuf.at[1 - slot], gsem.at[1 - slot]).wait()


def _moe_call(te, ne, ntiles, src3, x1, w1, w3, w2, max_tiles):
    grid_spec = pltpu.PrefetchScalarGridSpec(
        num_scalar_prefetch=3,
        grid=(max_tiles,),
        in_specs=[
            pl.BlockSpec((1, 1, TM), lambda i, te, ne, nt: (i, 0, 0), memory_space=pltpu.SMEM),
            pl.BlockSpec((1, 1, TM), lambda i, te, ne, nt: (jnp.minimum(i + 1, max_tiles - 1), 0, 0),
                         memory_space=pltpu.SMEM),
            pl.BlockSpec(memory_space=pl.ANY),
            pl.BlockSpec(memory_space=pl.ANY),
            pl.BlockSpec(memory_space=pl.ANY),
            pl.BlockSpec(memory_space=pl.ANY),
        ],
        out_specs=pl.BlockSpec((TM, D_MODEL), lambda i, te, ne, nt: (jnp.minimum(i, nt[0]), 0)),
        scratch_shapes=[
            pltpu.VMEM((2, TM, D_MODEL), F32),
            pltpu.VMEM((2, D_MODEL, D_EXPERT), F32),
            pltpu.VMEM((2, D_MODEL, D_EXPERT), F32),
            pltpu.VMEM((2, D_EXPERT, D_MODEL), F32),
            pltpu.VMEM((D_MODEL, D_EXPERT), BF16),
            pltpu.VMEM((D_MODEL, D_EXPERT), BF16),
            pltpu.VMEM((D_EXPERT, D_MODEL), BF16),
            pltpu.SMEM((1,), I32),
            pltpu.SemaphoreType.DMA((2,)),
            pltpu.SemaphoreType.DMA((2, 3)),
        ],
    )
    return pl.pallas_call(
        _moe_kernel,
        grid_spec=grid_spec,
        out_shape=jax.ShapeDtypeStruct(((max_tiles + 1) * TM, D_MODEL), F32),
        compiler_params=pltpu.CompilerParams(dimension_semantics=("arbitrary",), vmem_limit_bytes=VMEM_LIMIT),
        name="moe",
    )(te, ne, ntiles, src3, src3, x1, w1, w3, w2)


def _combine_kernel(dest_cur, dest_nxt, x1_ref, tw_ref, ys_hbm, ln_g, ln_b, o_ref, buf, sem, *, alpha, nsteps):
    i = pl.program_id(0)
    slot = lax.rem(i, 2)

    def gather_start(dest_ref, s):
        for r in range(ROWS):
            for k in range(2):
                _row_copy(ys_hbm, dest_ref[0, k, r], buf.at[s, k], r, sem.at[s, k]).start()

    def gather_wait(s):
        for k in range(2):
            _tile_copy(ys_hbm, buf.at[s, k], sem.at[s, k]).wait()

    @pl.when(i == 0)
    def _():
        gather_start(dest_cur, 0)

    gather_wait(slot)
    tw = tw_ref[...]
    moe = tw[:, 0:1] * buf[slot, 0] + tw[:, 1:2] * buf[slot, 1]
    gather_start(dest_nxt, 1 - slot)
    xin = alpha * x1_ref[...] + moe
    mu = jnp.mean(xin, axis=-1, keepdims=True)
    xc = xin - mu
    var = jnp.mean(xc * xc, axis=-1, keepdims=True)
    y = xc * lax.rsqrt(var + LN_EPS) * ln_g[...] + ln_b[...]
    o_ref[...] = y.reshape(o_ref.shape)

    @pl.when(i == nsteps - 1)
    def _():
        gather_wait(1 - slot)


def _combine_call(dest3, x1, tw, ys, ln_g, ln_b, out_shape, blk, blk0, alpha):
    bsz, tlen, _ = out_shape
    nt = tlen // blk[1]
    nsteps = (bsz // blk[0]) * nt
    return pl.pallas_call(
        functools.partial(_combine_kernel, alpha=alpha, nsteps=nsteps),
        grid=(nsteps,),
        in_specs=[
            pl.BlockSpec((1, 2, ROWS), lambda i: (blk0 + i, 0, 0), memory_space=pltpu.SMEM),
            pl.BlockSpec((1, 2, ROWS), lambda i: (blk0 + jnp.minimum(i + 1, nsteps - 1), 0, 0),
                         memory_space=pltpu.SMEM),
            pl.BlockSpec((ROWS, D_MODEL), lambda i: (blk0 + i, 0)),
            pl.BlockSpec((ROWS, 2), lambda i: (blk0 + i, 0)),
            pl.BlockSpec(memory_space=pl.ANY),
            _const_spec(ln_g.shape),
            _const_spec(ln_b.shape),
        ],
        out_specs=pl.BlockSpec(blk, lambda i: (i // nt, i % nt, 0)),
        out_shape=jax.ShapeDtypeStruct(out_shape, F32),
        scratch_shapes=[
            pltpu.VMEM((2, 2, ROWS, D_MODEL), F32),
            pltpu.SemaphoreType.DMA((2, 2)),
        ],
        compiler_params=pltpu.CompilerParams(dimension_semantics=("arbitrary",), vmem_limit_bytes=VMEM_LIMIT),
        name="combine",
    )(dest3, dest3, x1, tw, ys, ln_g, ln_b)


def _block_diag_chunks(w):
    per = LANE_CHUNK // LRU_HEAD_DIM
    nchunk = LRU_HEADS // per
    w4 = w.reshape(nchunk, per, LRU_HEAD_DIM, LRU_HEAD_DIM)
    bd = jnp.einsum("cjio,jk->cjiko", w4, jnp.eye(per, dtype=w.dtype))
    return bd.reshape(nchunk, LANE_CHUNK, LANE_CHUNK)


def _layer(yp, ys, pool_s, conv_s, h_s, lw, alpha, past_len):
    (w_in, pool_w, pool_b, pool_scale, conv_w, conv_b, rg_w, rg_b, ig_w, ig_b, lru_lambda, w_out,
     ln1_g, ln1_b, rgw, rgb, rew, reb, w1, w3, w2, ln2_g, ln2_b) = lw
    bp, tp, _ = yp.shape
    bs, ts, _ = ys.shape
    n_p, n_s = bp * tp, bs * ts
    n = n_p + n_s
    assert n % ROWS == 0 and n_p % ROWS == 0
    max_tiles = (2 * n) // TM + N_EXPERTS
    assert max_tiles <= META_LANES

    nchunk = D_LRU // LANE_CHUNK
    gate_w = jnp.concatenate([_block_diag_chunks(rg_w), _block_diag_chunks(ig_w)], axis=-1).astype(BF16)
    gate_b = jnp.concatenate(
        [rg_b.reshape(nchunk, LANE_CHUNK), ig_b.reshape(nchunk, LANE_CHUNK)], axis=-1).reshape(1, 2 * D_LRU)
    rt = jnp.concatenate(
        [rgw.T, jnp.zeros((SUBLANES - N_GROUPS, D_MODEL), F32),
         jnp.transpose(rew, (0, 2, 1)).reshape(N_EXPERTS, D_MODEL)], axis=0)
    rt_hi = rt.astype(BF16)
    rt_lo = (rt - rt_hi.astype(F32)).astype(BF16)
    r_bias = jnp.concatenate([rgb, jnp.zeros((SUBLANES - N_GROUPS,), F32), reb.reshape(N_EXPERTS)]).reshape(ROUTER_ROWS, 1)
    wts = (
        w_in.astype(BF16), pool_w.astype(BF16), pool_b.reshape(1, D_POOL), pool_scale.reshape(1, D_POOL),
        conv_w, conv_b.reshape(1, D_LRU), gate_w, gate_b, lru_lambda.reshape(1, D_LRU), w_out.astype(BF16),
        ln1_g.reshape(1, D_MODEL), ln1_b.reshape(1, D_MODEL), rt_hi, rt_lo,
    )

    x1, lg, pool_p, conv_p, h_p = _mixer_call(
        yp, jnp.zeros((bp, POOL_HALO, D_POOL), F32), jnp.zeros((bp, CONV_HALO, D_LRU), F32),
        jnp.zeros((bp, 1, D_LRU), F32), wts, None, None,
        ns=1, l=ROWS, pos0=0, alpha=alpha, n_total=n, blk0=0)
    pool_in = jnp.pad(pool_s, ((0, 0), (POOL_HALO - POOL_STATE, 0), (0, 0)))
    conv_in = jnp.pad(conv_s, ((0, 0), (CONV_HALO - (CONV_WIDTH - 1), 0), (0, 0)))
    x1, lg, pool_n, conv_n, h_n = _mixer_call(
        ys, pool_in, conv_in, h_s.reshape(bs, 1, D_LRU), wts, x1, lg,
        ns=ROWS // ts, l=ts, pos0=past_len, alpha=alpha, n_total=n, blk0=n_p // ROWS)

    dest, tw, meta = _route_call(lg, r_bias, n)
    src = _inverse_call(dest.reshape(2 * n), n, max_tiles * TM)
    ysort = _moe_call(meta[0], meta[2], meta[1, 0:1], src.reshape(max_tiles, 1, TM), x1, w1, w3, w2, max_tiles)

    dest3 = jnp.transpose(dest.reshape(2, n // ROWS, ROWS), (1, 0, 2))
    tw_t = tw.T
    g2, b2 = ln2_g.reshape(1, D_MODEL), ln2_b.reshape(1, D_MODEL)
    out_p = _combine_call(dest3, x1, tw_t, ysort, g2, b2, (bp, tp, D_MODEL), (1, ROWS, D_MODEL), 0, alpha)
    out_s = _combine_call(dest3, x1, tw_t, ysort, g2, b2, (bs, ts, D_MODEL), (ROWS // ts, ts, D_MODEL),
                          n_p // ROWS, alpha)
    states = (pool_p[:, 1:], conv_p[:, CONV_HALO - (CONV_WIDTH - 1):], h_p[:, SUBLANES - 1],
              pool_n[:, 1:], conv_n[:, CONV_HALO - (CONV_WIDTH - 1):], h_n[:, SUBLANES - 1])
    return out_p, out_s, states


def kernel(x_prompt, x_sample, state_pool, state_conv, state_h, w_in, pool_w, pool_b, pool_scale, conv_w, conv_b, rg_w, rg_b, ig_w, ig_b, lru_lambda, w_out, ln1_g, ln1_b, router_group_w, router_group_b, router_expert_w, router_expert_b, expert_w1, expert_w3, expert_w2, ln2_g, ln2_b):
    depth = w_in.shape[0]
    alpha = (2.0 * depth) ** 0.25
    past_len = 16384
    layer_weights = (w_in, pool_w, pool_b, pool_scale, conv_w, conv_b, rg_w, rg_b, ig_w, ig_b, lru_lambda, w_out,
                     ln1_g, ln1_b, router_group_w, router_group_b, router_expert_w, router_expert_b,
                     expert_w1, expert_w3, expert_w2, ln2_g, ln2_b)
    yp, ys = x_prompt, x_sample
    outs = [[] for _ in range(6)]
    for layer in range(depth):
        lw = tuple(w[layer] for w in layer_weights)
        yp, ys, states = _layer(yp, ys, state_pool[layer], state_conv[layer], state_h[layer], lw, alpha, past_len)
        for acc, s in zip(outs, states):
            acc.append(s)
    return (yp, ys) + tuple(jnp.stack(o, axis=0) for o in outs)
```

```python
import functools

import jax
import jax.numpy as jnp
from jax import lax
from jax.experimental import pallas as pl
from jax.experimental.pallas import tpu as pltpu

F32 = jnp.float32
BF16 = jnp.bfloat16
I32 = jnp.int32

D_MODEL = 2048
D_POOL = 1024
D_LRU = 1024
POOL_WINDOWS = (2, 4, 8, 16)
POOL_GROUP = D_POOL // len(POOL_WINDOWS)
POOL_STATE = max(POOL_WINDOWS) - 1
CONV_WIDTH = 4
LRU_HEADS = 16
LRU_HEAD_DIM = D_LRU // LRU_HEADS
LRU_C = 8.0
N_GROUPS = 4
PER_GROUP = 8
N_EXPERTS = N_GROUPS * PER_GROUP
D_EXPERT = D_MODEL // 4
LN_EPS = 1e-5

SUBLANES = 8
LANES = 128
LANE_CHUNK = 256
POOL_HALO = 16
CONV_HALO = 8
ROWS = 256
TM = 256
ROUTER_ROWS = 8 + N_EXPERTS
META_LANES = LANES
VMEM_LIMIT = 56 * 1024 * 1024
BULK_DMA_PRIORITY = 1

_NT = (((1,), (1,)), ((), ()))


def _const_spec(shape):
    nd = len(shape)
    return pl.BlockSpec(shape, lambda *_: (0,) * nd, pipeline_mode=pl.Buffered(1))


def _mixer_kernel(x_ref, pool_in, conv_in, h_in, w_in, pool_w, pool_b, pool_scale, conv_w, conv_b,
                  gate_w, gate_b, lam, w_out, ln_g, ln_b, rt_hi, rt_lo, *rest, ns, l, pos0, alpha, aliased):
    if aliased:
        rest = rest[2:]
    (x1_ref, lg_ref, pool_o, conv_o, h_o,
     pool_ext, conv_ext, h_prev, gate_s, a_s, u_s, y_s) = rest
    t = pl.program_id(1)
    rows = ns * l

    @pl.when(t == 0)
    def _():
        pool_ext[:, 0:POOL_HALO, :] = pool_in[...]
        conv_ext[:, 0:CONV_HALO, :] = conv_in[...]
        h_prev[...] = h_in[...]

    xb = x_ref[...].reshape(rows, D_MODEL).astype(BF16)
    pool_ext[:, POOL_HALO:, :] = jnp.dot(
        xb, w_in[:, 0:D_POOL], preferred_element_type=F32).reshape(ns, l, D_POOL)
    conv_ext[:, CONV_HALO:, :] = jnp.dot(
        xb, w_in[:, D_POOL:D_POOL + D_LRU], preferred_element_type=F32).reshape(ns, l, D_LRU)
    gate_s[...] = jnp.dot(xb, w_in[:, D_POOL + D_LRU:], preferred_element_type=F32)

    pos = pos0 + t * l + lax.broadcasted_iota(I32, (ns, l, LANE_CHUNK), 1).reshape(rows, LANE_CHUNK)

    for g, w in enumerate(POOL_WINDOWS):
        cols = slice(g * POOL_GROUP, (g + 1) * POOL_GROUP)
        s = pool_ext[:, :, cols].reshape(ns * (POOL_HALO + l), POOL_GROUP)
        shift = 1
        while shift < w:
            s = s + pltpu.roll(s, shift, 0)
            shift *= 2
        win = s.reshape(ns, POOL_HALO + l, POOL_GROUP)[:, POOL_HALO:, :].reshape(rows, POOL_GROUP)
        u = pool_ext[:, POOL_HALO:, cols].reshape(rows, POOL_GROUP)
        if pos0 >= w - 1:
            inv = 1.0 / w
        else:
            inv = 1.0 / jnp.minimum(pos + 1, w).astype(F32)
        d = win * inv - u
        z = jnp.dot(d.astype(BF16), pool_w[g], preferred_element_type=F32) + pool_b[:, cols]
        y_s[:, cols] = (z * pool_scale[:, cols]).astype(BF16)

    lam_v = lam[...]
    softplus_neg = jnp.maximum(-lam_v, 0.0) + jnp.log1p(jnp.exp(-jnp.abs(lam_v)))
    log_a_scale = -LRU_C * softplus_neg
    sub = lax.broadcasted_iota(I32, (rows // SUBLANES, SUBLANES, LANE_CHUNK), 1)
    for c in range(D_LRU // LANE_CHUNK):
        cols = slice(c * LANE_CHUNK, (c + 1) * LANE_CHUNK)
        ce = conv_ext[:, :, cols].reshape(ns * (CONV_HALO + l), LANE_CHUNK)
        conv = conv_b[:, cols] + conv_w[CONV_WIDTH - 1:CONV_WIDTH, cols] * ce
        for k in range(1, CONV_WIDTH):
            conv = conv + conv_w[CONV_WIDTH - 1 - k:CONV_WIDTH - k, cols] * pltpu.roll(ce, k, 0)
        conv = conv.reshape(ns, CONV_HALO + l, LANE_CHUNK)[:, CONV_HALO:, :].reshape(rows, LANE_CHUNK)
        gz = jnp.dot(conv.astype(BF16), gate_w[c], preferred_element_type=F32)
        gz = gz + gate_b[:, 2 * c * LANE_CHUNK:2 * (c + 1) * LANE_CHUNK]
        r = jax.nn.sigmoid(gz[:, :LANE_CHUNK])
        i = jax.nn.sigmoid(gz[:, LANE_CHUNK:])
        log_a = r * log_a_scale[:, cols]
        a = jnp.exp(log_a)
        th = jnp.tanh(log_a)
        mult = jnp.sqrt(-2.0 * th / (1.0 - th))
        if pos0 == 0:
            mult = jnp.where(pos == 0, 1.0, mult)
        u = mult * (i * conv)
        a3 = a.reshape(rows // SUBLANES, SUBLANES, LANE_CHUNK)
        u3 = u.reshape(rows // SUBLANES, SUBLANES, LANE_CHUNK)
        for sh in (1, 2, 4):
            keep = sub >= sh
            a_sh = jnp.where(keep, pltpu.roll(a3, sh, 1), 1.0)
            u_sh = jnp.where(keep, pltpu.roll(u3, sh, 1), 0.0)
            u3 = a3 * u_sh + u3
            a3 = a3 * a_sh
        a_s[:, :, cols] = a3.reshape(ns, l, LANE_CHUNK)
        u_s[:, :, cols] = u3.reshape(ns, l, LANE_CHUNK)

    hp = h_prev[...]
    for gi in range(l // SUBLANES):
        sl = slice(gi * SUBLANES, (gi + 1) * SUBLANES)
        h = a_s[:, sl, :] * hp + u_s[:, sl, :]
        u_s[:, sl, :] = h
        hp = h[:, SUBLANES - 1:SUBLANES, :]
    h_prev[...] = hp

    for c in range(D_LRU // LANE_CHUNK):
        cols = slice(c * LANE_CHUNK, (c + 1) * LANE_CHUNK)
        h = u_s[:, :, cols].reshape(rows, LANE_CHUNK)
        y_s[:, D_POOL + c * LANE_CHUNK:D_POOL + (c + 1) * LANE_CHUNK] = (
            h * jax.nn.gelu(gate_s[:, cols])).astype(BF16)

    mix = jnp.dot(y_s[...], w_out[...], preferred_element_type=F32)
    xin = alpha * x_ref[...].reshape(rows, D_MODEL) + mix
    mu = jnp.mean(xin, axis=-1, keepdims=True)
    xc = xin - mu
    var = jnp.mean(xc * xc, axis=-1, keepdims=True)
    x1 = xc * lax.rsqrt(var + LN_EPS) * ln_g[...] + ln_b[...]
    x1_ref[...] = x1

    hi = x1.astype(BF16)
    lo = (x1 - hi.astype(F32)).astype(BF16)
    lg = lax.dot_general(rt_hi[...], hi, _NT, preferred_element_type=F32)
    lg = lg + lax.dot_general(rt_lo[...], hi, _NT, preferred_element_type=F32)
    lg = lg + lax.dot_general(rt_hi[...], lo, _NT, preferred_element_type=F32)
    lg_ref[...] = lg

    pool_tail = pool_ext[:, l:l + POOL_HALO, :]
    conv_tail = conv_ext[:, l:l + CONV_HALO, :]
    pool_o[...] = pool_tail
    conv_o[...] = conv_tail
    h_o[...] = u_s[:, l - SUBLANES:l, :]
    pool_ext[:, 0:POOL_HALO, :] = pool_tail
    conv_ext[:, 0:CONV_HALO, :] = conv_tail


def _mixer_call(x, pool_in, conv_in, h_in, wts, x1_buf, lg_buf, *, ns, l, pos0, alpha, n_total, blk0):
    bsz, tlen, _ = x.shape
    nb, nt = bsz // ns, tlen // l
    rows = ns * l
    assert rows == ROWS and bsz % ns == 0 and tlen % l == 0 and l % SUBLANES == 0
    aliased = x1_buf is not None

    def row_blk(b, t):
        return (blk0 + b * nt + t, 0)

    in_specs = [
        pl.BlockSpec((ns, l, D_MODEL), lambda b, t: (b, t, 0)),
        pl.BlockSpec((ns, POOL_HALO, D_POOL), lambda b, t: (b, 0, 0)),
        pl.BlockSpec((ns, CONV_HALO, D_LRU), lambda b, t: (b, 0, 0)),
        pl.BlockSpec((ns, 1, D_LRU), lambda b, t: (b, 0, 0)),
    ] + [_const_spec(w.shape) for w in wts]
    args = [x, pool_in, conv_in, h_in, *wts]
    aliases = {}
    if aliased:
        in_specs += [pl.BlockSpec(memory_space=pl.ANY), pl.BlockSpec(memory_space=pl.ANY)]
        aliases = {len(args): 0, len(args) + 1: 1}
        args += [x1_buf, lg_buf]
    out_shape = (
        jax.ShapeDtypeStruct((n_total, D_MODEL), F32),
        jax.ShapeDtypeStruct((ROUTER_ROWS, n_total), F32),
        jax.ShapeDtypeStruct((bsz, POOL_HALO, D_POOL), F32),
        jax.ShapeDtypeStruct((bsz, CONV_HALO, D_LRU), F32),
        jax.ShapeDtypeStruct((bsz, SUBLANES, D_LRU), F32),
    )
    out_specs = (
        pl.BlockSpec((rows, D_MODEL), row_blk),
        pl.BlockSpec((ROUTER_ROWS, rows), lambda b, t: (0, blk0 + b * nt + t)),
        pl.BlockSpec((ns, POOL_HALO, D_POOL), lambda b, t: (b, 0, 0)),
        pl.BlockSpec((ns, CONV_HALO, D_LRU), lambda b, t: (b, 0, 0)),
        pl.BlockSpec((ns, SUBLANES, D_LRU), lambda b, t: (b, 0, 0)),
    )
    scratch = [
        pltpu.VMEM((ns, POOL_HALO + l, D_POOL), F32),
        pltpu.VMEM((ns, CONV_HALO + l, D_LRU), F32),
        pltpu.VMEM((ns, 1, D_LRU), F32),
        pltpu.VMEM((rows, D_LRU), F32),
        pltpu.VMEM((ns, l, D_LRU), F32),
        pltpu.VMEM((ns, l, D_LRU), F32),
        pltpu.VMEM((rows, D_MODEL), BF16),
    ]
    return pl.pallas_call(
        functools.partial(_mixer_kernel, ns=ns, l=l, pos0=pos0, alpha=alpha, aliased=aliased),
        grid=(nb, nt),
        in_specs=in_specs,
        out_specs=out_specs,
        out_shape=out_shape,
        scratch_shapes=scratch,
        input_output_aliases=aliases,
        compiler_params=pltpu.CompilerParams(
            dimension_semantics=("arbitrary", "arbitrary"), vmem_limit_bytes=VMEM_LIMIT),
        name="mixer_seq" if nt > 1 else "mixer_step",
    )(*args)


def _route_kernel(lg_ref, bias_ref, dest_ref, tw_ref, meta_ref, e_s, r_s, *, n):
    blk = 256
    nblk = n // blk
    neg_inf = -jnp.inf
    ridx8 = lax.broadcasted_iota(I32, (SUBLANES, blk), 0).astype(F32)
    eidx = lax.broadcasted_iota(I32, (N_EXPERTS, blk), 0).astype(F32)
    tri = (lax.broadcasted_iota(I32, (blk, blk), 0) < lax.broadcasted_iota(I32, (blk, blk), 1))
    tri = jnp.where(tri, 1.0, 0.0).astype(BF16)

    def first_idx(vals, m):
        return jnp.min(jnp.where(vals == m, ridx8, float(SUBLANES)), axis=0, keepdims=True)

    def pass1(j, base):
        ls = pl.ds(pl.multiple_of(j * blk, blk), blk)
        lg = lg_ref[0:SUBLANES, ls] + bias_ref[0:SUBLANES, :]
        lg = jnp.where(ridx8 < N_GROUPS, lg, neg_inf)
        m = jnp.max(lg, axis=0, keepdims=True)
        gi = first_idx(lg, m)
        pg_sel = 1.0 / jnp.sum(jnp.exp(lg - m), axis=0, keepdims=True)
        le = jnp.zeros((PER_GROUP, blk), F32)
        for g in range(N_GROUPS):
            rows = slice(SUBLANES + g * PER_GROUP, SUBLANES + (g + 1) * PER_GROUP)
            le = jnp.where(gi == float(g), lg_ref[rows, ls] + bias_ref[rows, :], le)
        m1 = jnp.max(le, axis=0, keepdims=True)
        i1 = first_idx(le, m1)
        le2 = jnp.where(ridx8 == i1, neg_inf, le)
        m2 = jnp.max(le2, axis=0, keepdims=True)
        i2 = first_idx(le2, m2)
        e21 = jnp.exp(m2 - m1)
        denom = 1.0 / (1.0 + e21)
        tw_ref[0:1, ls] = pg_sel * denom
        tw_ref[1:2, ls] = pg_sel * (e21 * denom)
        e1 = gi * float(PER_GROUP) + i1
        e2 = gi * float(PER_GROUP) + i2
        e_s[0:1, ls] = e1
        e_s[1:2, ls] = e2
        oh1 = jnp.where(eidx == e1, 1.0, 0.0)
        oh2 = jnp.where(eidx == e2, 1.0, 0.0)
        oh = oh1 + oh2
        before = base + jnp.dot(oh.astype(BF16), tri, preferred_element_type=F32)
        r_s[0:1, ls] = jnp.sum(oh1 * before, axis=0, keepdims=True)
        r_s[1:2, ls] = jnp.sum(oh2 * before, axis=0, keepdims=True)
        return base + jnp.sum(oh, axis=1, keepdims=True)

    cnt = lax.fori_loop(0, nblk, pass1, jnp.zeros((N_EXPERTS, blk), F32))

    ntile = jnp.floor((cnt + float(TM - 1)) * (1.0 / TM))
    lt = (lax.broadcasted_iota(I32, (N_EXPERTS, N_EXPERTS), 1) < lax.broadcasted_iota(I32, (N_EXPERTS, N_EXPERTS), 0))
    lt = jnp.where(lt, 1.0, 0.0).astype(BF16)
    tile0 = jnp.dot(lt, ntile.astype(BF16), preferred_element_type=F32)
    row0 = tile0 * float(TM)

    def pass2(j, carry):
        ls = pl.ds(pl.multiple_of(j * blk, blk), blk)
        for k in range(2):
            ohk = jnp.where(eidx == e_s[k:k + 1, ls], 1.0, 0.0)
            dest = r_s[k:k + 1, ls] + jnp.sum(ohk * row0, axis=0, keepdims=True)
            dest_ref[k:k + 1, ls] = dest.astype(I32)
        return carry

    lax.fori_loop(0, nblk, pass2, 0)

    ml = slice(0, META_LANES)
    total = jnp.sum(ntile[:, ml], axis=0, keepdims=True)
    tile_i = lax.broadcasted_iota(I32, (N_EXPERTS, META_LANES), 1).astype(F32)
    tile_c = jnp.minimum(tile_i, total - 1.0)
    tile_end = tile0[:, ml] + ntile[:, ml]
    te = jnp.sum(jnp.where(tile_end <= tile_c, 1.0, 0.0), axis=0, keepdims=True)
    meta_ref[...] = jnp.zeros((SUBLANES, META_LANES), I32)
    meta_ref[0:1, :] = te.astype(I32)
    meta_ref[1:2, :] = total.astype(I32)
    later = (eidx[:, ml] > te) & (ntile[:, ml] > 0.0)
    nxt = jnp.min(jnp.where(later, eidx[:, ml], float(N_EXPERTS)), axis=0, keepdims=True)
    meta_ref[2:3, :] = nxt.astype(I32)


def _route_call(lg, bias, n):
    return pl.pallas_call(
        functools.partial(_route_kernel, n=n),
        out_shape=(
            jax.ShapeDtypeStruct((2, n), I32),
            jax.ShapeDtypeStruct((2, n), F32),
            jax.ShapeDtypeStruct((SUBLANES, META_LANES), I32),
        ),
        scratch_shapes=[pltpu.VMEM((2, n), F32), pltpu.VMEM((2, n), F32)],
        compiler_params=pltpu.CompilerParams(vmem_limit_bytes=VMEM_LIMIT),
        name="route",
    )(lg, bias)


def _inverse_kernel(dest_ref, init_hbm, src_ref, sem, *, n):
    init = pltpu.make_async_copy(init_hbm, src_ref, sem)
    init.start()
    init.wait()

    def fill(t, c):
        src_ref[dest_ref[t]] = t
        src_ref[dest_ref[n + t]] = t
        return c

    lax.fori_loop(0, n, fill, 0, unroll=8)


def _inverse_call(dest_flat, n, n_slots):
    return pl.pallas_call(
        functools.partial(_inverse_kernel, n=n),
        in_specs=[pl.BlockSpec(memory_space=pltpu.SMEM), pl.BlockSpec(memory_space=pl.ANY)],
        out_specs=pl.BlockSpec(memory_space=pltpu.SMEM),
        out_shape=jax.ShapeDtypeStruct((n_slots,), I32),
        scratch_shapes=[pltpu.SemaphoreType.DMA(())],
        name="inverse",
    )(dest_flat, jnp.arange(n_slots, dtype=I32) % n)


def _row_copy(src_hbm, row, dst_vmem, slot, sem):
    return pltpu.make_async_copy(src_hbm.at[pl.ds(row, 1), :], dst_vmem.at[pl.ds(slot, 1), :], sem)


def _tile_copy(src_hbm, dst_vmem, sem):
    return pltpu.make_async_copy(src_hbm.at[pl.ds(0, dst_vmem.shape[0]), :], dst_vmem, sem)


def _moe_kernel(te_ref, ne_ref, nt_ref, src_cur, src_nxt, x1_hbm, w1_hbm, w3_hbm, w2_hbm, o_ref,
                xbuf, wf1, wf3, wf2, w1b, w3b, w2b, wslot, gsem, wsem):
    i = pl.program_id(0)
    nt = nt_ref[0]

    def gather_start(src_ref, slot):
        for r in range(TM):
            _row_copy(x1_hbm, src_ref[0, 0, r], xbuf.at[slot], r, gsem.at[slot]).start()

    def weight_copies(e, slot):
        return (pltpu.make_async_copy(w1_hbm.at[e], wf1.at[slot], wsem.at[slot, 0]),
                pltpu.make_async_copy(w3_hbm.at[e], wf3.at[slot], wsem.at[slot, 1]),
                pltpu.make_async_copy(w2_hbm.at[e], wf2.at[slot], wsem.at[slot, 2]))

    @pl.when(i == 0)
    def _():
        for cp in weight_copies(te_ref[0], 0):
            cp.start(priority=BULK_DMA_PRIORITY)
        wslot[0] = 0
        gather_start(src_cur, 0)

    @pl.when(i >= nt)
    def _():
        o_ref[...] = jnp.zeros_like(o_ref)

    @pl.when(i < nt)
    def _():
        slot = lax.rem(i, 2)
        _tile_copy(x1_hbm, xbuf.at[slot], gsem.at[slot]).wait()

        @pl.when((i == 0) | (te_ref[i] != te_ref[jnp.maximum(i - 1, 0)]))
        def _():
            cur = wslot[0]
            for cp in weight_copies(te_ref[i], cur):
                cp.wait()
            w1b[...] = wf1[cur].astype(BF16)
            w3b[...] = wf3[cur].astype(BF16)
            w2b[...] = wf2[cur].astype(BF16)

            @pl.when(ne_ref[i] < N_EXPERTS)
            def _():
                for cp in weight_copies(ne_ref[i], 1 - cur):
                    cp.start(priority=BULK_DMA_PRIORITY)

            wslot[0] = 1 - cur

        xb = xbuf[slot].astype(BF16)
        gather_start(src_nxt, 1 - slot)
        h1 = jnp.dot(xb, w1b[...], preferred_element_type=F32)
        h3 = jnp.dot(xb, w3b[...], preferred_element_type=F32)
        h = (jax.nn.silu(h1) * h3).astype(BF16)
        o_ref[...] = jnp.dot(h, w2b[...], preferred_element_type=F32)

        @pl.when(i + 1 >= nt)
        def _():
            _tile_copy(x1_hbm, xbuf.at[1 - slot], gsem.at[1 - slot]).wait()


def _moe_call(te, ne, ntiles, src3, x1, w1, w3, w2, max_tiles):
    grid_spec = pltpu.PrefetchScalarGridSpec(
        num_scalar_prefetch=3,
        grid=(max_tiles,),
        in_specs=[
            pl.BlockSpec((1, 1, TM), lambda i, te, ne, nt: (i, 0, 0), memory_space=pltpu.SMEM),
            pl.BlockSpec((1, 1, TM), lambda i, te, ne, nt: (jnp.minimum(i + 1, max_tiles - 1), 0, 0),
                         memory_space=pltpu.SMEM),
            pl.BlockSpec(memory_space=pl.ANY),
            pl.BlockSpec(memory_space=pl.ANY),
            pl.BlockSpec(memory_space=pl.ANY),
            pl.BlockSpec(memory_space=pl.ANY),
        ],
        out_specs=pl.BlockSpec((TM, D_MODEL), lambda i, te, ne, nt: (jnp.minimum(i, nt[0]), 0)),
        scratch_shapes=[
            pltpu.VMEM((2, TM, D_MODEL), F32),
            pltpu.VMEM((2, D_MODEL, D_EXPERT), F32),
            pltpu.VMEM((2, D_MODEL, D_EXPERT), F32),
            pltpu.VMEM((2, D_EXPERT, D_MODEL), F32),
            pltpu.VMEM((D_MODEL, D_EXPERT), BF16),
            pltpu.VMEM((D_MODEL, D_EXPERT), BF16),
            pltpu.VMEM((D_EXPERT, D_MODEL), BF16),
            pltpu.SMEM((1,), I32),
            pltpu.SemaphoreType.DMA((2,)),
            pltpu.SemaphoreType.DMA((2, 3)),
        ],
    )
    return pl.pallas_call(
        _moe_kernel,
        grid_spec=grid_spec,
        out_shape=jax.ShapeDtypeStruct(((max_tiles + 1) * TM, D_MODEL), F32),
        compiler_params=pltpu.CompilerParams(dimension_semantics=("arbitrary",), vmem_limit_bytes=VMEM_LIMIT),
        name="moe",
    )(te, ne, ntiles, src3, src3, x1, w1, w3, w2)


def _combine_kernel(dest_cur, dest_nxt, x1_ref, tw_ref, ys_hbm, ln_g, ln_b, o_ref, buf, sem, *, alpha, nsteps):
    i = pl.program_id(0)
    slot = lax.rem(i, 2)

    def gather_start(dest_ref, s):
        for r in range(ROWS):
            for k in range(2):
                _row_copy(ys_hbm, dest_ref[0, k, r], buf.at[s, k], r, sem.at[s, k]).start(priority=k)

    def gather_wait(s):
        for k in range(2):
            _tile_copy(ys_hbm, buf.at[s, k], sem.at[s, k]).wait()

    @pl.when(i == 0)
    def _():
        gather_start(dest_cur, 0)

    gather_wait(slot)
    tw = tw_ref[...]
    moe = tw[:, 0:1] * buf[slot, 0] + tw[:, 1:2] * buf[slot, 1]
    gather_start(dest_nxt, 1 - slot)
    xin = alpha * x1_ref[...] + moe
    mu = jnp.mean(xin, axis=-1, keepdims=True)
    xc = xin - mu
    var = jnp.mean(xc * xc, axis=-1, keepdims=True)
    y = xc * lax.rsqrt(var + LN_EPS) * ln_g[...] + ln_b[...]
    o_ref[...] = y.reshape(o_ref.shape)

    @pl.when(i == nsteps - 1)
    def _():
        gather_wait(1 - slot)


def _combine_call(dest3, x1, tw, ys, ln_g, ln_b, out_shape, blk, blk0, alpha):
    bsz, tlen, _ = out_shape
    nt = tlen // blk[1]
    nsteps = (bsz // blk[0]) * nt
    return pl.pallas_call(
        functools.partial(_combine_kernel, alpha=alpha, nsteps=nsteps),
        grid=(nsteps,),
        in_specs=[
            pl.BlockSpec((1, 2, ROWS), lambda i: (blk0 + i, 0, 0), memory_space=pltpu.SMEM),
            pl.BlockSpec((1, 2, ROWS), lambda i: (blk0 + jnp.minimum(i + 1, nsteps - 1), 0, 0),
                         memory_space=pltpu.SMEM),
            pl.BlockSpec((ROWS, D_MODEL), lambda i: (blk0 + i, 0)),
            pl.BlockSpec((ROWS, 2), lambda i: (blk0 + i, 0)),
            pl.BlockSpec(memory_space=pl.ANY),
            _const_spec(ln_g.shape),
            _const_spec(ln_b.shape),
        ],
        out_specs=pl.BlockSpec(blk, lambda i: (i // nt, i % nt, 0)),
        out_shape=jax.ShapeDtypeStruct(out_shape, F32),
        scratch_shapes=[
            pltpu.VMEM((2, 2, ROWS, D_MODEL), F32),
            pltpu.SemaphoreType.DMA((2, 2)),
        ],
        compiler_params=pltpu.CompilerParams(dimension_semantics=("arbitrary",), vmem_limit_bytes=VMEM_LIMIT),
        name="combine",
    )(dest3, dest3, x1, tw, ys, ln_g, ln_b)


def _block_diag_chunks(w):
    per = LANE_CHUNK // LRU_HEAD_DIM
    nchunk = LRU_HEADS // per
    w4 = w.reshape(nchunk, per, LRU_HEAD_DIM, LRU_HEAD_DIM)
    bd = jnp.einsum("cjio,jk->cjiko", w4, jnp.eye(per, dtype=w.dtype))
    return bd.reshape(nchunk, LANE_CHUNK, LANE_CHUNK)


def _layer(yp, ys, pool_s, conv_s, h_s, lw, alpha, past_len):
    (w_in, pool_w, pool_b, pool_scale, conv_w, conv_b, rg_w, rg_b, ig_w, ig_b, lru_lambda, w_out,
     ln1_g, ln1_b, rgw, rgb, rew, reb, w1, w3, w2, ln2_g, ln2_b) = lw
    bp, tp, _ = yp.shape
    bs, ts, _ = ys.shape
    n_p, n_s = bp * tp, bs * ts
    n = n_p + n_s
    assert n % ROWS == 0 and n_p % ROWS == 0
    max_tiles = (2 * n) // TM + N_EXPERTS
    assert max_tiles <= META_LANES

    nchunk = D_LRU // LANE_CHUNK
    gate_w = jnp.concatenate([_block_diag_chunks(rg_w), _block_diag_chunks(ig_w)], axis=-1).astype(BF16)
    gate_b = jnp.concatenate(
        [rg_b.reshape(nchunk, LANE_CHUNK), ig_b.reshape(nchunk, LANE_CHUNK)], axis=-1).reshape(1, 2 * D_LRU)
    rt = jnp.concatenate(
        [rgw.T, jnp.zeros((SUBLANES - N_GROUPS, D_MODEL), F32),
         jnp.transpose(rew, (0, 2, 1)).reshape(N_EXPERTS, D_MODEL)], axis=0)
    rt_hi = rt.astype(BF16)
    rt_lo = (rt - rt_hi.astype(F32)).astype(BF16)
    r_bias = jnp.concatenate([rgb, jnp.zeros((SUBLANES - N_GROUPS,), F32), reb.reshape(N_EXPERTS)]).reshape(ROUTER_ROWS, 1)
    wts = (
        w_in.astype(BF16), pool_w.astype(BF16), pool_b.reshape(1, D_POOL), pool_scale.reshape(1, D_POOL),
        conv_w, conv_b.reshape(1, D_LRU), gate_w, gate_b, lru_lambda.reshape(1, D_LRU), w_out.astype(BF16),
        ln1_g.reshape(1, D_MODEL), ln1_b.reshape(1, D_MODEL), rt_hi, rt_lo,
    )

    x1, lg, pool_p, conv_p, h_p = _mixer_call(
        yp, jnp.zeros((bp, POOL_HALO, D_POOL), F32), jnp.zeros((bp, CONV_HALO, D_LRU), F32),
        jnp.zeros((bp, 1, D_LRU), F32), wts, None, None,
        ns=1, l=ROWS, pos0=0, alpha=alpha, n_total=n, blk0=0)
    pool_in = jnp.pad(pool_s, ((0, 0), (POOL_HALO - POOL_STATE, 0), (0, 0)))
    conv_in = jnp.pad(conv_s, ((0, 0), (CONV_HALO - (CONV_WIDTH - 1), 0), (0, 0)))
    x1, lg, pool_n, conv_n, h_n = _mixer_call(
        ys, pool_in, conv_in, h_s.reshape(bs, 1, D_LRU), wts, x1, lg,
        ns=ROWS // ts, l=ts, pos0=past_len, alpha=alpha, n_total=n, blk0=n_p // ROWS)

    dest, tw, meta = _route_call(lg, r_bias, n)
    src = _inverse_call(dest.reshape(2 * n), n, max_tiles * TM)
    ysort = _moe_call(meta[0], meta[2], meta[1, 0:1], src.reshape(max_tiles, 1, TM), x1, w1, w3, w2, max_tiles)

    dest3 = jnp.transpose(dest.reshape(2, n // ROWS, ROWS), (1, 0, 2))
    tw_t = tw.T
    g2, b2 = ln2_g.reshape(1, D_MODEL), ln2_b.reshape(1, D_MODEL)
    out_p = _combine_call(dest3, x1, tw_t, ysort, g2, b2, (bp, tp, D_MODEL), (1, ROWS, D_MODEL), 0, alpha)
    out_s = _combine_call(dest3, x1, tw_t, ysort, g2, b2, (bs, ts, D_MODEL), (ROWS // ts, ts, D_MODEL),
                          n_p // ROWS, alpha)
    states = (pool_p[:, 1:], conv_p[:, CONV_HALO - (CONV_WIDTH - 1):], h_p[:, SUBLANES - 1],
              pool_n[:, 1:], conv_n[:, CONV_HALO - (CONV_WIDTH - 1):], h_n[:, SUBLANES - 1])
    return out_p, out_s, states


def kernel(x_prompt, x_sample, state_pool, state_conv, state_h, w_in, pool_w, pool_b, pool_scale, conv_w, conv_b, rg_w, rg_b, ig_w, ig_b, lru_lambda, w_out, ln1_g, ln1_b, router_group_w, router_group_b, router_expert_w, router_expert_b, expert_w1, expert_w3, expert_w2, ln2_g, ln2_b):
    depth = w_in.shape[0]
    alpha = (2.0 * depth) ** 0.25
    past_len = 16384
    layer_weights = (w_in, pool_w, pool_b, pool_scale, conv_w, conv_b, rg_w, rg_b, ig_w, ig_b, lru_lambda, w_out,
                     ln1_g, ln1_b, router_group_w, router_group_b, router_expert_w, router_expert_b,
                     expert_w1, expert_w3, expert_w2, ln2_g, ln2_b)
    yp, ys = x_prompt, x_sample
    outs = [[] for _ in range(6)]
    for layer in range(depth):
        lw = tuple(w[layer] for w in layer_weights)
        yp, ys, states = _layer(yp, ys, state_pool[layer], state_conv[layer], state_h[layer], lw, alpha, past_len)
        for acc, s in zip(outs, states):
            acc.append(s)
    return (yp, ys) + tuple(jnp.stack(o, axis=0) for o in outs)
```

```python
import functools

import jax
import jax.numpy as jnp
from jax import lax
from jax.experimental import pallas as pl
from jax.experimental.pallas import tpu as pltpu

F32 = jnp.float32
BF16 = jnp.bfloat16
I32 = jnp.int32

D_MODEL = 2048
D_POOL = 1024
D_LRU = 1024
POOL_WINDOWS = (2, 4, 8, 16)
POOL_GROUP = D_POOL // len(POOL_WINDOWS)
POOL_STATE = max(POOL_WINDOWS) - 1
CONV_WIDTH = 4
LRU_HEADS = 16
LRU_HEAD_DIM = D_LRU // LRU_HEADS
LRU_C = 8.0
N_GROUPS = 4
PER_GROUP = 8
N_EXPERTS = N_GROUPS * PER_GROUP
D_EXPERT = D_MODEL // 4
LN_EPS = 1e-5

SUBLANES = 8
LANES = 128
LANE_CHUNK = 256
POOL_HALO = 16
CONV_HALO = 8
ROWS = 256
TM = 256
ROUTER_ROWS = 8 + N_EXPERTS
META_LANES = LANES
VMEM_LIMIT = 56 * 1024 * 1024
BULK_DMA_PRIORITY = 1

_NT = (((1,), (1,)), ((), ()))


def _const_spec(shape):
    nd = len(shape)
    return pl.BlockSpec(shape, lambda *_: (0,) * nd, pipeline_mode=pl.Buffered(1))


def _mixer_kernel(x_ref, pool_in, conv_in, h_in, w_in, pool_w, pool_b, pool_scale, conv_w, conv_b,
                  gate_w, gate_b, lam, w_out, ln_g, ln_b, rt_hi, rt_lo, *rest, ns, l, pos0, alpha, aliased):
    if aliased:
        rest = rest[2:]
    (x1_ref, lg_ref, pool_o, conv_o, h_o,
     pool_ext, conv_ext, h_prev, gate_s, a_s, u_s, y_s) = rest
    t = pl.program_id(1)
    rows = ns * l

    @pl.when(t == 0)
    def _():
        pool_ext[:, 0:POOL_HALO, :] = pool_in[...]
        conv_ext[:, 0:CONV_HALO, :] = conv_in[...]
        h_prev[...] = h_in[...]

    xb = x_ref[...].reshape(rows, D_MODEL).astype(BF16)
    pool_ext[:, POOL_HALO:, :] = jnp.dot(
        xb, w_in[:, 0:D_POOL], preferred_element_type=F32).reshape(ns, l, D_POOL)
    conv_ext[:, CONV_HALO:, :] = jnp.dot(
        xb, w_in[:, D_POOL:D_POOL + D_LRU], preferred_element_type=F32).reshape(ns, l, D_LRU)
    gate_s[...] = jnp.dot(xb, w_in[:, D_POOL + D_LRU:], preferred_element_type=F32)

    pos = pos0 + t * l + lax.broadcasted_iota(I32, (ns, l, LANE_CHUNK), 1).reshape(rows, LANE_CHUNK)

    for g, w in enumerate(POOL_WINDOWS):
        cols = slice(g * POOL_GROUP, (g + 1) * POOL_GROUP)
        s = pool_ext[:, :, cols].reshape(ns * (POOL_HALO + l), POOL_GROUP)
        shift = 1
        while shift < w:
            s = s + pltpu.roll(s, shift, 0)
            shift *= 2
        win = s.reshape(ns, POOL_HALO + l, POOL_GROUP)[:, POOL_HALO:, :].reshape(rows, POOL_GROUP)
        u = pool_ext[:, POOL_HALO:, cols].reshape(rows, POOL_GROUP)
        if pos0 >= w - 1:
            inv = 1.0 / w
        else:
            inv = 1.0 / jnp.minimum(pos + 1, w).astype(F32)
        d = win * inv - u
        z = jnp.dot(d.astype(BF16), pool_w[g], preferred_element_type=F32) + pool_b[:, cols]
        y_s[:, cols] = (z * pool_scale[:, cols]).astype(BF16)

    lam_v = lam[...]
    softplus_neg = jnp.maximum(-lam_v, 0.0) + jnp.log1p(jnp.exp(-jnp.abs(lam_v)))
    log_a_scale = -LRU_C * softplus_neg
    sub = lax.broadcasted_iota(I32, (rows // SUBLANES, SUBLANES, LANE_CHUNK), 1)
    for c in range(D_LRU // LANE_CHUNK):
        cols = slice(c * LANE_CHUNK, (c + 1) * LANE_CHUNK)
        ce = conv_ext[:, :, cols].reshape(ns * (CONV_HALO + l), LANE_CHUNK)
        conv = conv_b[:, cols] + conv_w[CONV_WIDTH - 1:CONV_WIDTH, cols] * ce
        for k in range(1, CONV_WIDTH):
            conv = conv + conv_w[CONV_WIDTH - 1 - k:CONV_WIDTH - k, cols] * pltpu.roll(ce, k, 0)
        conv = conv.reshape(ns, CONV_HALO + l, LANE_CHUNK)[:, CONV_HALO:, :].reshape(rows, LANE_CHUNK)
        gz = jnp.dot(conv.astype(BF16), gate_w[c], preferred_element_type=F32)
        gz = gz + gate_b[:, 2 * c * LANE_CHUNK:2 * (c + 1) * LANE_CHUNK]
        r = jax.nn.sigmoid(gz[:, :LANE_CHUNK])
        i = jax.nn.sigmoid(gz[:, LANE_CHUNK:])
        log_a = r * log_a_scale[:, cols]
        a = jnp.exp(log_a)
        th = jnp.tanh(log_a)
        mult = jnp.sqrt(-2.0 * th / (1.0 - th))
        if pos0 == 0:
            mult = jnp.where(pos == 0, 1.0, mult)
        u = mult * (i * conv)
        a3 = a.reshape(rows // SUBLANES, SUBLANES, LANE_CHUNK)
        u3 = u.reshape(rows // SUBLANES, SUBLANES, LANE_CHUNK)
        for sh in (1, 2, 4):
            keep = sub >= sh
            a_sh = jnp.where(keep, pltpu.roll(a3, sh, 1), 1.0)
            u_sh = jnp.where(keep, pltpu.roll(u3, sh, 1), 0.0)
            u3 = a3 * u_sh + u3
            a3 = a3 * a_sh
        a_s[:, :, cols] = a3.reshape(ns, l, LANE_CHUNK)
        u_s[:, :, cols] = u3.reshape(ns, l, LANE_CHUNK)

    hp = h_prev[...]
    for gi in range(l // SUBLANES):
        sl = slice(gi * SUBLANES, (gi + 1) * SUBLANES)
        h = a_s[:, sl, :] * hp + u_s[:, sl, :]
        u_s[:, sl, :] = h
        hp = h[:, SUBLANES - 1:SUBLANES, :]
    h_prev[...] = hp

    for c in range(D_LRU // LANE_CHUNK):
        cols = slice(c * LANE_CHUNK, (c + 1) * LANE_CHUNK)
        h = u_s[:, :, cols].reshape(rows, LANE_CHUNK)
        y_s[:, D_POOL + c * LANE_CHUNK:D_POOL + (c + 1) * LANE_CHUNK] = (
            h * jax.nn.gelu(gate_s[:, cols])).astype(BF16)

    mix = jnp.dot(y_s[...], w_out[...], preferred_element_type=F32)
    xin = alpha * x_ref[...].reshape(rows, D_MODEL) + mix
    mu = jnp.mean(xin, axis=-1, keepdims=True)
    xc = xin - mu
    var = jnp.mean(xc * xc, axis=-1, keepdims=True)
    x1 = xc * lax.rsqrt(var + LN_EPS) * ln_g[...] + ln_b[...]
    x1_ref[...] = x1

    hi = x1.astype(BF16)
    lo = (x1 - hi.astype(F32)).astype(BF16)
    lg = lax.dot_general(rt_hi[...], hi, _NT, preferred_element_type=F32)
    lg = lg + lax.dot_general(rt_lo[...], hi, _NT, preferred_element_type=F32)
    lg = lg + lax.dot_general(rt_hi[...], lo, _NT, preferred_element_type=F32)
    lg_ref[...] = lg

    pool_tail = pool_ext[:, l:l + POOL_HALO, :]
    conv_tail = conv_ext[:, l:l + CONV_HALO, :]
    pool_o[...] = pool_tail
    conv_o[...] = conv_tail
    h_o[...] = u_s[:, l - SUBLANES:l, :]
    pool_ext[:, 0:POOL_HALO, :] = pool_tail
    conv_ext[:, 0:CONV_HALO, :] = conv_tail


def _mixer_call(x, pool_in, conv_in, h_in, wts, x1_buf, lg_buf, *, ns, l, pos0, alpha, n_total, blk0):
    bsz, tlen, _ = x.shape
    nb, nt = bsz // ns, tlen // l
    rows = ns * l
    assert rows == ROWS and bsz % ns == 0 and tlen % l == 0 and l % SUBLANES == 0
    aliased = x1_buf is not None

    def row_blk(b, t):
        return (blk0 + b * nt + t, 0)

    in_specs = [
        pl.BlockSpec((ns, l, D_MODEL), lambda b, t: (b, t, 0)),
        pl.BlockSpec((ns, POOL_HALO, D_POOL), lambda b, t: (b, 0, 0)),
        pl.BlockSpec((ns, CONV_HALO, D_LRU), lambda b, t: (b, 0, 0)),
        pl.BlockSpec((ns, 1, D_LRU), lambda b, t: (b, 0, 0)),
    ] + [_const_spec(w.shape) for w in wts]
    args = [x, pool_in, conv_in, h_in, *wts]
    aliases = {}
    if aliased:
        in_specs += [pl.BlockSpec(memory_space=pl.ANY), pl.BlockSpec(memory_space=pl.ANY)]
        aliases = {len(args): 0, len(args) + 1: 1}
        args += [x1_buf, lg_buf]
    out_shape = (
        jax.ShapeDtypeStruct((n_total, D_MODEL), F32),
        jax.ShapeDtypeStruct((ROUTER_ROWS, n_total), F32),
        jax.ShapeDtypeStruct((bsz, POOL_HALO, D_POOL), F32),
        jax.ShapeDtypeStruct((bsz, CONV_HALO, D_LRU), F32),
        jax.ShapeDtypeStruct((bsz, SUBLANES, D_LRU), F32),
    )
    out_specs = (
        pl.BlockSpec((rows, D_MODEL), row_blk),
        pl.BlockSpec((ROUTER_ROWS, rows), lambda b, t: (0, blk0 + b * nt + t)),
        pl.BlockSpec((ns, POOL_HALO, D_POOL), lambda b, t: (b, 0, 0)),
        pl.BlockSpec((ns, CONV_HALO, D_LRU), lambda b, t: (b, 0, 0)),
        pl.BlockSpec((ns, SUBLANES, D_LRU), lambda b, t: (b, 0, 0)),
    )
    scratch = [
        pltpu.VMEM((ns, POOL_HALO + l, D_POOL), F32),
        pltpu.VMEM((ns, CONV_HALO + l, D_LRU), F32),
        pltpu.VMEM((ns, 1, D_LRU), F32),
        pltpu.VMEM((rows, D_LRU), F32),
        pltpu.VMEM((ns, l, D_LRU), F32),
        pltpu.VMEM((ns, l, D_LRU), F32),
        pltpu.VMEM((rows, D_MODEL), BF16),
    ]
    return pl.pallas_call(
        functools.partial(_mixer_kernel, ns=ns, l=l, pos0=pos0, alpha=alpha, aliased=aliased),
        grid=(nb, nt),
        in_specs=in_specs,
        out_specs=out_specs,
        out_shape=out_shape,
        scratch_shapes=scratch,
        input_output_aliases=aliases,
        compiler_params=pltpu.CompilerParams(
            dimension_semantics=("arbitrary", "arbitrary"), vmem_limit_bytes=VMEM_LIMIT),
        name="mixer_seq" if nt > 1 else "mixer_step",
    )(*args)


def _route_kernel(lg_ref, bias_ref, dest_ref, tw_ref, meta_ref, e_s, r_s, *, n):
    blk = 256
    nblk = n // blk
    neg_inf = -jnp.inf
    ridx8 = lax.broadcasted_iota(I32, (SUBLANES, blk), 0).astype(F32)
    eidx = lax.broadcasted_iota(I32, (N_EXPERTS, blk), 0).astype(F32)
    tri = (lax.broadcasted_iota(I32, (blk, blk), 0) < lax.broadcasted_iota(I32, (blk, blk), 1))
    tri = jnp.where(tri, 1.0, 0.0).astype(BF16)

    def first_idx(vals, m):
        return jnp.min(jnp.where(vals == m, ridx8, float(SUBLANES)), axis=0, keepdims=True)

    def pass1(j, base):
        ls = pl.ds(pl.multiple_of(j * blk, blk), blk)
        lg = lg_ref[0:SUBLANES, ls] + bias_ref[0:SUBLANES, :]
        lg = jnp.where(ridx8 < N_GROUPS, lg, neg_inf)
        m = jnp.max(lg, axis=0, keepdims=True)
        gi = first_idx(lg, m)
        pg_sel = 1.0 / jnp.sum(jnp.exp(lg - m), axis=0, keepdims=True)
        le = jnp.zeros((PER_GROUP, blk), F32)
        for g in range(N_GROUPS):
            rows = slice(SUBLANES + g * PER_GROUP, SUBLANES + (g + 1) * PER_GROUP)
            le = jnp.where(gi == float(g), lg_ref[rows, ls] + bias_ref[rows, :], le)
        m1 = jnp.max(le, axis=0, keepdims=True)
        i1 = first_idx(le, m1)
        le2 = jnp.where(ridx8 == i1, neg_inf, le)
        m2 = jnp.max(le2, axis=0, keepdims=True)
        i2 = first_idx(le2, m2)
        e21 = jnp.exp(m2 - m1)
        denom = 1.0 / (1.0 + e21)
        tw_ref[0:1, ls] = pg_sel * denom
        tw_ref[1:2, ls] = pg_sel * (e21 * denom)
        e1 = gi * float(PER_GROUP) + i1
        e2 = gi * float(PER_GROUP) + i2
        e_s[0:1, ls] = e1
        e_s[1:2, ls] = e2
        oh1 = jnp.where(eidx == e1, 1.0, 0.0)
        oh2 = jnp.where(eidx == e2, 1.0, 0.0)
        oh = oh1 + oh2
        before = base + jnp.dot(oh.astype(BF16), tri, preferred_element_type=F32)
        r_s[0:1, ls] = jnp.sum(oh1 * before, axis=0, keepdims=True)
        r_s[1:2, ls] = jnp.sum(oh2 * before, axis=0, keepdims=True)
        return base + jnp.sum(oh, axis=1, keepdims=True)

    cnt = lax.fori_loop(0, nblk, pass1, jnp.zeros((N_EXPERTS, blk), F32))

    ntile = jnp.floor((cnt + float(TM - 1)) * (1.0 / TM))
    lt = (lax.broadcasted_iota(I32, (N_EXPERTS, N_EXPERTS), 1) < lax.broadcasted_iota(I32, (N_EXPERTS, N_EXPERTS), 0))
    lt = jnp.where(lt, 1.0, 0.0).astype(BF16)
    tile0 = jnp.dot(lt, ntile.astype(BF16), preferred_element_type=F32)
    row0 = tile0 * float(TM)

    def pass2(j, carry):
        ls = pl.ds(pl.multiple_of(j * blk, blk), blk)
        for k in range(2):
            ohk = jnp.where(eidx == e_s[k:k + 1, ls], 1.0, 0.0)
            dest = r_s[k:k + 1, ls] + jnp.sum(ohk * row0, axis=0, keepdims=True)
            dest_ref[k:k + 1, ls] = dest.astype(I32)
        return carry

    lax.fori_loop(0, nblk, pass2, 0)

    ml = slice(0, META_LANES)
    total = jnp.sum(ntile[:, ml], axis=0, keepdims=True)
    tile_i = lax.broadcasted_iota(I32, (N_EXPERTS, META_LANES), 1).astype(F32)
    tile_c = jnp.minimum(tile_i, total - 1.0)
    tile_end = tile0[:, ml] + ntile[:, ml]
    te = jnp.sum(jnp.where(tile_end <= tile_c, 1.0, 0.0), axis=0, keepdims=True)
    meta_ref[...] = jnp.zeros((SUBLANES, META_LANES), I32)
    meta_ref[0:1, :] = te.astype(I32)
    meta_ref[1:2, :] = total.astype(I32)
    later = (eidx[:, ml] > te) & (ntile[:, ml] > 0.0)
    nxt = jnp.min(jnp.where(later, eidx[:, ml], float(N_EXPERTS)), axis=0, keepdims=True)
    meta_ref[2:3, :] = nxt.astype(I32)


def _route_call(lg, bias, n):
    return pl.pallas_call(
        functools.partial(_route_kernel, n=n),
        out_shape=(
            jax.ShapeDtypeStruct((2, n), I32),
            jax.ShapeDtypeStruct((2, n), F32),
            jax.ShapeDtypeStruct((SUBLANES, META_LANES), I32),
        ),
        scratch_shapes=[pltpu.VMEM((2, n), F32), pltpu.VMEM((2, n), F32)],
        compiler_params=pltpu.CompilerParams(vmem_limit_bytes=VMEM_LIMIT),
        name="route",
    )(lg, bias)


def _inverse_kernel(dest_ref, init_hbm, src_ref, sem, *, n):
    init = pltpu.make_async_copy(init_hbm, src_ref, sem)
    init.start()
    init.wait()

    def fill(t, c):
        src_ref[dest_ref[t]] = t
        src_ref[dest_ref[n + t]] = t
        return c

    lax.fori_loop(0, n, fill, 0, unroll=8)


def _inverse_call(dest_flat, n, n_slots):
    return pl.pallas_call(
        functools.partial(_inverse_kernel, n=n),
        in_specs=[pl.BlockSpec(memory_space=pltpu.SMEM), pl.BlockSpec(memory_space=pl.ANY)],
        out_specs=pl.BlockSpec(memory_space=pltpu.SMEM),
        out_shape=jax.ShapeDtypeStruct((n_slots,), I32),
        scratch_shapes=[pltpu.SemaphoreType.DMA(())],
        name="inverse",
    )(dest_flat, jnp.arange(n_slots, dtype=I32) % n)


def _row_copy(src_hbm, row, dst_vmem, slot, sem):
    return pltpu.make_async_copy(src_hbm.at[pl.ds(row, 1), :], dst_vmem.at[pl.ds(slot, 1), :], sem)


def _tile_copy(src_hbm, dst_vmem, sem):
    return pltpu.make_async_copy(src_hbm.at[pl.ds(0, dst_vmem.shape[0]), :], dst_vmem, sem)


def _moe_kernel(te_ref, ne_ref, nt_ref, src_cur, src_nxt, x1_hbm, w1_hbm, w3_hbm, w2_hbm, o_ref,
                xbuf, wf1, wf3, wf2, w1b, w3b, w2b, wslot, gsem, wsem):
    i = pl.program_id(0)
    nt = nt_ref[0]

    def gather_start(src_ref, slot):
        for r in range(TM):
            _row_copy(x1_hbm, src_ref[0, 0, r], xbuf.at[slot], r, gsem.at[slot]).start(
                priority=BULK_DMA_PRIORITY if r % 4 == 3 else 0)

    def weight_copies(e, slot):
        return (pltpu.make_async_copy(w1_hbm.at[e], wf1.at[slot], wsem.at[slot, 0]),
                pltpu.make_async_copy(w3_hbm.at[e], wf3.at[slot], wsem.at[slot, 1]),
                pltpu.make_async_copy(w2_hbm.at[e], wf2.at[slot], wsem.at[slot, 2]))

    @pl.when(i == 0)
    def _():
        for cp in weight_copies(te_ref[0], 0):
            cp.start(priority=BULK_DMA_PRIORITY)
        wslot[0] = 0
        gather_start(src_cur, 0)

    @pl.when(i >= nt)
    def _():
        o_ref[...] = jnp.zeros_like(o_ref)

    @pl.when(i < nt)
    def _():
        slot = lax.rem(i, 2)
        _tile_copy(x1_hbm, xbuf.at[slot], gsem.at[slot]).wait()

        @pl.when((i == 0) | (te_ref[i] != te_ref[jnp.maximum(i - 1, 0)]))
        def _():
            cur = wslot[0]
            for cp in weight_copies(te_ref[i], cur):
                cp.wait()
            w1b[...] = wf1[cur].astype(BF16)
            w3b[...] = wf3[cur].astype(BF16)
            w2b[...] = wf2[cur].astype(BF16)

            @pl.when(ne_ref[i] < N_EXPERTS)
            def _():
                for cp in weight_copies(ne_ref[i], 1 - cur):
                    cp.start(priority=BULK_DMA_PRIORITY)

            wslot[0] = 1 - cur

        xb = xbuf[slot].astype(BF16)
        gather_start(src_nxt, 1 - slot)
        h1 = jnp.dot(xb, w1b[...], preferred_element_type=F32)
        h3 = jnp.dot(xb, w3b[...], preferred_element_type=F32)
        h = (jax.nn.silu(h1) * h3).astype(BF16)
        o_ref[...] = jnp.dot(h, w2b[...], preferred_element_type=F32)

        @pl.when(i + 1 >= nt)
        def _():
            _tile_copy(x1_hbm, xbuf.at[1 - slot], gsem.at[1 - slot]).wait()


def _moe_call(te, ne, ntiles, src3, x1, w1, w3, w2, max_tiles):
    grid_spec = pltpu.PrefetchScalarGridSpec(
        num_scalar_prefetch=3,
        grid=(max_tiles,),
        in_specs=[
            pl.BlockSpec((1, 1, TM), lambda i, te, ne, nt: (i, 0, 0), memory_space=pltpu.SMEM),
            pl.BlockSpec((1, 1, TM), lambda i, te, ne, nt: (jnp.minimum(i + 1, max_tiles - 1), 0, 0),
                         memory_space=pltpu.SMEM),
            pl.BlockSpec(memory_space=pl.ANY),
            pl.BlockSpec(memory_space=pl.ANY),
            pl.BlockSpec(memory_space=pl.ANY),
            pl.BlockSpec(memory_space=pl.ANY),
        ],
        out_specs=pl.BlockSpec((TM, D_MODEL), lambda i, te, ne, nt: (jnp.minimum(i, nt[0]), 0)),
        scratch_shapes=[
            pltpu.VMEM((2, TM, D_MODEL), F32),
            pltpu.VMEM((2, D_MODEL, D_EXPERT), F32),
            pltpu.VMEM((2, D_MODEL, D_EXPERT), F32),
            pltpu.VMEM((2, D_EXPERT, D_MODEL), F32),
            pltpu.VMEM((D_MODEL, D_EXPERT), BF16),
            pltpu.VMEM((D_MODEL, D_EXPERT), BF16),
            pltpu.VMEM((D_EXPERT, D_MODEL), BF16),
            pltpu.SMEM((1,), I32),
            pltpu.SemaphoreType.DMA((2,)),
            pltpu.SemaphoreType.DMA((2, 3)),
        ],
    )
    return pl.pallas_call(
        _moe_kernel,
        grid_spec=grid_spec,
        out_shape=jax.ShapeDtypeStruct(((max_tiles + 1) * TM, D_MODEL), F32),
        compiler_params=pltpu.CompilerParams(dimension_semantics=("arbitrary",), vmem_limit_bytes=VMEM_LIMIT),
        name="moe",
    )(te, ne, ntiles, src3, src3, x1, w1, w3, w2)


def _combine_kernel(dest_cur, dest_nxt, x1_ref, tw_ref, ys_hbm, ln_g, ln_b, o_ref, buf, sem, *, alpha, nsteps):
    i = pl.program_id(0)
    slot = lax.rem(i, 2)

    def gather_start(dest_ref, s):
        for r in range(ROWS):
            for k in range(2):
                _row_copy(ys_hbm, dest_ref[0, k, r], buf.at[s, k], r, sem.at[s, k]).start(priority=k)

    def gather_wait(s):
        for k in range(2):
            _tile_copy(ys_hbm, buf.at[s, k], sem.at[s, k]).wait()

    @pl.when(i == 0)
    def _():
        gather_start(dest_cur, 0)

    gather_wait(slot)
    tw = tw_ref[...]
    moe = tw[:, 0:1] * buf[slot, 0] + tw[:, 1:2] * buf[slot, 1]
    gather_start(dest_nxt, 1 - slot)
    xin = alpha * x1_ref[...] + moe
    mu = jnp.mean(xin, axis=-1, keepdims=True)
    xc = xin - mu
    var = jnp.mean(xc * xc, axis=-1, keepdims=True)
    y = xc * lax.rsqrt(var + LN_EPS) * ln_g[...] + ln_b[...]
    o_ref[...] = y.reshape(o_ref.shape)

    @pl.when(i == nsteps - 1)
    def _():
        gather_wait(1 - slot)


def _combine_call(dest3, x1, tw, ys, ln_g, ln_b, out_shape, blk, blk0, alpha):
    bsz, tlen, _ = out_shape
    nt = tlen // blk[1]
    nsteps = (bsz // blk[0]) * nt
    return pl.pallas_call(
        functools.partial(_combine_kernel, alpha=alpha, nsteps=nsteps),
        grid=(nsteps,),
        in_specs=[
            pl.BlockSpec((1, 2, ROWS), lambda i: (blk0 + i, 0, 0), memory_space=pltpu.SMEM),
            pl.BlockSpec((1, 2, ROWS), lambda i: (blk0 + jnp.minimum(i + 1, nsteps - 1), 0, 0),
                         memory_space=pltpu.SMEM),
            pl.BlockSpec((ROWS, D_MODEL), lambda i: (blk0 + i, 0)),
            pl.BlockSpec((ROWS, 2), lambda i: (blk0 + i, 0)),
            pl.BlockSpec(memory_space=pl.ANY),
            _const_spec(ln_g.shape),
            _const_spec(ln_b.shape),
        ],
        out_specs=pl.BlockSpec(blk, lambda i: (i // nt, i % nt, 0)),
        out_shape=jax.ShapeDtypeStruct(out_shape, F32),
        scratch_shapes=[
            pltpu.VMEM((2, 2, ROWS, D_MODEL), F32),
            pltpu.SemaphoreType.DMA((2, 2)),
        ],
        compiler_params=pltpu.CompilerParams(dimension_semantics=("arbitrary",), vmem_limit_bytes=VMEM_LIMIT),
        name="combine",
    )(dest3, dest3, x1, tw, ys, ln_g, ln_b)


def _block_diag_chunks(w):
    per = LANE_CHUNK // LRU_HEAD_DIM
    nchunk = LRU_HEADS // per
    w4 = w.reshape(nchunk, per, LRU_HEAD_DIM, LRU_HEAD_DIM)
    bd = jnp.einsum("cjio,jk->cjiko", w4, jnp.eye(per, dtype=w.dtype))
    return bd.reshape(nchunk, LANE_CHUNK, LANE_CHUNK)


def _layer(yp, ys, pool_s, conv_s, h_s, lw, alpha, past_len):
    (w_in, pool_w, pool_b, pool_scale, conv_w, conv_b, rg_w, rg_b, ig_w, ig_b, lru_lambda, w_out,
     ln1_g, ln1_b, rgw, rgb, rew, reb, w1, w3, w2, ln2_g, ln2_b) = lw
    bp, tp, _ = yp.shape
    bs, ts, _ = ys.shape
    n_p, n_s = bp * tp, bs * ts
    n = n_p + n_s
    assert n % ROWS == 0 and n_p % ROWS == 0
    max_tiles = (2 * n) // TM + N_EXPERTS
    assert max_tiles <= META_LANES

    nchunk = D_LRU // LANE_CHUNK
    gate_w = jnp.concatenate([_block_diag_chunks(rg_w), _block_diag_chunks(ig_w)], axis=-1).astype(BF16)
    gate_b = jnp.concatenate(
        [rg_b.reshape(nchunk, LANE_CHUNK), ig_b.reshape(nchunk, LANE_CHUNK)], axis=-1).reshape(1, 2 * D_LRU)
    rt = jnp.concatenate(
        [rgw.T, jnp.zeros((SUBLANES - N_GROUPS, D_MODEL), F32),
         jnp.transpose(rew, (0, 2, 1)).reshape(N_EXPERTS, D_MODEL)], axis=0)
    rt_hi = rt.astype(BF16)
    rt_lo = (rt - rt_hi.astype(F32)).astype(BF16)
    r_bias = jnp.concatenate([rgb, jnp.zeros((SUBLANES - N_GROUPS,), F32), reb.reshape(N_EXPERTS)]).reshape(ROUTER_ROWS, 1)
    wts = (
        w_in.astype(BF16), pool_w.astype(BF16), pool_b.reshape(1, D_POOL), pool_scale.reshape(1, D_POOL),
        conv_w, conv_b.reshape(1, D_LRU), gate_w, gate_b, lru_lambda.reshape(1, D_LRU), w_out.astype(BF16),
        ln1_g.reshape(1, D_MODEL), ln1_b.reshape(1, D_MODEL), rt_hi, rt_lo,
    )

    x1, lg, pool_p, conv_p, h_p = _mixer_call(
        yp, jnp.zeros((bp, POOL_HALO, D_POOL), F32), jnp.zeros((bp, CONV_HALO, D_LRU), F32),
        jnp.zeros((bp, 1, D_LRU), F32), wts, None, None,
        ns=1, l=ROWS, pos0=0, alpha=alpha, n_total=n, blk0=0)
    pool_in = jnp.pad(pool_s, ((0, 0), (POOL_HALO - POOL_STATE, 0), (0, 0)))
    conv_in = jnp.pad(conv_s, ((0, 0), (CONV_HALO - (CONV_WIDTH - 1), 0), (0, 0)))
    x1, lg, pool_n, conv_n, h_n = _mixer_call(
        ys, pool_in, conv_in, h_s.reshape(bs, 1, D_LRU), wts, x1, lg,
        ns=ROWS // ts, l=ts, pos0=past_len, alpha=alpha, n_total=n, blk0=n_p // ROWS)

    dest, tw, meta = _route_call(lg, r_bias, n)
    src = _inverse_call(dest.reshape(2 * n), n, max_tiles * TM)
    ysort = _moe_call(meta[0], meta[2], meta[1, 0:1], src.reshape(max_tiles, 1, TM), x1, w1, w3, w2, max_tiles)

    dest3 = jnp.transpose(dest.reshape(2, n // ROWS, ROWS), (1, 0, 2))
    tw_t = tw.T
    g2, b2 = ln2_g.reshape(1, D_MODEL), ln2_b.reshape(1, D_MODEL)
    out_p = _combine_call(dest3, x1, tw_t, ysort, g2, b2, (bp, tp, D_MODEL), (1, ROWS, D_MODEL), 0, alpha)
    out_s = _combine_call(dest3, x1, tw_t, ysort, g2, b2, (bs, ts, D_MODEL), (ROWS // ts, ts, D_MODEL),
                          n_p // ROWS, alpha)
    states = (pool_p[:, 1:], conv_p[:, CONV_HALO - (CONV_WIDTH - 1):], h_p[:, SUBLANES - 1],
              pool_n[:, 1:], conv_n[:, CONV_HALO - (CONV_WIDTH - 1):], h_n[:, SUBLANES - 1])
    return out_p, out_s, states


def kernel(x_prompt, x_sample, state_pool, state_conv, state_h, w_in, pool_w, pool_b, pool_scale, conv_w, conv_b, rg_w, rg_b, ig_w, ig_b, lru_lambda, w_out, ln1_g, ln1_b, router_group_w, router_group_b, router_expert_w, router_expert_b, expert_w1, expert_w3, expert_w2, ln2_g, ln2_b):
    depth = w_in.shape[0]
    alpha = (2.0 * depth) ** 0.25
    past_len = 16384
    layer_weights = (w_in, pool_w, pool_b, pool_scale, conv_w, conv_b, rg_w, rg_b, ig_w, ig_b, lru_lambda, w_out,
                     ln1_g, ln1_b, router_group_w, router_group_b, router_expert_w, router_expert_b,
                     expert_w1, expert_w3, expert_w2, ln2_g, ln2_b)
    yp, ys = x_prompt, x_sample
    outs = [[] for _ in range(6)]
    for layer in range(depth):
        lw = tuple(w[layer] for w in layer_weights)
        yp, ys, states = _layer(yp, ys, state_pool[layer], state_conv[layer], state_h[layer], lw, alpha, past_len)
        for acc, s in zip(outs, states):
            acc.append(s)
    return (yp, ys) + tuple(jnp.stack(o, axis=0) for o in outs)
```

```python
import functools

import jax
import jax.numpy as jnp
from jax import lax
from jax.experimental import pallas as pl
from jax.experimental.pallas import tpu as pltpu

F32 = jnp.float32
BF16 = jnp.bfloat16
I32 = jnp.int32

D_MODEL = 2048
D_POOL = 1024
D_LRU = 1024
POOL_WINDOWS = (2, 4, 8, 16)
POOL_GROUP = D_POOL // len(POOL_WINDOWS)
POOL_STATE = max(POOL_WINDOWS) - 1
CONV_WIDTH = 4
CONV_STATE = CONV_WIDTH - 1
LRU_HEADS = 16
LRU_HEAD_DIM = D_LRU // LRU_HEADS
LRU_C = 8.0
N_GROUPS = 4
PER_GROUP = 8
N_EXPERTS = N_GROUPS * PER_GROUP
D_EXPERT = D_MODEL // 4
LN_EPS = 1e-5

SUBLANES = 8
LANES = 128
LANE_CHUNK = 256
POOL_HALO = 16
CONV_HALO = 8
ROWS = 256
TM = 256
ROUTER_ROWS = 8 + N_EXPERTS
META_LANES = LANES
VMEM_LIMIT = 56 * 1024 * 1024
BULK_DMA_PRIORITY = 1

_NT = (((1,), (1,)), ((), ()))


def _const_spec(shape):
    nd = len(shape)
    return pl.BlockSpec(shape, lambda *_: (0,) * nd, pipeline_mode=pl.Buffered(1))


def _mixer_kernel(x_ref, pool_in, conv_in, h_in, w_in, pool_w, pool_b, pool_scale, conv_w, conv_b,
                  gate_w, gate_b, lam, w_out, ln_g, ln_b, rt_hi, rt_lo, *rest, ns, l, pos0, alpha, aliased):
    if aliased:
        rest = rest[2:]
    (x1_ref, lg_ref, pool_o, conv_o, h_o,
     pool_ext, conv_ext, h_prev, gate_s, a_s, u_s, y_s) = rest
    t = pl.program_id(1)
    rows = ns * l

    @pl.when(t == 0)
    def _():
        pool_ext[:, 0:POOL_HALO - POOL_STATE, :] = jnp.zeros((ns, POOL_HALO - POOL_STATE, D_POOL), F32)
        pool_ext[:, POOL_HALO - POOL_STATE:POOL_HALO, :] = pool_in[...]
        conv_ext[:, 0:CONV_HALO - CONV_STATE, :] = jnp.zeros((ns, CONV_HALO - CONV_STATE, D_LRU), F32)
        conv_ext[:, CONV_HALO - CONV_STATE:CONV_HALO, :] = conv_in[...]
        h_prev[...] = h_in[...]

    xb = x_ref[...].reshape(rows, D_MODEL).astype(BF16)
    pool_ext[:, POOL_HALO:, :] = jnp.dot(
        xb, w_in[:, 0:D_POOL], preferred_element_type=F32).reshape(ns, l, D_POOL)
    conv_ext[:, CONV_HALO:, :] = jnp.dot(
        xb, w_in[:, D_POOL:D_POOL + D_LRU], preferred_element_type=F32).reshape(ns, l, D_LRU)
    gate_s[...] = jnp.dot(xb, w_in[:, D_POOL + D_LRU:], preferred_element_type=F32)

    pos = pos0 + t * l + lax.broadcasted_iota(I32, (ns, l, LANE_CHUNK), 1).reshape(rows, LANE_CHUNK)

    for g, w in enumerate(POOL_WINDOWS):
        cols = slice(g * POOL_GROUP, (g + 1) * POOL_GROUP)
        s = pool_ext[:, :, cols].reshape(ns * (POOL_HALO + l), POOL_GROUP)
        shift = 1
        while shift < w:
            s = s + pltpu.roll(s, shift, 0)
            shift *= 2
        win = s.reshape(ns, POOL_HALO + l, POOL_GROUP)[:, POOL_HALO:, :].reshape(rows, POOL_GROUP)
        u = pool_ext[:, POOL_HALO:, cols].reshape(rows, POOL_GROUP)
        if pos0 >= w - 1:
            inv = 1.0 / w
        else:
            inv = 1.0 / jnp.minimum(pos + 1, w).astype(F32)
        d = win * inv - u
        z = jnp.dot(d.astype(BF16), pool_w[g], preferred_element_type=F32) + pool_b[:, cols]
        y_s[:, cols] = (z * pool_scale[:, cols]).astype(BF16)

    lam_v = lam[...]
    softplus_neg = jnp.maximum(-lam_v, 0.0) + jnp.log1p(jnp.exp(-jnp.abs(lam_v)))
    log_a_scale = -LRU_C * softplus_neg
    sub = lax.broadcasted_iota(I32, (rows // SUBLANES, SUBLANES, LANE_CHUNK), 1)
    for c in range(D_LRU // LANE_CHUNK):
        cols = slice(c * LANE_CHUNK, (c + 1) * LANE_CHUNK)
        ce = conv_ext[:, :, cols].reshape(ns * (CONV_HALO + l), LANE_CHUNK)
        conv = conv_b[:, cols] + conv_w[CONV_WIDTH - 1:CONV_WIDTH, cols] * ce
        for k in range(1, CONV_WIDTH):
            conv = conv + conv_w[CONV_WIDTH - 1 - k:CONV_WIDTH - k, cols] * pltpu.roll(ce, k, 0)
        conv = conv.reshape(ns, CONV_HALO + l, LANE_CHUNK)[:, CONV_HALO:, :].reshape(rows, LANE_CHUNK)
        gz = jnp.dot(conv.astype(BF16), gate_w[c], preferred_element_type=F32)
        gz = gz + gate_b[:, 2 * c * LANE_CHUNK:2 * (c + 1) * LANE_CHUNK]
        r = jax.nn.sigmoid(gz[:, :LANE_CHUNK])
        i = jax.nn.sigmoid(gz[:, LANE_CHUNK:])
        log_a = r * log_a_scale[:, cols]
        a = jnp.exp(log_a)
        th = jnp.tanh(log_a)
        mult = jnp.sqrt(-2.0 * th / (1.0 - th))
        if pos0 == 0:
            mult = jnp.where(pos == 0, 1.0, mult)
        u = mult * (i * conv)
        a3 = a.reshape(rows // SUBLANES, SUBLANES, LANE_CHUNK)
        u3 = u.reshape(rows // SUBLANES, SUBLANES, LANE_CHUNK)
        for sh in (1, 2, 4):
            keep = sub >= sh
            a_sh = jnp.where(keep, pltpu.roll(a3, sh, 1), 1.0)
            u_sh = jnp.where(keep, pltpu.roll(u3, sh, 1), 0.0)
            u3 = a3 * u_sh + u3
            a3 = a3 * a_sh
        a_s[:, :, cols] = a3.reshape(ns, l, LANE_CHUNK)
        u_s[:, :, cols] = u3.reshape(ns, l, LANE_CHUNK)

    hp = h_prev[...]
    for gi in range(l // SUBLANES):
        sl = slice(gi * SUBLANES, (gi + 1) * SUBLANES)
        h = a_s[:, sl, :] * hp + u_s[:, sl, :]
        u_s[:, sl, :] = h
        hp = h[:, SUBLANES - 1:SUBLANES, :]
    h_prev[...] = hp

    for c in range(D_LRU // LANE_CHUNK):
        cols = slice(c * LANE_CHUNK, (c + 1) * LANE_CHUNK)
        h = u_s[:, :, cols].reshape(rows, LANE_CHUNK)
        y_s[:, D_POOL + c * LANE_CHUNK:D_POOL + (c + 1) * LANE_CHUNK] = (
            h * jax.nn.gelu(gate_s[:, cols])).astype(BF16)

    mix = jnp.dot(y_s[...], w_out[...], preferred_element_type=F32)
    xin = alpha * x_ref[...].reshape(rows, D_MODEL) + mix
    mu = jnp.mean(xin, axis=-1, keepdims=True)
    xc = xin - mu
    var = jnp.mean(xc * xc, axis=-1, keepdims=True)
    x1 = xc * lax.rsqrt(var + LN_EPS) * ln_g[...] + ln_b[...]
    x1_ref[...] = x1

    hi = x1.astype(BF16)
    lo = (x1 - hi.astype(F32)).astype(BF16)
    lg = lax.dot_general(rt_hi[...], hi, _NT, preferred_element_type=F32)
    lg = lg + lax.dot_general(rt_lo[...], hi, _NT, preferred_element_type=F32)
    lg = lg + lax.dot_general(rt_hi[...], lo, _NT, preferred_element_type=F32)
    lg_ref[...] = lg

    pool_o[...] = pool_ext[:, POOL_HALO + l - POOL_STATE:, :]
    conv_o[...] = conv_ext[:, CONV_HALO + l - CONV_STATE:, :]
    h_o[...] = hp
    pool_ext[:, 0:POOL_HALO, :] = pool_ext[:, l:l + POOL_HALO, :]
    conv_ext[:, 0:CONV_HALO, :] = conv_ext[:, l:l + CONV_HALO, :]


def _mixer_call(x, pool_in, conv_in, h_in, wts, x1_buf, lg_buf, *, ns, l, pos0, alpha, n_total, blk0):
    bsz, tlen, _ = x.shape
    nb, nt = bsz // ns, tlen // l
    rows = ns * l
    assert rows == ROWS and bsz % ns == 0 and tlen % l == 0 and l % SUBLANES == 0
    aliased = x1_buf is not None

    def row_blk(b, t):
        return (blk0 + b * nt + t, 0)

    in_specs = [
        pl.BlockSpec((ns, l, D_MODEL), lambda b, t: (b, t, 0)),
        pl.BlockSpec((ns, POOL_STATE, D_POOL), lambda b, t: (b, 0, 0)),
        pl.BlockSpec((ns, CONV_STATE, D_LRU), lambda b, t: (b, 0, 0)),
        pl.BlockSpec((ns, 1, D_LRU), lambda b, t: (b, 0, 0)),
    ] + [_const_spec(w.shape) for w in wts]
    args = [x, pool_in, conv_in, h_in, *wts]
    aliases = {}
    if aliased:
        in_specs += [pl.BlockSpec(memory_space=pl.ANY), pl.BlockSpec(memory_space=pl.ANY)]
        aliases = {len(args): 0, len(args) + 1: 1}
        args += [x1_buf, lg_buf]
    out_shape = (
        jax.ShapeDtypeStruct((n_total, D_MODEL), F32),
        jax.ShapeDtypeStruct((ROUTER_ROWS, n_total), F32),
        jax.ShapeDtypeStruct((bsz, POOL_STATE, D_POOL), F32),
        jax.ShapeDtypeStruct((bsz, CONV_STATE, D_LRU), F32),
        jax.ShapeDtypeStruct((bsz, 1, D_LRU), F32),
    )
    out_specs = (
        pl.BlockSpec((rows, D_MODEL), row_blk),
        pl.BlockSpec((ROUTER_ROWS, rows), lambda b, t: (0, blk0 + b * nt + t)),
        pl.BlockSpec((ns, POOL_STATE, D_POOL), lambda b, t: (b, 0, 0)),
        pl.BlockSpec((ns, CONV_STATE, D_LRU), lambda b, t: (b, 0, 0)),
        pl.BlockSpec((ns, 1, D_LRU), lambda b, t: (b, 0, 0)),
    )
    scratch = [
        pltpu.VMEM((ns, POOL_HALO + l, D_POOL), F32),
        pltpu.VMEM((ns, CONV_HALO + l, D_LRU), F32),
        pltpu.VMEM((ns, 1, D_LRU), F32),
        pltpu.VMEM((rows, D_LRU), F32),
        pltpu.VMEM((ns, l, D_LRU), F32),
        pltpu.VMEM((ns, l, D_LRU), F32),
        pltpu.VMEM((rows, D_MODEL), BF16),
    ]
    return pl.pallas_call(
        functools.partial(_mixer_kernel, ns=ns, l=l, pos0=pos0, alpha=alpha, aliased=aliased),
        grid=(nb, nt),
        in_specs=in_specs,
        out_specs=out_specs,
        out_shape=out_shape,
        scratch_shapes=scratch,
        input_output_aliases=aliases,
        compiler_params=pltpu.CompilerParams(
            dimension_semantics=("arbitrary", "arbitrary"), vmem_limit_bytes=VMEM_LIMIT),
        name="mixer_seq" if nt > 1 else "mixer_step",
    )(*args)


def _route_kernel(lg_ref, bias_ref, dest_ref, tw_ref, meta_ref, e_s, r_s, *, n):
    blk = 256
    nblk = n // blk
    neg_inf = -jnp.inf
    ridx8 = lax.broadcasted_iota(I32, (SUBLANES, blk), 0).astype(F32)
    eidx = lax.broadcasted_iota(I32, (N_EXPERTS, blk), 0).astype(F32)
    tri = (lax.broadcasted_iota(I32, (blk, blk), 0) < lax.broadcasted_iota(I32, (blk, blk), 1))
    tri = jnp.where(tri, 1.0, 0.0).astype(BF16)

    def first_idx(vals, m):
        return jnp.min(jnp.where(vals == m, ridx8, float(SUBLANES)), axis=0, keepdims=True)

    def pass1(j, base):
        ls = pl.ds(pl.multiple_of(j * blk, blk), blk)
        lg = lg_ref[0:SUBLANES, ls] + bias_ref[0:SUBLANES, :]
        lg = jnp.where(ridx8 < N_GROUPS, lg, neg_inf)
        m = jnp.max(lg, axis=0, keepdims=True)
        gi = first_idx(lg, m)
        pg_sel = 1.0 / jnp.sum(jnp.exp(lg - m), axis=0, keepdims=True)
        le = jnp.zeros((PER_GROUP, blk), F32)
        for g in range(N_GROUPS):
            rows = slice(SUBLANES + g * PER_GROUP, SUBLANES + (g + 1) * PER_GROUP)
            le = jnp.where(gi == float(g), lg_ref[rows, ls] + bias_ref[rows, :], le)
        m1 = jnp.max(le, axis=0, keepdims=True)
        i1 = first_idx(le, m1)
        le2 = jnp.where(ridx8 == i1, neg_inf, le)
        m2 = jnp.max(le2, axis=0, keepdims=True)
        i2 = first_idx(le2, m2)
        e21 = jnp.exp(m2 - m1)
        denom = 1.0 / (1.0 + e21)
        tw_ref[0:1, ls] = pg_sel * denom
        tw_ref[1:2, ls] = pg_sel * (e21 * denom)
        e1 = gi * float(PER_GROUP) + i1
        e2 = gi * float(PER_GROUP) + i2
        e_s[0:1, ls] = e1
        e_s[1:2, ls] = e2
        oh1 = jnp.where(eidx == e1, 1.0, 0.0)
        oh2 = jnp.where(eidx == e2, 1.0, 0.0)
        oh = oh1 + oh2
        before = base + jnp.dot(oh.astype(BF16), tri, preferred_element_type=F32)
        r_s[0:1, ls] = jnp.sum(oh1 * before, axis=0, keepdims=True)
        r_s[1:2, ls] = jnp.sum(oh2 * before, axis=0, keepdims=True)
        return base + jnp.sum(oh, axis=1, keepdims=True)

    cnt = lax.fori_loop(0, nblk, pass1, jnp.zeros((N_EXPERTS, blk), F32))

    ntile = jnp.floor((cnt + float(TM - 1)) * (1.0 / TM))
    lt = (lax.broadcasted_iota(I32, (N_EXPERTS, N_EXPERTS), 1) < lax.broadcasted_iota(I32, (N_EXPERTS, N_EXPERTS), 0))
    lt = jnp.where(lt, 1.0, 0.0).astype(BF16)
    tile0 = jnp.dot(lt, ntile.astype(BF16), preferred_element_type=F32)
    row0 = tile0 * float(TM)

    def pass2(j, carry):
        ls = pl.ds(pl.multiple_of(j * blk, blk), blk)
        for k in range(2):
            ohk = jnp.where(eidx == e_s[k:k + 1, ls], 1.0, 0.0)
            dest = r_s[k:k + 1, ls] + jnp.sum(ohk * row0, axis=0, keepdims=True)
            dest_ref[k:k + 1, ls] = dest.astype(I32)
        return carry

    lax.fori_loop(0, nblk, pass2, 0)

    ml = slice(0, META_LANES)
    total = jnp.sum(ntile[:, ml], axis=0, keepdims=True)
    tile_i = lax.broadcasted_iota(I32, (N_EXPERTS, META_LANES), 1).astype(F32)
    tile_c = jnp.minimum(tile_i, total - 1.0)
    tile_end = tile0[:, ml] + ntile[:, ml]
    te = jnp.sum(jnp.where(tile_end <= tile_c, 1.0, 0.0), axis=0, keepdims=True)
    meta_ref[...] = jnp.zeros((SUBLANES, META_LANES), I32)
    meta_ref[0:1, :] = te.astype(I32)
    meta_ref[1:2, :] = total.astype(I32)
    later = (eidx[:, ml] > te) & (ntile[:, ml] > 0.0)
    nxt = jnp.min(jnp.where(later, eidx[:, ml], float(N_EXPERTS)), axis=0, keepdims=True)
    meta_ref[2:3, :] = nxt.astype(I32)


def _route_call(lg, bias, n):
    return pl.pallas_call(
        functools.partial(_route_kernel, n=n),
        out_shape=(
            jax.ShapeDtypeStruct((2, n), I32),
            jax.ShapeDtypeStruct((2, n), F32),
            jax.ShapeDtypeStruct((SUBLANES, META_LANES), I32),
        ),
        scratch_shapes=[pltpu.VMEM((2, n), F32), pltpu.VMEM((2, n), F32)],
        compiler_params=pltpu.CompilerParams(vmem_limit_bytes=VMEM_LIMIT),
        name="route",
    )(lg, bias)


def _inverse_kernel(dest_ref, init_hbm, src_ref, sem, *, n):
    init = pltpu.make_async_copy(init_hbm, src_ref, sem)
    init.start()
    init.wait()

    def fill(t, c):
        src_ref[dest_ref[t]] = t
        src_ref[dest_ref[n + t]] = t
        return c

    lax.fori_loop(0, n, fill, 0, unroll=8)


def _inverse_call(dest_flat, n, n_slots):
    return pl.pallas_call(
        functools.partial(_inverse_kernel, n=n),
        in_specs=[pl.BlockSpec(memory_space=pltpu.SMEM), pl.BlockSpec(memory_space=pl.ANY)],
        out_specs=pl.BlockSpec(memory_space=pltpu.SMEM),
        out_shape=jax.ShapeDtypeStruct((n_slots,), I32),
        scratch_shapes=[pltpu.SemaphoreType.DMA(())],
        name="inverse",
    )(dest_flat, jnp.arange(n_slots, dtype=I32) % n)


def _row_copy(src_hbm, row, dst_vmem, slot, sem):
    return pltpu.make_async_copy(src_hbm.at[pl.ds(row, 1), :], dst_vmem.at[pl.ds(slot, 1), :], sem)


def _tile_copy(src_hbm, dst_vmem, sem):
    return pltpu.make_async_copy(src_hbm.at[pl.ds(0, dst_vmem.shape[0]), :], dst_vmem, sem)


def _moe_kernel(te_ref, ne_ref, nt_ref, src_cur, src_nxt, x1_hbm, w1_hbm, w3_hbm, w2_hbm, o_ref,
                xbuf, wf1, wf3, wf2, w1b, w3b, w2b, wslot, gsem, wsem):
    i = pl.program_id(0)
    nt = nt_ref[0]

    def gather_start(src_ref, slot):
        for r in range(TM):
            _row_copy(x1_hbm, src_ref[0, 0, r], xbuf.at[slot], r, gsem.at[slot]).start()

    def weight_copies(e, slot):
        return (pltpu.make_async_copy(w1_hbm.at[e], wf1.at[slot], wsem.at[slot, 0]),
                pltpu.make_async_copy(w3_hbm.at[e], wf3.at[slot], wsem.at[slot, 1]),
                pltpu.make_async_copy(w2_hbm.at[e], wf2.at[slot], wsem.at[slot, 2]))

    @pl.when(i == 0)
    def _():
        for cp in weight_copies(te_ref[0], 0):
            cp.start(priority=BULK_DMA_PRIORITY)
        wslot[0] = 0
        gather_start(src_cur, 0)

    @pl.when(i >= nt)
    def _():
        o_ref[...] = jnp.zeros_like(o_ref)

    @pl.when(i < nt)
    def _():
        slot = lax.rem(i, 2)
        _tile_copy(x1_hbm, xbuf.at[slot], gsem.at[slot]).wait()

        @pl.when((i == 0) | (te_ref[i] != te_ref[jnp.maximum(i - 1, 0)]))
        def _():
            cur = wslot[0]
            for cp in weight_copies(te_ref[i], cur):
                cp.wait()
            w1b[...] = wf1[cur].astype(BF16)
            w3b[...] = wf3[cur].astype(BF16)
            w2b[...] = wf2[cur].astype(BF16)

            @pl.when(ne_ref[i] < N_EXPERTS)
            def _():
                for cp in weight_copies(ne_ref[i], 1 - cur):
                    cp.start(priority=BULK_DMA_PRIORITY)

            wslot[0] = 1 - cur

        xb = xbuf[slot].astype(BF16)
        gather_start(src_nxt, 1 - slot)
        h1 = jnp.dot(xb, w1b[...], preferred_element_type=F32)
        h3 = jnp.dot(xb, w3b[...], preferred_element_type=F32)
        h = (jax.nn.silu(h1) * h3).astype(BF16)
        o_ref[...] = jnp.dot(h, w2b[...], preferred_element_type=F32)

        @pl.when(i + 1 >= nt)
        def _():
            _tile_copy(x1_hbm, xbuf.at[1 - slot], gsem.at[1 - slot]).wait()


def _moe_call(te, ne, ntiles, src3, x1, w1, w3, w2, max_tiles):
    grid_spec = pltpu.PrefetchScalarGridSpec(
        num_scalar_prefetch=3,
        grid=(max_tiles,),
        in_specs=[
            pl.BlockSpec((1, 1, TM), lambda i, te, ne, nt: (i, 0, 0), memory_space=pltpu.SMEM),
            pl.BlockSpec((1, 1, TM), lambda i, te, ne, nt: (jnp.minimum(i + 1, max_tiles - 1), 0, 0),
                         memory_space=pltpu.SMEM),
            pl.BlockSpec(memory_space=pl.ANY),
            pl.BlockSpec(memory_space=pl.ANY),
            pl.BlockSpec(memory_space=pl.ANY),
            pl.BlockSpec(memory_space=pl.ANY),
        ],
        out_specs=pl.BlockSpec((TM, D_MODEL), lambda i, te, ne, nt: (jnp.minimum(i, nt[0]), 0)),
        scratch_shapes=[
            pltpu.VMEM((2, TM, D_MODEL), F32),
            pltpu.VMEM((2, D_MODEL, D_EXPERT), F32),
            pltpu.VMEM((2, D_MODEL, D_EXPERT), F32),
            pltpu.VMEM((2, D_EXPERT, D_MODEL), F32),
            pltpu.VMEM((D_MODEL, D_EXPERT), BF16),
            pltpu.VMEM((D_MODEL, D_EXPERT), BF16),
            pltpu.VMEM((D_EXPERT, D_MODEL), BF16),
            pltpu.SMEM((1,), I32),
            pltpu.SemaphoreType.DMA((2,)),
            pltpu.SemaphoreType.DMA((2, 3)),
        ],
    )
    return pl.pallas_call(
        _moe_kernel,
        grid_spec=grid_spec,
        out_shape=jax.ShapeDtypeStruct(((max_tiles + 1) * TM, D_MODEL), F32),
        compiler_params=pltpu.CompilerParams(dimension_semantics=("arbitrary",), vmem_limit_bytes=VMEM_LIMIT),
        name="moe",
    )(te, ne, ntiles, src3, src3, x1, w1, w3, w2)


def _combine_kernel(dest_cur, dest_nxt, x1_ref, tw_ref, ys_hbm, ln_g, ln_b, o_ref, buf, sem, *, alpha, nsteps):
    i = pl.program_id(0)
    slot = lax.rem(i, 2)

    def gather_start(dest_ref, s):
        for r in range(ROWS):
            for k in range(2):
                _row_copy(ys_hbm, dest_ref[0, k, r], buf.at[s, k], r, sem.at[s, k]).start(priority=k)

    def gather_wait(s):
        for k in range(2):
            _tile_copy(ys_hbm, buf.at[s, k], sem.at[s, k]).wait()

    @pl.when(i == 0)
    def _():
        gather_start(dest_cur, 0)

    gather_wait(slot)
    tw = tw_ref[...]
    moe = tw[:, 0:1] * buf[slot, 0] + tw[:, 1:2] * buf[slot, 1]
    gather_start(dest_nxt, 1 - slot)
    xin = alpha * x1_ref[...] + moe
    mu = jnp.mean(xin, axis=-1, keepdims=True)
    xc = xin - mu
    var = jnp.mean(xc * xc, axis=-1, keepdims=True)
    y = xc * lax.rsqrt(var + LN_EPS) * ln_g[...] + ln_b[...]
    o_ref[...] = y.reshape(o_ref.shape)

    @pl.when(i == nsteps - 1)
    def _():
        gather_wait(1 - slot)


def _combine_call(dest3, x1, tw, ys, ln_g, ln_b, out_shape, blk, blk0, alpha):
    bsz, tlen, _ = out_shape
    nt = tlen // blk[1]
    nsteps = (bsz // blk[0]) * nt
    return pl.pallas_call(
        functools.partial(_combine_kernel, alpha=alpha, nsteps=nsteps),
        grid=(nsteps,),
        in_specs=[
            pl.BlockSpec((1, 2, ROWS), lambda i: (blk0 + i, 0, 0), memory_space=pltpu.SMEM),
            pl.BlockSpec((1, 2, ROWS), lambda i: (blk0 + jnp.minimum(i + 1, nsteps - 1), 0, 0),
                         memory_space=pltpu.SMEM),
            pl.BlockSpec((ROWS, D_MODEL), lambda i: (blk0 + i, 0)),
            pl.BlockSpec((ROWS, 2), lambda i: (blk0 + i, 0)),
            pl.BlockSpec(memory_space=pl.ANY),
            _const_spec(ln_g.shape),
            _const_spec(ln_b.shape),
        ],
        out_specs=pl.BlockSpec(blk, lambda i: (i // nt, i % nt, 0)),
        out_shape=jax.ShapeDtypeStruct(out_shape, F32),
        scratch_shapes=[
            pltpu.VMEM((2, 2, ROWS, D_MODEL), F32),
            pltpu.SemaphoreType.DMA((2, 2)),
        ],
        compiler_params=pltpu.CompilerParams(dimension_semantics=("arbitrary",), vmem_limit_bytes=VMEM_LIMIT),
        name="combine",
    )(dest3, dest3, x1, tw, ys, ln_g, ln_b)


def _block_diag_chunks(w):
    per = LANE_CHUNK // LRU_HEAD_DIM
    nchunk = LRU_HEADS // per
    w4 = w.reshape(nchunk, per, LRU_HEAD_DIM, LRU_HEAD_DIM)
    bd = jnp.einsum("cjio,jk->cjiko", w4, jnp.eye(per, dtype=w.dtype))
    return bd.reshape(nchunk, LANE_CHUNK, LANE_CHUNK)


def _layer(yp, ys, pool_s, conv_s, h_s, lw, alpha, past_len):
    (w_in, pool_w, pool_b, pool_scale, conv_w, conv_b, rg_w, rg_b, ig_w, ig_b, lru_lambda, w_out,
     ln1_g, ln1_b, rgw, rgb, rew, reb, w1, w3, w2, ln2_g, ln2_b) = lw
    bp, tp, _ = yp.shape
    bs, ts, _ = ys.shape
    n_p, n_s = bp * tp, bs * ts
    n = n_p + n_s
    assert n % ROWS == 0 and n_p % ROWS == 0
    max_tiles = (2 * n) // TM + N_EXPERTS
    assert max_tiles <= META_LANES

    nchunk = D_LRU // LANE_CHUNK
    gate_w = jnp.concatenate([_block_diag_chunks(rg_w), _block_diag_chunks(ig_w)], axis=-1).astype(BF16)
    gate_b = jnp.concatenate(
        [rg_b.reshape(nchunk, LANE_CHUNK), ig_b.reshape(nchunk, LANE_CHUNK)], axis=-1).reshape(1, 2 * D_LRU)
    rt = jnp.concatenate(
        [rgw.T, jnp.zeros((SUBLANES - N_GROUPS, D_MODEL), F32),
         jnp.transpose(rew, (0, 2, 1)).reshape(N_EXPERTS, D_MODEL)], axis=0)
    rt_hi = rt.astype(BF16)
    rt_lo = (rt - rt_hi.astype(F32)).astype(BF16)
    r_bias = jnp.concatenate([rgb, jnp.zeros((SUBLANES - N_GROUPS,), F32), reb.reshape(N_EXPERTS)]).reshape(ROUTER_ROWS, 1)
    wts = (
        w_in.astype(BF16), pool_w.astype(BF16), pool_b.reshape(1, D_POOL), pool_scale.reshape(1, D_POOL),
        conv_w, conv_b.reshape(1, D_LRU), gate_w, gate_b, lru_lambda.reshape(1, D_LRU), w_out.astype(BF16),
        ln1_g.reshape(1, D_MODEL), ln1_b.reshape(1, D_MODEL), rt_hi, rt_lo,
    )

    x1, lg, pool_p, conv_p, h_p = _mixer_call(
        yp, jnp.zeros((bp, POOL_STATE, D_POOL), F32), jnp.zeros((bp, CONV_STATE, D_LRU), F32),
        jnp.zeros((bp, 1, D_LRU), F32), wts, None, None,
        ns=1, l=ROWS, pos0=0, alpha=alpha, n_total=n, blk0=0)
    x1, lg, pool_n, conv_n, h_n = _mixer_call(
        ys, pool_s, conv_s, h_s.reshape(bs, 1, D_LRU), wts, x1, lg,
        ns=ROWS // ts, l=ts, pos0=past_len, alpha=alpha, n_total=n, blk0=n_p // ROWS)

    dest, tw, meta = _route_call(lg, r_bias, n)
    src = _inverse_call(dest.reshape(2 * n), n, max_tiles * TM)
    ysort = _moe_call(meta[0], meta[2], meta[1, 0:1], src.reshape(max_tiles, 1, TM), x1, w1, w3, w2, max_tiles)

    dest3 = jnp.transpose(dest.reshape(2, n // ROWS, ROWS), (1, 0, 2))
    tw_t = tw.T
    g2, b2 = ln2_g.reshape(1, D_MODEL), ln2_b.reshape(1, D_MODEL)
    out_p = _combine_call(dest3, x1, tw_t, ysort, g2, b2, (bp, tp, D_MODEL), (1, ROWS, D_MODEL), 0, alpha)
    out_s = _combine_call(dest3, x1, tw_t, ysort, g2, b2, (bs, ts, D_MODEL), (ROWS // ts, ts, D_MODEL),
                          n_p // ROWS, alpha)
    states = (pool_p, conv_p, h_p.reshape(bp, D_LRU), pool_n, conv_n, h_n.reshape(bs, D_LRU))
    return out_p, out_s, states


def kernel(x_prompt, x_sample, state_pool, state_conv, state_h, w_in, pool_w, pool_b, pool_scale, conv_w, conv_b, rg_w, rg_b, ig_w, ig_b, lru_lambda, w_out, ln1_g, ln1_b, router_group_w, router_group_b, router_expert_w, router_expert_b, expert_w1, expert_w3, expert_w2, ln2_g, ln2_b):
    depth = w_in.shape[0]
    alpha = (2.0 * depth) ** 0.25
    past_len = 16384
    layer_weights = (w_in, pool_w, pool_b, pool_scale, conv_w, conv_b, rg_w, rg_b, ig_w, ig_b, lru_lambda, w_out,
                     ln1_g, ln1_b, router_group_w, router_group_b, router_expert_w, router_expert_b,
                     expert_w1, expert_w3, expert_w2, ln2_g, ln2_b)
    yp, ys = x_prompt, x_sample
    outs = [[] for _ in range(6)]
    for layer in range(depth):
        lw = tuple(w[layer] for w in layer_weights)
        yp, ys, states = _layer(yp, ys, state_pool[layer], state_conv[layer], state_h[layer], lw, alpha, past_len)
        for acc, s in zip(outs, states):
            acc.append(s)
    return (yp, ys) + tuple(jnp.stack(o, axis=0) for o in outs)
```

```python
import functools

import jax
import jax.numpy as jnp
from jax import lax
from jax.experimental import pallas as pl
from jax.experimental.pallas import tpu as pltpu

F32 = jnp.float32
BF16 = jnp.bfloat16
I32 = jnp.int32

D_MODEL = 2048
D_POOL = 1024
D_LRU = 1024
POOL_WINDOWS = (2, 4, 8, 16)
POOL_GROUP = D_POOL // len(POOL_WINDOWS)
POOL_STATE = max(POOL_WINDOWS) - 1
CONV_WIDTH = 4
CONV_STATE = CONV_WIDTH - 1
LRU_HEADS = 16
LRU_HEAD_DIM = D_LRU // LRU_HEADS
LRU_C = 8.0
N_GROUPS = 4
PER_GROUP = 8
N_EXPERTS = N_GROUPS * PER_GROUP
D_EXPERT = D_MODEL // 4
LN_EPS = 1e-5

SUBLANES = 8
LANES = 128
LANE_CHUNK = 256
POOL_HALO = 16
CONV_HALO = 8
ROWS = 256
TM = 256
ROUTER_ROWS = 8 + N_EXPERTS
META_LANES = LANES
VMEM_LIMIT = 56 * 1024 * 1024
BULK_DMA_PRIORITY = 1

_NT = (((1,), (1,)), ((), ()))


def _const_spec(shape):
    nd = len(shape)
    return pl.BlockSpec(shape, lambda *_: (0,) * nd, pipeline_mode=pl.Buffered(1))


def _mixer_kernel(x_ref, pool_in, conv_in, h_in, w_in, pool_w, pool_b, pool_scale, conv_w, conv_b,
                  gate_w, gate_b, lam, w_out, ln_g, ln_b, rt_hi, rt_lo, *rest, ns, l, pos0, alpha, aliased):
    if aliased:
        rest = rest[2:]
    (x1_ref, lg_ref, pool_o, conv_o, h_o,
     pool_ext, conv_ext, h_prev, gate_s, a_s, u_s, y_s) = rest
    t = pl.program_id(1)
    rows = ns * l

    @pl.when(t == 0)
    def _():
        pool_ext[:, 0:POOL_HALO - POOL_STATE, :] = jnp.zeros((ns, POOL_HALO - POOL_STATE, D_POOL), F32)
        pool_ext[:, POOL_HALO - POOL_STATE:POOL_HALO, :] = pool_in[...]
        conv_ext[:, 0:CONV_HALO - CONV_STATE, :] = jnp.zeros((ns, CONV_HALO - CONV_STATE, D_LRU), F32)
        conv_ext[:, CONV_HALO - CONV_STATE:CONV_HALO, :] = conv_in[...]
        h_prev[...] = h_in[...]

    xb = x_ref[...].reshape(rows, D_MODEL).astype(BF16)
    pool_ext[:, POOL_HALO:, :] = jnp.dot(
        xb, w_in[:, 0:D_POOL], preferred_element_type=F32).reshape(ns, l, D_POOL)
    conv_ext[:, CONV_HALO:, :] = jnp.dot(
        xb, w_in[:, D_POOL:D_POOL + D_LRU], preferred_element_type=F32).reshape(ns, l, D_LRU)
    gate_s[...] = jnp.dot(xb, w_in[:, D_POOL + D_LRU:], preferred_element_type=F32)

    pos = pos0 + t * l + lax.broadcasted_iota(I32, (ns, l, LANE_CHUNK), 1).reshape(rows, LANE_CHUNK)

    for g, w in enumerate(POOL_WINDOWS):
        cols = slice(g * POOL_GROUP, (g + 1) * POOL_GROUP)
        s = pool_ext[:, :, cols].reshape(ns * (POOL_HALO + l), POOL_GROUP)
        shift = 1
        while shift < w:
            s = s + pltpu.roll(s, shift, 0)
            shift *= 2
        win = s.reshape(ns, POOL_HALO + l, POOL_GROUP)[:, POOL_HALO:, :].reshape(rows, POOL_GROUP)
        u = pool_ext[:, POOL_HALO:, cols].reshape(rows, POOL_GROUP)
        if pos0 >= w - 1:
            inv = 1.0 / w
        else:
            inv = 1.0 / jnp.minimum(pos + 1, w).astype(F32)
        d = win * inv - u
        z = jnp.dot(d.astype(BF16), pool_w[g], preferred_element_type=F32) + pool_b[:, cols]
        y_s[:, cols] = (z * pool_scale[:, cols]).astype(BF16)

    lam_v = lam[...]
    softplus_neg = jnp.maximum(-lam_v, 0.0) + jnp.log1p(jnp.exp(-jnp.abs(lam_v)))
    log_a_scale = -LRU_C * softplus_neg
    sub = lax.broadcasted_iota(I32, (rows // SUBLANES, SUBLANES, LANE_CHUNK), 1)
    for c in range(D_LRU // LANE_CHUNK):
        cols = slice(c * LANE_CHUNK, (c + 1) * LANE_CHUNK)
        ce = conv_ext[:, :, cols].reshape(ns * (CONV_HALO + l), LANE_CHUNK)
        conv = conv_b[:, cols] + conv_w[CONV_WIDTH - 1:CONV_WIDTH, cols] * ce
        for k in range(1, CONV_WIDTH):
            conv = conv + conv_w[CONV_WIDTH - 1 - k:CONV_WIDTH - k, cols] * pltpu.roll(ce, k, 0)
        conv = conv.reshape(ns, CONV_HALO + l, LANE_CHUNK)[:, CONV_HALO:, :].reshape(rows, LANE_CHUNK)
        gz = jnp.dot(conv.astype(BF16), gate_w[c], preferred_element_type=F32)
        gz = gz + gate_b[:, 2 * c * LANE_CHUNK:2 * (c + 1) * LANE_CHUNK]
        r = jax.nn.sigmoid(gz[:, :LANE_CHUNK])
        i = jax.nn.sigmoid(gz[:, LANE_CHUNK:])
        log_a = r * log_a_scale[:, cols]
        a = jnp.exp(log_a)
        th = jnp.tanh(log_a)
        mult = jnp.sqrt(-2.0 * th / (1.0 - th))
        if pos0 == 0:
            mult = jnp.where(pos == 0, 1.0, mult)
        u = mult * (i * conv)
        a3 = a.reshape(rows // SUBLANES, SUBLANES, LANE_CHUNK)
        u3 = u.reshape(rows // SUBLANES, SUBLANES, LANE_CHUNK)
        for sh in (1, 2, 4):
            keep = sub >= sh
            a_sh = jnp.where(keep, pltpu.roll(a3, sh, 1), 1.0)
            u_sh = jnp.where(keep, pltpu.roll(u3, sh, 1), 0.0)
            u3 = a3 * u_sh + u3
            a3 = a3 * a_sh
        a_s[:, :, cols] = a3.reshape(ns, l, LANE_CHUNK)
        u_s[:, :, cols] = u3.reshape(ns, l, LANE_CHUNK)

    hp = h_prev[...]
    for gi in range(l // SUBLANES):
        sl = slice(gi * SUBLANES, (gi + 1) * SUBLANES)
        h = a_s[:, sl, :] * hp + u_s[:, sl, :]
        u_s[:, sl, :] = h
        hp = h[:, SUBLANES - 1:SUBLANES, :]
    h_prev[...] = hp

    for c in range(D_LRU // LANE_CHUNK):
        cols = slice(c * LANE_CHUNK, (c + 1) * LANE_CHUNK)
        h = u_s[:, :, cols].reshape(rows, LANE_CHUNK)
        y_s[:, D_POOL + c * LANE_CHUNK:D_POOL + (c + 1) * LANE_CHUNK] = (
            h * jax.nn.gelu(gate_s[:, cols])).astype(BF16)

    mix = jnp.dot(y_s[...], w_out[...], preferred_element_type=F32)
    xin = alpha * x_ref[...].reshape(rows, D_MODEL) + mix
    mu = jnp.mean(xin, axis=-1, keepdims=True)
    xc = xin - mu
    var = jnp.mean(xc * xc, axis=-1, keepdims=True)
    x1 = xc * lax.rsqrt(var + LN_EPS) * ln_g[...] + ln_b[...]
    x1_ref[...] = x1

    hi = x1.astype(BF16)
    lo = (x1 - hi.astype(F32)).astype(BF16)
    lg = lax.dot_general(rt_hi[...], hi, _NT, preferred_element_type=F32)
    lg = lg + lax.dot_general(rt_lo[...], hi, _NT, preferred_element_type=F32)
    lg = lg + lax.dot_general(rt_hi[...], lo, _NT, preferred_element_type=F32)
    lg_ref[...] = lg

    pool_o[...] = pool_ext[:, POOL_HALO + l - POOL_STATE:, :]
    conv_o[...] = conv_ext[:, CONV_HALO + l - CONV_STATE:, :]
    h_o[...] = hp
    pool_ext[:, 0:POOL_HALO, :] = pool_ext[:, l:l + POOL_HALO, :]
    conv_ext[:, 0:CONV_HALO, :] = conv_ext[:, l:l + CONV_HALO, :]


def _mixer_call(x, pool_in, conv_in, h_in, wts, x1_buf, lg_buf, *, ns, l, pos0, alpha, n_total, blk0):
    bsz, tlen, _ = x.shape
    nb, nt = bsz // ns, tlen // l
    rows = ns * l
    assert rows == ROWS and bsz % ns == 0 and tlen % l == 0 and l % SUBLANES == 0
    aliased = x1_buf is not None

    def row_blk(b, t):
        return (blk0 + b * nt + t, 0)

    in_specs = [
        pl.BlockSpec((ns, l, D_MODEL), lambda b, t: (b, t, 0)),
        pl.BlockSpec((ns, POOL_STATE, D_POOL), lambda b, t: (b, 0, 0)),
        pl.BlockSpec((ns, CONV_STATE, D_LRU), lambda b, t: (b, 0, 0)),
        pl.BlockSpec((ns, 1, D_LRU), lambda b, t: (b, 0, 0)),
    ] + [_const_spec(w.shape) for w in wts]
    args = [x, pool_in, conv_in, h_in, *wts]
    aliases = {}
    if aliased:
        in_specs += [pl.BlockSpec(memory_space=pl.ANY), pl.BlockSpec(memory_space=pl.ANY)]
        aliases = {len(args): 0, len(args) + 1: 1}
        args += [x1_buf, lg_buf]
    out_shape = (
        jax.ShapeDtypeStruct((n_total, D_MODEL), F32),
        jax.ShapeDtypeStruct((ROUTER_ROWS, n_total), F32),
        jax.ShapeDtypeStruct((bsz, POOL_STATE, D_POOL), F32),
        jax.ShapeDtypeStruct((bsz, CONV_STATE, D_LRU), F32),
        jax.ShapeDtypeStruct((bsz, 1, D_LRU), F32),
    )
    out_specs = (
        pl.BlockSpec((rows, D_MODEL), row_blk),
        pl.BlockSpec((ROUTER_ROWS, rows), lambda b, t: (0, blk0 + b * nt + t)),
        pl.BlockSpec((ns, POOL_STATE, D_POOL), lambda b, t: (b, 0, 0)),
        pl.BlockSpec((ns, CONV_STATE, D_LRU), lambda b, t: (b, 0, 0)),
        pl.BlockSpec((ns, 1, D_LRU), lambda b, t: (b, 0, 0)),
    )
    scratch = [
        pltpu.VMEM((ns, POOL_HALO + l, D_POOL), F32),
        pltpu.VMEM((ns, CONV_HALO + l, D_LRU), F32),
        pltpu.VMEM((ns, 1, D_LRU), F32),
        pltpu.VMEM((rows, D_LRU), F32),
        pltpu.VMEM((ns, l, D_LRU), F32),
        pltpu.VMEM((ns, l, D_LRU), F32),
        pltpu.VMEM((rows, D_MODEL), BF16),
    ]
    return pl.pallas_call(
        functools.partial(_mixer_kernel, ns=ns, l=l, pos0=pos0, alpha=alpha, aliased=aliased),
        grid=(nb, nt),
        in_specs=in_specs,
        out_specs=out_specs,
        out_shape=out_shape,
        scratch_shapes=scratch,
        input_output_aliases=aliases,
        compiler_params=pltpu.CompilerParams(
            dimension_semantics=("arbitrary", "arbitrary"), vmem_limit_bytes=VMEM_LIMIT),
        name="mixer_seq" if nt > 1 else "mixer_step",
    )(*args)


def _route_kernel(lg_ref, bias_ref, dest_ref, tw_ref, meta_ref, e_s, r_s, *, n):
    blk = 256
    nblk = n // blk
    neg_inf = -jnp.inf
    ridx8 = lax.broadcasted_iota(I32, (SUBLANES, blk), 0).astype(F32)
    eidx = lax.broadcasted_iota(I32, (N_EXPERTS, blk), 0).astype(F32)
    tri = (lax.broadcasted_iota(I32, (blk, blk), 0) < lax.broadcasted_iota(I32, (blk, blk), 1))
    tri = jnp.where(tri, 1.0, 0.0).astype(BF16)

    def first_idx(vals, m):
        return jnp.min(jnp.where(vals == m, ridx8, float(SUBLANES)), axis=0, keepdims=True)

    def pass1(j, base):
        ls = pl.ds(pl.multiple_of(j * blk, blk), blk)
        lg = lg_ref[0:SUBLANES, ls] + bias_ref[0:SUBLANES, :]
        lg = jnp.where(ridx8 < N_GROUPS, lg, neg_inf)
        m = jnp.max(lg, axis=0, keepdims=True)
        gi = first_idx(lg, m)
        pg_sel = 1.0 / jnp.sum(jnp.exp(lg - m), axis=0, keepdims=True)
        le = jnp.zeros((PER_GROUP, blk), F32)
        for g in range(N_GROUPS):
            rows = slice(SUBLANES + g * PER_GROUP, SUBLANES + (g + 1) * PER_GROUP)
            le = jnp.where(gi == float(g), lg_ref[rows, ls] + bias_ref[rows, :], le)
        m1 = jnp.max(le, axis=0, keepdims=True)
        i1 = first_idx(le, m1)
        le2 = jnp.where(ridx8 == i1, neg_inf, le)
        m2 = jnp.max(le2, axis=0, keepdims=True)
        i2 = first_idx(le2, m2)
        e21 = jnp.exp(m2 - m1)
        denom = 1.0 / (1.0 + e21)
        tw_ref[0:1, ls] = pg_sel * denom
        tw_ref[1:2, ls] = pg_sel * (e21 * denom)
        e1 = gi * float(PER_GROUP) + i1
        e2 = gi * float(PER_GROUP) + i2
        e_s[0:1, ls] = e1
        e_s[1:2, ls] = e2
        oh1 = jnp.where(eidx == e1, 1.0, 0.0)
        oh2 = jnp.where(eidx == e2, 1.0, 0.0)
        oh = oh1 + oh2
        before = base + jnp.dot(oh.astype(BF16), tri, preferred_element_type=F32)
        r_s[0:1, ls] = jnp.sum(oh1 * before, axis=0, keepdims=True)
        r_s[1:2, ls] = jnp.sum(oh2 * before, axis=0, keepdims=True)
        return base + jnp.sum(oh, axis=1, keepdims=True)

    cnt = lax.fori_loop(0, nblk, pass1, jnp.zeros((N_EXPERTS, blk), F32))

    ntile = jnp.floor((cnt + float(TM - 1)) * (1.0 / TM))
    lt = (lax.broadcasted_iota(I32, (N_EXPERTS, N_EXPERTS), 1) < lax.broadcasted_iota(I32, (N_EXPERTS, N_EXPERTS), 0))
    lt = jnp.where(lt, 1.0, 0.0).astype(BF16)
    tile0 = jnp.dot(lt, ntile.astype(BF16), preferred_element_type=F32)
    row0 = tile0 * float(TM)

    def pass2(j, carry):
        ls = pl.ds(pl.multiple_of(j * blk, blk), blk)
        for k in range(2):
            ohk = jnp.where(eidx == e_s[k:k + 1, ls], 1.0, 0.0)
            dest = r_s[k:k + 1, ls] + jnp.sum(ohk * row0, axis=0, keepdims=True)
            dest_ref[k:k + 1, ls] = dest.astype(I32)
        return carry

    lax.fori_loop(0, nblk, pass2, 0)

    ml = slice(0, META_LANES)
    total = jnp.sum(ntile[:, ml], axis=0, keepdims=True)
    tile_i = lax.broadcasted_iota(I32, (N_EXPERTS, META_LANES), 1).astype(F32)
    tile_c = jnp.minimum(tile_i, total - 1.0)
    tile_end = tile0[:, ml] + ntile[:, ml]
    te = jnp.sum(jnp.where(tile_end <= tile_c, 1.0, 0.0), axis=0, keepdims=True)
    meta_ref[...] = jnp.zeros((SUBLANES, META_LANES), I32)
    meta_ref[0:1, :] = te.astype(I32)
    meta_ref[1:2, :] = total.astype(I32)
    later = (eidx[:, ml] > te) & (ntile[:, ml] > 0.0)
    nxt = jnp.min(jnp.where(later, eidx[:, ml], float(N_EXPERTS)), axis=0, keepdims=True)
    meta_ref[2:3, :] = nxt.astype(I32)


def _route_call(lg, bias, n):
    return pl.pallas_call(
        functools.partial(_route_kernel, n=n),
        out_shape=(
            jax.ShapeDtypeStruct((2, n), I32),
            jax.ShapeDtypeStruct((2, n), F32),
            jax.ShapeDtypeStruct((SUBLANES, META_LANES), I32),
        ),
        scratch_shapes=[pltpu.VMEM((2, n), F32), pltpu.VMEM((2, n), F32)],
        compiler_params=pltpu.CompilerParams(vmem_limit_bytes=VMEM_LIMIT),
        name="route",
    )(lg, bias)


def _inverse_kernel(dest_ref, init_hbm, src_ref, sem, *, n):
    init = pltpu.make_async_copy(init_hbm, src_ref, sem)
    init.start()
    init.wait()

    def fill(t, c):
        src_ref[dest_ref[t]] = t
        src_ref[dest_ref[n + t]] = t
        return c

    lax.fori_loop(0, n, fill, 0, unroll=8)


def _inverse_call(dest_flat, n, n_slots):
    return pl.pallas_call(
        functools.partial(_inverse_kernel, n=n),
        in_specs=[pl.BlockSpec(memory_space=pltpu.SMEM), pl.BlockSpec(memory_space=pl.ANY)],
        out_specs=pl.BlockSpec(memory_space=pltpu.SMEM),
        out_shape=jax.ShapeDtypeStruct((n_slots,), I32),
        scratch_shapes=[pltpu.SemaphoreType.DMA(())],
        name="inverse",
    )(dest_flat, jnp.arange(n_slots, dtype=I32) % n)


def _row_copy(src_hbm, row, dst_vmem, slot, sem):
    return pltpu.make_async_copy(src_hbm.at[pl.ds(row, 1), :], dst_vmem.at[pl.ds(slot, 1), :], sem)


def _tile_copy(src_hbm, dst_vmem, sem):
    return pltpu.make_async_copy(src_hbm.at[pl.ds(0, dst_vmem.shape[0]), :], dst_vmem, sem)


def _moe_kernel(te_ref, ne_ref, nt_ref, src_cur, src_nxt, x1_hbm, w1_hbm, w3_hbm, w2_hbm, o_ref,
                xbuf, wf1, wf3, wf2, w1b, w3b, w2b, wset, gsem, wsem):
    i = pl.program_id(0)
    nt = nt_ref[0]

    def gather_start(src_ref, slot):
        for r in range(TM):
            _row_copy(x1_hbm, src_ref[0, 0, r], xbuf.at[slot], r, gsem.at[slot]).start()

    def weight_copies(e):
        return (pltpu.make_async_copy(w1_hbm.at[e], wf1, wsem.at[0]),
                pltpu.make_async_copy(w3_hbm.at[e], wf3, wsem.at[1]),
                pltpu.make_async_copy(w2_hbm.at[e], wf2, wsem.at[2]))

    def receive_weights(e, bset, e_after):
        for cp in weight_copies(e):
            cp.wait()
        w1b[bset] = wf1[...].astype(BF16)
        w3b[bset] = wf3[...].astype(BF16)
        w2b[bset] = wf2[...].astype(BF16)
        wset[0] = bset

        @pl.when(e_after < N_EXPERTS)
        def _():
            for cp in weight_copies(e_after):
                cp.start(priority=BULK_DMA_PRIORITY)

    @pl.when(i == 0)
    def _():
        for cp in weight_copies(te_ref[0]):
            cp.start(priority=BULK_DMA_PRIORITY)
        gather_start(src_cur, 0)
        receive_weights(te_ref[0], 0, ne_ref[0])

    @pl.when(i >= nt)
    def _():
        o_ref[...] = jnp.zeros_like(o_ref)

    @pl.when(i < nt)
    def _():
        slot = lax.rem(i, 2)
        bset = wset[0]
        _tile_copy(x1_hbm, xbuf.at[slot], gsem.at[slot]).wait()
        xb = xbuf[slot].astype(BF16)
        gather_start(src_nxt, 1 - slot)
        h1 = jnp.dot(xb, w1b[bset], preferred_element_type=F32)
        h3 = jnp.dot(xb, w3b[bset], preferred_element_type=F32)
        h = (jax.nn.silu(h1) * h3).astype(BF16)
        o_ref[...] = jnp.dot(h, w2b[bset], preferred_element_type=F32)

        @pl.when((i + 1 < nt) & (te_ref[i + 1] != te_ref[i]))
        def _():
            receive_weights(te_ref[i + 1], 1 - bset, ne_ref[i + 1])

        @pl.when(i + 1 >= nt)
        def _():
            _tile_copy(x1_hbm, xbuf.at[1 - slot], gsem.at[1 - slot]).wait()


def _moe_call(te, ne, ntiles, src3, x1, w1, w3, w2, max_tiles):
    grid_spec = pltpu.PrefetchScalarGridSpec(
        num_scalar_prefetch=3,
        grid=(max_tiles,),
        in_specs=[
            pl.BlockSpec((1, 1, TM), lambda i, te, ne, nt: (i, 0, 0), memory_space=pltpu.SMEM),
            pl.BlockSpec((1, 1, TM), lambda i, te, ne, nt: (jnp.minimum(i + 1, max_tiles - 1), 0, 0),
                         memory_space=pltpu.SMEM),
            pl.BlockSpec(memory_space=pl.ANY),
            pl.BlockSpec(memory_space=pl.ANY),
            pl.BlockSpec(memory_space=pl.ANY),
            pl.BlockSpec(memory_space=pl.ANY),
        ],
        out_specs=pl.BlockSpec((TM, D_MODEL), lambda i, te, ne, nt: (jnp.minimum(i, nt[0]), 0)),
        scratch_shapes=[
            pltpu.VMEM((2, TM, D_MODEL), F32),
            pltpu.VMEM((D_MODEL, D_EXPERT), F32),
            pltpu.VMEM((D_MODEL, D_EXPERT), F32),
            pltpu.VMEM((D_EXPERT, D_MODEL), F32),
            pltpu.VMEM((2, D_MODEL, D_EXPERT), BF16),
            pltpu.VMEM((2, D_MODEL, D_EXPERT), BF16),
            pltpu.VMEM((2, D_EXPERT, D_MODEL), BF16),
            pltpu.SMEM((1,), I32),
            pltpu.SemaphoreType.DMA((2,)),
            pltpu.SemaphoreType.DMA((3,)),
        ],
    )
    return pl.pallas_call(
        _moe_kernel,
        grid_spec=grid_spec,
        out_shape=jax.ShapeDtypeStruct(((max_tiles + 1) * TM, D_MODEL), F32),
        compiler_params=pltpu.CompilerParams(dimension_semantics=("arbitrary",), vmem_limit_bytes=VMEM_LIMIT),
        name="moe",
    )(te, ne, ntiles, src3, src3, x1, w1, w3, w2)


def _combine_kernel(dest_cur, dest_nxt, x1_ref, tw_ref, ys_hbm, ln_g, ln_b, o_ref, buf, sem, *, alpha, nsteps):
    i = pl.program_id(0)
    slot = lax.rem(i, 2)

    def gather_start(dest_ref, s):
        for r in range(ROWS):
            for k in range(2):
                _row_copy(ys_hbm, dest_ref[0, k, r], buf.at[s, k], r, sem.at[s, k]).start(priority=k)

    def gather_wait(s):
        for k in range(2):
            _tile_copy(ys_hbm, buf.at[s, k], sem.at[s, k]).wait()

    @pl.when(i == 0)
    def _():
        gather_start(dest_cur, 0)

    gather_wait(slot)
    tw = tw_ref[...]
    moe = tw[:, 0:1] * buf[slot, 0] + tw[:, 1:2] * buf[slot, 1]
    gather_start(dest_nxt, 1 - slot)
    xin = alpha * x1_ref[...] + moe
    mu = jnp.mean(xin, axis=-1, keepdims=True)
    xc = xin - mu
    var = jnp.mean(xc * xc, axis=-1, keepdims=True)
    y = xc * lax.rsqrt(var + LN_EPS) * ln_g[...] + ln_b[...]
    o_ref[...] = y.reshape(o_ref.shape)

    @pl.when(i == nsteps - 1)
    def _():
        gather_wait(1 - slot)


def _combine_call(dest3, x1, tw, ys, ln_g, ln_b, out_shape, blk, blk0, alpha):
    bsz, tlen, _ = out_shape
    nt = tlen // blk[1]
    nsteps = (bsz // blk[0]) * nt
    return pl.pallas_call(
        functools.partial(_combine_kernel, alpha=alpha, nsteps=nsteps),
        grid=(nsteps,),
        in_specs=[
            pl.BlockSpec((1, 2, ROWS), lambda i: (blk0 + i, 0, 0), memory_space=pltpu.SMEM),
            pl.BlockSpec((1, 2, ROWS), lambda i: (blk0 + jnp.minimum(i + 1, nsteps - 1), 0, 0),
                         memory_space=pltpu.SMEM),
            pl.BlockSpec((ROWS, D_MODEL), lambda i: (blk0 + i, 0)),
            pl.BlockSpec((ROWS, 2), lambda i: (blk0 + i, 0)),
            pl.BlockSpec(memory_space=pl.ANY),
            _const_spec(ln_g.shape),
            _const_spec(ln_b.shape),
        ],
        out_specs=pl.BlockSpec(blk, lambda i: (i // nt, i % nt, 0)),
        out_shape=jax.ShapeDtypeStruct(out_shape, F32),
        scratch_shapes=[
            pltpu.VMEM((2, 2, ROWS, D_MODEL), F32),
            pltpu.SemaphoreType.DMA((2, 2)),
        ],
        compiler_params=pltpu.CompilerParams(dimension_semantics=("arbitrary",), vmem_limit_bytes=VMEM_LIMIT),
        name="combine",
    )(dest3, dest3, x1, tw, ys, ln_g, ln_b)


def _block_diag_chunks(w):
    per = LANE_CHUNK // LRU_HEAD_DIM
    nchunk = LRU_HEADS // per
    w4 = w.reshape(nchunk, per, LRU_HEAD_DIM, LRU_HEAD_DIM)
    bd = jnp.einsum("cjio,jk->cjiko", w4, jnp.eye(per, dtype=w.dtype))
    return bd.reshape(nchunk, LANE_CHUNK, LANE_CHUNK)


def _layer(yp, ys, pool_s, conv_s, h_s, lw, alpha, past_len):
    (w_in, pool_w, pool_b, pool_scale, conv_w, conv_b, rg_w, rg_b, ig_w, ig_b, lru_lambda, w_out,
     ln1_g, ln1_b, rgw, rgb, rew, reb, w1, w3, w2, ln2_g, ln2_b) = lw
    bp, tp, _ = yp.shape
    bs, ts, _ = ys.shape
    n_p, n_s = bp * tp, bs * ts
    n = n_p + n_s
    assert n % ROWS == 0 and n_p % ROWS == 0
    max_tiles = (2 * n) // TM + N_EXPERTS
    assert max_tiles <= META_LANES

    nchunk = D_LRU // LANE_CHUNK
    gate_w = jnp.concatenate([_block_diag_chunks(rg_w), _block_diag_chunks(ig_w)], axis=-1).astype(BF16)
    gate_b = jnp.concatenate(
        [rg_b.reshape(nchunk, LANE_CHUNK), ig_b.reshape(nchunk, LANE_CHUNK)], axis=-1).reshape(1, 2 * D_LRU)
    rt = jnp.concatenate(
        [rgw.T, jnp.zeros((SUBLANES - N_GROUPS, D_MODEL), F32),
         jnp.transpose(rew, (0, 2, 1)).reshape(N_EXPERTS, D_MODEL)], axis=0)
    rt_hi = rt.astype(BF16)
    rt_lo = (rt - rt_hi.astype(F32)).astype(BF16)
    r_bias = jnp.concatenate([rgb, jnp.zeros((SUBLANES - N_GROUPS,), F32), reb.reshape(N_EXPERTS)]).reshape(ROUTER_ROWS, 1)
    wts = (
        w_in.astype(BF16), pool_w.astype(BF16), pool_b.reshape(1, D_POOL), pool_scale.reshape(1, D_POOL),
        conv_w, conv_b.reshape(1, D_LRU), gate_w, gate_b, lru_lambda.reshape(1, D_LRU), w_out.astype(BF16),
        ln1_g.reshape(1, D_MODEL), ln1_b.reshape(1, D_MODEL), rt_hi, rt_lo,
    )

    x1, lg, pool_p, conv_p, h_p = _mixer_call(
        yp, jnp.zeros((bp, POOL_STATE, D_POOL), F32), jnp.zeros((bp, CONV_STATE, D_LRU), F32),
        jnp.zeros((bp, 1, D_LRU), F32), wts, None, None,
        ns=1, l=ROWS, pos0=0, alpha=alpha, n_total=n, blk0=0)
    x1, lg, pool_n, conv_n, h_n = _mixer_call(
        ys, pool_s, conv_s, h_s.reshape(bs, 1, D_LRU), wts, x1, lg,
        ns=ROWS // ts, l=ts, pos0=past_len, alpha=alpha, n_total=n, blk0=n_p // ROWS)

    dest, tw, meta = _route_call(lg, r_bias, n)
    src = _inverse_call(dest.reshape(2 * n), n, max_tiles * TM)
    ysort = _moe_call(meta[0], meta[2], meta[1, 0:1], src.reshape(max_tiles, 1, TM), x1, w1, w3, w2, max_tiles)

    dest3 = jnp.transpose(dest.reshape(2, n // ROWS, ROWS), (1, 0, 2))
    tw_t = tw.T
    g2, b2 = ln2_g.reshape(1, D_MODEL), ln2_b.reshape(1, D_MODEL)
    out_p = _combine_call(dest3, x1, tw_t, ysort, g2, b2, (bp, tp, D_MODEL), (1, ROWS, D_MODEL), 0, alpha)
    out_s = _combine_call(dest3, x1, tw_t, ysort, g2, b2, (bs, ts, D_MODEL), (ROWS // ts, ts, D_MODEL),
                          n_p // ROWS, alpha)
    states = (pool_p, conv_p, h_p.reshape(bp, D_LRU), pool_n, conv_n, h_n.reshape(bs, D_LRU))
    return out_p, out_s, states


def kernel(x_prompt, x_sample, state_pool, state_conv, state_h, w_in, pool_w, pool_b, pool_scale, conv_w, conv_b, rg_w, rg_b, ig_w, ig_b, lru_lambda, w_out, ln1_g, ln1_b, router_group_w, router_group_b, router_expert_w, router_expert_b, expert_w1, expert_w3, expert_w2, ln2_g, ln2_b):
    depth = w_in.shape[0]
    alpha = (2.0 * depth) ** 0.25
    past_len = 16384
    layer_weights = (w_in, pool_w, pool_b, pool_scale, conv_w, conv_b, rg_w, rg_b, ig_w, ig_b, lru_lambda, w_out,
                     ln1_g, ln1_b, router_group_w, router_group_b, router_expert_w, router_expert_b,
                     expert_w1, expert_w3, expert_w2, ln2_g, ln2_b)
    yp, ys = x_prompt, x_sample
    outs = [[] for _ in range(6)]
    for layer in range(depth):
        lw = tuple(w[layer] for w in layer_weights)
        yp, ys, states = _layer(yp, ys, state_pool[layer], state_conv[layer], state_h[layer], lw, alpha, past_len)
        for acc, s in zip(outs, states):
            acc.append(s)
    return (yp, ys) + tuple(o[0][None] if depth == 1 else jnp.stack(o, axis=0) for o in outs)
```

```python
import functools

import jax
import jax.numpy as jnp
from jax import lax
from jax.experimental import pallas as pl
from jax.experimental.pallas import tpu as pltpu

F32 = jnp.float32
BF16 = jnp.bfloat16
I32 = jnp.int32

D_MODEL = 2048
D_POOL = 1024
D_LRU = 1024
POOL_WINDOWS = (2, 4, 8, 16)
POOL_GROUP = D_POOL // len(POOL_WINDOWS)
POOL_STATE = max(POOL_WINDOWS) - 1
CONV_WIDTH = 4
CONV_STATE = CONV_WIDTH - 1
LRU_HEADS = 16
LRU_HEAD_DIM = D_LRU // LRU_HEADS
LRU_C = 8.0
N_GROUPS = 4
PER_GROUP = 8
N_EXPERTS = N_GROUPS * PER_GROUP
D_EXPERT = D_MODEL // 4
LN_EPS = 1e-5

SUBLANES = 8
LANES = 128
LANE_CHUNK = 256
POOL_HALO = 16
CONV_HALO = 8
ROWS = 256
TM = 256
ROUTER_ROWS = 8 + N_EXPERTS
ROUTER_PAD = 48
META_LANES = LANES
VMEM_LIMIT = 56 * 1024 * 1024
BULK_DMA_PRIORITY = 1

_NT = (((1,), (1,)), ((), ()))


def _const_spec(shape):
    nd = len(shape)
    return pl.BlockSpec(shape, lambda *_: (0,) * nd, pipeline_mode=pl.Buffered(1))


def _mixer_kernel(x_ref, pool_in, conv_in, h_in, w_in, pool_w, pool_b, pool_scale, conv_w, conv_b,
                  gate_w, gate_b, lam, w_out, ln_g, ln_b, rt_hl, *rest, ns, l, pos0, alpha, aliased):
    if aliased:
        rest = rest[2:]
    (x1_ref, lg_ref, pool_o, conv_o, h_o,
     pool_ext, conv_ext, h_prev, gate_s, a_s, u_s, y_s) = rest
    t = pl.program_id(1)
    rows = ns * l

    @pl.when(t == 0)
    def _():
        pool_ext[:, 0:POOL_HALO - POOL_STATE, :] = jnp.zeros((ns, POOL_HALO - POOL_STATE, D_POOL), F32)
        pool_ext[:, POOL_HALO - POOL_STATE:POOL_HALO, :] = pool_in[...]
        conv_ext[:, 0:CONV_HALO - CONV_STATE, :] = jnp.zeros((ns, CONV_HALO - CONV_STATE, D_LRU), F32)
        conv_ext[:, CONV_HALO - CONV_STATE:CONV_HALO, :] = conv_in[...]
        h_prev[...] = h_in[...]

    xb = x_ref[...].reshape(rows, D_MODEL).astype(BF16)
    pool_ext[:, POOL_HALO:, :] = jnp.dot(
        xb, w_in[:, 0:D_POOL], preferred_element_type=F32).reshape(ns, l, D_POOL)
    conv_ext[:, CONV_HALO:, :] = jnp.dot(
        xb, w_in[:, D_POOL:D_POOL + D_LRU], preferred_element_type=F32).reshape(ns, l, D_LRU)
    gate_s[...] = jnp.dot(xb, w_in[:, D_POOL + D_LRU:], preferred_element_type=F32)

    pos = pos0 + t * l + lax.broadcasted_iota(I32, (ns, l, LANE_CHUNK), 1).reshape(rows, LANE_CHUNK)

    for g, w in enumerate(POOL_WINDOWS):
        cols = slice(g * POOL_GROUP, (g + 1) * POOL_GROUP)
        s = pool_ext[:, :, cols].reshape(ns * (POOL_HALO + l), POOL_GROUP)
        shift = 1
        while shift < w:
            s = s + pltpu.roll(s, shift, 0)
            shift *= 2
        win = s.reshape(ns, POOL_HALO + l, POOL_GROUP)[:, POOL_HALO:, :].reshape(rows, POOL_GROUP)
        u = pool_ext[:, POOL_HALO:, cols].reshape(rows, POOL_GROUP)
        if pos0 >= w - 1:
            inv = 1.0 / w
        else:
            inv = 1.0 / jnp.minimum(pos + 1, w).astype(F32)
        d = win * inv - u
        z = jnp.dot(d.astype(BF16), pool_w[g], preferred_element_type=F32) + pool_b[:, cols]
        y_s[:, cols] = (z * pool_scale[:, cols]).astype(BF16)

    lam_v = lam[...]
    softplus_neg = jnp.maximum(-lam_v, 0.0) + jnp.log1p(jnp.exp(-jnp.abs(lam_v)))
    log_a_scale = -LRU_C * softplus_neg
    sub = lax.broadcasted_iota(I32, (rows // SUBLANES, SUBLANES, LANE_CHUNK), 1)
    for c in range(D_LRU // LANE_CHUNK):
        cols = slice(c * LANE_CHUNK, (c + 1) * LANE_CHUNK)
        ce = conv_ext[:, :, cols].reshape(ns * (CONV_HALO + l), LANE_CHUNK)
        conv = conv_b[:, cols] + conv_w[CONV_WIDTH - 1:CONV_WIDTH, cols] * ce
        for k in range(1, CONV_WIDTH):
            conv = conv + conv_w[CONV_WIDTH - 1 - k:CONV_WIDTH - k, cols] * pltpu.roll(ce, k, 0)
        conv = conv.reshape(ns, CONV_HALO + l, LANE_CHUNK)[:, CONV_HALO:, :].reshape(rows, LANE_CHUNK)
        gz = jnp.dot(conv.astype(BF16), gate_w[c], preferred_element_type=F32)
        gz = gz + gate_b[:, 2 * c * LANE_CHUNK:2 * (c + 1) * LANE_CHUNK]
        r = jax.nn.sigmoid(gz[:, :LANE_CHUNK])
        i = jax.nn.sigmoid(gz[:, LANE_CHUNK:])
        log_a = r * log_a_scale[:, cols]
        a = jnp.exp(log_a)
        th = jnp.tanh(log_a)
        mult = jnp.sqrt(-2.0 * th / (1.0 - th))
        if pos0 == 0:
            mult = jnp.where(pos == 0, 1.0, mult)
        u = mult * (i * conv)
        a3 = a.reshape(rows // SUBLANES, SUBLANES, LANE_CHUNK)
        u3 = u.reshape(rows // SUBLANES, SUBLANES, LANE_CHUNK)
        for sh in (1, 2, 4):
            keep = sub >= sh
            a_sh = jnp.where(keep, pltpu.roll(a3, sh, 1), 1.0)
            u_sh = jnp.where(keep, pltpu.roll(u3, sh, 1), 0.0)
            u3 = a3 * u_sh + u3
            a3 = a3 * a_sh
        a_s[:, :, cols] = a3.reshape(ns, l, LANE_CHUNK)
        u_s[:, :, cols] = u3.reshape(ns, l, LANE_CHUNK)

    hp = h_prev[...]
    for gi in range(l // SUBLANES):
        sl = slice(gi * SUBLANES, (gi + 1) * SUBLANES)
        h = a_s[:, sl, :] * hp + u_s[:, sl, :]
        u_s[:, sl, :] = h
        hp = h[:, SUBLANES - 1:SUBLANES, :]
    h_prev[...] = hp

    for c in range(D_LRU // LANE_CHUNK):
        cols = slice(c * LANE_CHUNK, (c + 1) * LANE_CHUNK)
        h = u_s[:, :, cols].reshape(rows, LANE_CHUNK)
        y_s[:, D_POOL + c * LANE_CHUNK:D_POOL + (c + 1) * LANE_CHUNK] = (
            h * jax.nn.gelu(gate_s[:, cols])).astype(BF16)

    mix = jnp.dot(y_s[...], w_out[...], preferred_element_type=F32)
    xin = alpha * x_ref[...].reshape(rows, D_MODEL) + mix
    mu = jnp.mean(xin, axis=-1, keepdims=True)
    xc = xin - mu
    var = jnp.mean(xc * xc, axis=-1, keepdims=True)
    x1 = xc * lax.rsqrt(var + LN_EPS) * ln_g[...] + ln_b[...]
    x1_ref[...] = x1

    hi = x1.astype(BF16)
    lo = (x1 - hi.astype(F32)).astype(BF16)
    both = lax.dot_general(rt_hl[...], hi, _NT, preferred_element_type=F32)
    lg = both[0:ROUTER_ROWS] + both[ROUTER_PAD:ROUTER_PAD + ROUTER_ROWS]
    lg = lg + lax.dot_general(rt_hl[0:ROUTER_PAD, :], lo, _NT, preferred_element_type=F32)[0:ROUTER_ROWS]
    lg_ref[...] = lg

    pool_o[...] = pool_ext[:, POOL_HALO + l - POOL_STATE:, :]
    conv_o[...] = conv_ext[:, CONV_HALO + l - CONV_STATE:, :]
    h_o[...] = hp
    pool_ext[:, 0:POOL_HALO, :] = pool_ext[:, l:l + POOL_HALO, :]
    conv_ext[:, 0:CONV_HALO, :] = conv_ext[:, l:l + CONV_HALO, :]


def _mixer_call(x, pool_in, conv_in, h_in, wts, x1_buf, lg_buf, *, ns, l, pos0, alpha, n_total, blk0):
    bsz, tlen, _ = x.shape
    nb, nt = bsz // ns, tlen // l
    rows = ns * l
    assert rows == ROWS and bsz % ns == 0 and tlen % l == 0 and l % SUBLANES == 0
    aliased = x1_buf is not None

    def row_blk(b, t):
        return (blk0 + b * nt + t, 0)

    in_specs = [
        pl.BlockSpec((ns, l, D_MODEL), lambda b, t: (b, t, 0)),
        pl.BlockSpec((ns, POOL_STATE, D_POOL), lambda b, t: (b, 0, 0)),
        pl.BlockSpec((ns, CONV_STATE, D_LRU), lambda b, t: (b, 0, 0)),
        pl.BlockSpec((ns, 1, D_LRU), lambda b, t: (b, 0, 0)),
    ] + [_const_spec(w.shape) for w in wts]
    args = [x, pool_in, conv_in, h_in, *wts]
    aliases = {}
    if aliased:
        in_specs += [pl.BlockSpec(memory_space=pl.ANY), pl.BlockSpec(memory_space=pl.ANY)]
        aliases = {len(args): 0, len(args) + 1: 1}
        args += [x1_buf, lg_buf]
    out_shape = (
        jax.ShapeDtypeStruct((n_total, D_MODEL), F32),
        jax.ShapeDtypeStruct((ROUTER_ROWS, n_total), F32),
        jax.ShapeDtypeStruct((bsz, POOL_STATE, D_POOL), F32),
        jax.ShapeDtypeStruct((bsz, CONV_STATE, D_LRU), F32),
        jax.ShapeDtypeStruct((bsz, 1, D_LRU), F32),
    )
    out_specs = (
        pl.BlockSpec((rows, D_MODEL), row_blk),
        pl.BlockSpec((ROUTER_ROWS, rows), lambda b, t: (0, blk0 + b * nt + t)),
        pl.BlockSpec((ns, POOL_STATE, D_POOL), lambda b, t: (b, 0, 0)),
        pl.BlockSpec((ns, CONV_STATE, D_LRU), lambda b, t: (b, 0, 0)),
        pl.BlockSpec((ns, 1, D_LRU), lambda b, t: (b, 0, 0)),
    )
    scratch = [
        pltpu.VMEM((ns, POOL_HALO + l, D_POOL), F32),
        pltpu.VMEM((ns, CONV_HALO + l, D_LRU), F32),
        pltpu.VMEM((ns, 1, D_LRU), F32),
        pltpu.VMEM((rows, D_LRU), F32),
        pltpu.VMEM((ns, l, D_LRU), F32),
        pltpu.VMEM((ns, l, D_LRU), F32),
        pltpu.VMEM((rows, D_MODEL), BF16),
    ]
    return pl.pallas_call(
        functools.partial(_mixer_kernel, ns=ns, l=l, pos0=pos0, alpha=alpha, aliased=aliased),
        grid=(nb, nt),
        in_specs=in_specs,
        out_specs=out_specs,
        out_shape=out_shape,
        scratch_shapes=scratch,
        input_output_aliases=aliases,
        compiler_params=pltpu.CompilerParams(
            dimension_semantics=("arbitrary", "arbitrary"), vmem_limit_bytes=VMEM_LIMIT),
        name="mixer_seq" if nt > 1 else "mixer_step",
    )(*args)


def _route_kernel(lg_ref, bias_ref, dest_ref, tw_ref, meta_ref, e_s, r_s, *, n):
    blk = 256
    nblk = n // blk
    neg_inf = -jnp.inf
    ridx8 = lax.broadcasted_iota(I32, (SUBLANES, blk), 0).astype(F32)
    eidx = lax.broadcasted_iota(I32, (N_EXPERTS, blk), 0).astype(F32)
    tri = (lax.broadcasted_iota(I32, (blk, blk), 0) < lax.broadcasted_iota(I32, (blk, blk), 1))
    tri = jnp.where(tri, 1.0, 0.0).astype(BF16)

    def first_idx(vals, m):
        return jnp.min(jnp.where(vals == m, ridx8, float(SUBLANES)), axis=0, keepdims=True)

    def pass1(j, base):
        ls = pl.ds(pl.multiple_of(j * blk, blk), blk)
        lg = lg_ref[0:SUBLANES, ls] + bias_ref[0:SUBLANES, :]
        lg = jnp.where(ridx8 < N_GROUPS, lg, neg_inf)
        m = jnp.max(lg, axis=0, keepdims=True)
        gi = first_idx(lg, m)
        pg_sel = 1.0 / jnp.sum(jnp.exp(lg - m), axis=0, keepdims=True)
        le = jnp.zeros((PER_GROUP, blk), F32)
        for g in range(N_GROUPS):
            rows = slice(SUBLANES + g * PER_GROUP, SUBLANES + (g + 1) * PER_GROUP)
            le = jnp.where(gi == float(g), lg_ref[rows, ls] + bias_ref[rows, :], le)
        m1 = jnp.max(le, axis=0, keepdims=True)
        i1 = first_idx(le, m1)
        le2 = jnp.where(ridx8 == i1, neg_inf, le)
        m2 = jnp.max(le2, axis=0, keepdims=True)
        i2 = first_idx(le2, m2)
        e21 = jnp.exp(m2 - m1)
        denom = 1.0 / (1.0 + e21)
        tw_ref[0:1, ls] = pg_sel * denom
        tw_ref[1:2, ls] = pg_sel * (e21 * denom)
        e1 = gi * float(PER_GROUP) + i1
        e2 = gi * float(PER_GROUP) + i2
        e_s[0:1, ls] = e1
        e_s[1:2, ls] = e2
        oh1 = jnp.where(eidx == e1, 1.0, 0.0)
        oh2 = jnp.where(eidx == e2, 1.0, 0.0)
        oh = oh1 + oh2
        before = base + jnp.dot(oh.astype(BF16), tri, preferred_element_type=F32)
        r_s[0:1, ls] = jnp.sum(oh1 * before, axis=0, keepdims=True)
        r_s[1:2, ls] = jnp.sum(oh2 * before, axis=0, keepdims=True)
        return base + jnp.sum(oh, axis=1, keepdims=True)

    cnt = lax.fori_loop(0, nblk, pass1, jnp.zeros((N_EXPERTS, blk), F32))

    ntile = jnp.floor((cnt + float(TM - 1)) * (1.0 / TM))
    lt = (lax.broadcasted_iota(I32, (N_EXPERTS, N_EXPERTS), 1) < lax.broadcasted_iota(I32, (N_EXPERTS, N_EXPERTS), 0))
    lt = jnp.where(lt, 1.0, 0.0).astype(BF16)
    tile0 = jnp.dot(lt, ntile.astype(BF16), preferred_element_type=F32)
    row0 = tile0 * float(TM)

    def pass2(j, carry):
        ls = pl.ds(pl.multiple_of(j * blk, blk), blk)
        for k in range(2):
            ohk = jnp.where(eidx == e_s[k:k + 1, ls], 1.0, 0.0)
            dest = r_s[k:k + 1, ls] + jnp.sum(ohk * row0, axis=0, keepdims=True)
            dest_ref[k:k + 1, ls] = dest.astype(I32)
        return carry

    lax.fori_loop(0, nblk, pass2, 0)

    ml = slice(0, META_LANES)
    total = jnp.sum(ntile[:, ml], axis=0, keepdims=True)
    tile_i = lax.broadcasted_iota(I32, (N_EXPERTS, META_LANES), 1).astype(F32)
    tile_c = jnp.minimum(tile_i, total - 1.0)
    tile_end = tile0[:, ml] + ntile[:, ml]
    te = jnp.sum(jnp.where(tile_end <= tile_c, 1.0, 0.0), axis=0, keepdims=True)
    meta_ref[...] = jnp.zeros((SUBLANES, META_LANES), I32)
    meta_ref[0:1, :] = te.astype(I32)
    meta_ref[1:2, :] = total.astype(I32)
    later = (eidx[:, ml] > te) & (ntile[:, ml] > 0.0)
    nxt = jnp.min(jnp.where(later, eidx[:, ml], float(N_EXPERTS)), axis=0, keepdims=True)
    meta_ref[2:3, :] = nxt.astype(I32)


def _route_call(lg, bias, n):
    return pl.pallas_call(
        functools.partial(_route_kernel, n=n),
        out_shape=(
            jax.ShapeDtypeStruct((2, n), I32),
            jax.ShapeDtypeStruct((2, n), F32),
            jax.ShapeDtypeStruct((SUBLANES, META_LANES), I32),
        ),
        scratch_shapes=[pltpu.VMEM((2, n), F32), pltpu.VMEM((2, n), F32)],
        compiler_params=pltpu.CompilerParams(vmem_limit_bytes=VMEM_LIMIT),
        name="route",
    )(lg, bias)


def _inverse_kernel(dest_ref, init_hbm, src_ref, sem, *, n):
    init = pltpu.make_async_copy(init_hbm, src_ref, sem)
    init.start()
    init.wait()

    def fill(t, c):
        src_ref[dest_ref[t]] = t
        src_ref[dest_ref[n + t]] = t
        return c

    lax.fori_loop(0, n, fill, 0, unroll=8)


def _inverse_call(dest_flat, n, n_slots):
    return pl.pallas_call(
        functools.partial(_inverse_kernel, n=n),
        in_specs=[pl.BlockSpec(memory_space=pltpu.SMEM), pl.BlockSpec(memory_space=pl.ANY)],
        out_specs=pl.BlockSpec(memory_space=pltpu.SMEM),
        out_shape=jax.ShapeDtypeStruct((n_slots,), I32),
        scratch_shapes=[pltpu.SemaphoreType.DMA(())],
        name="inverse",
    )(dest_flat, jnp.arange(n_slots, dtype=I32) % n)


def _row_copy(src_hbm, row, dst_vmem, slot, sem):
    return pltpu.make_async_copy(src_hbm.at[pl.ds(row, 1), :], dst_vmem.at[pl.ds(slot, 1), :], sem)


def _tile_copy(src_hbm, dst_vmem, sem):
    return pltpu.make_async_copy(src_hbm.at[pl.ds(0, dst_vmem.shape[0]), :], dst_vmem, sem)


def _moe_kernel(te_ref, ne_ref, nt_ref, src_cur, src_nxt, x1_hbm, w1_hbm, w3_hbm, w2_hbm, o_ref,
                xbuf, wf1, wf3, wf2, w1b, w3b, w2b, wslot, gsem, wsem):
    i = pl.program_id(0)
    nt = nt_ref[0]

    def gather_start(src_ref, slot):
        for r in range(TM):
            _row_copy(x1_hbm, src_ref[0, 0, r], xbuf.at[slot], r, gsem.at[slot]).start()

    def weight_copies(e, slot):
        return (pltpu.make_async_copy(w1_hbm.at[e], wf1.at[slot], wsem.at[slot, 0]),
                pltpu.make_async_copy(w3_hbm.at[e], wf3.at[slot], wsem.at[slot, 1]),
                pltpu.make_async_copy(w2_hbm.at[e], wf2.at[slot], wsem.at[slot, 2]))

    @pl.when(i == 0)
    def _():
        for cp in weight_copies(te_ref[0], 0):
            cp.start(priority=BULK_DMA_PRIORITY)
        wslot[0] = 0
        gather_start(src_cur, 0)

    @pl.when(i >= nt)
    def _():
        o_ref[...] = jnp.zeros_like(o_ref)

    @pl.when(i < nt)
    def _():
        slot = lax.rem(i, 2)
        _tile_copy(x1_hbm, xbuf.at[slot], gsem.at[slot]).wait()

        @pl.when((i == 0) | (te_ref[i] != te_ref[jnp.maximum(i - 1, 0)]))
        def _():
            cur = wslot[0]
            for cp in weight_copies(te_ref[i], cur):
                cp.wait()
            w1b[...] = wf1[cur].astype(BF16)
            w3b[...] = wf3[cur].astype(BF16)
            w2b[...] = wf2[cur].astype(BF16)

            @pl.when(ne_ref[i] < N_EXPERTS)
            def _():
                for cp in weight_copies(ne_ref[i], 1 - cur):
                    cp.start(priority=BULK_DMA_PRIORITY)

            wslot[0] = 1 - cur

        xb = xbuf[slot].astype(BF16)
        gather_start(src_nxt, 1 - slot)
        h1 = jnp.dot(xb, w1b[...], preferred_element_type=F32)
        h3 = jnp.dot(xb, w3b[...], preferred_element_type=F32)
        h = (jax.nn.silu(h1) * h3).astype(BF16)
        o_ref[...] = jnp.dot(h, w2b[...], preferred_element_type=F32)

        @pl.when(i + 1 >= nt)
        def _():
            _tile_copy(x1_hbm, xbuf.at[1 - slot], gsem.at[1 - slot]).wait()


def _moe_call(te, ne, ntiles, src3, x1, w1, w3, w2, max_tiles):
    grid_spec = pltpu.PrefetchScalarGridSpec(
        num_scalar_prefetch=3,
        grid=(max_tiles,),
        in_specs=[
            pl.BlockSpec((1, 1, TM), lambda i, te, ne, nt: (i, 0, 0), memory_space=pltpu.SMEM),
            pl.BlockSpec((1, 1, TM), lambda i, te, ne, nt: (jnp.minimum(i + 1, max_tiles - 1), 0, 0),
                         memory_space=pltpu.SMEM),
            pl.BlockSpec(memory_space=pl.ANY),
            pl.BlockSpec(memory_space=pl.ANY),
            pl.BlockSpec(memory_space=pl.ANY),
            pl.BlockSpec(memory_space=pl.ANY),
        ],
        out_specs=pl.BlockSpec((TM, D_MODEL), lambda i, te, ne, nt: (jnp.minimum(i, nt[0]), 0)),
        scratch_shapes=[
            pltpu.VMEM((2, TM, D_MODEL), F32),
            pltpu.VMEM((2, D_MODEL, D_EXPERT), F32),
            pltpu.VMEM((2, D_MODEL, D_EXPERT), F32),
            pltpu.VMEM((2, D_EXPERT, D_MODEL), F32),
            pltpu.VMEM((D_MODEL, D_EXPERT), BF16),
            pltpu.VMEM((D_MODEL, D_EXPERT), BF16),
            pltpu.VMEM((D_EXPERT, D_MODEL), BF16),
            pltpu.SMEM((1,), I32),
            pltpu.SemaphoreType.DMA((2,)),
            pltpu.SemaphoreType.DMA((2, 3)),
        ],
    )
    return pl.pallas_call(
        _moe_kernel,
        grid_spec=grid_spec,
        out_shape=jax.ShapeDtypeStruct(((max_tiles + 1) * TM, D_MODEL), F32),
        compiler_params=pltpu.CompilerParams(dimension_semantics=("arbitrary",), vmem_limit_bytes=VMEM_LIMIT),
        name="moe",
    )(te, ne, ntiles, src3, src3, x1, w1, w3, w2)


def _combine_kernel(dest_cur, dest_nxt, x1_ref, tw_ref, ys_hbm, ln_g, ln_b, o_ref, buf, sem, *, alpha, nsteps):
    i = pl.program_id(0)
    slot = lax.rem(i, 2)

    def gather_start(dest_ref, s):
        for r in range(ROWS):
            for k in range(2):
                _row_copy(ys_hbm, dest_ref[0, k, r], buf.at[s, k], r, sem.at[s, k]).start(priority=k)

    def gather_wait(s):
        for k in range(2):
            _tile_copy(ys_hbm, buf.at[s, k], sem.at[s, k]).wait()

    @pl.when(i == 0)
    def _():
        gather_start(dest_cur, 0)

    gather_wait(slot)
    tw = tw_ref[...]
    moe = tw[:, 0:1] * buf[slot, 0] + tw[:, 1:2] * buf[slot, 1]
    gather_start(dest_nxt, 1 - slot)
    xin = alpha * x1_ref[...] + moe
    mu = jnp.mean(xin, axis=-1, keepdims=True)
    xc = xin - mu
    var = jnp.mean(xc * xc, axis=-1, keepdims=True)
    y = xc * lax.rsqrt(var + LN_EPS) * ln_g[...] + ln_b[...]
    o_ref[...] = y.reshape(o_ref.shape)

    @pl.when(i == nsteps - 1)
    def _():
        gather_wait(1 - slot)


def _combine_call(dest3, x1, tw, ys, ln_g, ln_b, out_shape, blk, blk0, alpha):
    bsz, tlen, _ = out_shape
    nt = tlen // blk[1]
    nsteps = (bsz // blk[0]) * nt
    return pl.pallas_call(
        functools.partial(_combine_kernel, alpha=alpha, nsteps=nsteps),
        grid=(nsteps,),
        in_specs=[
            pl.BlockSpec((1, 2, ROWS), lambda i: (blk0 + i, 0, 0), memory_space=pltpu.SMEM),
            pl.BlockSpec((1, 2, ROWS), lambda i: (blk0 + jnp.minimum(i + 1, nsteps - 1), 0, 0),
                         memory_space=pltpu.SMEM),
            pl.BlockSpec((ROWS, D_MODEL), lambda i: (blk0 + i, 0)),
            pl.BlockSpec((ROWS, 2), lambda i: (blk0 + i, 0)),
            pl.BlockSpec(memory_space=pl.ANY),
            _const_spec(ln_g.shape),
            _const_spec(ln_b.shape),
        ],
        out_specs=pl.BlockSpec(blk, lambda i: (i // nt, i % nt, 0)),
        out_shape=jax.ShapeDtypeStruct(out_shape, F32),
        scratch_shapes=[
            pltpu.VMEM((2, 2, ROWS, D_MODEL), F32),
            pltpu.SemaphoreType.DMA((2, 2)),
        ],
        compiler_params=pltpu.CompilerParams(dimension_semantics=("arbitrary",), vmem_limit_bytes=VMEM_LIMIT),
        name="combine",
    )(dest3, dest3, x1, tw, ys, ln_g, ln_b)


def _block_diag_chunks(w):
    per = LANE_CHUNK // LRU_HEAD_DIM
    nchunk = LRU_HEADS // per
    w4 = w.reshape(nchunk, per, LRU_HEAD_DIM, LRU_HEAD_DIM)
    bd = jnp.einsum("cjio,jk->cjiko", w4, jnp.eye(per, dtype=w.dtype))
    return bd.reshape(nchunk, LANE_CHUNK, LANE_CHUNK)


def _layer(yp, ys, pool_s, conv_s, h_s, lw, alpha, past_len):
    (w_in, pool_w, pool_b, pool_scale, conv_w, conv_b, rg_w, rg_b, ig_w, ig_b, lru_lambda, w_out,
     ln1_g, ln1_b, rgw, rgb, rew, reb, w1, w3, w2, ln2_g, ln2_b) = lw
    bp, tp, _ = yp.shape
    bs, ts, _ = ys.shape
    n_p, n_s = bp * tp, bs * ts
    n = n_p + n_s
    assert n % ROWS == 0 and n_p % ROWS == 0
    max_tiles = (2 * n) // TM + N_EXPERTS
    assert max_tiles <= META_LANES

    nchunk = D_LRU // LANE_CHUNK
    gate_w = jnp.concatenate([_block_diag_chunks(rg_w), _block_diag_chunks(ig_w)], axis=-1).astype(BF16)
    gate_b = jnp.concatenate(
        [rg_b.reshape(nchunk, LANE_CHUNK), ig_b.reshape(nchunk, LANE_CHUNK)], axis=-1).reshape(1, 2 * D_LRU)
    rt = jnp.concatenate(
        [rgw.T, jnp.zeros((SUBLANES - N_GROUPS, D_MODEL), F32),
         jnp.transpose(rew, (0, 2, 1)).reshape(N_EXPERTS, D_MODEL)], axis=0)
    rt_hi = rt.astype(BF16)
    rt_lo = (rt - rt_hi.astype(F32)).astype(BF16)
    r_bias = jnp.concatenate([rgb, jnp.zeros((SUBLANES - N_GROUPS,), F32), reb.reshape(N_EXPERTS)]).reshape(ROUTER_ROWS, 1)
    wts = (
        w_in.astype(BF16), pool_w.astype(BF16), pool_b.reshape(1, D_POOL), pool_scale.reshape(1, D_POOL),
        conv_w, conv_b.reshape(1, D_LRU), gate_w, gate_b, lru_lambda.reshape(1, D_LRU), w_out.astype(BF16),
        ln1_g.reshape(1, D_MODEL), ln1_b.reshape(1, D_MODEL),
        jnp.pad(jnp.stack([rt_hi, rt_lo]), ((0, 0), (0, ROUTER_PAD - ROUTER_ROWS), (0, 0))).reshape(2 * ROUTER_PAD, D_MODEL),
    )

    x1, lg, pool_p, conv_p, h_p = _mixer_call(
        yp, jnp.zeros((bp, POOL_STATE, D_POOL), F32), jnp.zeros((bp, CONV_STATE, D_LRU), F32),
        jnp.zeros((bp, 1, D_LRU), F32), wts, None, None,
        ns=1, l=ROWS, pos0=0, alpha=alpha, n_total=n, blk0=0)
    x1, lg, pool_n, conv_n, h_n = _mixer_call(
        ys, pool_s, conv_s, h_s.reshape(bs, 1, D_LRU), wts, x1, lg,
        ns=ROWS // ts, l=ts, pos0=past_len, alpha=alpha, n_total=n, blk0=n_p // ROWS)

    dest, tw, meta = _route_call(lg, r_bias, n)
    src = _inverse_call(dest.reshape(2 * n), n, max_tiles * TM)
    ysort = _moe_call(meta[0], meta[2], meta[1, 0:1], src.reshape(max_tiles, 1, TM), x1, w1, w3, w2, max_tiles)

    dest3 = jnp.transpose(dest.reshape(2, n // ROWS, ROWS), (1, 0, 2))
    tw_t = tw.T
    g2, b2 = ln2_g.reshape(1, D_MODEL), ln2_b.reshape(1, D_MODEL)
    out_p = _combine_call(dest3, x1, tw_t, ysort, g2, b2, (bp, tp, D_MODEL), (1, ROWS, D_MODEL), 0, alpha)
    out_s = _combine_call(dest3, x1, tw_t, ysort, g2, b2, (bs, ts, D_MODEL), (ROWS // ts, ts, D_MODEL),
                          n_p // ROWS, alpha)
    states = (pool_p, conv_p, h_p.reshape(bp, D_LRU), pool_n, conv_n, h_n.reshape(bs, D_LRU))
    return out_p, out_s, states


def kernel(x_prompt, x_sample, state_pool, state_conv, state_h, w_in, pool_w, pool_b, pool_scale, conv_w, conv_b, rg_w, rg_b, ig_w, ig_b, lru_lambda, w_out, ln1_g, ln1_b, router_group_w, router_group_b, router_expert_w, router_expert_b, expert_w1, expert_w3, expert_w2, ln2_g, ln2_b):
    depth = w_in.shape[0]
    alpha = (2.0 * depth) ** 0.25
    past_len = 16384
    layer_weights = (w_in, pool_w, pool_b, pool_scale, conv_w, conv_b, rg_w, rg_b, ig_w, ig_b, lru_lambda, w_out,
                     ln1_g, ln1_b, router_group_w, router_group_b, router_expert_w, router_expert_b,
                     expert_w1, expert_w3, expert_w2, ln2_g, ln2_b)
    yp, ys = x_prompt, x_sample
    outs = [[] for _ in range(6)]
    for layer in range(depth):
        lw = tuple(w[layer] for w in layer_weights)
        yp, ys, states = _layer(yp, ys, state_pool[layer], state_conv[layer], state_h[layer], lw, alpha, past_len)
        for acc, s in zip(outs, states):
            acc.append(s)
    return (yp, ys) + tuple(jnp.stack(o, axis=0) for o in outs)
```

```python
import functools

import jax
import jax.numpy as jnp
from jax import lax
from jax.experimental import pallas as pl
from jax.experimental.pallas import tpu as pltpu

F32 = jnp.float32
BF16 = jnp.bfloat16
I32 = jnp.int32

D_MODEL = 2048
D_POOL = 1024
D_LRU = 1024
POOL_WINDOWS = (2, 4, 8, 16)
POOL_GROUP = D_POOL // len(POOL_WINDOWS)
POOL_STATE = max(POOL_WINDOWS) - 1
CONV_WIDTH = 4
CONV_STATE = CONV_WIDTH - 1
LRU_HEADS = 16
LRU_HEAD_DIM = D_LRU // LRU_HEADS
LRU_C = 8.0
N_GROUPS = 4
PER_GROUP = 8
N_EXPERTS = N_GROUPS * PER_GROUP
D_EXPERT = D_MODEL // 4
LN_EPS = 1e-5

SUBLANES = 8
LANES = 128
LANE_CHUNK = 256
POOL_HALO = 16
CONV_HALO = 8
ROWS = 256
TM = 256
ROUTER_ROWS = 8 + N_EXPERTS
ROUTER_PAD = 48
META_LANES = LANES
VMEM_LIMIT = 56 * 1024 * 1024
BULK_DMA_PRIORITY = 1

_NT = (((1,), (1,)), ((), ()))


def _const_spec(shape):
    nd = len(shape)
    return pl.BlockSpec(shape, lambda *_: (0,) * nd, pipeline_mode=pl.Buffered(1))


def _mixer_kernel(x_ref, pool_in, conv_in, h_in, w_in, pool_w, pool_b, pool_scale, conv_w, conv_b,
                  gate_w, gate_b, lam, w_out, ln_g, ln_b, rt_hl, *rest, ns, l, pos0, alpha, aliased):
    if aliased:
        rest = rest[2:]
    (x1_ref, lg_ref, pool_o, conv_o, h_o,
     pool_ext, conv_ext, h_prev, gate_s, a_s, u_s, y_s) = rest
    t = pl.program_id(1)
    rows = ns * l

    @pl.when(t == 0)
    def _():
        pool_ext[:, 0:POOL_HALO - POOL_STATE, :] = jnp.zeros((ns, POOL_HALO - POOL_STATE, D_POOL), F32)
        pool_ext[:, POOL_HALO - POOL_STATE:POOL_HALO, :] = pool_in[...]
        conv_ext[:, 0:CONV_HALO - CONV_STATE, :] = jnp.zeros((ns, CONV_HALO - CONV_STATE, D_LRU), F32)
        conv_ext[:, CONV_HALO - CONV_STATE:CONV_HALO, :] = conv_in[...]
        h_prev[...] = h_in[...]

    xb = x_ref[...].reshape(rows, D_MODEL).astype(BF16)
    pool_ext[:, POOL_HALO:, :] = jnp.dot(
        xb, w_in[:, 0:D_POOL], preferred_element_type=F32).reshape(ns, l, D_POOL)
    conv_ext[:, CONV_HALO:, :] = jnp.dot(
        xb, w_in[:, D_POOL:D_POOL + D_LRU], preferred_element_type=F32).reshape(ns, l, D_LRU)
    gate_s[...] = jnp.dot(xb, w_in[:, D_POOL + D_LRU:], preferred_element_type=F32)

    pos = pos0 + t * l + lax.broadcasted_iota(I32, (ns, l, LANE_CHUNK), 1).reshape(rows, LANE_CHUNK)

    for g, w in enumerate(POOL_WINDOWS):
        cols = slice(g * POOL_GROUP, (g + 1) * POOL_GROUP)
        s = pool_ext[:, :, cols].reshape(ns * (POOL_HALO + l), POOL_GROUP)
        shift = 1
        while shift < w:
            s = s + pltpu.roll(s, shift, 0)
            shift *= 2
        win = s.reshape(ns, POOL_HALO + l, POOL_GROUP)[:, POOL_HALO:, :].reshape(rows, POOL_GROUP)
        u = pool_ext[:, POOL_HALO:, cols].reshape(rows, POOL_GROUP)
        if pos0 >= w - 1:
            inv = 1.0 / w
        else:
            inv = 1.0 / jnp.minimum(pos + 1, w).astype(F32)
        d = win * inv - u
        z = jnp.dot(d.astype(BF16), pool_w[g], preferred_element_type=F32) + pool_b[:, cols]
        y_s[:, cols] = (z * pool_scale[:, cols]).astype(BF16)

    lam_v = lam[...]
    softplus_neg = jnp.maximum(-lam_v, 0.0) + jnp.log1p(jnp.exp(-jnp.abs(lam_v)))
    log_a_scale = -LRU_C * softplus_neg
    sub = lax.broadcasted_iota(I32, (rows // SUBLANES, SUBLANES, LANE_CHUNK), 1)
    for c in range(D_LRU // LANE_CHUNK):
        cols = slice(c * LANE_CHUNK, (c + 1) * LANE_CHUNK)
        ce = conv_ext[:, :, cols].reshape(ns * (CONV_HALO + l), LANE_CHUNK)
        conv = conv_b[:, cols] + conv_w[CONV_WIDTH - 1:CONV_WIDTH, cols] * ce
        for k in range(1, CONV_WIDTH):
            conv = conv + conv_w[CONV_WIDTH - 1 - k:CONV_WIDTH - k, cols] * pltpu.roll(ce, k, 0)
        conv = conv.reshape(ns, CONV_HALO + l, LANE_CHUNK)[:, CONV_HALO:, :].reshape(rows, LANE_CHUNK)
        gz = jnp.dot(conv.astype(BF16), gate_w[c], preferred_element_type=F32)
        gz = gz + gate_b[:, 2 * c * LANE_CHUNK:2 * (c + 1) * LANE_CHUNK]
        r = 0.5 * jnp.tanh(0.5 * gz[:, :LANE_CHUNK]) + 0.5
        i = 0.5 * jnp.tanh(0.5 * gz[:, LANE_CHUNK:]) + 0.5
        log_a = r * log_a_scale[:, cols]
        a = jnp.exp(log_a)
        th = jnp.tanh(log_a)
        q = (th + th) / (th - 1.0)
        mult = jnp.where(q > 0.0, q * lax.rsqrt(q), 0.0)
        if pos0 == 0:
            mult = jnp.where(pos == 0, 1.0, mult)
        u = mult * (i * conv)
        a3 = a.reshape(rows // SUBLANES, SUBLANES, LANE_CHUNK)
        u3 = u.reshape(rows // SUBLANES, SUBLANES, LANE_CHUNK)
        for sh in (1, 2, 4):
            keep = sub >= sh
            a_sh = jnp.where(keep, pltpu.roll(a3, sh, 1), 1.0)
            u_sh = jnp.where(keep, pltpu.roll(u3, sh, 1), 0.0)
            u3 = a3 * u_sh + u3
            a3 = a3 * a_sh
        a_s[:, :, cols] = a3.reshape(ns, l, LANE_CHUNK)
        u_s[:, :, cols] = u3.reshape(ns, l, LANE_CHUNK)

    hp = h_prev[...]
    for gi in range(l // SUBLANES):
        sl = slice(gi * SUBLANES, (gi + 1) * SUBLANES)
        h = a_s[:, sl, :] * hp + u_s[:, sl, :]
        u_s[:, sl, :] = h
        hp = h[:, SUBLANES - 1:SUBLANES, :]
    h_prev[...] = hp

    for c in range(D_LRU // LANE_CHUNK):
        cols = slice(c * LANE_CHUNK, (c + 1) * LANE_CHUNK)
        h = u_s[:, :, cols].reshape(rows, LANE_CHUNK)
        y_s[:, D_POOL + c * LANE_CHUNK:D_POOL + (c + 1) * LANE_CHUNK] = (
            h * jax.nn.gelu(gate_s[:, cols])).astype(BF16)

    mix = jnp.dot(y_s[...], w_out[...], preferred_element_type=F32)
    xin = alpha * x_ref[...].reshape(rows, D_MODEL) + mix
    mu = jnp.mean(xin, axis=-1, keepdims=True)
    xc = xin - mu
    var = jnp.mean(xc * xc, axis=-1, keepdims=True)
    x1 = xc * lax.rsqrt(var + LN_EPS) * ln_g[...] + ln_b[...]
    x1_ref[...] = x1

    hi = x1.astype(BF16)
    lo = (x1 - hi.astype(F32)).astype(BF16)
    both = lax.dot_general(rt_hl[...], hi, _NT, preferred_element_type=F32)
    lg = both[0:ROUTER_ROWS] + both[ROUTER_PAD:ROUTER_PAD + ROUTER_ROWS]
    lg = lg + lax.dot_general(rt_hl[0:ROUTER_PAD, :], lo, _NT, preferred_element_type=F32)[0:ROUTER_ROWS]
    lg_ref[...] = lg

    pool_o[...] = pool_ext[:, POOL_HALO + l - POOL_STATE:, :]
    conv_o[...] = conv_ext[:, CONV_HALO + l - CONV_STATE:, :]
    h_o[...] = hp
    pool_ext[:, 0:POOL_HALO, :] = pool_ext[:, l:l + POOL_HALO, :]
    conv_ext[:, 0:CONV_HALO, :] = conv_ext[:, l:l + CONV_HALO, :]


def _mixer_call(x, pool_in, conv_in, h_in, wts, x1_buf, lg_buf, *, ns, l, pos0, alpha, n_total, blk0):
    bsz, tlen, _ = x.shape
    nb, nt = bsz // ns, tlen // l
    rows = ns * l
    assert rows == ROWS and bsz % ns == 0 and tlen % l == 0 and l % SUBLANES == 0
    aliased = x1_buf is not None

    def row_blk(b, t):
        return (blk0 + b * nt + t, 0)

    in_specs = [
        pl.BlockSpec((ns, l, D_MODEL), lambda b, t: (b, t, 0)),
        pl.BlockSpec((ns, POOL_STATE, D_POOL), lambda b, t: (b, 0, 0)),
        pl.BlockSpec((ns, CONV_STATE, D_LRU), lambda b, t: (b, 0, 0)),
        pl.BlockSpec((ns, 1, D_LRU), lambda b, t: (b, 0, 0)),
    ] + [_const_spec(w.shape) for w in wts]
    args = [x, pool_in, conv_in, h_in, *wts]
    aliases = {}
    if aliased:
        in_specs += [pl.BlockSpec(memory_space=pl.ANY), pl.BlockSpec(memory_space=pl.ANY)]
        aliases = {len(args): 0, len(args) + 1: 1}
        args += [x1_buf, lg_buf]
    out_shape = (
        jax.ShapeDtypeStruct((n_total, D_MODEL), F32),
        jax.ShapeDtypeStruct((ROUTER_ROWS, n_total), F32),
        jax.ShapeDtypeStruct((bsz, POOL_STATE, D_POOL), F32),
        jax.ShapeDtypeStruct((bsz, CONV_STATE, D_LRU), F32),
        jax.ShapeDtypeStruct((bsz, 1, D_LRU), F32),
    )
    out_specs = (
        pl.BlockSpec((rows, D_MODEL), row_blk),
        pl.BlockSpec((ROUTER_ROWS, rows), lambda b, t: (0, blk0 + b * nt + t)),
        pl.BlockSpec((ns, POOL_STATE, D_POOL), lambda b, t: (b, 0, 0)),
        pl.BlockSpec((ns, CONV_STATE, D_LRU), lambda b, t: (b, 0, 0)),
        pl.BlockSpec((ns, 1, D_LRU), lambda b, t: (b, 0, 0)),
    )
    scratch = [
        pltpu.VMEM((ns, POOL_HALO + l, D_POOL), F32),
        pltpu.VMEM((ns, CONV_HALO + l, D_LRU), F32),
        pltpu.VMEM((ns, 1, D_LRU), F32),
        pltpu.VMEM((rows, D_LRU), F32),
        pltpu.VMEM((ns, l, D_LRU), F32),
        pltpu.VMEM((ns, l, D_LRU), F32),
        pltpu.VMEM((rows, D_MODEL), BF16),
    ]
    return pl.pallas_call(
        functools.partial(_mixer_kernel, ns=ns, l=l, pos0=pos0, alpha=alpha, aliased=aliased),
        grid=(nb, nt),
        in_specs=in_specs,
        out_specs=out_specs,
        out_shape=out_shape,
        scratch_shapes=scratch,
        input_output_aliases=aliases,
        compiler_params=pltpu.CompilerParams(
            dimension_semantics=("arbitrary", "arbitrary"), vmem_limit_bytes=VMEM_LIMIT),
        name="mixer_seq" if nt > 1 else "mixer_step",
    )(*args)


def _route_kernel(lg_ref, bias_ref, dest_ref, tw_ref, meta_ref, e_s, r_s, *, n):
    blk = 256
    nblk = n // blk
    neg_inf = -jnp.inf
    ridx8 = lax.broadcasted_iota(I32, (SUBLANES, blk), 0).astype(F32)
    eidx = lax.broadcasted_iota(I32, (N_EXPERTS, blk), 0).astype(F32)
    tri = (lax.broadcasted_iota(I32, (blk, blk), 0) < lax.broadcasted_iota(I32, (blk, blk), 1))
    tri = jnp.where(tri, 1.0, 0.0).astype(BF16)

    def first_idx(vals, m):
        return jnp.min(jnp.where(vals == m, ridx8, float(SUBLANES)), axis=0, keepdims=True)

    def pass1(j, base):
        ls = pl.ds(pl.multiple_of(j * blk, blk), blk)
        lg = lg_ref[0:SUBLANES, ls] + bias_ref[0:SUBLANES, :]
        lg = jnp.where(ridx8 < N_GROUPS, lg, neg_inf)
        m = jnp.max(lg, axis=0, keepdims=True)
        gi = first_idx(lg, m)
        pg_sel = 1.0 / jnp.sum(jnp.exp(lg - m), axis=0, keepdims=True)
        le = jnp.zeros((PER_GROUP, blk), F32)
        for g in range(N_GROUPS):
            rows = slice(SUBLANES + g * PER_GROUP, SUBLANES + (g + 1) * PER_GROUP)
            le = jnp.where(gi == float(g), lg_ref[rows, ls] + bias_ref[rows, :], le)
        m1 = jnp.max(le, axis=0, keepdims=True)
        i1 = first_idx(le, m1)
        le2 = jnp.where(ridx8 == i1, neg_inf, le)
        m2 = jnp.max(le2, axis=0, keepdims=True)
        i2 = first_idx(le2, m2)
        e21 = jnp.exp(m2 - m1)
        denom = 1.0 / (1.0 + e21)
        tw_ref[0:1, ls] = pg_sel * denom
        tw_ref[1:2, ls] = pg_sel * (e21 * denom)
        e1 = gi * float(PER_GROUP) + i1
        e2 = gi * float(PER_GROUP) + i2
        e_s[0:1, ls] = e1
        e_s[1:2, ls] = e2
        oh1 = jnp.where(eidx == e1, 1.0, 0.0)
        oh2 = jnp.where(eidx == e2, 1.0, 0.0)
        oh = oh1 + oh2
        before = base + jnp.dot(oh.astype(BF16), tri, preferred_element_type=F32)
        r_s[0:1, ls] = jnp.sum(oh1 * before, axis=0, keepdims=True)
        r_s[1:2, ls] = jnp.sum(oh2 * before, axis=0, keepdims=True)
        return base + jnp.sum(oh, axis=1, keepdims=True)

    cnt = lax.fori_loop(0, nblk, pass1, jnp.zeros((N_EXPERTS, blk), F32))

    ntile = jnp.floor((cnt + float(TM - 1)) * (1.0 / TM))
    lt = (lax.broadcasted_iota(I32, (N_EXPERTS, N_EXPERTS), 1) < lax.broadcasted_iota(I32, (N_EXPERTS, N_EXPERTS), 0))
    lt = jnp.where(lt, 1.0, 0.0).astype(BF16)
    tile0 = jnp.dot(lt, ntile.astype(BF16), preferred_element_type=F32)
    row0 = tile0 * float(TM)

    def pass2(j, carry):
        ls = pl.ds(pl.multiple_of(j * blk, blk), blk)
        for k in range(2):
            ohk = jnp.where(eidx == e_s[k:k + 1, ls], 1.0, 0.0)
            dest = r_s[k:k + 1, ls] + jnp.sum(ohk * row0, axis=0, keepdims=True)
            dest_ref[k:k + 1, ls] = dest.astype(I32)
        return carry

    lax.fori_loop(0, nblk, pass2, 0)

    ml = slice(0, META_LANES)
    total = jnp.sum(ntile[:, ml], axis=0, keepdims=True)
    tile_i = lax.broadcasted_iota(I32, (N_EXPERTS, META_LANES), 1).astype(F32)
    tile_c = jnp.minimum(tile_i, total - 1.0)
    tile_end = tile0[:, ml] + ntile[:, ml]
    te = jnp.sum(jnp.where(tile_end <= tile_c, 1.0, 0.0), axis=0, keepdims=True)
    meta_ref[...] = jnp.zeros((SUBLANES, META_LANES), I32)
    meta_ref[0:1, :] = te.astype(I32)
    meta_ref[1:2, :] = total.astype(I32)
    later = (eidx[:, ml] > te) & (ntile[:, ml] > 0.0)
    nxt = jnp.min(jnp.where(later, eidx[:, ml], float(N_EXPERTS)), axis=0, keepdims=True)
    meta_ref[2:3, :] = nxt.astype(I32)


def _route_call(lg, bias, n):
    return pl.pallas_call(
        functools.partial(_route_kernel, n=n),
        out_shape=(
            jax.ShapeDtypeStruct((2, n), I32),
            jax.ShapeDtypeStruct((2, n), F32),
            jax.ShapeDtypeStruct((SUBLANES, META_LANES), I32),
        ),
        scratch_shapes=[pltpu.VMEM((2, n), F32), pltpu.VMEM((2, n), F32)],
        compiler_params=pltpu.CompilerParams(vmem_limit_bytes=VMEM_LIMIT),
        name="route",
    )(lg, bias)


def _inverse_kernel(dest_ref, init_hbm, src_ref, sem, *, n):
    init = pltpu.make_async_copy(init_hbm, src_ref, sem)
    init.start()
    init.wait()

    def fill(t, c):
        src_ref[dest_ref[t]] = t
        src_ref[dest_ref[n + t]] = t
        return c

    lax.fori_loop(0, n, fill, 0, unroll=8)


def _inverse_call(dest_flat, n, n_slots):
    return pl.pallas_call(
        functools.partial(_inverse_kernel, n=n),
        in_specs=[pl.BlockSpec(memory_space=pltpu.SMEM), pl.BlockSpec(memory_space=pl.ANY)],
        out_specs=pl.BlockSpec(memory_space=pltpu.SMEM),
        out_shape=jax.ShapeDtypeStruct((n_slots,), I32),
        scratch_shapes=[pltpu.SemaphoreType.DMA(())],
        name="inverse",
    )(dest_flat, jnp.arange(n_slots, dtype=I32) % n)


def _row_copy(src_hbm, row, dst_vmem, slot, sem):
    return pltpu.make_async_copy(src_hbm.at[pl.ds(row, 1), :], dst_vmem.at[pl.ds(slot, 1), :], sem)


def _tile_copy(src_hbm, dst_vmem, sem):
    return pltpu.make_async_copy(src_hbm.at[pl.ds(0, dst_vmem.shape[0]), :], dst_vmem, sem)


def _moe_kernel(te_ref, ne_ref, nt_ref, src_cur, src_nxt, x1_hbm, w1_hbm, w3_hbm, w2_hbm, o_hbm,
                xbuf, obuf, wf1, wf3, wf2, w1b, w3b, w2b, wslot, gsem, osem, wsem):
    i = pl.program_id(0)
    nt = nt_ref[0]

    def gather_start(src_ref, slot):
        for r in range(TM):
            _row_copy(x1_hbm, src_ref[0, 0, r], xbuf.at[slot], r, gsem.at[slot]).start()

    def out_copy(tile, slot):
        return pltpu.make_async_copy(obuf.at[slot], o_hbm.at[pl.ds(tile * TM, TM), :], osem.at[slot])

    def weight_copies(e, slot):
        return (pltpu.make_async_copy(w1_hbm.at[e], wf1.at[slot], wsem.at[slot, 0]),
                pltpu.make_async_copy(w3_hbm.at[e], wf3.at[slot], wsem.at[slot, 1]),
                pltpu.make_async_copy(w2_hbm.at[e], wf2.at[slot], wsem.at[slot, 2]))

    @pl.when(i == 0)
    def _():
        for cp in weight_copies(te_ref[0], 0):
            cp.start(priority=BULK_DMA_PRIORITY)
        wslot[0] = 0
        gather_start(src_cur, 0)

    @pl.when(i < nt)
    def _():
        slot = lax.rem(i, 2)

        @pl.when(i >= 2)
        def _():
            out_copy(0, slot).wait()

        _tile_copy(x1_hbm, xbuf.at[slot], gsem.at[slot]).wait()

        @pl.when((i == 0) | (te_ref[i] != te_ref[jnp.maximum(i - 1, 0)]))
        def _():
            cur = wslot[0]
            for cp in weight_copies(te_ref[i], cur):
                cp.wait()
            w1b[...] = wf1[cur].astype(BF16)
            w3b[...] = wf3[cur].astype(BF16)
            w2b[...] = wf2[cur].astype(BF16)

            @pl.when(ne_ref[i] < N_EXPERTS)
            def _():
                for cp in weight_copies(ne_ref[i], 1 - cur):
                    cp.start(priority=BULK_DMA_PRIORITY)

            wslot[0] = 1 - cur

        xb = xbuf[slot].astype(BF16)
        gather_start(src_nxt, 1 - slot)
        h1 = jnp.dot(xb, w1b[...], preferred_element_type=F32)
        h3 = jnp.dot(xb, w3b[...], preferred_element_type=F32)
        h = (jax.nn.silu(h1) * h3).astype(BF16)
        obuf[slot] = jnp.dot(h, w2b[...], preferred_element_type=F32)
        out_copy(i, slot).start(priority=BULK_DMA_PRIORITY)

        @pl.when(i + 1 >= nt)
        def _():
            _tile_copy(x1_hbm, xbuf.at[1 - slot], gsem.at[1 - slot]).wait()
            out_copy(0, slot).wait()

            @pl.when(i >= 1)
            def _():
                out_copy(0, 1 - slot).wait()


def _moe_call(te, ne, ntiles, src3, x1, w1, w3, w2, max_tiles):
    grid_spec = pltpu.PrefetchScalarGridSpec(
        num_scalar_prefetch=3,
        grid=(max_tiles,),
        in_specs=[
            pl.BlockSpec((1, 1, TM), lambda i, te, ne, nt: (i, 0, 0), memory_space=pltpu.SMEM),
            pl.BlockSpec((1, 1, TM), lambda i, te, ne, nt: (jnp.minimum(i + 1, max_tiles - 1), 0, 0),
                         memory_space=pltpu.SMEM),
            pl.BlockSpec(memory_space=pl.ANY),
            pl.BlockSpec(memory_space=pl.ANY),
            pl.BlockSpec(memory_space=pl.ANY),
            pl.BlockSpec(memory_space=pl.ANY),
        ],
        out_specs=pl.BlockSpec(memory_space=pl.ANY),
        scratch_shapes=[
            pltpu.VMEM((2, TM, D_MODEL), F32),
            pltpu.VMEM((2, TM, D_MODEL), F32),
            pltpu.VMEM((2, D_MODEL, D_EXPERT), F32),
            pltpu.VMEM((2, D_MODEL, D_EXPERT), F32),
            pltpu.VMEM((2, D_EXPERT, D_MODEL), F32),
            pltpu.VMEM((D_MODEL, D_EXPERT), BF16),
            pltpu.VMEM((D_MODEL, D_EXPERT), BF16),
            pltpu.VMEM((D_EXPERT, D_MODEL), BF16),
            pltpu.SMEM((1,), I32),
            pltpu.SemaphoreType.DMA((2,)),
            pltpu.SemaphoreType.DMA((2,)),
            pltpu.SemaphoreType.DMA((2, 3)),
        ],
    )
    return pl.pallas_call(
        _moe_kernel,
        grid_spec=grid_spec,
        out_shape=jax.ShapeDtypeStruct((max_tiles * TM, D_MODEL), F32),
        compiler_params=pltpu.CompilerParams(dimension_semantics=("arbitrary",), vmem_limit_bytes=VMEM_LIMIT),
        name="moe",
    )(te, ne, ntiles, src3, src3, x1, w1, w3, w2)


def _combine_kernel(dest_cur, dest_nxt, x1_ref, tw_ref, ys_hbm, ln_g, ln_b, o_ref, buf, sem, *, alpha, nsteps):
    i = pl.program_id(0)
    slot = lax.rem(i, 2)

    def gather_start(dest_ref, s):
        for r in range(ROWS):
            for k in range(2):
                _row_copy(ys_hbm, dest_ref[0, k, r], buf.at[s, k], r, sem.at[s, k]).start(priority=k)

    def gather_wait(s):
        for k in range(2):
            _tile_copy(ys_hbm, buf.at[s, k], sem.at[s, k]).wait()

    @pl.when(i == 0)
    def _():
        gather_start(dest_cur, 0)

    gather_wait(slot)
    tw = tw_ref[...]
    moe = tw[:, 0:1] * buf[slot, 0] + tw[:, 1:2] * buf[slot, 1]
    gather_start(dest_nxt, 1 - slot)
    xin = alpha * x1_ref[...] + moe
    mu = jnp.mean(xin, axis=-1, keepdims=True)
    xc = xin - mu
    var = jnp.mean(xc * xc, axis=-1, keepdims=True)
    y = xc * lax.rsqrt(var + LN_EPS) * ln_g[...] + ln_b[...]
    o_ref[...] = y.reshape(o_ref.shape)

    @pl.when(i == nsteps - 1)
    def _():
        gather_wait(1 - slot)


def _combine_call(dest3, x1, tw, ys, ln_g, ln_b, out_shape, blk, blk0, alpha):
    bsz, tlen, _ = out_shape
    nt = tlen // blk[1]
    nsteps = (bsz // blk[0]) * nt
    return pl.pallas_call(
        functools.partial(_combine_kernel, alpha=alpha, nsteps=nsteps),
        grid=(nsteps,),
        in_specs=[
            pl.BlockSpec((1, 2, ROWS), lambda i: (blk0 + i, 0, 0), memory_space=pltpu.SMEM),
            pl.BlockSpec((1, 2, ROWS), lambda i: (blk0 + jnp.minimum(i + 1, nsteps - 1), 0, 0),
                         memory_space=pltpu.SMEM),
            pl.BlockSpec((ROWS, D_MODEL), lambda i: (blk0 + i, 0)),
            pl.BlockSpec((ROWS, 2), lambda i: (blk0 + i, 0)),
            pl.BlockSpec(memory_space=pl.ANY),
            _const_spec(ln_g.shape),
            _const_spec(ln_b.shape),
        ],
        out_specs=pl.BlockSpec(blk, lambda i: (i // nt, i % nt, 0)),
        out_shape=jax.ShapeDtypeStruct(out_shape, F32),
        scratch_shapes=[
            pltpu.VMEM((2, 2, ROWS, D_MODEL), F32),
            pltpu.SemaphoreType.DMA((2, 2)),
        ],
        compiler_params=pltpu.CompilerParams(dimension_semantics=("arbitrary",), vmem_limit_bytes=VMEM_LIMIT),
        name="combine",
    )(dest3, dest3, x1, tw, ys, ln_g, ln_b)


def _block_diag_chunks(w):
    per = LANE_CHUNK // LRU_HEAD_DIM
    nchunk = LRU_HEADS // per
    w4 = w.reshape(nchunk, per, LRU_HEAD_DIM, LRU_HEAD_DIM)
    bd = jnp.einsum("cjio,jk->cjiko", w4, jnp.eye(per, dtype=w.dtype))
    return bd.reshape(nchunk, LANE_CHUNK, LANE_CHUNK)


def _layer(yp, ys, pool_s, conv_s, h_s, lw, alpha, past_len):
    (w_in, pool_w, pool_b, pool_scale, conv_w, conv_b, rg_w, rg_b, ig_w, ig_b, lru_lambda, w_out,
     ln1_g, ln1_b, rgw, rgb, rew, reb, w1, w3, w2, ln2_g, ln2_b) = lw
    bp, tp, _ = yp.shape
    bs, ts, _ = ys.shape
    n_p, n_s = bp * tp, bs * ts
    n = n_p + n_s
    assert n % ROWS == 0 and n_p % ROWS == 0
    max_tiles = (2 * n) // TM + N_EXPERTS
    assert max_tiles <= META_LANES

    nchunk = D_LRU // LANE_CHUNK
    gate_w = jnp.concatenate([_block_diag_chunks(rg_w), _block_diag_chunks(ig_w)], axis=-1).astype(BF16)
    gate_b = jnp.concatenate(
        [rg_b.reshape(nchunk, LANE_CHUNK), ig_b.reshape(nchunk, LANE_CHUNK)], axis=-1).reshape(1, 2 * D_LRU)
    rt = jnp.concatenate(
        [rgw.T, jnp.zeros((SUBLANES - N_GROUPS, D_MODEL), F32),
         jnp.transpose(rew, (0, 2, 1)).reshape(N_EXPERTS, D_MODEL)], axis=0)
    rt_hi = rt.astype(BF16)
    rt_lo = (rt - rt_hi.astype(F32)).astype(BF16)
    r_bias = jnp.concatenate([rgb, jnp.zeros((SUBLANES - N_GROUPS,), F32), reb.reshape(N_EXPERTS)]).reshape(ROUTER_ROWS, 1)
    wts = (
        w_in.astype(BF16), pool_w.astype(BF16), pool_b.reshape(1, D_POOL), pool_scale.reshape(1, D_POOL),
        conv_w, conv_b.reshape(1, D_LRU), gate_w, gate_b, lru_lambda.reshape(1, D_LRU), w_out.astype(BF16),
        ln1_g.reshape(1, D_MODEL), ln1_b.reshape(1, D_MODEL),
        jnp.pad(jnp.stack([rt_hi, rt_lo]), ((0, 0), (0, ROUTER_PAD - ROUTER_ROWS), (0, 0))).reshape(2 * ROUTER_PAD, D_MODEL),
    )

    x1, lg, pool_p, conv_p, h_p = _mixer_call(
        yp, jnp.zeros((bp, POOL_STATE, D_POOL), F32), jnp.zeros((bp, CONV_STATE, D_LRU), F32),
        jnp.zeros((bp, 1, D_LRU), F32), wts, None, None,
        ns=1, l=ROWS, pos0=0, alpha=alpha, n_total=n, blk0=0)
    x1, lg, pool_n, conv_n, h_n = _mixer_call(
        ys, pool_s, conv_s, h_s.reshape(bs, 1, D_LRU), wts, x1, lg,
        ns=ROWS // ts, l=ts, pos0=past_len, alpha=alpha, n_total=n, blk0=n_p // ROWS)

    dest, tw, meta = _route_call(lg, r_bias, n)
    src = _inverse_call(dest.reshape(2 * n), n, max_tiles * TM)
    ysort = _moe_call(meta[0], meta[2], meta[1, 0:1], src.reshape(max_tiles, 1, TM), x1, w1, w3, w2, max_tiles)

    dest3 = jnp.transpose(dest.reshape(2, n // ROWS, ROWS), (1, 0, 2))
    tw_t = tw.T
    g2, b2 = ln2_g.reshape(1, D_MODEL), ln2_b.reshape(1, D_MODEL)
    out_p = _combine_call(dest3, x1, tw_t, ysort, g2, b2, (bp, tp, D_MODEL), (1, ROWS, D_MODEL), 0, alpha)
    out_s = _combine_call(dest3, x1, tw_t, ysort, g2, b2, (bs, ts, D_MODEL), (ROWS // ts, ts, D_MODEL),
                          n_p // ROWS, alpha)
    states = (pool_p, conv_p, h_p.reshape(bp, D_LRU), pool_n, conv_n, h_n.reshape(bs, D_LRU))
    return out_p, out_s, states


def kernel(x_prompt, x_sample, state_pool, state_conv, state_h, w_in, pool_w, pool_b, pool_scale, conv_w, conv_b, rg_w, rg_b, ig_w, ig_b, lru_lambda, w_out, ln1_g, ln1_b, router_group_w, router_group_b, router_expert_w, router_expert_b, expert_w1, expert_w3, expert_w2, ln2_g, ln2_b):
    depth = w_in.shape[0]
    alpha = (2.0 * depth) ** 0.25
    past_len = 16384
    layer_weights = (w_in, pool_w, pool_b, pool_scale, conv_w, conv_b, rg_w, rg_b, ig_w, ig_b, lru_lambda, w_out,
                     ln1_g, ln1_b, router_group_w, router_group_b, router_expert_w, router_expert_b,
                     expert_w1, expert_w3, expert_w2, ln2_g, ln2_b)
    yp, ys = x_prompt, x_sample
    outs = [[] for _ in range(6)]
    for layer in range(depth):
        lw = tuple(w[layer] for w in layer_weights)
        yp, ys, states = _layer(yp, ys, state_pool[layer], state_conv[layer], state_h[layer], lw, alpha, past_len)
        for acc, s in zip(outs, states):
            acc.append(s)
    return (yp, ys) + tuple(jnp.stack(o, axis=0) for o in outs)
```

```python
import functools

import jax
import jax.numpy as jnp
from jax import lax
from jax.experimental import pallas as pl
from jax.experimental.pallas import tpu as pltpu

F32 = jnp.float32
BF16 = jnp.bfloat16
I32 = jnp.int32

D_MODEL = 2048
D_POOL = 1024
D_LRU = 1024
POOL_WINDOWS = (2, 4, 8, 16)
POOL_GROUP = D_POOL // len(POOL_WINDOWS)
POOL_STATE = max(POOL_WINDOWS) - 1
CONV_WIDTH = 4
CONV_STATE = CONV_WIDTH - 1
LRU_HEADS = 16
LRU_HEAD_DIM = D_LRU // LRU_HEADS
LRU_C = 8.0
N_GROUPS = 4
PER_GROUP = 8
N_EXPERTS = N_GROUPS * PER_GROUP
D_EXPERT = D_MODEL // 4
LN_EPS = 1e-5

SUBLANES = 8
LANES = 128
LANE_CHUNK = 256
POOL_HALO = 16
CONV_HALO = 8
ROWS = 256
TM = 256
ROUTER_ROWS = 8 + N_EXPERTS
ROUTER_PAD = 48
META_LANES = LANES
VMEM_LIMIT = 56 * 1024 * 1024
BULK_DMA_PRIORITY = 1

_NT = (((1,), (1,)), ((), ()))


def _const_spec(shape):
    nd = len(shape)
    return pl.BlockSpec(shape, lambda *_: (0,) * nd, pipeline_mode=pl.Buffered(1))


def _mixer_kernel(x_ref, pool_in, conv_in, h_in, w_in, pool_w, pool_b, pool_scale, conv_w, conv_b,
                  gate_w, gate_b, lam, w_out, ln_g, ln_b, rt_hl, *rest, ns, l, pos0, alpha, aliased):
    if aliased:
        rest = rest[2:]
    (x1_ref, lg_ref, pool_o, conv_o, h_o,
     pool_ext, conv_ext, h_prev, gate_s, a_s, u_s, y_s) = rest
    t = pl.program_id(1)
    rows = ns * l

    @pl.when(t == 0)
    def _():
        pool_ext[:, 0:POOL_HALO - POOL_STATE, :] = jnp.zeros((ns, POOL_HALO - POOL_STATE, D_POOL), F32)
        pool_ext[:, POOL_HALO - POOL_STATE:POOL_HALO, :] = pool_in[...]
        conv_ext[:, 0:CONV_HALO - CONV_STATE, :] = jnp.zeros((ns, CONV_HALO - CONV_STATE, D_LRU), F32)
        conv_ext[:, CONV_HALO - CONV_STATE:CONV_HALO, :] = conv_in[...]
        h_prev[...] = h_in[...]

    xb = x_ref[...].reshape(rows, D_MODEL).astype(BF16)
    pool_ext[:, POOL_HALO:, :] = jnp.dot(
        xb, w_in[:, 0:D_POOL], preferred_element_type=F32).reshape(ns, l, D_POOL)
    conv_ext[:, CONV_HALO:, :] = jnp.dot(
        xb, w_in[:, D_POOL:D_POOL + D_LRU], preferred_element_type=F32).reshape(ns, l, D_LRU)
    gate_s[...] = jnp.dot(xb, w_in[:, D_POOL + D_LRU:], preferred_element_type=F32)

    pos = pos0 + t * l + lax.broadcasted_iota(I32, (ns, l, LANE_CHUNK), 1).reshape(rows, LANE_CHUNK)

    for g, w in enumerate(POOL_WINDOWS):
        cols = slice(g * POOL_GROUP, (g + 1) * POOL_GROUP)
        s = pool_ext[:, :, cols].reshape(ns * (POOL_HALO + l), POOL_GROUP)
        shift = 1
        while shift < w:
            s = s + pltpu.roll(s, shift, 0)
            shift *= 2
        win = s.reshape(ns, POOL_HALO + l, POOL_GROUP)[:, POOL_HALO:, :].reshape(rows, POOL_GROUP)
        u = pool_ext[:, POOL_HALO:, cols].reshape(rows, POOL_GROUP)
        if pos0 >= w - 1:
            inv = 1.0 / w
        else:
            inv = 1.0 / jnp.minimum(pos + 1, w).astype(F32)
        d = win * inv - u
        z = jnp.dot(d.astype(BF16), pool_w[g], preferred_element_type=F32) + pool_b[:, cols]
        y_s[:, cols] = (z * pool_scale[:, cols]).astype(BF16)

    lam_v = lam[...]
    softplus_neg = jnp.maximum(-lam_v, 0.0) + jnp.log1p(jnp.exp(-jnp.abs(lam_v)))
    log_a_scale = -LRU_C * softplus_neg
    sub = lax.broadcasted_iota(I32, (rows // SUBLANES, SUBLANES, LANE_CHUNK), 1)
    for c in range(D_LRU // LANE_CHUNK):
        cols = slice(c * LANE_CHUNK, (c + 1) * LANE_CHUNK)
        ce = conv_ext[:, :, cols].reshape(ns * (CONV_HALO + l), LANE_CHUNK)
        conv = conv_b[:, cols] + conv_w[CONV_WIDTH - 1:CONV_WIDTH, cols] * ce
        for k in range(1, CONV_WIDTH):
            conv = conv + conv_w[CONV_WIDTH - 1 - k:CONV_WIDTH - k, cols] * pltpu.roll(ce, k, 0)
        conv = conv.reshape(ns, CONV_HALO + l, LANE_CHUNK)[:, CONV_HALO:, :].reshape(rows, LANE_CHUNK)
        gz = jnp.dot(conv.astype(BF16), gate_w[c], preferred_element_type=F32)
        gz = gz + gate_b[:, 2 * c * LANE_CHUNK:2 * (c + 1) * LANE_CHUNK]
        r = 0.5 * jnp.tanh(0.5 * gz[:, :LANE_CHUNK]) + 0.5
        i = 0.5 * jnp.tanh(0.5 * gz[:, LANE_CHUNK:]) + 0.5
        log_a = r * log_a_scale[:, cols]
        a = jnp.exp(log_a)
        th = jnp.tanh(log_a)
        q = (th + th) / (th - 1.0)
        mult = jnp.where(q > 0.0, q * lax.rsqrt(q), 0.0)
        if pos0 == 0:
            mult = jnp.where(pos == 0, 1.0, mult)
        u = mult * (i * conv)
        a3 = a.reshape(rows // SUBLANES, SUBLANES, LANE_CHUNK)
        u3 = u.reshape(rows // SUBLANES, SUBLANES, LANE_CHUNK)
        for sh in (1, 2, 4):
            keep = sub >= sh
            a_sh = jnp.where(keep, pltpu.roll(a3, sh, 1), 1.0)
            u_sh = jnp.where(keep, pltpu.roll(u3, sh, 1), 0.0)
            u3 = a3 * u_sh + u3
            a3 = a3 * a_sh
        a_s[:, :, cols] = a3.reshape(ns, l, LANE_CHUNK)
        u_s[:, :, cols] = u3.reshape(ns, l, LANE_CHUNK)

    hp = h_prev[...]
    for gi in range(l // SUBLANES):
        sl = slice(gi * SUBLANES, (gi + 1) * SUBLANES)
        h = a_s[:, sl, :] * hp + u_s[:, sl, :]
        u_s[:, sl, :] = h
        hp = h[:, SUBLANES - 1:SUBLANES, :]
    h_prev[...] = hp

    for c in range(D_LRU // LANE_CHUNK):
        cols = slice(c * LANE_CHUNK, (c + 1) * LANE_CHUNK)
        h = u_s[:, :, cols].reshape(rows, LANE_CHUNK)
        y_s[:, D_POOL + c * LANE_CHUNK:D_POOL + (c + 1) * LANE_CHUNK] = (
            h * jax.nn.gelu(gate_s[:, cols])).astype(BF16)

    mix = jnp.dot(y_s[...], w_out[...], preferred_element_type=F32)
    xin = alpha * x_ref[...].reshape(rows, D_MODEL) + mix
    mu = jnp.mean(xin, axis=-1, keepdims=True)
    xc = xin - mu
    var = jnp.mean(xc * xc, axis=-1, keepdims=True)
    x1 = xc * lax.rsqrt(var + LN_EPS) * ln_g[...] + ln_b[...]
    x1_ref[...] = x1

    hi = x1.astype(BF16)
    lo = (x1 - hi.astype(F32)).astype(BF16)
    both = lax.dot_general(rt_hl[...], hi, _NT, preferred_element_type=F32)
    lg = both[0:ROUTER_ROWS] + both[ROUTER_PAD:ROUTER_PAD + ROUTER_ROWS]
    lg = lg + lax.dot_general(rt_hl[0:ROUTER_PAD, :], lo, _NT, preferred_element_type=F32)[0:ROUTER_ROWS]
    lg_ref[...] = lg

    pool_o[...] = pool_ext[:, POOL_HALO + l - POOL_STATE:, :]
    conv_o[...] = conv_ext[:, CONV_HALO + l - CONV_STATE:, :]
    h_o[...] = hp
    pool_ext[:, 0:POOL_HALO, :] = pool_ext[:, l:l + POOL_HALO, :]
    conv_ext[:, 0:CONV_HALO, :] = conv_ext[:, l:l + CONV_HALO, :]


def _mixer_call(x, pool_in, conv_in, h_in, wts, x1_buf, lg_buf, *, ns, l, pos0, alpha, n_total, blk0):
    bsz, tlen, _ = x.shape
    nb, nt = bsz // ns, tlen // l
    rows = ns * l
    assert rows == ROWS and bsz % ns == 0 and tlen % l == 0 and l % SUBLANES == 0
    aliased = x1_buf is not None

    def row_blk(b, t):
        return (blk0 + b * nt + t, 0)

    in_specs = [
        pl.BlockSpec((ns, l, D_MODEL), lambda b, t: (b, t, 0)),
        pl.BlockSpec((ns, POOL_STATE, D_POOL), lambda b, t: (b, 0, 0)),
        pl.BlockSpec((ns, CONV_STATE, D_LRU), lambda b, t: (b, 0, 0)),
        pl.BlockSpec((ns, 1, D_LRU), lambda b, t: (b, 0, 0)),
    ] + [_const_spec(w.shape) for w in wts]
    args = [x, pool_in, conv_in, h_in, *wts]
    aliases = {}
    if aliased:
        in_specs += [pl.BlockSpec(memory_space=pl.ANY), pl.BlockSpec(memory_space=pl.ANY)]
        aliases = {len(args): 0, len(args) + 1: 1}
        args += [x1_buf, lg_buf]
    out_shape = (
        jax.ShapeDtypeStruct((n_total, D_MODEL), F32),
        jax.ShapeDtypeStruct((ROUTER_ROWS, n_total), F32),
        jax.ShapeDtypeStruct((bsz, POOL_STATE, D_POOL), F32),
        jax.ShapeDtypeStruct((bsz, CONV_STATE, D_LRU), F32),
        jax.ShapeDtypeStruct((bsz, 1, D_LRU), F32),
    )
    out_specs = (
        pl.BlockSpec((rows, D_MODEL), row_blk),
        pl.BlockSpec((ROUTER_ROWS, rows), lambda b, t: (0, blk0 + b * nt + t)),
        pl.BlockSpec((ns, POOL_STATE, D_POOL), lambda b, t: (b, 0, 0)),
        pl.BlockSpec((ns, CONV_STATE, D_LRU), lambda b, t: (b, 0, 0)),
        pl.BlockSpec((ns, 1, D_LRU), lambda b, t: (b, 0, 0)),
    )
    scratch = [
        pltpu.VMEM((ns, POOL_HALO + l, D_POOL), F32),
        pltpu.VMEM((ns, CONV_HALO + l, D_LRU), F32),
        pltpu.VMEM((ns, 1, D_LRU), F32),
        pltpu.VMEM((rows, D_LRU), F32),
        pltpu.VMEM((ns, l, D_LRU), F32),
        pltpu.VMEM((ns, l, D_LRU), F32),
        pltpu.VMEM((rows, D_MODEL), BF16),
    ]
    return pl.pallas_call(
        functools.partial(_mixer_kernel, ns=ns, l=l, pos0=pos0, alpha=alpha, aliased=aliased),
        grid=(nb, nt),
        in_specs=in_specs,
        out_specs=out_specs,
        out_shape=out_shape,
        scratch_shapes=scratch,
        input_output_aliases=aliases,
        compiler_params=pltpu.CompilerParams(
            dimension_semantics=("arbitrary", "arbitrary"), vmem_limit_bytes=VMEM_LIMIT),
        name="mixer_seq" if nt > 1 else "mixer_step",
    )(*args)


def _route_kernel(lg_ref, bias_ref, dest_ref, tw_ref, meta_ref, e_s, r_s, *, n):
    blk = 256
    nblk = n // blk
    neg_inf = -jnp.inf
    ridx8 = lax.broadcasted_iota(I32, (SUBLANES, blk), 0).astype(F32)
    eidx = lax.broadcasted_iota(I32, (N_EXPERTS, blk), 0).astype(F32)
    tri = (lax.broadcasted_iota(I32, (blk, blk), 0) < lax.broadcasted_iota(I32, (blk, blk), 1))
    tri = jnp.where(tri, 1.0, 0.0).astype(BF16)

    def first_idx(vals, m):
        return jnp.min(jnp.where(vals == m, ridx8, float(SUBLANES)), axis=0, keepdims=True)

    def pass1(j, base):
        ls = pl.ds(pl.multiple_of(j * blk, blk), blk)
        lg = lg_ref[0:SUBLANES, ls] + bias_ref[0:SUBLANES, :]
        lg = jnp.where(ridx8 < N_GROUPS, lg, neg_inf)
        m = jnp.max(lg, axis=0, keepdims=True)
        gi = first_idx(lg, m)
        pg_sel = 1.0 / jnp.sum(jnp.exp(lg - m), axis=0, keepdims=True)
        le = jnp.zeros((PER_GROUP, blk), F32)
        for g in range(N_GROUPS):
            rows = slice(SUBLANES + g * PER_GROUP, SUBLANES + (g + 1) * PER_GROUP)
            le = jnp.where(gi == float(g), lg_ref[rows, ls] + bias_ref[rows, :], le)
        m1 = jnp.max(le, axis=0, keepdims=True)
        i1 = first_idx(le, m1)
        le2 = jnp.where(ridx8 == i1, neg_inf, le)
        m2 = jnp.max(le2, axis=0, keepdims=True)
        i2 = first_idx(le2, m2)
        e21 = jnp.exp(m2 - m1)
        denom = 1.0 / (1.0 + e21)
        tw_ref[0:1, ls] = pg_sel * denom
        tw_ref[1:2, ls] = pg_sel * (e21 * denom)
        e1 = gi * float(PER_GROUP) + i1
        e2 = gi * float(PER_GROUP) + i2
        e_s[0:1, ls] = e1
        e_s[1:2, ls] = e2
        oh1 = jnp.where(eidx == e1, 1.0, 0.0)
        oh2 = jnp.where(eidx == e2, 1.0, 0.0)
        oh = oh1 + oh2
        before = base + jnp.dot(oh.astype(BF16), tri, preferred_element_type=F32)
        r_s[0:1, ls] = jnp.sum(oh1 * before, axis=0, keepdims=True)
        r_s[1:2, ls] = jnp.sum(oh2 * before, axis=0, keepdims=True)
        return base + jnp.sum(oh, axis=1, keepdims=True)

    cnt = lax.fori_loop(0, nblk, pass1, jnp.zeros((N_EXPERTS, blk), F32))

    ntile = jnp.floor((cnt + float(TM - 1)) * (1.0 / TM))
    lt = (lax.broadcasted_iota(I32, (N_EXPERTS, N_EXPERTS), 1) < lax.broadcasted_iota(I32, (N_EXPERTS, N_EXPERTS), 0))
    lt = jnp.where(lt, 1.0, 0.0).astype(BF16)
    tile0 = jnp.dot(lt, ntile.astype(BF16), preferred_element_type=F32)
    row0 = tile0 * float(TM)

    def pass2(j, carry):
        ls = pl.ds(pl.multiple_of(j * blk, blk), blk)
        for k in range(2):
            ohk = jnp.where(eidx == e_s[k:k + 1, ls], 1.0, 0.0)
            dest = r_s[k:k + 1, ls] + jnp.sum(ohk * row0, axis=0, keepdims=True)
            dest_ref[k:k + 1, ls] = dest.astype(I32)
        return carry

    lax.fori_loop(0, nblk, pass2, 0)

    ml = slice(0, META_LANES)
    total = jnp.sum(ntile[:, ml], axis=0, keepdims=True)
    tile_i = lax.broadcasted_iota(I32, (N_EXPERTS, META_LANES), 1).astype(F32)
    tile_c = jnp.minimum(tile_i, total - 1.0)
    tile_end = tile0[:, ml] + ntile[:, ml]
    te = jnp.sum(jnp.where(tile_end <= tile_c, 1.0, 0.0), axis=0, keepdims=True)
    meta_ref[...] = jnp.zeros((SUBLANES, META_LANES), I32)
    meta_ref[0:1, :] = te.astype(I32)
    meta_ref[1:2, :] = total.astype(I32)
    later = (eidx[:, ml] > te) & (ntile[:, ml] > 0.0)
    nxt = jnp.min(jnp.where(later, eidx[:, ml], float(N_EXPERTS)), axis=0, keepdims=True)
    meta_ref[2:3, :] = nxt.astype(I32)
    left = jnp.sum(jnp.where(eidx[:, ml] == te, cnt[:, ml] - (tile_i - tile0[:, ml]) * float(TM), 0.0),
                   axis=0, keepdims=True)
    nvalid = jnp.where(tile_i[0:1] < total, jnp.clip(left, 0.0, float(TM)), 0.0)
    meta_ref[3:4, :] = nvalid.astype(I32)


def _route_call(lg, bias, n):
    return pl.pallas_call(
        functools.partial(_route_kernel, n=n),
        out_shape=(
            jax.ShapeDtypeStruct((2, n), I32),
            jax.ShapeDtypeStruct((2, n), F32),
            jax.ShapeDtypeStruct((SUBLANES, META_LANES), I32),
        ),
        scratch_shapes=[pltpu.VMEM((2, n), F32), pltpu.VMEM((2, n), F32)],
        compiler_params=pltpu.CompilerParams(vmem_limit_bytes=VMEM_LIMIT),
        name="route",
    )(lg, bias)


def _inverse_kernel(dest_ref, init_hbm, src_ref, sem, *, n):
    init = pltpu.make_async_copy(init_hbm, src_ref, sem)
    init.start()
    init.wait()

    def fill(t, c):
        src_ref[dest_ref[t]] = t
        src_ref[dest_ref[n + t]] = t
        return c

    lax.fori_loop(0, n, fill, 0, unroll=8)


def _inverse_call(dest_flat, n, n_slots):
    return pl.pallas_call(
        functools.partial(_inverse_kernel, n=n),
        in_specs=[pl.BlockSpec(memory_space=pltpu.SMEM), pl.BlockSpec(memory_space=pl.ANY)],
        out_specs=pl.BlockSpec(memory_space=pltpu.SMEM),
        out_shape=jax.ShapeDtypeStruct((n_slots,), I32),
        scratch_shapes=[pltpu.SemaphoreType.DMA(())],
        name="inverse",
    )(dest_flat, jnp.arange(n_slots, dtype=I32) % n)


def _row_copy(src_hbm, row, dst_vmem, slot, sem):
    return pltpu.make_async_copy(src_hbm.at[pl.ds(row, 1), :], dst_vmem.at[pl.ds(slot, 1), :], sem)


def _tile_copy(src_hbm, dst_vmem, sem):
    return pltpu.make_async_copy(src_hbm.at[pl.ds(0, dst_vmem.shape[0]), :], dst_vmem, sem)


def _moe_kernel(te_ref, ne_ref, nv_ref, nt_ref, src_cur, src_nxt, x1_hbm, w1_hbm, w3_hbm, w2_hbm, o_hbm,
                xbuf, obuf, wf1, wf3, wf2, w1b, w3b, w2b, wslot, gsem, osem, wsem):
    i = pl.program_id(0)
    nt = nt_ref[0]

    def gather_start(src_ref, slot):
        for r in range(TM):
            _row_copy(x1_hbm, src_ref[0, 0, r], xbuf.at[slot], r, gsem.at[slot]).start()

    def gather_start_rows(src_ref, slot, nv):
        def body(r, c):
            _row_copy(x1_hbm, src_ref[0, 0, r], xbuf.at[slot], r, gsem.at[slot]).start()
            return c

        lax.fori_loop(0, nv, body, 0)

    def gather_wait(slot, nv):
        @pl.when(nv == TM)
        def _():
            _tile_copy(x1_hbm, xbuf.at[slot], gsem.at[slot]).wait()

        @pl.when(nv < TM)
        def _():
            def body(r, c):
                _row_copy(x1_hbm, 0, xbuf.at[slot], r, gsem.at[slot]).wait()
                return c

            lax.fori_loop(0, nv, body, 0)

    def out_copy(tile, slot):
        return pltpu.make_async_copy(obuf.at[slot], o_hbm.at[pl.ds(tile * TM, TM), :], osem.at[slot])

    def weight_copies(e, slot):
        return (pltpu.make_async_copy(w1_hbm.at[e], wf1.at[slot], wsem.at[slot, 0]),
                pltpu.make_async_copy(w3_hbm.at[e], wf3.at[slot], wsem.at[slot, 1]),
                pltpu.make_async_copy(w2_hbm.at[e], wf2.at[slot], wsem.at[slot, 2]))

    @pl.when(i == 0)
    def _():
        for cp in weight_copies(te_ref[0], 0):
            cp.start(priority=BULK_DMA_PRIORITY)
        wslot[0] = 0
        xbuf[...] = jnp.zeros_like(xbuf)
        gather_start_rows(src_cur, 0, nv_ref[0])

    @pl.when(i < nt)
    def _():
        slot = lax.rem(i, 2)
        nv_next = nv_ref[i + 1]

        @pl.when(i >= 2)
        def _():
            out_copy(0, slot).wait()

        gather_wait(slot, nv_ref[i])

        @pl.when((i == 0) | (te_ref[i] != te_ref[jnp.maximum(i - 1, 0)]))
        def _():
            cur = wslot[0]
            for cp in weight_copies(te_ref[i], cur):
                cp.wait()
            w1b[...] = wf1[cur].astype(BF16)
            w3b[...] = wf3[cur].astype(BF16)
            w2b[...] = wf2[cur].astype(BF16)

            @pl.when(ne_ref[i] < N_EXPERTS)
            def _():
                for cp in weight_copies(ne_ref[i], 1 - cur):
                    cp.start(priority=BULK_DMA_PRIORITY)

            wslot[0] = 1 - cur

        def ffn(request_full_next):
            xb = xbuf[slot].astype(BF16)
            if request_full_next:
                gather_start(src_nxt, 1 - slot)
            h1 = jnp.dot(xb, w1b[...], preferred_element_type=F32)
            h3 = jnp.dot(xb, w3b[...], preferred_element_type=F32)
            h = (jax.nn.silu(h1) * h3).astype(BF16)
            obuf[slot] = jnp.dot(h, w2b[...], preferred_element_type=F32)
            out_copy(i, slot).start(priority=BULK_DMA_PRIORITY)

        @pl.when(nv_next == TM)
        def _():
            ffn(True)

        @pl.when(nv_next < TM)
        def _():
            gather_start_rows(src_nxt, 1 - slot, nv_next)
            ffn(False)

        @pl.when(i + 1 >= nt)
        def _():
            out_copy(0, slot).wait()

            @pl.when(i >= 1)
            def _():
                out_copy(0, 1 - slot).wait()


def _moe_call(te, ne, nvalid, ntiles, src3, x1, w1, w3, w2, max_tiles):
    grid_spec = pltpu.PrefetchScalarGridSpec(
        num_scalar_prefetch=4,
        grid=(max_tiles,),
        in_specs=[
            pl.BlockSpec((1, 1, TM), lambda i, *_: (i, 0, 0), memory_space=pltpu.SMEM),
            pl.BlockSpec((1, 1, TM), lambda i, *_: (jnp.minimum(i + 1, max_tiles - 1), 0, 0),
                         memory_space=pltpu.SMEM),
            pl.BlockSpec(memory_space=pl.ANY),
            pl.BlockSpec(memory_space=pl.ANY),
            pl.BlockSpec(memory_space=pl.ANY),
            pl.BlockSpec(memory_space=pl.ANY),
        ],
        out_specs=pl.BlockSpec(memory_space=pl.ANY),
        scratch_shapes=[
            pltpu.VMEM((2, TM, D_MODEL), F32),
            pltpu.VMEM((2, TM, D_MODEL), F32),
            pltpu.VMEM((2, D_MODEL, D_EXPERT), F32),
            pltpu.VMEM((2, D_MODEL, D_EXPERT), F32),
            pltpu.VMEM((2, D_EXPERT, D_MODEL), F32),
            pltpu.VMEM((D_MODEL, D_EXPERT), BF16),
            pltpu.VMEM((D_MODEL, D_EXPERT), BF16),
            pltpu.VMEM((D_EXPERT, D_MODEL), BF16),
            pltpu.SMEM((1,), I32),
            pltpu.SemaphoreType.DMA((2,)),
            pltpu.SemaphoreType.DMA((2,)),
            pltpu.SemaphoreType.DMA((2, 3)),
        ],
    )
    return pl.pallas_call(
        _moe_kernel,
        grid_spec=grid_spec,
        out_shape=jax.ShapeDtypeStruct((max_tiles * TM, D_MODEL), F32),
        compiler_params=pltpu.CompilerParams(dimension_semantics=("arbitrary",), vmem_limit_bytes=VMEM_LIMIT),
        name="moe",
    )(te, ne, nvalid, ntiles, src3, src3, x1, w1, w3, w2)


def _combine_kernel(dest_cur, dest_nxt, x1_ref, tw_ref, ys_hbm, ln_g, ln_b, o_ref, buf, sem, *, alpha, nsteps):
    i = pl.program_id(0)
    slot = lax.rem(i, 2)

    def gather_start(dest_ref, s):
        for r in range(ROWS):
            for k in range(2):
                _row_copy(ys_hbm, dest_ref[0, k, r], buf.at[s, k], r, sem.at[s, k]).start(priority=k)

    def gather_wait(s):
        for k in range(2):
            _tile_copy(ys_hbm, buf.at[s, k], sem.at[s, k]).wait()

    @pl.when(i == 0)
    def _():
        gather_start(dest_cur, 0)

    gather_wait(slot)
    tw = tw_ref[...]
    moe = tw[:, 0:1] * buf[slot, 0] + tw[:, 1:2] * buf[slot, 1]
    gather_start(dest_nxt, 1 - slot)
    xin = alpha * x1_ref[...] + moe
    mu = jnp.mean(xin, axis=-1, keepdims=True)
    xc = xin - mu
    var = jnp.mean(xc * xc, axis=-1, keepdims=True)
    y = xc * lax.rsqrt(var + LN_EPS) * ln_g[...] + ln_b[...]
    o_ref[...] = y.reshape(o_ref.shape)

    @pl.when(i == nsteps - 1)
    def _():
        gather_wait(1 - slot)


def _combine_call(dest3, x1, tw, ys, ln_g, ln_b, out_shape, blk, blk0, alpha):
    bsz, tlen, _ = out_shape
    nt = tlen // blk[1]
    nsteps = (bsz // blk[0]) * nt
    return pl.pallas_call(
        functools.partial(_combine_kernel, alpha=alpha, nsteps=nsteps),
        grid=(nsteps,),
        in_specs=[
            pl.BlockSpec((1, 2, ROWS), lambda i: (blk0 + i, 0, 0), memory_space=pltpu.SMEM),
            pl.BlockSpec((1, 2, ROWS), lambda i: (blk0 + jnp.minimum(i + 1, nsteps - 1), 0, 0),
                         memory_space=pltpu.SMEM),
            pl.BlockSpec((ROWS, D_MODEL), lambda i: (blk0 + i, 0)),
            pl.BlockSpec((ROWS, 2), lambda i: (blk0 + i, 0)),
            pl.BlockSpec(memory_space=pl.ANY),
            _const_spec(ln_g.shape),
            _const_spec(ln_b.shape),
        ],
        out_specs=pl.BlockSpec(blk, lambda i: (i // nt, i % nt, 0)),
        out_shape=jax.ShapeDtypeStruct(out_shape, F32),
        scratch_shapes=[
            pltpu.VMEM((2, 2, ROWS, D_MODEL), F32),
            pltpu.SemaphoreType.DMA((2, 2)),
        ],
        compiler_params=pltpu.CompilerParams(dimension_semantics=("arbitrary",), vmem_limit_bytes=VMEM_LIMIT),
        name="combine",
    )(dest3, dest3, x1, tw, ys, ln_g, ln_b)


def _block_diag_chunks(w):
    per = LANE_CHUNK // LRU_HEAD_DIM
    nchunk = LRU_HEADS // per
    w4 = w.reshape(nchunk, per, LRU_HEAD_DIM, LRU_HEAD_DIM)
    bd = jnp.einsum("cjio,jk->cjiko", w4, jnp.eye(per, dtype=w.dtype))
    return bd.reshape(nchunk, LANE_CHUNK, LANE_CHUNK)


def _layer(yp, ys, pool_s, conv_s, h_s, lw, alpha, past_len):
    (w_in, pool_w, pool_b, pool_scale, conv_w, conv_b, rg_w, rg_b, ig_w, ig_b, lru_lambda, w_out,
     ln1_g, ln1_b, rgw, rgb, rew, reb, w1, w3, w2, ln2_g, ln2_b) = lw
    bp, tp, _ = yp.shape
    bs, ts, _ = ys.shape
    n_p, n_s = bp * tp, bs * ts
    n = n_p + n_s
    assert n % ROWS == 0 and n_p % ROWS == 0
    max_tiles = (2 * n) // TM + N_EXPERTS
    assert max_tiles <= META_LANES

    nchunk = D_LRU // LANE_CHUNK
    gate_w = jnp.concatenate([_block_diag_chunks(rg_w), _block_diag_chunks(ig_w)], axis=-1).astype(BF16)
    gate_b = jnp.concatenate(
        [rg_b.reshape(nchunk, LANE_CHUNK), ig_b.reshape(nchunk, LANE_CHUNK)], axis=-1).reshape(1, 2 * D_LRU)
    rt = jnp.concatenate(
        [rgw.T, jnp.zeros((SUBLANES - N_GROUPS, D_MODEL), F32),
         jnp.transpose(rew, (0, 2, 1)).reshape(N_EXPERTS, D_MODEL)], axis=0)
    rt_hi = rt.astype(BF16)
    rt_lo = (rt - rt_hi.astype(F32)).astype(BF16)
    r_bias = jnp.concatenate([rgb, jnp.zeros((SUBLANES - N_GROUPS,), F32), reb.reshape(N_EXPERTS)]).reshape(ROUTER_ROWS, 1)
    wts = (
        w_in.astype(BF16), pool_w.astype(BF16), pool_b.reshape(1, D_POOL), pool_scale.reshape(1, D_POOL),
        conv_w, conv_b.reshape(1, D_LRU), gate_w, gate_b, lru_lambda.reshape(1, D_LRU), w_out.astype(BF16),
        ln1_g.reshape(1, D_MODEL), ln1_b.reshape(1, D_MODEL),
        jnp.pad(jnp.stack([rt_hi, rt_lo]), ((0, 0), (0, ROUTER_PAD - ROUTER_ROWS), (0, 0))).reshape(2 * ROUTER_PAD, D_MODEL),
    )

    x1, lg, pool_p, conv_p, h_p = _mixer_call(
        yp, jnp.zeros((bp, POOL_STATE, D_POOL), F32), jnp.zeros((bp, CONV_STATE, D_LRU), F32),
        jnp.zeros((bp, 1, D_LRU), F32), wts, None, None,
        ns=1, l=ROWS, pos0=0, alpha=alpha, n_total=n, blk0=0)
    x1, lg, pool_n, conv_n, h_n = _mixer_call(
        ys, pool_s, conv_s, h_s.reshape(bs, 1, D_LRU), wts, x1, lg,
        ns=ROWS // ts, l=ts, pos0=past_len, alpha=alpha, n_total=n, blk0=n_p // ROWS)

    dest, tw, meta = _route_call(lg, r_bias, n)
    src = _inverse_call(dest.reshape(2 * n), n, max_tiles * TM)
    ysort = _moe_call(meta[0], meta[2], meta[3], meta[1, 0:1], src.reshape(max_tiles, 1, TM), x1, w1, w3, w2,
                      max_tiles)

    dest3 = jnp.transpose(dest.reshape(2, n // ROWS, ROWS), (1, 0, 2))
    tw_t = tw.T
    g2, b2 = ln2_g.reshape(1, D_MODEL), ln2_b.reshape(1, D_MODEL)
    out_p = _combine_call(dest3, x1, tw_t, ysort, g2, b2, (bp, tp, D_MODEL), (1, ROWS, D_MODEL), 0, alpha)
    out_s = _combine_call(dest3, x1, tw_t, ysort, g2, b2, (bs, ts, D_MODEL), (ROWS // ts, ts, D_MODEL),
                          n_p // ROWS, alpha)
    states = (pool_p, conv_p, h_p.reshape(bp, D_LRU), pool_n, conv_n, h_n.reshape(bs, D_LRU))
    return out_p, out_s, states


def kernel(x_prompt, x_sample, state_pool, state_conv, state_h, w_in, pool_w, pool_b, pool_scale, conv_w, conv_b, rg_w, rg_b, ig_w, ig_b, lru_lambda, w_out, ln1_g, ln1_b, router_group_w, router_group_b, router_expert_w, router_expert_b, expert_w1, expert_w3, expert_w2, ln2_g, ln2_b):
    depth = w_in.shape[0]
    alpha = (2.0 * depth) ** 0.25
    past_len = 16384
    layer_weights = (w_in, pool_w, pool_b, pool_scale, conv_w, conv_b, rg_w, rg_b, ig_w, ig_b, lru_lambda, w_out,
                     ln1_g, ln1_b, router_group_w, router_group_b, router_expert_w, router_expert_b,
                     expert_w1, expert_w3, expert_w2, ln2_g, ln2_b)
    yp, ys = x_prompt, x_sample
    outs = [[] for _ in range(6)]
    for layer in range(depth):
        lw = tuple(w[layer] for w in layer_weights)
        yp, ys, states = _layer(yp, ys, state_pool[layer], state_conv[layer], state_h[layer], lw, alpha, past_len)
        for acc, s in zip(outs, states):
            acc.append(s)
    return (yp, ys) + tuple(jnp.stack(o, axis=0) for o in outs)
```

```python
import functools

import jax
import jax.numpy as jnp
from jax import lax
from jax.experimental import pallas as pl
from jax.experimental.pallas import tpu as pltpu

F32 = jnp.float32
BF16 = jnp.bfloat16
I32 = jnp.int32

D_MODEL = 2048
D_POOL = 1024
D_LRU = 1024
POOL_WINDOWS = (2, 4, 8, 16)
POOL_GROUP = D_POOL // len(POOL_WINDOWS)
POOL_STATE = max(POOL_WINDOWS) - 1
CONV_WIDTH = 4
CONV_STATE = CONV_WIDTH - 1
LRU_HEADS = 16
LRU_HEAD_DIM = D_LRU // LRU_HEADS
LRU_C = 8.0
N_GROUPS = 4
PER_GROUP = 8
N_EXPERTS = N_GROUPS * PER_GROUP
D_EXPERT = D_MODEL // 4
LN_EPS = 1e-5

SUBLANES = 8
LANES = 128
LANE_CHUNK = 256
POOL_HALO = 16
CONV_HALO = 8
ROWS = 256
SEQ_ROWS = 512
TM = 256
ROUTER_ROWS = 8 + N_EXPERTS
ROUTER_PAD = 48
META_LANES = LANES
VMEM_LIMIT = 56 * 1024 * 1024
BULK_DMA_PRIORITY = 1

_NT = (((1,), (1,)), ((), ()))


def _const_spec(shape):
    nd = len(shape)
    return pl.BlockSpec(shape, lambda *_: (0,) * nd, pipeline_mode=pl.Buffered(1))


def _mixer_kernel(x_ref, pool_in, conv_in, h_in, w_in, pool_w, pool_b, pool_scale, conv_w, conv_b,
                  gate_w, gate_b, lam, w_out, ln_g, ln_b, rt_hl, *rest, ns, l, pos0, alpha, aliased):
    if aliased:
        rest = rest[2:]
    (x1_ref, lg_ref, pool_o, conv_o, h_o,
     pool_ext, conv_ext, h_prev, gate_s, a_s, u_s, y_s) = rest
    t = pl.program_id(1)
    rows = ns * l

    @pl.when(t == 0)
    def _():
        pool_ext[:, 0:POOL_HALO - POOL_STATE, :] = jnp.zeros((ns, POOL_HALO - POOL_STATE, D_POOL), F32)
        pool_ext[:, POOL_HALO - POOL_STATE:POOL_HALO, :] = pool_in[...]
        conv_ext[:, 0:CONV_HALO - CONV_STATE, :] = jnp.zeros((ns, CONV_HALO - CONV_STATE, D_LRU), F32)
        conv_ext[:, CONV_HALO - CONV_STATE:CONV_HALO, :] = conv_in[...]
        h_prev[...] = h_in[...]

    xb = x_ref[...].reshape(rows, D_MODEL).astype(BF16)
    pool_ext[:, POOL_HALO:, :] = jnp.dot(
        xb, w_in[:, 0:D_POOL], preferred_element_type=F32).reshape(ns, l, D_POOL)
    conv_ext[:, CONV_HALO:, :] = jnp.dot(
        xb, w_in[:, D_POOL:D_POOL + D_LRU], preferred_element_type=F32).reshape(ns, l, D_LRU)
    gate_s[...] = jnp.dot(xb, w_in[:, D_POOL + D_LRU:], preferred_element_type=F32)

    pos = pos0 + t * l + lax.broadcasted_iota(I32, (ns, l, LANE_CHUNK), 1).reshape(rows, LANE_CHUNK)

    for g, w in enumerate(POOL_WINDOWS):
        cols = slice(g * POOL_GROUP, (g + 1) * POOL_GROUP)
        s = pool_ext[:, :, cols].reshape(ns * (POOL_HALO + l), POOL_GROUP)
        shift = 1
        while shift < w:
            s = s + pltpu.roll(s, shift, 0)
            shift *= 2
        win = s.reshape(ns, POOL_HALO + l, POOL_GROUP)[:, POOL_HALO:, :].reshape(rows, POOL_GROUP)
        u = pool_ext[:, POOL_HALO:, cols].reshape(rows, POOL_GROUP)
        if pos0 >= w - 1:
            inv = 1.0 / w
        else:
            inv = 1.0 / jnp.minimum(pos + 1, w).astype(F32)
        d = win * inv - u
        z = jnp.dot(d.astype(BF16), pool_w[g], preferred_element_type=F32) + pool_b[:, cols]
        y_s[:, cols] = (z * pool_scale[:, cols]).astype(BF16)

    lam_v = lam[...]
    softplus_neg = jnp.maximum(-lam_v, 0.0) + jnp.log1p(jnp.exp(-jnp.abs(lam_v)))
    log_a_scale = -LRU_C * softplus_neg
    sub = lax.broadcasted_iota(I32, (rows // SUBLANES, SUBLANES, LANE_CHUNK), 1)
    for c in range(D_LRU // LANE_CHUNK):
        cols = slice(c * LANE_CHUNK, (c + 1) * LANE_CHUNK)
        ce = conv_ext[:, :, cols].reshape(ns * (CONV_HALO + l), LANE_CHUNK)
        conv = conv_b[:, cols] + conv_w[CONV_WIDTH - 1:CONV_WIDTH, cols] * ce
        for k in range(1, CONV_WIDTH):
            conv = conv + conv_w[CONV_WIDTH - 1 - k:CONV_WIDTH - k, cols] * pltpu.roll(ce, k, 0)
        conv = conv.reshape(ns, CONV_HALO + l, LANE_CHUNK)[:, CONV_HALO:, :].reshape(rows, LANE_CHUNK)
        gz = jnp.dot(conv.astype(BF16), gate_w[c], preferred_element_type=F32)
        gz = gz + gate_b[:, 2 * c * LANE_CHUNK:2 * (c + 1) * LANE_CHUNK]
        r = 0.5 * jnp.tanh(0.5 * gz[:, :LANE_CHUNK]) + 0.5
        i = 0.5 * jnp.tanh(0.5 * gz[:, LANE_CHUNK:]) + 0.5
        log_a = r * log_a_scale[:, cols]
        a = jnp.exp(log_a)
        th = jnp.tanh(log_a)
        q = (th + th) / (th - 1.0)
        mult = jnp.where(q > 0.0, q * lax.rsqrt(q), 0.0)
        if pos0 == 0:
            mult = jnp.where(pos == 0, 1.0, mult)
        u = mult * (i * conv)
        a3 = a.reshape(rows // SUBLANES, SUBLANES, LANE_CHUNK)
        u3 = u.reshape(rows // SUBLANES, SUBLANES, LANE_CHUNK)
        for sh in (1, 2, 4):
            keep = sub >= sh
            a_sh = jnp.where(keep, pltpu.roll(a3, sh, 1), 1.0)
            u_sh = jnp.where(keep, pltpu.roll(u3, sh, 1), 0.0)
            u3 = a3 * u_sh + u3
            a3 = a3 * a_sh
        a_s[:, :, cols] = a3.reshape(ns, l, LANE_CHUNK)
        u_s[:, :, cols] = u3.reshape(ns, l, LANE_CHUNK)

    hp = h_prev[...]
    for gi in range(l // SUBLANES):
        sl = slice(gi * SUBLANES, (gi + 1) * SUBLANES)
        h = a_s[:, sl, :] * hp + u_s[:, sl, :]
        u_s[:, sl, :] = h
        hp = h[:, SUBLANES - 1:SUBLANES, :]
    h_prev[...] = hp

    for c in range(D_LRU // LANE_CHUNK):
        cols = slice(c * LANE_CHUNK, (c + 1) * LANE_CHUNK)
        h = u_s[:, :, cols].reshape(rows, LANE_CHUNK)
        y_s[:, D_POOL + c * LANE_CHUNK:D_POOL + (c + 1) * LANE_CHUNK] = (
            h * jax.nn.gelu(gate_s[:, cols])).astype(BF16)

    mix = jnp.dot(y_s[...], w_out[...], preferred_element_type=F32)
    xin = alpha * x_ref[...].reshape(rows, D_MODEL) + mix
    mu = jnp.mean(xin, axis=-1, keepdims=True)
    xc = xin - mu
    var = jnp.mean(xc * xc, axis=-1, keepdims=True)
    x1 = xc * lax.rsqrt(var + LN_EPS) * ln_g[...] + ln_b[...]
    x1_ref[...] = x1

    hi = x1.astype(BF16)
    lo = (x1 - hi.astype(F32)).astype(BF16)
    both = lax.dot_general(rt_hl[...], hi, _NT, preferred_element_type=F32)
    lg = both[0:ROUTER_ROWS] + both[ROUTER_PAD:ROUTER_PAD + ROUTER_ROWS]
    lg = lg + lax.dot_general(rt_hl[0:ROUTER_PAD, :], lo, _NT, preferred_element_type=F32)[0:ROUTER_ROWS]
    lg_ref[...] = lg

    pool_o[...] = pool_ext[:, POOL_HALO + l - POOL_STATE:, :]
    conv_o[...] = conv_ext[:, CONV_HALO + l - CONV_STATE:, :]
    h_o[...] = hp
    pool_ext[:, 0:POOL_HALO, :] = pool_ext[:, l:l + POOL_HALO, :]
    conv_ext[:, 0:CONV_HALO, :] = conv_ext[:, l:l + CONV_HALO, :]


def _mixer_call(x, pool_in, conv_in, h_in, wts, x1_buf, lg_buf, *, ns, l, pos0, alpha, n_total, blk0):
    bsz, tlen, _ = x.shape
    nb, nt = bsz // ns, tlen // l
    rows = ns * l
    assert rows % ROWS == 0 and bsz % ns == 0 and tlen % l == 0 and l % SUBLANES == 0
    aliased = x1_buf is not None

    def row_blk(b, t):
        return (blk0 + b * nt + t, 0)

    in_specs = [
        pl.BlockSpec((ns, l, D_MODEL), lambda b, t: (b, t, 0)),
        pl.BlockSpec((ns, POOL_STATE, D_POOL), lambda b, t: (b, 0, 0)),
        pl.BlockSpec((ns, CONV_STATE, D_LRU), lambda b, t: (b, 0, 0)),
        pl.BlockSpec((ns, 1, D_LRU), lambda b, t: (b, 0, 0)),
    ] + [_const_spec(w.shape) for w in wts]
    args = [x, pool_in, conv_in, h_in, *wts]
    aliases = {}
    if aliased:
        in_specs += [pl.BlockSpec(memory_space=pl.ANY), pl.BlockSpec(memory_space=pl.ANY)]
        aliases = {len(args): 0, len(args) + 1: 1}
        args += [x1_buf, lg_buf]
    out_shape = (
        jax.ShapeDtypeStruct((n_total, D_MODEL), F32),
        jax.ShapeDtypeStruct((ROUTER_ROWS, n_total), F32),
        jax.ShapeDtypeStruct((bsz, POOL_STATE, D_POOL), F32),
        jax.ShapeDtypeStruct((bsz, CONV_STATE, D_LRU), F32),
        jax.ShapeDtypeStruct((bsz, 1, D_LRU), F32),
    )
    out_specs = (
        pl.BlockSpec((rows, D_MODEL), row_blk),
        pl.BlockSpec((ROUTER_ROWS, rows), lambda b, t: (0, blk0 + b * nt + t)),
        pl.BlockSpec((ns, POOL_STATE, D_POOL), lambda b, t: (b, 0, 0)),
        pl.BlockSpec((ns, CONV_STATE, D_LRU), lambda b, t: (b, 0, 0)),
        pl.BlockSpec((ns, 1, D_LRU), lambda b, t: (b, 0, 0)),
    )
    scratch = [
        pltpu.VMEM((ns, POOL_HALO + l, D_POOL), F32),
        pltpu.VMEM((ns, CONV_HALO + l, D_LRU), F32),
        pltpu.VMEM((ns, 1, D_LRU), F32),
        pltpu.VMEM((rows, D_LRU), F32),
        pltpu.VMEM((ns, l, D_LRU), F32),
        pltpu.VMEM((ns, l, D_LRU), F32),
        pltpu.VMEM((rows, D_MODEL), BF16),
    ]
    return pl.pallas_call(
        functools.partial(_mixer_kernel, ns=ns, l=l, pos0=pos0, alpha=alpha, aliased=aliased),
        grid=(nb, nt),
        in_specs=in_specs,
        out_specs=out_specs,
        out_shape=out_shape,
        scratch_shapes=scratch,
        input_output_aliases=aliases,
        compiler_params=pltpu.CompilerParams(
            dimension_semantics=("arbitrary", "arbitrary"), vmem_limit_bytes=VMEM_LIMIT),
        name="mixer_seq" if nt > 1 else "mixer_step",
    )(*args)


def _route_kernel(lg_ref, bias_ref, dest_ref, tw_ref, meta_ref, e_s, r_s, *, n):
    blk = 256
    nblk = n // blk
    neg_inf = -jnp.inf
    ridx8 = lax.broadcasted_iota(I32, (SUBLANES, blk), 0).astype(F32)
    eidx = lax.broadcasted_iota(I32, (N_EXPERTS, blk), 0).astype(F32)
    tri = (lax.broadcasted_iota(I32, (blk, blk), 0) < lax.broadcasted_iota(I32, (blk, blk), 1))
    tri = jnp.where(tri, 1.0, 0.0).astype(BF16)

    def first_idx(vals, m):
        return jnp.min(jnp.where(vals == m, ridx8, float(SUBLANES)), axis=0, keepdims=True)

    def pass1(j, base):
        ls = pl.ds(pl.multiple_of(j * blk, blk), blk)
        lg = lg_ref[0:SUBLANES, ls] + bias_ref[0:SUBLANES, :]
        lg = jnp.where(ridx8 < N_GROUPS, lg, neg_inf)
        m = jnp.max(lg, axis=0, keepdims=True)
        gi = first_idx(lg, m)
        pg_sel = 1.0 / jnp.sum(jnp.exp(lg - m), axis=0, keepdims=True)
        le = jnp.zeros((PER_GROUP, blk), F32)
        for g in range(N_GROUPS):
            rows = slice(SUBLANES + g * PER_GROUP, SUBLANES + (g + 1) * PER_GROUP)
            le = jnp.where(gi == float(g), lg_ref[rows, ls] + bias_ref[rows, :], le)
        m1 = jnp.max(le, axis=0, keepdims=True)
        i1 = first_idx(le, m1)
        le2 = jnp.where(ridx8 == i1, neg_inf, le)
        m2 = jnp.max(le2, axis=0, keepdims=True)
        i2 = first_idx(le2, m2)
        e21 = jnp.exp(m2 - m1)
        denom = 1.0 / (1.0 + e21)
        tw_ref[0:1, ls] = pg_sel * denom
        tw_ref[1:2, ls] = pg_sel * (e21 * denom)
        e1 = gi * float(PER_GROUP) + i1
        e2 = gi * float(PER_GROUP) + i2
        e_s[0:1, ls] = e1
        e_s[1:2, ls] = e2
        oh1 = jnp.where(eidx == e1, 1.0, 0.0)
        oh2 = jnp.where(eidx == e2, 1.0, 0.0)
        oh = oh1 + oh2
        before = base + jnp.dot(oh.astype(BF16), tri, preferred_element_type=F32)
        r_s[0:1, ls] = jnp.sum(oh1 * before, axis=0, keepdims=True)
        r_s[1:2, ls] = jnp.sum(oh2 * before, axis=0, keepdims=True)
        return base + jnp.sum(oh, axis=1, keepdims=True)

    cnt = lax.fori_loop(0, nblk, pass1, jnp.zeros((N_EXPERTS, blk), F32))

    ntile = jnp.floor((cnt + float(TM - 1)) * (1.0 / TM))
    lt = (lax.broadcasted_iota(I32, (N_EXPERTS, N_EXPERTS), 1) < lax.broadcasted_iota(I32, (N_EXPERTS, N_EXPERTS), 0))
    lt = jnp.where(lt, 1.0, 0.0).astype(BF16)
    tile0 = jnp.dot(lt, ntile.astype(BF16), preferred_element_type=F32)
    row0 = tile0 * float(TM)

    def pass2(j, carry):
        ls = pl.ds(pl.multiple_of(j * blk, blk), blk)
        for k in range(2):
            ohk = jnp.where(eidx == e_s[k:k + 1, ls], 1.0, 0.0)
            dest = r_s[k:k + 1, ls] + jnp.sum(ohk * row0, axis=0, keepdims=True)
            dest_ref[k:k + 1, ls] = dest.astype(I32)
        return carry

    lax.fori_loop(0, nblk, pass2, 0)

    ml = slice(0, META_LANES)
    total = jnp.sum(ntile[:, ml], axis=0, keepdims=True)
    tile_i = lax.broadcasted_iota(I32, (N_EXPERTS, META_LANES), 1).astype(F32)
    tile_c = jnp.minimum(tile_i, total - 1.0)
    tile_end = tile0[:, ml] + ntile[:, ml]
    te = jnp.sum(jnp.where(tile_end <= tile_c, 1.0, 0.0), axis=0, keepdims=True)
    meta_ref[...] = jnp.zeros((SUBLANES, META_LANES), I32)
    meta_ref[0:1, :] = te.astype(I32)
    meta_ref[1:2, :] = total.astype(I32)
    later = (eidx[:, ml] > te) & (ntile[:, ml] > 0.0)
    nxt = jnp.min(jnp.where(later, eidx[:, ml], float(N_EXPERTS)), axis=0, keepdims=True)
    meta_ref[2:3, :] = nxt.astype(I32)


def _route_call(lg, bias, n):
    return pl.pallas_call(
        functools.partial(_route_kernel, n=n),
        out_shape=(
            jax.ShapeDtypeStruct((2, n), I32),
            jax.ShapeDtypeStruct((2, n), F32),
            jax.ShapeDtypeStruct((SUBLANES, META_LANES), I32),
        ),
        scratch_shapes=[pltpu.VMEM((2, n), F32), pltpu.VMEM((2, n), F32)],
        compiler_params=pltpu.CompilerParams(vmem_limit_bytes=VMEM_LIMIT),
        name="route",
    )(lg, bias)


def _inverse_kernel(dest_ref, init_hbm, src_ref, sem, *, n):
    init = pltpu.make_async_copy(init_hbm, src_ref, sem)
    init.start()
    init.wait()

    def fill(t, c):
        src_ref[dest_ref[t]] = t
        src_ref[dest_ref[n + t]] = t
        return c

    lax.fori_loop(0, n, fill, 0, unroll=8)


def _inverse_call(dest_flat, n, n_slots):
    return pl.pallas_call(
        functools.partial(_inverse_kernel, n=n),
        in_specs=[pl.BlockSpec(memory_space=pltpu.SMEM), pl.BlockSpec(memory_space=pl.ANY)],
        out_specs=pl.BlockSpec(memory_space=pltpu.SMEM),
        out_shape=jax.ShapeDtypeStruct((n_slots,), I32),
        scratch_shapes=[pltpu.SemaphoreType.DMA(())],
        name="inverse",
    )(dest_flat, jnp.arange(n_slots, dtype=I32) % n)


def _row_copy(src_hbm, row, dst_vmem, slot, sem):
    return pltpu.make_async_copy(src_hbm.at[pl.ds(row, 1), :], dst_vmem.at[pl.ds(slot, 1), :], sem)


def _tile_copy(src_hbm, dst_vmem, sem):
    return pltpu.make_async_copy(src_hbm.at[pl.ds(0, dst_vmem.shape[0]), :], dst_vmem, sem)


def _moe_kernel(te_ref, ne_ref, nt_ref, src_cur, src_nxt, x1_hbm, w1_hbm, w3_hbm, w2_hbm, o_hbm,
                xbuf, obuf, wf1, wf3, wf2, w1b, w3b, w2b, wslot, gsem, osem, wsem):
    i = pl.program_id(0)
    nt = nt_ref[0]

    def gather_start(src_ref, slot):
        for r in range(TM):
            _row_copy(x1_hbm, src_ref[0, 0, r], xbuf.at[slot], r, gsem.at[slot]).start()

    def out_copy(tile, slot):
        return pltpu.make_async_copy(obuf.at[slot], o_hbm.at[pl.ds(tile * TM, TM), :], osem.at[slot])

    def weight_copies(e, slot):
        return (pltpu.make_async_copy(w1_hbm.at[e], wf1.at[slot], wsem.at[slot, 0]),
                pltpu.make_async_copy(w3_hbm.at[e], wf3.at[slot], wsem.at[slot, 1]),
                pltpu.make_async_copy(w2_hbm.at[e], wf2.at[slot], wsem.at[slot, 2]))

    @pl.when(i == 0)
    def _():
        for cp in weight_copies(te_ref[0], 0):
            cp.start(priority=BULK_DMA_PRIORITY)
        wslot[0] = 0
        gather_start(src_cur, 0)

    @pl.when(i < nt)
    def _():
        slot = lax.rem(i, 2)

        @pl.when(i >= 2)
        def _():
            out_copy(0, slot).wait()

        _tile_copy(x1_hbm, xbuf.at[slot], gsem.at[slot]).wait()

        @pl.when((i == 0) | (te_ref[i] != te_ref[jnp.maximum(i - 1, 0)]))
        def _():
            cur = wslot[0]
            for cp in weight_copies(te_ref[i], cur):
                cp.wait()
            w1b[...] = wf1[cur].astype(BF16)
            w3b[...] = wf3[cur].astype(BF16)
            w2b[...] = wf2[cur].astype(BF16)

            @pl.when(ne_ref[i] < N_EXPERTS)
            def _():
                for cp in weight_copies(ne_ref[i], 1 - cur):
                    cp.start(priority=BULK_DMA_PRIORITY)

            wslot[0] = 1 - cur

        xb = xbuf[slot].astype(BF16)
        gather_start(src_nxt, 1 - slot)
        h1 = jnp.dot(xb, w1b[...], preferred_element_type=F32)
        h3 = jnp.dot(xb, w3b[...], preferred_element_type=F32)
        h = (jax.nn.silu(h1) * h3).astype(BF16)
        obuf[slot] = jnp.dot(h, w2b[...], preferred_element_type=F32)
        out_copy(i, slot).start(priority=BULK_DMA_PRIORITY)

        @pl.when(i + 1 >= nt)
        def _():
            _tile_copy(x1_hbm, xbuf.at[1 - slot], gsem.at[1 - slot]).wait()
            out_copy(0, slot).wait()

            @pl.when(i >= 1)
            def _():
                out_copy(0, 1 - slot).wait()


def _moe_call(te, ne, ntiles, src3, x1, w1, w3, w2, max_tiles):
    grid_spec = pltpu.PrefetchScalarGridSpec(
        num_scalar_prefetch=3,
        grid=(max_tiles,),
        in_specs=[
            pl.BlockSpec((1, 1, TM), lambda i, te, ne, nt: (i, 0, 0), memory_space=pltpu.SMEM),
            pl.BlockSpec((1, 1, TM), lambda i, te, ne, nt: (jnp.minimum(i + 1, max_tiles - 1), 0, 0),
                         memory_space=pltpu.SMEM),
            pl.BlockSpec(memory_space=pl.ANY),
            pl.BlockSpec(memory_space=pl.ANY),
            pl.BlockSpec(memory_space=pl.ANY),
            pl.BlockSpec(memory_space=pl.ANY),
        ],
        out_specs=pl.BlockSpec(memory_space=pl.ANY),
        scratch_shapes=[
            pltpu.VMEM((2, TM, D_MODEL), F32),
            pltpu.VMEM((2, TM, D_MODEL), F32),
            pltpu.VMEM((2, D_MODEL, D_EXPERT), F32),
            pltpu.VMEM((2, D_MODEL, D_EXPERT), F32),
            pltpu.VMEM((2, D_EXPERT, D_MODEL), F32),
            pltpu.VMEM((D_MODEL, D_EXPERT), BF16),
            pltpu.VMEM((D_MODEL, D_EXPERT), BF16),
            pltpu.VMEM((D_EXPERT, D_MODEL), BF16),
            pltpu.SMEM((1,), I32),
            pltpu.SemaphoreType.DMA((2,)),
            pltpu.SemaphoreType.DMA((2,)),
            pltpu.SemaphoreType.DMA((2, 3)),
        ],
    )
    return pl.pallas_call(
        _moe_kernel,
        grid_spec=grid_spec,
        out_shape=jax.ShapeDtypeStruct((max_tiles * TM, D_MODEL), F32),
        compiler_params=pltpu.CompilerParams(dimension_semantics=("arbitrary",), vmem_limit_bytes=VMEM_LIMIT),
        name="moe",
    )(te, ne, ntiles, src3, src3, x1, w1, w3, w2)


def _combine_kernel(dest_cur, dest_nxt, x1_ref, tw_ref, ys_hbm, ln_g, ln_b, o_ref, buf, sem, *, alpha, nsteps):
    i = pl.program_id(0)
    slot = lax.rem(i, 2)

    def gather_start(dest_ref, s):
        for r in range(ROWS):
            for k in range(2):
                _row_copy(ys_hbm, dest_ref[0, k, r], buf.at[s, k], r, sem.at[s, k]).start(priority=k)

    def gather_wait(s):
        for k in range(2):
            _tile_copy(ys_hbm, buf.at[s, k], sem.at[s, k]).wait()

    @pl.when(i == 0)
    def _():
        gather_start(dest_cur, 0)

    gather_wait(slot)
    tw = tw_ref[...]
    moe = tw[:, 0:1] * buf[slot, 0] + tw[:, 1:2] * buf[slot, 1]
    gather_start(dest_nxt, 1 - slot)
    xin = alpha * x1_ref[...] + moe
    mu = jnp.mean(xin, axis=-1, keepdims=True)
    xc = xin - mu
    var = jnp.mean(xc * xc, axis=-1, keepdims=True)
    y = xc * lax.rsqrt(var + LN_EPS) * ln_g[...] + ln_b[...]
    o_ref[...] = y.reshape(o_ref.shape)

    @pl.when(i == nsteps - 1)
    def _():
        gather_wait(1 - slot)


def _combine_call(dest3, x1, tw, ys, ln_g, ln_b, out_shape, blk, blk0, alpha):
    bsz, tlen, _ = out_shape
    nt = tlen // blk[1]
    nsteps = (bsz // blk[0]) * nt
    return pl.pallas_call(
        functools.partial(_combine_kernel, alpha=alpha, nsteps=nsteps),
        grid=(nsteps,),
        in_specs=[
            pl.BlockSpec((1, 2, ROWS), lambda i: (blk0 + i, 0, 0), memory_space=pltpu.SMEM),
            pl.BlockSpec((1, 2, ROWS), lambda i: (blk0 + jnp.minimum(i + 1, nsteps - 1), 0, 0),
                         memory_space=pltpu.SMEM),
            pl.BlockSpec((ROWS, D_MODEL), lambda i: (blk0 + i, 0)),
            pl.BlockSpec((ROWS, 2), lambda i: (blk0 + i, 0)),
            pl.BlockSpec(memory_space=pl.ANY),
            _const_spec(ln_g.shape),
            _const_spec(ln_b.shape),
        ],
        out_specs=pl.BlockSpec(blk, lambda i: (i // nt, i % nt, 0)),
        out_shape=jax.ShapeDtypeStruct(out_shape, F32),
        scratch_shapes=[
            pltpu.VMEM((2, 2, ROWS, D_MODEL), F32),
            pltpu.SemaphoreType.DMA((2, 2)),
        ],
        compiler_params=pltpu.CompilerParams(dimension_semantics=("arbitrary",), vmem_limit_bytes=VMEM_LIMIT),
        name="combine",
    )(dest3, dest3, x1, tw, ys, ln_g, ln_b)


def _block_diag_chunks(w):
    per = LANE_CHUNK // LRU_HEAD_DIM
    nchunk = LRU_HEADS // per
    w4 = w.reshape(nchunk, per, LRU_HEAD_DIM, LRU_HEAD_DIM)
    bd = jnp.einsum("cjio,jk->cjiko", w4, jnp.eye(per, dtype=w.dtype))
    return bd.reshape(nchunk, LANE_CHUNK, LANE_CHUNK)


def _layer(yp, ys, pool_s, conv_s, h_s, lw, alpha, past_len):
    (w_in, pool_w, pool_b, pool_scale, conv_w, conv_b, rg_w, rg_b, ig_w, ig_b, lru_lambda, w_out,
     ln1_g, ln1_b, rgw, rgb, rew, reb, w1, w3, w2, ln2_g, ln2_b) = lw
    bp, tp, _ = yp.shape
    bs, ts, _ = ys.shape
    n_p, n_s = bp * tp, bs * ts
    n = n_p + n_s
    assert n % ROWS == 0 and n_p % ROWS == 0
    max_tiles = (2 * n) // TM + N_EXPERTS
    assert max_tiles <= META_LANES

    nchunk = D_LRU // LANE_CHUNK
    gate_w = jnp.concatenate([_block_diag_chunks(rg_w), _block_diag_chunks(ig_w)], axis=-1).astype(BF16)
    gate_b = jnp.concatenate(
        [rg_b.reshape(nchunk, LANE_CHUNK), ig_b.reshape(nchunk, LANE_CHUNK)], axis=-1).reshape(1, 2 * D_LRU)
    rt = jnp.concatenate(
        [rgw.T, jnp.zeros((SUBLANES - N_GROUPS, D_MODEL), F32),
         jnp.transpose(rew, (0, 2, 1)).reshape(N_EXPERTS, D_MODEL)], axis=0)
    rt_hi = rt.astype(BF16)
    rt_lo = (rt - rt_hi.astype(F32)).astype(BF16)
    r_bias = jnp.concatenate([rgb, jnp.zeros((SUBLANES - N_GROUPS,), F32), reb.reshape(N_EXPERTS)]).reshape(ROUTER_ROWS, 1)
    wts = (
        w_in.astype(BF16), pool_w.astype(BF16), pool_b.reshape(1, D_POOL), pool_scale.reshape(1, D_POOL),
        conv_w, conv_b.reshape(1, D_LRU), gate_w, gate_b, lru_lambda.reshape(1, D_LRU), w_out.astype(BF16),
        ln1_g.reshape(1, D_MODEL), ln1_b.reshape(1, D_MODEL),
        jnp.pad(jnp.stack([rt_hi, rt_lo]), ((0, 0), (0, ROUTER_PAD - ROUTER_ROWS), (0, 0))).reshape(2 * ROUTER_PAD, D_MODEL),
    )

    x1, lg, pool_p, conv_p, h_p = _mixer_call(
        yp, jnp.zeros((bp, POOL_STATE, D_POOL), F32), jnp.zeros((bp, CONV_STATE, D_LRU), F32),
        jnp.zeros((bp, 1, D_LRU), F32), wts, None, None,
        ns=1, l=SEQ_ROWS, pos0=0, alpha=alpha, n_total=n, blk0=0)
    x1, lg, pool_n, conv_n, h_n = _mixer_call(
        ys, pool_s, conv_s, h_s.reshape(bs, 1, D_LRU), wts, x1, lg,
        ns=ROWS // ts, l=ts, pos0=past_len, alpha=alpha, n_total=n, blk0=n_p // ROWS)

    dest, tw, meta = _route_call(lg, r_bias, n)
    src = _inverse_call(dest.reshape(2 * n), n, max_tiles * TM)
    ysort = _moe_call(meta[0], meta[2], meta[1, 0:1], src.reshape(max_tiles, 1, TM), x1, w1, w3, w2, max_tiles)

    dest3 = jnp.transpose(dest.reshape(2, n // ROWS, ROWS), (1, 0, 2))
    tw_t = tw.T
    g2, b2 = ln2_g.reshape(1, D_MODEL), ln2_b.reshape(1, D_MODEL)
    out_p = _combine_call(dest3, x1, tw_t, ysort, g2, b2, (bp, tp, D_MODEL), (1, ROWS, D_MODEL), 0, alpha)
    out_s = _combine_call(dest3, x1, tw_t, ysort, g2, b2, (bs, ts, D_MODEL), (ROWS // ts, ts, D_MODEL),
                          n_p // ROWS, alpha)
    states = (pool_p, conv_p, h_p.reshape(bp, D_LRU), pool_n, conv_n, h_n.reshape(bs, D_LRU))
    return out_p, out_s, states


def kernel(x_prompt, x_sample, state_pool, state_conv, state_h, w_in, pool_w, pool_b, pool_scale, conv_w, conv_b, rg_w, rg_b, ig_w, ig_b, lru_lambda, w_out, ln1_g, ln1_b, router_group_w, router_group_b, router_expert_w, router_expert_b, expert_w1, expert_w3, expert_w2, ln2_g, ln2_b):
    depth = w_in.shape[0]
    alpha = (2.0 * depth) ** 0.25
    past_len = 16384
    layer_weights = (w_in, pool_w, pool_b, pool_scale, conv_w, conv_b, rg_w, rg_b, ig_w, ig_b, lru_lambda, w_out,
                     ln1_g, ln1_b, router_group_w, router_group_b, router_expert_w, router_expert_b,
                     expert_w1, expert_w3, expert_w2, ln2_g, ln2_b)
    yp, ys = x_prompt, x_sample
    outs = [[] for _ in range(6)]
    for layer in range(depth):
        lw = tuple(w[layer] for w in layer_weights)
        yp, ys, states = _layer(yp, ys, state_pool[layer], state_conv[layer], state_h[layer], lw, alpha, past_len)
        for acc, s in zip(outs, states):
            acc.append(s)
    return (yp, ys) + tuple(jnp.stack(o, axis=0) for o in outs)
```

```python
import functools

import jax
import jax.numpy as jnp
from jax import lax
from jax.experimental import pallas as pl
from jax.experimental.pallas import tpu as pltpu

F32 = jnp.float32
BF16 = jnp.bfloat16
I32 = jnp.int32

D_MODEL = 2048
D_POOL = 1024
D_LRU = 1024
POOL_WINDOWS = (2, 4, 8, 16)
POOL_GROUP = D_POOL // len(POOL_WINDOWS)
POOL_STATE = max(POOL_WINDOWS) - 1
CONV_WIDTH = 4
CONV_STATE = CONV_WIDTH - 1
LRU_HEADS = 16
LRU_HEAD_DIM = D_LRU // LRU_HEADS
LRU_C = 8.0
N_GROUPS = 4
PER_GROUP = 8
N_EXPERTS = N_GROUPS * PER_GROUP
D_EXPERT = D_MODEL // 4
LN_EPS = 1e-5

SUBLANES = 8
LANES = 128
LANE_CHUNK = 256
POOL_HALO = 16
CONV_HALO = 8
ROWS = 256
SEQ_ROWS = 512
TM = 256
GATHER_DEPTH = 3
ROUTER_ROWS = 8 + N_EXPERTS
ROUTER_PAD = 48
META_LANES = LANES
VMEM_LIMIT = 56 * 1024 * 1024
BULK_DMA_PRIORITY = 1

_NT = (((1,), (1,)), ((), ()))


def _const_spec(shape):
    nd = len(shape)
    return pl.BlockSpec(shape, lambda *_: (0,) * nd, pipeline_mode=pl.Buffered(1))


def _mixer_kernel(x_ref, pool_in, conv_in, h_in, w_in, pool_w, pool_b, pool_scale, conv_w, conv_b,
                  gate_w, gate_b, lam, w_out, ln_g, ln_b, rt_hl, *rest, ns, l, pos0, alpha, aliased):
    if aliased:
        rest = rest[2:]
    (x1_ref, lg_ref, pool_o, conv_o, h_o,
     pool_ext, conv_ext, h_prev, gate_s, a_s, u_s, y_s) = rest
    t = pl.program_id(1)
    rows = ns * l

    @pl.when(t == 0)
    def _():
        pool_ext[:, 0:POOL_HALO - POOL_STATE, :] = jnp.zeros((ns, POOL_HALO - POOL_STATE, D_POOL), F32)
        pool_ext[:, POOL_HALO - POOL_STATE:POOL_HALO, :] = pool_in[...]
        conv_ext[:, 0:CONV_HALO - CONV_STATE, :] = jnp.zeros((ns, CONV_HALO - CONV_STATE, D_LRU), F32)
        conv_ext[:, CONV_HALO - CONV_STATE:CONV_HALO, :] = conv_in[...]
        h_prev[...] = h_in[...]

    xb = x_ref[...].reshape(rows, D_MODEL).astype(BF16)
    pool_ext[:, POOL_HALO:, :] = jnp.dot(
        xb, w_in[:, 0:D_POOL], preferred_element_type=F32).reshape(ns, l, D_POOL)
    conv_ext[:, CONV_HALO:, :] = jnp.dot(
        xb, w_in[:, D_POOL:D_POOL + D_LRU], preferred_element_type=F32).reshape(ns, l, D_LRU)
    gate_s[...] = jnp.dot(xb, w_in[:, D_POOL + D_LRU:], preferred_element_type=F32)

    pos = pos0 + t * l + lax.broadcasted_iota(I32, (ns, l, LANE_CHUNK), 1).reshape(rows, LANE_CHUNK)

    for g, w in enumerate(POOL_WINDOWS):
        cols = slice(g * POOL_GROUP, (g + 1) * POOL_GROUP)
        s = pool_ext[:, :, cols].reshape(ns * (POOL_HALO + l), POOL_GROUP)
        shift = 1
        while shift < w:
            s = s + pltpu.roll(s, shift, 0)
            shift *= 2
        win = s.reshape(ns, POOL_HALO + l, POOL_GROUP)[:, POOL_HALO:, :].reshape(rows, POOL_GROUP)
        u = pool_ext[:, POOL_HALO:, cols].reshape(rows, POOL_GROUP)
        if pos0 >= w - 1:
            inv = 1.0 / w
        else:
            inv = 1.0 / jnp.minimum(pos + 1, w).astype(F32)
        d = win * inv - u
        z = jnp.dot(d.astype(BF16), pool_w[g], preferred_element_type=F32) + pool_b[:, cols]
        y_s[:, cols] = (z * pool_scale[:, cols]).astype(BF16)

    lam_v = lam[...]
    softplus_neg = jnp.maximum(-lam_v, 0.0) + jnp.log1p(jnp.exp(-jnp.abs(lam_v)))
    log_a_scale = -LRU_C * softplus_neg
    sub = lax.broadcasted_iota(I32, (rows // SUBLANES, SUBLANES, LANE_CHUNK), 1)
    for c in range(D_LRU // LANE_CHUNK):
        cols = slice(c * LANE_CHUNK, (c + 1) * LANE_CHUNK)
        ce = conv_ext[:, :, cols].reshape(ns * (CONV_HALO + l), LANE_CHUNK)
        conv = conv_b[:, cols] + conv_w[CONV_WIDTH - 1:CONV_WIDTH, cols] * ce
        for k in range(1, CONV_WIDTH):
            conv = conv + conv_w[CONV_WIDTH - 1 - k:CONV_WIDTH - k, cols] * pltpu.roll(ce, k, 0)
        conv = conv.reshape(ns, CONV_HALO + l, LANE_CHUNK)[:, CONV_HALO:, :].reshape(rows, LANE_CHUNK)
        gz = jnp.dot(conv.astype(BF16), gate_w[c], preferred_element_type=F32)
        gz = gz + gate_b[:, 2 * c * LANE_CHUNK:2 * (c + 1) * LANE_CHUNK]
        r = 0.5 * jnp.tanh(0.5 * gz[:, :LANE_CHUNK]) + 0.5
        i = 0.5 * jnp.tanh(0.5 * gz[:, LANE_CHUNK:]) + 0.5
        log_a = r * log_a_scale[:, cols]
        a = jnp.exp(log_a)
        th = jnp.tanh(log_a)
        q = (th + th) / (th - 1.0)
        mult = jnp.where(q > 0.0, q * lax.rsqrt(q), 0.0)
        if pos0 == 0:
            mult = jnp.where(pos == 0, 1.0, mult)
        u = mult * (i * conv)
        a3 = a.reshape(rows // SUBLANES, SUBLANES, LANE_CHUNK)
        u3 = u.reshape(rows // SUBLANES, SUBLANES, LANE_CHUNK)
        for sh in (1, 2, 4):
            keep = sub >= sh
            a_sh = jnp.where(keep, pltpu.roll(a3, sh, 1), 1.0)
            u_sh = jnp.where(keep, pltpu.roll(u3, sh, 1), 0.0)
            u3 = a3 * u_sh + u3
            a3 = a3 * a_sh
        a_s[:, :, cols] = a3.reshape(ns, l, LANE_CHUNK)
        u_s[:, :, cols] = u3.reshape(ns, l, LANE_CHUNK)

    hp = h_prev[...]
    for gi in range(l // SUBLANES):
        sl = slice(gi * SUBLANES, (gi + 1) * SUBLANES)
        h = a_s[:, sl, :] * hp + u_s[:, sl, :]
        u_s[:, sl, :] = h
        hp = h[:, SUBLANES - 1:SUBLANES, :]
    h_prev[...] = hp

    for c in range(D_LRU // LANE_CHUNK):
        cols = slice(c * LANE_CHUNK, (c + 1) * LANE_CHUNK)
        h = u_s[:, :, cols].reshape(rows, LANE_CHUNK)
        y_s[:, D_POOL + c * LANE_CHUNK:D_POOL + (c + 1) * LANE_CHUNK] = (
            h * jax.nn.gelu(gate_s[:, cols])).astype(BF16)

    mix = jnp.dot(y_s[...], w_out[...], preferred_element_type=F32)
    xin = alpha * x_ref[...].reshape(rows, D_MODEL) + mix
    mu = jnp.mean(xin, axis=-1, keepdims=True)
    xc = xin - mu
    var = jnp.mean(xc * xc, axis=-1, keepdims=True)
    x1 = xc * lax.rsqrt(var + LN_EPS) * ln_g[...] + ln_b[...]
    x1_ref[...] = x1

    hi = x1.astype(BF16)
    lo = (x1 - hi.astype(F32)).astype(BF16)
    both = lax.dot_general(rt_hl[...], hi, _NT, preferred_element_type=F32)
    lg = both[0:ROUTER_ROWS] + both[ROUTER_PAD:ROUTER_PAD + ROUTER_ROWS]
    lg = lg + lax.dot_general(rt_hl[0:ROUTER_PAD, :], lo, _NT, preferred_element_type=F32)[0:ROUTER_ROWS]
    lg_ref[...] = lg

    pool_o[...] = pool_ext[:, POOL_HALO + l - POOL_STATE:, :]
    conv_o[...] = conv_ext[:, CONV_HALO + l - CONV_STATE:, :]
    h_o[...] = hp
    pool_ext[:, 0:POOL_HALO, :] = pool_ext[:, l:l + POOL_HALO, :]
    conv_ext[:, 0:CONV_HALO, :] = conv_ext[:, l:l + CONV_HALO, :]


def _mixer_call(x, pool_in, conv_in, h_in, wts, x1_buf, lg_buf, *, ns, l, pos0, alpha, n_total, blk0):
    bsz, tlen, _ = x.shape
    nb, nt = bsz // ns, tlen // l
    rows = ns * l
    assert rows % ROWS == 0 and bsz % ns == 0 and tlen % l == 0 and l % SUBLANES == 0
    aliased = x1_buf is not None

    def row_blk(b, t):
        return (blk0 + b * nt + t, 0)

    in_specs = [
        pl.BlockSpec((ns, l, D_MODEL), lambda b, t: (b, t, 0)),
        pl.BlockSpec((ns, POOL_STATE, D_POOL), lambda b, t: (b, 0, 0)),
        pl.BlockSpec((ns, CONV_STATE, D_LRU), lambda b, t: (b, 0, 0)),
        pl.BlockSpec((ns, 1, D_LRU), lambda b, t: (b, 0, 0)),
    ] + [_const_spec(w.shape) for w in wts]
    args = [x, pool_in, conv_in, h_in, *wts]
    aliases = {}
    if aliased:
        in_specs += [pl.BlockSpec(memory_space=pl.ANY), pl.BlockSpec(memory_space=pl.ANY)]
        aliases = {len(args): 0, len(args) + 1: 1}
        args += [x1_buf, lg_buf]
    out_shape = (
        jax.ShapeDtypeStruct((n_total, D_MODEL), F32),
        jax.ShapeDtypeStruct((ROUTER_ROWS, n_total), F32),
        jax.ShapeDtypeStruct((bsz, POOL_STATE, D_POOL), F32),
        jax.ShapeDtypeStruct((bsz, CONV_STATE, D_LRU), F32),
        jax.ShapeDtypeStruct((bsz, 1, D_LRU), F32),
    )
    out_specs = (
        pl.BlockSpec((rows, D_MODEL), row_blk),
        pl.BlockSpec((ROUTER_ROWS, rows), lambda b, t: (0, blk0 + b * nt + t)),
        pl.BlockSpec((ns, POOL_STATE, D_POOL), lambda b, t: (b, 0, 0)),
        pl.BlockSpec((ns, CONV_STATE, D_LRU), lambda b, t: (b, 0, 0)),
        pl.BlockSpec((ns, 1, D_LRU), lambda b, t: (b, 0, 0)),
    )
    scratch = [
        pltpu.VMEM((ns, POOL_HALO + l, D_POOL), F32),
        pltpu.VMEM((ns, CONV_HALO + l, D_LRU), F32),
        pltpu.VMEM((ns, 1, D_LRU), F32),
        pltpu.VMEM((rows, D_LRU), F32),
        pltpu.VMEM((ns, l, D_LRU), F32),
        pltpu.VMEM((ns, l, D_LRU), F32),
        pltpu.VMEM((rows, D_MODEL), BF16),
    ]
    return pl.pallas_call(
        functools.partial(_mixer_kernel, ns=ns, l=l, pos0=pos0, alpha=alpha, aliased=aliased),
        grid=(nb, nt),
        in_specs=in_specs,
        out_specs=out_specs,
        out_shape=out_shape,
        scratch_shapes=scratch,
        input_output_aliases=aliases,
        compiler_params=pltpu.CompilerParams(
            dimension_semantics=("arbitrary", "arbitrary"), vmem_limit_bytes=VMEM_LIMIT),
        name="mixer_seq" if nt > 1 else "mixer_step",
    )(*args)


def _route_kernel(lg_ref, bias_ref, dest_ref, tw_ref, meta_ref, e_s, r_s, *, n):
    blk = 256
    nblk = n // blk
    neg_inf = -jnp.inf
    ridx8 = lax.broadcasted_iota(I32, (SUBLANES, blk), 0).astype(F32)
    eidx = lax.broadcasted_iota(I32, (N_EXPERTS, blk), 0).astype(F32)
    tri = (lax.broadcasted_iota(I32, (blk, blk), 0) < lax.broadcasted_iota(I32, (blk, blk), 1))
    tri = jnp.where(tri, 1.0, 0.0).astype(BF16)

    def first_idx(vals, m):
        return jnp.min(jnp.where(vals == m, ridx8, float(SUBLANES)), axis=0, keepdims=True)

    def pass1(j, base):
        ls = pl.ds(pl.multiple_of(j * blk, blk), blk)
        lg = lg_ref[0:SUBLANES, ls] + bias_ref[0:SUBLANES, :]
        lg = jnp.where(ridx8 < N_GROUPS, lg, neg_inf)
        m = jnp.max(lg, axis=0, keepdims=True)
        gi = first_idx(lg, m)
        pg_sel = 1.0 / jnp.sum(jnp.exp(lg - m), axis=0, keepdims=True)
        le = jnp.zeros((PER_GROUP, blk), F32)
        for g in range(N_GROUPS):
            rows = slice(SUBLANES + g * PER_GROUP, SUBLANES + (g + 1) * PER_GROUP)
            le = jnp.where(gi == float(g), lg_ref[rows, ls] + bias_ref[rows, :], le)
        m1 = jnp.max(le, axis=0, keepdims=True)
        i1 = first_idx(le, m1)
        le2 = jnp.where(ridx8 == i1, neg_inf, le)
        m2 = jnp.max(le2, axis=0, keepdims=True)
        i2 = first_idx(le2, m2)
        e21 = jnp.exp(m2 - m1)
        denom = 1.0 / (1.0 + e21)
        tw_ref[0:1, ls] = pg_sel * denom
        tw_ref[1:2, ls] = pg_sel * (e21 * denom)
        e1 = gi * float(PER_GROUP) + i1
        e2 = gi * float(PER_GROUP) + i2
        e_s[0:1, ls] = e1
        e_s[1:2, ls] = e2
        oh1 = jnp.where(eidx == e1, 1.0, 0.0)
        oh2 = jnp.where(eidx == e2, 1.0, 0.0)
        oh = oh1 + oh2
        before = base + jnp.dot(oh.astype(BF16), tri, preferred_element_type=F32)
        r_s[0:1, ls] = jnp.sum(oh1 * before, axis=0, keepdims=True)
        r_s[1:2, ls] = jnp.sum(oh2 * before, axis=0, keepdims=True)
        return base + jnp.sum(oh, axis=1, keepdims=True)

    cnt = lax.fori_loop(0, nblk, pass1, jnp.zeros((N_EXPERTS, blk), F32))

    ntile = jnp.floor((cnt + float(TM - 1)) * (1.0 / TM))
    lt = (lax.broadcasted_iota(I32, (N_EXPERTS, N_EXPERTS), 1) < lax.broadcasted_iota(I32, (N_EXPERTS, N_EXPERTS), 0))
    lt = jnp.where(lt, 1.0, 0.0).astype(BF16)
    tile0 = jnp.dot(lt, ntile.astype(BF16), preferred_element_type=F32)
    row0 = tile0 * float(TM)

    def pass2(j, carry):
        ls = pl.ds(pl.multiple_of(j * blk, blk), blk)
        for k in range(2):
            ohk = jnp.where(eidx == e_s[k:k + 1, ls], 1.0, 0.0)
            dest = r_s[k:k + 1, ls] + jnp.sum(ohk * row0, axis=0, keepdims=True)
            dest_ref[k:k + 1, ls] = dest.astype(I32)
        return carry

    lax.fori_loop(0, nblk, pass2, 0)

    ml = slice(0, META_LANES)
    total = jnp.sum(ntile[:, ml], axis=0, keepdims=True)
    tile_i = lax.broadcasted_iota(I32, (N_EXPERTS, META_LANES), 1).astype(F32)
    tile_c = jnp.minimum(tile_i, total - 1.0)
    tile_end = tile0[:, ml] + ntile[:, ml]
    te = jnp.sum(jnp.where(tile_end <= tile_c, 1.0, 0.0), axis=0, keepdims=True)
    meta_ref[...] = jnp.zeros((SUBLANES, META_LANES), I32)
    meta_ref[0:1, :] = te.astype(I32)
    meta_ref[1:2, :] = total.astype(I32)
    later = (eidx[:, ml] > te) & (ntile[:, ml] > 0.0)
    nxt = jnp.min(jnp.where(later, eidx[:, ml], float(N_EXPERTS)), axis=0, keepdims=True)
    meta_ref[2:3, :] = nxt.astype(I32)


def _route_call(lg, bias, n):
    return pl.pallas_call(
        functools.partial(_route_kernel, n=n),
        out_shape=(
            jax.ShapeDtypeStruct((2, n), I32),
            jax.ShapeDtypeStruct((2, n), F32),
            jax.ShapeDtypeStruct((SUBLANES, META_LANES), I32),
        ),
        scratch_shapes=[pltpu.VMEM((2, n), F32), pltpu.VMEM((2, n), F32)],
        compiler_params=pltpu.CompilerParams(vmem_limit_bytes=VMEM_LIMIT),
        name="route",
    )(lg, bias)


def _inverse_kernel(dest_ref, init_hbm, src_ref, sem, *, n):
    init = pltpu.make_async_copy(init_hbm, src_ref, sem)
    init.start()
    init.wait()

    def fill(t, c):
        src_ref[dest_ref[t]] = t
        src_ref[dest_ref[n + t]] = t
        return c

    lax.fori_loop(0, n, fill, 0, unroll=8)


def _inverse_call(dest_flat, n, n_slots):
    return pl.pallas_call(
        functools.partial(_inverse_kernel, n=n),
        in_specs=[pl.BlockSpec(memory_space=pltpu.SMEM), pl.BlockSpec(memory_space=pl.ANY)],
        out_specs=pl.BlockSpec(memory_space=pltpu.SMEM),
        out_shape=jax.ShapeDtypeStruct((n_slots,), I32),
        scratch_shapes=[pltpu.SemaphoreType.DMA(())],
        name="inverse",
    )(dest_flat, jnp.arange(n_slots, dtype=I32) % n)


def _row_copy(src_hbm, row, dst_vmem, slot, sem):
    return pltpu.make_async_copy(src_hbm.at[pl.ds(row, 1), :], dst_vmem.at[pl.ds(slot, 1), :], sem)


def _tile_copy(src_hbm, dst_vmem, sem):
    return pltpu.make_async_copy(src_hbm.at[pl.ds(0, dst_vmem.shape[0]), :], dst_vmem, sem)


def _moe_kernel(te_ref, ne_ref, nt_ref, src_cur, src_nxt, src_far, x1_hbm, w1_hbm, w3_hbm, w2_hbm, o_hbm,
                xbuf, obuf, wf1, wf3, wf2, w1b, w3b, w2b, wslot, gsem, osem, wsem):
    i = pl.program_id(0)
    nt = nt_ref[0]

    def gather_start(src_ref, slot):
        for r in range(TM):
            _row_copy(x1_hbm, src_ref[0, 0, r], xbuf.at[slot], r, gsem.at[slot]).start()

    def out_copy(tile, slot):
        return pltpu.make_async_copy(obuf.at[slot], o_hbm.at[pl.ds(tile * TM, TM), :], osem.at[slot])

    def weight_copies(e, slot):
        return (pltpu.make_async_copy(w1_hbm.at[e], wf1.at[slot], wsem.at[slot, 0]),
                pltpu.make_async_copy(w3_hbm.at[e], wf3.at[slot], wsem.at[slot, 1]),
                pltpu.make_async_copy(w2_hbm.at[e], wf2.at[slot], wsem.at[slot, 2]))

    @pl.when(i == 0)
    def _():
        for cp in weight_copies(te_ref[0], 0):
            cp.start(priority=BULK_DMA_PRIORITY)
        wslot[0] = 0
        gather_start(src_cur, 0)
        gather_start(src_nxt, 1)

    @pl.when(i < nt)
    def _():
        slot = lax.rem(i, 2)
        gslot = lax.rem(i, GATHER_DEPTH)

        @pl.when(i >= 2)
        def _():
            out_copy(0, slot).wait()

        _tile_copy(x1_hbm, xbuf.at[gslot], gsem.at[gslot]).wait()

        @pl.when((i == 0) | (te_ref[i] != te_ref[jnp.maximum(i - 1, 0)]))
        def _():
            cur = wslot[0]
            for cp in weight_copies(te_ref[i], cur):
                cp.wait()
            w1b[...] = wf1[cur].astype(BF16)
            w3b[...] = wf3[cur].astype(BF16)
            w2b[...] = wf2[cur].astype(BF16)

            @pl.when(ne_ref[i] < N_EXPERTS)
            def _():
                for cp in weight_copies(ne_ref[i], 1 - cur):
                    cp.start(priority=BULK_DMA_PRIORITY)

            wslot[0] = 1 - cur

        xb = xbuf[gslot].astype(BF16)
        gather_start(src_far, lax.rem(i + 2, GATHER_DEPTH))
        h1 = jnp.dot(xb, w1b[...], preferred_element_type=F32)
        h3 = jnp.dot(xb, w3b[...], preferred_element_type=F32)
        h = (jax.nn.silu(h1) * h3).astype(BF16)
        obuf[slot] = jnp.dot(h, w2b[...], preferred_element_type=F32)
        out_copy(i, slot).start(priority=BULK_DMA_PRIORITY)

        @pl.when(i + 1 >= nt)
        def _():
            for ahead in (1, 2):
                s = lax.rem(i + ahead, GATHER_DEPTH)
                _tile_copy(x1_hbm, xbuf.at[s], gsem.at[s]).wait()
            out_copy(0, slot).wait()

            @pl.when(i >= 1)
            def _():
                out_copy(0, 1 - slot).wait()


def _moe_call(te, ne, ntiles, src3, x1, w1, w3, w2, max_tiles):
    grid_spec = pltpu.PrefetchScalarGridSpec(
        num_scalar_prefetch=3,
        grid=(max_tiles,),
        in_specs=[
            pl.BlockSpec((1, 1, TM), lambda i, te, ne, nt: (i, 0, 0), memory_space=pltpu.SMEM),
            pl.BlockSpec((1, 1, TM), lambda i, te, ne, nt: (jnp.minimum(i + 1, max_tiles - 1), 0, 0),
                         memory_space=pltpu.SMEM),
            pl.BlockSpec((1, 1, TM), lambda i, te, ne, nt: (jnp.minimum(i + 2, max_tiles - 1), 0, 0),
                         memory_space=pltpu.SMEM),
            pl.BlockSpec(memory_space=pl.ANY),
            pl.BlockSpec(memory_space=pl.ANY),
            pl.BlockSpec(memory_space=pl.ANY),
            pl.BlockSpec(memory_space=pl.ANY),
        ],
        out_specs=pl.BlockSpec(memory_space=pl.ANY),
        scratch_shapes=[
            pltpu.VMEM((GATHER_DEPTH, TM, D_MODEL), F32),
            pltpu.VMEM((2, TM, D_MODEL), F32),
            pltpu.VMEM((2, D_MODEL, D_EXPERT), F32),
            pltpu.VMEM((2, D_MODEL, D_EXPERT), F32),
            pltpu.VMEM((2, D_EXPERT, D_MODEL), F32),
            pltpu.VMEM((D_MODEL, D_EXPERT), BF16),
            pltpu.VMEM((D_MODEL, D_EXPERT), BF16),
            pltpu.VMEM((D_EXPERT, D_MODEL), BF16),
            pltpu.SMEM((1,), I32),
            pltpu.SemaphoreType.DMA((GATHER_DEPTH,)),
            pltpu.SemaphoreType.DMA((2,)),
            pltpu.SemaphoreType.DMA((2, 3)),
        ],
    )
    return pl.pallas_call(
        _moe_kernel,
        grid_spec=grid_spec,
        out_shape=jax.ShapeDtypeStruct((max_tiles * TM, D_MODEL), F32),
        compiler_params=pltpu.CompilerParams(dimension_semantics=("arbitrary",), vmem_limit_bytes=VMEM_LIMIT),
        name="moe",
    )(te, ne, ntiles, src3, src3, src3, x1, w1, w3, w2)


def _combine_kernel(dest_cur, dest_nxt, x1_ref, tw_ref, ys_hbm, ln_g, ln_b, o_ref, buf, sem, *, alpha, nsteps):
    i = pl.program_id(0)
    slot = lax.rem(i, 2)

    def gather_start(dest_ref, s):
        for r in range(ROWS):
            for k in range(2):
                _row_copy(ys_hbm, dest_ref[0, k, r], buf.at[s, k], r, sem.at[s, k]).start(priority=k)

    def gather_wait(s):
        for k in range(2):
            _tile_copy(ys_hbm, buf.at[s, k], sem.at[s, k]).wait()

    @pl.when(i == 0)
    def _():
        gather_start(dest_cur, 0)

    gather_wait(slot)
    tw = tw_ref[...]
    moe = tw[:, 0:1] * buf[slot, 0] + tw[:, 1:2] * buf[slot, 1]
    gather_start(dest_nxt, 1 - slot)
    xin = alpha * x1_ref[...] + moe
    mu = jnp.mean(xin, axis=-1, keepdims=True)
    xc = xin - mu
    var = jnp.mean(xc * xc, axis=-1, keepdims=True)
    y = xc * lax.rsqrt(var + LN_EPS) * ln_g[...] + ln_b[...]
    o_ref[...] = y.reshape(o_ref.shape)

    @pl.when(i == nsteps - 1)
    def _():
        gather_wait(1 - slot)


def _combine_call(dest3, x1, tw, ys, ln_g, ln_b, out_shape, blk, blk0, alpha):
    bsz, tlen, _ = out_shape
    nt = tlen // blk[1]
    nsteps = (bsz // blk[0]) * nt
    return pl.pallas_call(
        functools.partial(_combine_kernel, alpha=alpha, nsteps=nsteps),
        grid=(nsteps,),
        in_specs=[
            pl.BlockSpec((1, 2, ROWS), lambda i: (blk0 + i, 0, 0), memory_space=pltpu.SMEM),
            pl.BlockSpec((1, 2, ROWS), lambda i: (blk0 + jnp.minimum(i + 1, nsteps - 1), 0, 0),
                         memory_space=pltpu.SMEM),
            pl.BlockSpec((ROWS, D_MODEL), lambda i: (blk0 + i, 0)),
            pl.BlockSpec((ROWS, 2), lambda i: (blk0 + i, 0)),
            pl.BlockSpec(memory_space=pl.ANY),
            _const_spec(ln_g.shape),
            _const_spec(ln_b.shape),
        ],
        out_specs=pl.BlockSpec(blk, lambda i: (i // nt, i % nt, 0)),
        out_shape=jax.ShapeDtypeStruct(out_shape, F32),
        scratch_shapes=[
            pltpu.VMEM((2, 2, ROWS, D_MODEL), F32),
            pltpu.SemaphoreType.DMA((2, 2)),
        ],
        compiler_params=pltpu.CompilerParams(dimension_semantics=("arbitrary",), vmem_limit_bytes=VMEM_LIMIT),
        name="combine",
    )(dest3, dest3, x1, tw, ys, ln_g, ln_b)


def _block_diag_chunks(w):
    per = LANE_CHUNK // LRU_HEAD_DIM
    nchunk = LRU_HEADS // per
    w4 = w.reshape(nchunk, per, LRU_HEAD_DIM, LRU_HEAD_DIM)
    bd = jnp.einsum("cjio,jk->cjiko", w4, jnp.eye(per, dtype=w.dtype))
    return bd.reshape(nchunk, LANE_CHUNK, LANE_CHUNK)


def _layer(yp, ys, pool_s, conv_s, h_s, lw, alpha, past_len):
    (w_in, pool_w, pool_b, pool_scale, conv_w, conv_b, rg_w, rg_b, ig_w, ig_b, lru_lambda, w_out,
     ln1_g, ln1_b, rgw, rgb, rew, reb, w1, w3, w2, ln2_g, ln2_b) = lw
    bp, tp, _ = yp.shape
    bs, ts, _ = ys.shape
    n_p, n_s = bp * tp, bs * ts
    n = n_p + n_s
    assert n % ROWS == 0 and n_p % ROWS == 0
    max_tiles = (2 * n) // TM + N_EXPERTS
    assert max_tiles <= META_LANES

    nchunk = D_LRU // LANE_CHUNK
    gate_w = jnp.concatenate([_block_diag_chunks(rg_w), _block_diag_chunks(ig_w)], axis=-1).astype(BF16)
    gate_b = jnp.concatenate(
        [rg_b.reshape(nchunk, LANE_CHUNK), ig_b.reshape(nchunk, LANE_CHUNK)], axis=-1).reshape(1, 2 * D_LRU)
    rt = jnp.concatenate(
        [rgw.T, jnp.zeros((SUBLANES - N_GROUPS, D_MODEL), F32),
         jnp.transpose(rew, (0, 2, 1)).reshape(N_EXPERTS, D_MODEL)], axis=0)
    rt_hi = rt.astype(BF16)
    rt_lo = (rt - rt_hi.astype(F32)).astype(BF16)
    r_bias = jnp.concatenate([rgb, jnp.zeros((SUBLANES - N_GROUPS,), F32), reb.reshape(N_EXPERTS)]).reshape(ROUTER_ROWS, 1)
    wts = (
        w_in.astype(BF16), pool_w.astype(BF16), pool_b.reshape(1, D_POOL), pool_scale.reshape(1, D_POOL),
        conv_w, conv_b.reshape(1, D_LRU), gate_w, gate_b, lru_lambda.reshape(1, D_LRU), w_out.astype(BF16),
        ln1_g.reshape(1, D_MODEL), ln1_b.reshape(1, D_MODEL),
        jnp.pad(jnp.stack([rt_hi, rt_lo]), ((0, 0), (0, ROUTER_PAD - ROUTER_ROWS), (0, 0))).reshape(2 * ROUTER_PAD, D_MODEL),
    )

    x1, lg, pool_p, conv_p, h_p = _mixer_call(
        yp, jnp.zeros((bp, POOL_STATE, D_POOL), F32), jnp.zeros((bp, CONV_STATE, D_LRU), F32),
        jnp.zeros((bp, 1, D_LRU), F32), wts, None, None,
        ns=1, l=SEQ_ROWS, pos0=0, alpha=alpha, n_total=n, blk0=0)
    x1, lg, pool_n, conv_n, h_n = _mixer_call(
        ys, pool_s, conv_s, h_s.reshape(bs, 1, D_LRU), wts, x1, lg,
        ns=ROWS // ts, l=ts, pos0=past_len, alpha=alpha, n_total=n, blk0=n_p // ROWS)

    dest, tw, meta = _route_call(lg, r_bias, n)
    src = _inverse_call(dest.reshape(2 * n), n, max_tiles * TM)
    ysort = _moe_call(meta[0], meta[2], meta[1, 0:1], src.reshape(max_tiles, 1, TM), x1, w1, w3, w2, max_tiles)

    dest3 = jnp.transpose(dest.reshape(2, n // ROWS, ROWS), (1, 0, 2))
    tw_t = tw.T
    g2, b2 = ln2_g.reshape(1, D_MODEL), ln2_b.reshape(1, D_MODEL)
    out_p = _combine_call(dest3, x1, tw_t, ysort, g2, b2, (bp, tp, D_MODEL), (1, ROWS, D_MODEL), 0, alpha)
    out_s = _combine_call(dest3, x1, tw_t, ysort, g2, b2, (bs, ts, D_MODEL), (ROWS // ts, ts, D_MODEL),
                          n_p // ROWS, alpha)
    states = (pool_p, conv_p, h_p.reshape(bp, D_LRU), pool_n, conv_n, h_n.reshape(bs, D_LRU))
    return out_p, out_s, states


def kernel(x_prompt, x_sample, state_pool, state_conv, state_h, w_in, pool_w, pool_b, pool_scale, conv_w, conv_b, rg_w, rg_b, ig_w, ig_b, lru_lambda, w_out, ln1_g, ln1_b, router_group_w, router_group_b, router_expert_w, router_expert_b, expert_w1, expert_w3, expert_w2, ln2_g, ln2_b):
    depth = w_in.shape[0]
    alpha = (2.0 * depth) ** 0.25
    past_len = 16384
    layer_weights = (w_in, pool_w, pool_b, pool_scale, conv_w, conv_b, rg_w, rg_b, ig_w, ig_b, lru_lambda, w_out,
                     ln1_g, ln1_b, router_group_w, router_group_b, router_expert_w, router_expert_b,
                     expert_w1, expert_w3, expert_w2, ln2_g, ln2_b)
    yp, ys = x_prompt, x_sample
    outs = [[] for _ in range(6)]
    for layer in range(depth):
        lw = tuple(w[layer] for w in layer_weights)
        yp, ys, states = _layer(yp, ys, state_pool[layer], state_conv[layer], state_h[layer], lw, alpha, past_len)
        for acc, s in zip(outs, states):
            acc.append(s)
    return (yp, ys) + tuple(jnp.stack(o, axis=0) for o in outs)
```

```python
import functools

import jax
import jax.numpy as jnp
from jax import lax
from jax.experimental import pallas as pl
from jax.experimental.pallas import tpu as pltpu

F32 = jnp.float32
BF16 = jnp.bfloat16
I32 = jnp.int32

D_MODEL = 2048
D_POOL = 1024
D_LRU = 1024
POOL_WINDOWS = (2, 4, 8, 16)
POOL_GROUP = D_POOL // len(POOL_WINDOWS)
POOL_STATE = max(POOL_WINDOWS) - 1
CONV_WIDTH = 4
CONV_STATE = CONV_WIDTH - 1
LRU_HEADS = 16
LRU_HEAD_DIM = D_LRU // LRU_HEADS
LRU_C = 8.0
N_GROUPS = 4
PER_GROUP = 8
N_EXPERTS = N_GROUPS * PER_GROUP
D_EXPERT = D_MODEL // 4
LN_EPS = 1e-5

SUBLANES = 8
LANES = 128
LANE_CHUNK = 256
POOL_HALO = 16
CONV_HALO = 8
ROWS = 256
SEQ_ROWS = 512
TM = 256
GATHER_DEPTH = 4
COMBINE_DEPTH = 3
ROUTER_ROWS = 8 + N_EXPERTS
ROUTER_PAD = 48
META_LANES = LANES
VMEM_LIMIT = 56 * 1024 * 1024
BULK_DMA_PRIORITY = 1

_NT = (((1,), (1,)), ((), ()))


def _const_spec(shape):
    nd = len(shape)
    return pl.BlockSpec(shape, lambda *_: (0,) * nd, pipeline_mode=pl.Buffered(1))


def _mixer_kernel(x_ref, pool_in, conv_in, h_in, w_in, pool_w, pool_b, pool_scale, conv_w, conv_b,
                  gate_w, gate_b, lam, w_out, ln_g, ln_b, rt_hl, *rest, ns, l, pos0, alpha, aliased):
    if aliased:
        rest = rest[2:]
    (x1_ref, lg_ref, pool_o, conv_o, h_o,
     pool_ext, conv_ext, h_prev, gate_s, a_s, u_s, y_s) = rest
    t = pl.program_id(1)
    rows = ns * l

    @pl.when(t == 0)
    def _():
        pool_ext[:, 0:POOL_HALO - POOL_STATE, :] = jnp.zeros((ns, POOL_HALO - POOL_STATE, D_POOL), F32)
        pool_ext[:, POOL_HALO - POOL_STATE:POOL_HALO, :] = pool_in[...]
        conv_ext[:, 0:CONV_HALO - CONV_STATE, :] = jnp.zeros((ns, CONV_HALO - CONV_STATE, D_LRU), F32)
        conv_ext[:, CONV_HALO - CONV_STATE:CONV_HALO, :] = conv_in[...]
        h_prev[...] = h_in[...]

    xb = x_ref[...].reshape(rows, D_MODEL).astype(BF16)
    pool_ext[:, POOL_HALO:, :] = jnp.dot(
        xb, w_in[:, 0:D_POOL], preferred_element_type=F32).reshape(ns, l, D_POOL)
    conv_ext[:, CONV_HALO:, :] = jnp.dot(
        xb, w_in[:, D_POOL:D_POOL + D_LRU], preferred_element_type=F32).reshape(ns, l, D_LRU)
    gate_s[...] = jnp.dot(xb, w_in[:, D_POOL + D_LRU:], preferred_element_type=F32)

    pos = pos0 + t * l + lax.broadcasted_iota(I32, (ns, l, LANE_CHUNK), 1).reshape(rows, LANE_CHUNK)

    for g, w in enumerate(POOL_WINDOWS):
        cols = slice(g * POOL_GROUP, (g + 1) * POOL_GROUP)
        s = pool_ext[:, :, cols].reshape(ns * (POOL_HALO + l), POOL_GROUP)
        shift = 1
        while shift < w:
            s = s + pltpu.roll(s, shift, 0)
            shift *= 2
        win = s.reshape(ns, POOL_HALO + l, POOL_GROUP)[:, POOL_HALO:, :].reshape(rows, POOL_GROUP)
        u = pool_ext[:, POOL_HALO:, cols].reshape(rows, POOL_GROUP)
        if pos0 >= w - 1:
            inv = 1.0 / w
        else:
            inv = 1.0 / jnp.minimum(pos + 1, w).astype(F32)
        d = win * inv - u
        z = jnp.dot(d.astype(BF16), pool_w[g], preferred_element_type=F32) + pool_b[:, cols]
        y_s[:, cols] = (z * pool_scale[:, cols]).astype(BF16)

    lam_v = lam[...]
    softplus_neg = jnp.maximum(-lam_v, 0.0) + jnp.log1p(jnp.exp(-jnp.abs(lam_v)))
    log_a_scale = -LRU_C * softplus_neg
    sub = lax.broadcasted_iota(I32, (rows // SUBLANES, SUBLANES, LANE_CHUNK), 1)
    for c in range(D_LRU // LANE_CHUNK):
        cols = slice(c * LANE_CHUNK, (c + 1) * LANE_CHUNK)
        ce = conv_ext[:, :, cols].reshape(ns * (CONV_HALO + l), LANE_CHUNK)
        conv = conv_b[:, cols] + conv_w[CONV_WIDTH - 1:CONV_WIDTH, cols] * ce
        for k in range(1, CONV_WIDTH):
            conv = conv + conv_w[CONV_WIDTH - 1 - k:CONV_WIDTH - k, cols] * pltpu.roll(ce, k, 0)
        conv = conv.reshape(ns, CONV_HALO + l, LANE_CHUNK)[:, CONV_HALO:, :].reshape(rows, LANE_CHUNK)
        gz = jnp.dot(conv.astype(BF16), gate_w[c], preferred_element_type=F32)
        gz = gz + gate_b[:, 2 * c * LANE_CHUNK:2 * (c + 1) * LANE_CHUNK]
        r = 0.5 * jnp.tanh(0.5 * gz[:, :LANE_CHUNK]) + 0.5
        i = 0.5 * jnp.tanh(0.5 * gz[:, LANE_CHUNK:]) + 0.5
        log_a = r * log_a_scale[:, cols]
        a = jnp.exp(log_a)
        th = jnp.tanh(log_a)
        q = (th + th) / (th - 1.0)
        mult = jnp.where(q > 0.0, q * lax.rsqrt(q), 0.0)
        if pos0 == 0:
            mult = jnp.where(pos == 0, 1.0, mult)
        u = mult * (i * conv)
        a3 = a.reshape(rows // SUBLANES, SUBLANES, LANE_CHUNK)
        u3 = u.reshape(rows // SUBLANES, SUBLANES, LANE_CHUNK)
        for sh in (1, 2, 4):
            keep = sub >= sh
            a_sh = jnp.where(keep, pltpu.roll(a3, sh, 1), 1.0)
            u_sh = jnp.where(keep, pltpu.roll(u3, sh, 1), 0.0)
            u3 = a3 * u_sh + u3
            a3 = a3 * a_sh
        a_s[:, :, cols] = a3.reshape(ns, l, LANE_CHUNK)
        u_s[:, :, cols] = u3.reshape(ns, l, LANE_CHUNK)

    hp = h_prev[...]
    for gi in range(l // SUBLANES):
        sl = slice(gi * SUBLANES, (gi + 1) * SUBLANES)
        h = a_s[:, sl, :] * hp + u_s[:, sl, :]
        u_s[:, sl, :] = h
        hp = h[:, SUBLANES - 1:SUBLANES, :]
    h_prev[...] = hp

    for c in range(D_LRU // LANE_CHUNK):
        cols = slice(c * LANE_CHUNK, (c + 1) * LANE_CHUNK)
        h = u_s[:, :, cols].reshape(rows, LANE_CHUNK)
        y_s[:, D_POOL + c * LANE_CHUNK:D_POOL + (c + 1) * LANE_CHUNK] = (
            h * jax.nn.gelu(gate_s[:, cols])).astype(BF16)

    mix = jnp.dot(y_s[...], w_out[...], preferred_element_type=F32)
    xin = alpha * x_ref[...].reshape(rows, D_MODEL) + mix
    mu = jnp.mean(xin, axis=-1, keepdims=True)
    xc = xin - mu
    var = jnp.mean(xc * xc, axis=-1, keepdims=True)
    x1 = xc * lax.rsqrt(var + LN_EPS) * ln_g[...] + ln_b[...]
    x1_ref[...] = x1

    hi = x1.astype(BF16)
    lo = (x1 - hi.astype(F32)).astype(BF16)
    both = lax.dot_general(rt_hl[...], hi, _NT, preferred_element_type=F32)
    lg = both[0:ROUTER_ROWS] + both[ROUTER_PAD:ROUTER_PAD + ROUTER_ROWS]
    lg = lg + lax.dot_general(rt_hl[0:ROUTER_PAD, :], lo, _NT, preferred_element_type=F32)[0:ROUTER_ROWS]
    lg_ref[...] = lg

    pool_o[...] = pool_ext[:, POOL_HALO + l - POOL_STATE:, :]
    conv_o[...] = conv_ext[:, CONV_HALO + l - CONV_STATE:, :]
    h_o[...] = hp
    pool_ext[:, 0:POOL_HALO, :] = pool_ext[:, l:l + POOL_HALO, :]
    conv_ext[:, 0:CONV_HALO, :] = conv_ext[:, l:l + CONV_HALO, :]


def _mixer_call(x, pool_in, conv_in, h_in, wts, x1_buf, lg_buf, *, ns, l, pos0, alpha, n_total, blk0):
    bsz, tlen, _ = x.shape
    nb, nt = bsz // ns, tlen // l
    rows = ns * l
    assert rows % ROWS == 0 and bsz % ns == 0 and tlen % l == 0 and l % SUBLANES == 0
    aliased = x1_buf is not None

    def row_blk(b, t):
        return (blk0 + b * nt + t, 0)

    in_specs = [
        pl.BlockSpec((ns, l, D_MODEL), lambda b, t: (b, t, 0)),
        pl.BlockSpec((ns, POOL_STATE, D_POOL), lambda b, t: (b, 0, 0)),
        pl.BlockSpec((ns, CONV_STATE, D_LRU), lambda b, t: (b, 0, 0)),
        pl.BlockSpec((ns, 1, D_LRU), lambda b, t: (b, 0, 0)),
    ] + [_const_spec(w.shape) for w in wts]
    args = [x, pool_in, conv_in, h_in, *wts]
    aliases = {}
    if aliased:
        in_specs += [pl.BlockSpec(memory_space=pl.ANY), pl.BlockSpec(memory_space=pl.ANY)]
        aliases = {len(args): 0, len(args) + 1: 1}
        args += [x1_buf, lg_buf]
    out_shape = (
        jax.ShapeDtypeStruct((n_total, D_MODEL), F32),
        jax.ShapeDtypeStruct((ROUTER_ROWS, n_total), F32),
        jax.ShapeDtypeStruct((bsz, POOL_STATE, D_POOL), F32),
        jax.ShapeDtypeStruct((bsz, CONV_STATE, D_LRU), F32),
        jax.ShapeDtypeStruct((bsz, 1, D_LRU), F32),
    )
    out_specs = (
        pl.BlockSpec((rows, D_MODEL), row_blk),
        pl.BlockSpec((ROUTER_ROWS, rows), lambda b, t: (0, blk0 + b * nt + t)),
        pl.BlockSpec((ns, POOL_STATE, D_POOL), lambda b, t: (b, 0, 0)),
        pl.BlockSpec((ns, CONV_STATE, D_LRU), lambda b, t: (b, 0, 0)),
        pl.BlockSpec((ns, 1, D_LRU), lambda b, t: (b, 0, 0)),
    )
    scratch = [
        pltpu.VMEM((ns, POOL_HALO + l, D_POOL), F32),
        pltpu.VMEM((ns, CONV_HALO + l, D_LRU), F32),
        pltpu.VMEM((ns, 1, D_LRU), F32),
        pltpu.VMEM((rows, D_LRU), F32),
        pltpu.VMEM((ns, l, D_LRU), F32),
        pltpu.VMEM((ns, l, D_LRU), F32),
        pltpu.VMEM((rows, D_MODEL), BF16),
    ]
    return pl.pallas_call(
        functools.partial(_mixer_kernel, ns=ns, l=l, pos0=pos0, alpha=alpha, aliased=aliased),
        grid=(nb, nt),
        in_specs=in_specs,
        out_specs=out_specs,
        out_shape=out_shape,
        scratch_shapes=scratch,
        input_output_aliases=aliases,
        compiler_params=pltpu.CompilerParams(
            dimension_semantics=("arbitrary", "arbitrary"), vmem_limit_bytes=VMEM_LIMIT),
        name="mixer_seq" if nt > 1 else "mixer_step",
    )(*args)


def _route_kernel(lg_ref, bias_ref, dest_ref, tw_ref, meta_ref, e_s, r_s, *, n):
    blk = 256
    nblk = n // blk
    neg_inf = -jnp.inf
    ridx8 = lax.broadcasted_iota(I32, (SUBLANES, blk), 0).astype(F32)
    eidx = lax.broadcasted_iota(I32, (N_EXPERTS, blk), 0).astype(F32)
    tri = (lax.broadcasted_iota(I32, (blk, blk), 0) < lax.broadcasted_iota(I32, (blk, blk), 1))
    tri = jnp.where(tri, 1.0, 0.0).astype(BF16)

    def first_idx(vals, m):
        return jnp.min(jnp.where(vals == m, ridx8, float(SUBLANES)), axis=0, keepdims=True)

    def pass1(j, base):
        ls = pl.ds(pl.multiple_of(j * blk, blk), blk)
        lg = lg_ref[0:SUBLANES, ls] + bias_ref[0:SUBLANES, :]
        lg = jnp.where(ridx8 < N_GROUPS, lg, neg_inf)
        m = jnp.max(lg, axis=0, keepdims=True)
        gi = first_idx(lg, m)
        pg_sel = 1.0 / jnp.sum(jnp.exp(lg - m), axis=0, keepdims=True)
        le = jnp.zeros((PER_GROUP, blk), F32)
        for g in range(N_GROUPS):
            rows = slice(SUBLANES + g * PER_GROUP, SUBLANES + (g + 1) * PER_GROUP)
            le = jnp.where(gi == float(g), lg_ref[rows, ls] + bias_ref[rows, :], le)
        m1 = jnp.max(le, axis=0, keepdims=True)
        i1 = first_idx(le, m1)
        le2 = jnp.where(ridx8 == i1, neg_inf, le)
        m2 = jnp.max(le2, axis=0, keepdims=True)
        i2 = first_idx(le2, m2)
        e21 = jnp.exp(m2 - m1)
        denom = 1.0 / (1.0 + e21)
        tw_ref[0:1, ls] = pg_sel * denom
        tw_ref[1:2, ls] = pg_sel * (e21 * denom)
        e1 = gi * float(PER_GROUP) + i1
        e2 = gi * float(PER_GROUP) + i2
        e_s[0:1, ls] = e1
        e_s[1:2, ls] = e2
        oh1 = jnp.where(eidx == e1, 1.0, 0.0)
        oh2 = jnp.where(eidx == e2, 1.0, 0.0)
        oh = oh1 + oh2
        before = base + jnp.dot(oh.astype(BF16), tri, preferred_element_type=F32)
        r_s[0:1, ls] = jnp.sum(oh1 * before, axis=0, keepdims=True)
        r_s[1:2, ls] = jnp.sum(oh2 * before, axis=0, keepdims=True)
        return base + jnp.sum(oh, axis=1, keepdims=True)

    cnt = lax.fori_loop(0, nblk, pass1, jnp.zeros((N_EXPERTS, blk), F32))

    ntile = jnp.floor((cnt + float(TM - 1)) * (1.0 / TM))
    lt = (lax.broadcasted_iota(I32, (N_EXPERTS, N_EXPERTS), 1) < lax.broadcasted_iota(I32, (N_EXPERTS, N_EXPERTS), 0))
    lt = jnp.where(lt, 1.0, 0.0).astype(BF16)
    tile0 = jnp.dot(lt, ntile.astype(BF16), preferred_element_type=F32)
    row0 = tile0 * float(TM)

    def pass2(j, carry):
        ls = pl.ds(pl.multiple_of(j * blk, blk), blk)
        for k in range(2):
            ohk = jnp.where(eidx == e_s[k:k + 1, ls], 1.0, 0.0)
            dest = r_s[k:k + 1, ls] + jnp.sum(ohk * row0, axis=0, keepdims=True)
            dest_ref[k:k + 1, ls] = dest.astype(I32)
        return carry

    lax.fori_loop(0, nblk, pass2, 0)

    ml = slice(0, META_LANES)
    total = jnp.sum(ntile[:, ml], axis=0, keepdims=True)
    tile_i = lax.broadcasted_iota(I32, (N_EXPERTS, META_LANES), 1).astype(F32)
    tile_c = jnp.minimum(tile_i, total - 1.0)
    tile_end = tile0[:, ml] + ntile[:, ml]
    te = jnp.sum(jnp.where(tile_end <= tile_c, 1.0, 0.0), axis=0, keepdims=True)
    meta_ref[...] = jnp.zeros((SUBLANES, META_LANES), I32)
    meta_ref[0:1, :] = te.astype(I32)
    meta_ref[1:2, :] = total.astype(I32)
    later = (eidx[:, ml] > te) & (ntile[:, ml] > 0.0)
    nxt = jnp.min(jnp.where(later, eidx[:, ml], float(N_EXPERTS)), axis=0, keepdims=True)
    meta_ref[2:3, :] = nxt.astype(I32)


def _route_call(lg, bias, n):
    return pl.pallas_call(
        functools.partial(_route_kernel, n=n),
        out_shape=(
            jax.ShapeDtypeStruct((2, n), I32),
            jax.ShapeDtypeStruct((2, n), F32),
            jax.ShapeDtypeStruct((SUBLANES, META_LANES), I32),
        ),
        scratch_shapes=[pltpu.VMEM((2, n), F32), pltpu.VMEM((2, n), F32)],
        compiler_params=pltpu.CompilerParams(vmem_limit_bytes=VMEM_LIMIT),
        name="route",
    )(lg, bias)


def _inverse_kernel(dest_ref, init_hbm, src_ref, sem, *, n):
    init = pltpu.make_async_copy(init_hbm, src_ref, sem)
    init.start()
    init.wait()

    def fill(t, c):
        src_ref[dest_ref[t]] = t
        src_ref[dest_ref[n + t]] = t
        return c

    lax.fori_loop(0, n, fill, 0, unroll=8)


def _inverse_call(dest_flat, n, n_slots):
    return pl.pallas_call(
        functools.partial(_inverse_kernel, n=n),
        in_specs=[pl.BlockSpec(memory_space=pltpu.SMEM), pl.BlockSpec(memory_space=pl.ANY)],
        out_specs=pl.BlockSpec(memory_space=pltpu.SMEM),
        out_shape=jax.ShapeDtypeStruct((n_slots,), I32),
        scratch_shapes=[pltpu.SemaphoreType.DMA(())],
        name="inverse",
    )(dest_flat, jnp.arange(n_slots, dtype=I32) % n)


def _row_copy(src_hbm, row, dst_vmem, slot, sem):
    return pltpu.make_async_copy(src_hbm.at[pl.ds(row, 1), :], dst_vmem.at[pl.ds(slot, 1), :], sem)


def _tile_copy(src_hbm, dst_vmem, sem):
    return pltpu.make_async_copy(src_hbm.at[pl.ds(0, dst_vmem.shape[0]), :], dst_vmem, sem)


def _moe_kernel(te_ref, ne_ref, nt_ref, *refs):
    src_refs = refs[:GATHER_DEPTH]
    (x1_hbm, w1_hbm, w3_hbm, w2_hbm, o_hbm,
     xbuf, obuf, wf1, wf3, wf2, w1b, w3b, w2b, wslot, gsem, osem, wsem) = refs[GATHER_DEPTH:]
    ahead = GATHER_DEPTH - 1
    i = pl.program_id(0)
    nt = nt_ref[0]

    def gather_start(src_ref, slot):
        for r in range(TM):
            _row_copy(x1_hbm, src_ref[0, 0, r], xbuf.at[slot], r, gsem.at[slot]).start()

    def out_copy(tile, slot):
        return pltpu.make_async_copy(obuf.at[slot], o_hbm.at[pl.ds(tile * TM, TM), :], osem.at[slot])

    def weight_copies(e, slot):
        return (pltpu.make_async_copy(w1_hbm.at[e], wf1.at[slot], wsem.at[slot, 0]),
                pltpu.make_async_copy(w3_hbm.at[e], wf3.at[slot], wsem.at[slot, 1]),
                pltpu.make_async_copy(w2_hbm.at[e], wf2.at[slot], wsem.at[slot, 2]))

    @pl.when(i == 0)
    def _():
        for cp in weight_copies(te_ref[0], 0):
            cp.start(priority=BULK_DMA_PRIORITY)
        wslot[0] = 0
        for k in range(ahead):
            gather_start(src_refs[k], k)

    @pl.when(i < nt)
    def _():
        slot = lax.rem(i, 2)
        gslot = lax.rem(i, GATHER_DEPTH)

        @pl.when(i >= 2)
        def _():
            out_copy(0, slot).wait()

        _tile_copy(x1_hbm, xbuf.at[gslot], gsem.at[gslot]).wait()

        @pl.when((i == 0) | (te_ref[i] != te_ref[jnp.maximum(i - 1, 0)]))
        def _():
            cur = wslot[0]
            for cp in weight_copies(te_ref[i], cur):
                cp.wait()
            w1b[...] = wf1[cur].astype(BF16)
            w3b[...] = wf3[cur].astype(BF16)
            w2b[...] = wf2[cur].astype(BF16)

            @pl.when(ne_ref[i] < N_EXPERTS)
            def _():
                for cp in weight_copies(ne_ref[i], 1 - cur):
                    cp.start(priority=BULK_DMA_PRIORITY)

            wslot[0] = 1 - cur

        xb = xbuf[gslot].astype(BF16)
        gather_start(src_refs[ahead], lax.rem(i + ahead, GATHER_DEPTH))
        h1 = jnp.dot(xb, w1b[...], preferred_element_type=F32)
        h3 = jnp.dot(xb, w3b[...], preferred_element_type=F32)
        h = (jax.nn.silu(h1) * h3).astype(BF16)
        obuf[slot] = jnp.dot(h, w2b[...], preferred_element_type=F32)
        out_copy(i, slot).start(priority=BULK_DMA_PRIORITY)

        @pl.when(i + 1 >= nt)
        def _():
            for k in range(1, GATHER_DEPTH):
                s = lax.rem(i + k, GATHER_DEPTH)
                _tile_copy(x1_hbm, xbuf.at[s], gsem.at[s]).wait()
            out_copy(0, slot).wait()

            @pl.when(i >= 1)
            def _():
                out_copy(0, 1 - slot).wait()


def _moe_call(te, ne, ntiles, src3, x1, w1, w3, w2, max_tiles):
    grid_spec = pltpu.PrefetchScalarGridSpec(
        num_scalar_prefetch=3,
        grid=(max_tiles,),
        in_specs=[
            pl.BlockSpec((1, 1, TM), lambda i, te, ne, nt, k=k: (jnp.minimum(i + k, max_tiles - 1), 0, 0),
                         memory_space=pltpu.SMEM)
            for k in range(GATHER_DEPTH)
        ] + [
            pl.BlockSpec(memory_space=pl.ANY),
            pl.BlockSpec(memory_space=pl.ANY),
            pl.BlockSpec(memory_space=pl.ANY),
            pl.BlockSpec(memory_space=pl.ANY),
        ],
        out_specs=pl.BlockSpec(memory_space=pl.ANY),
        scratch_shapes=[
            pltpu.VMEM((GATHER_DEPTH, TM, D_MODEL), F32),
            pltpu.VMEM((2, TM, D_MODEL), F32),
            pltpu.VMEM((2, D_MODEL, D_EXPERT), F32),
            pltpu.VMEM((2, D_MODEL, D_EXPERT), F32),
            pltpu.VMEM((2, D_EXPERT, D_MODEL), F32),
            pltpu.VMEM((D_MODEL, D_EXPERT), BF16),
            pltpu.VMEM((D_MODEL, D_EXPERT), BF16),
            pltpu.VMEM((D_EXPERT, D_MODEL), BF16),
            pltpu.SMEM((1,), I32),
            pltpu.SemaphoreType.DMA((GATHER_DEPTH,)),
            pltpu.SemaphoreType.DMA((2,)),
            pltpu.SemaphoreType.DMA((2, 3)),
        ],
    )
    return pl.pallas_call(
        _moe_kernel,
        grid_spec=grid_spec,
        out_shape=jax.ShapeDtypeStruct((max_tiles * TM, D_MODEL), F32),
        compiler_params=pltpu.CompilerParams(dimension_semantics=("arbitrary",), vmem_limit_bytes=VMEM_LIMIT),
        name="moe",
    )(te, ne, ntiles, *([src3] * GATHER_DEPTH), x1, w1, w3, w2)


def _combine_kernel(*refs, alpha, nsteps):
    dest_refs = refs[:COMBINE_DEPTH]
    x1_ref, tw_ref, ys_hbm, ln_g, ln_b, o_ref, buf, sem = refs[COMBINE_DEPTH:]
    ahead = COMBINE_DEPTH - 1
    i = pl.program_id(0)
    slot = lax.rem(i, COMBINE_DEPTH)

    def gather_start(dest_ref, s):
        for r in range(ROWS):
            for k in range(2):
                _row_copy(ys_hbm, dest_ref[0, k, r], buf.at[s, k], r, sem.at[s, k]).start(priority=k)

    def gather_wait(s):
        for k in range(2):
            _tile_copy(ys_hbm, buf.at[s, k], sem.at[s, k]).wait()

    @pl.when(i == 0)
    def _():
        for j in range(ahead):
            gather_start(dest_refs[j], j)

    gather_wait(slot)
    tw = tw_ref[...]
    moe = tw[:, 0:1] * buf[slot, 0] + tw[:, 1:2] * buf[slot, 1]
    gather_start(dest_refs[ahead], lax.rem(i + ahead, COMBINE_DEPTH))
    xin = alpha * x1_ref[...] + moe
    mu = jnp.mean(xin, axis=-1, keepdims=True)
    xc = xin - mu
    var = jnp.mean(xc * xc, axis=-1, keepdims=True)
    y = xc * lax.rsqrt(var + LN_EPS) * ln_g[...] + ln_b[...]
    o_ref[...] = y.reshape(o_ref.shape)

    @pl.when(i == nsteps - 1)
    def _():
        for j in range(1, COMBINE_DEPTH):
            gather_wait(lax.rem(i + j, COMBINE_DEPTH))


def _combine_call(dest3, x1, tw, ys, ln_g, ln_b, out_shape, blk, blk0, alpha):
    bsz, tlen, _ = out_shape
    nt = tlen // blk[1]
    nsteps = (bsz // blk[0]) * nt
    return pl.pallas_call(
        functools.partial(_combine_kernel, alpha=alpha, nsteps=nsteps),
        grid=(nsteps,),
        in_specs=[
            pl.BlockSpec((1, 2, ROWS), lambda i, j=j: (blk0 + jnp.minimum(i + j, nsteps - 1), 0, 0),
                         memory_space=pltpu.SMEM)
            for j in range(COMBINE_DEPTH)
        ] + [
            pl.BlockSpec((ROWS, D_MODEL), lambda i: (blk0 + i, 0)),
            pl.BlockSpec((ROWS, 2), lambda i: (blk0 + i, 0)),
            pl.BlockSpec(memory_space=pl.ANY),
            _const_spec(ln_g.shape),
            _const_spec(ln_b.shape),
        ],
        out_specs=pl.BlockSpec(blk, lambda i: (i // nt, i % nt, 0)),
        out_shape=jax.ShapeDtypeStruct(out_shape, F32),
        scratch_shapes=[
            pltpu.VMEM((COMBINE_DEPTH, 2, ROWS, D_MODEL), F32),
            pltpu.SemaphoreType.DMA((COMBINE_DEPTH, 2)),
        ],
        compiler_params=pltpu.CompilerParams(dimension_semantics=("arbitrary",), vmem_limit_bytes=VMEM_LIMIT),
        name="combine",
    )(*([dest3] * COMBINE_DEPTH), x1, tw, ys, ln_g, ln_b)


def _block_diag_chunks(w):
    per = LANE_CHUNK // LRU_HEAD_DIM
    nchunk = LRU_HEADS // per
    w4 = w.reshape(nchunk, per, LRU_HEAD_DIM, LRU_HEAD_DIM)
    bd = jnp.einsum("cjio,jk->cjiko", w4, jnp.eye(per, dtype=w.dtype))
    return bd.reshape(nchunk, LANE_CHUNK, LANE_CHUNK)


def _layer(yp, ys, pool_s, conv_s, h_s, lw, alpha, past_len):
    (w_in, pool_w, pool_b, pool_scale, conv_w, conv_b, rg_w, rg_b, ig_w, ig_b, lru_lambda, w_out,
     ln1_g, ln1_b, rgw, rgb, rew, reb, w1, w3, w2, ln2_g, ln2_b) = lw
    bp, tp, _ = yp.shape
    bs, ts, _ = ys.shape
    n_p, n_s = bp * tp, bs * ts
    n = n_p + n_s
    assert n % ROWS == 0 and n_p % ROWS == 0
    max_tiles = (2 * n) // TM + N_EXPERTS
    assert max_tiles <= META_LANES

    nchunk = D_LRU // LANE_CHUNK
    gate_w = jnp.concatenate([_block_diag_chunks(rg_w), _block_diag_chunks(ig_w)], axis=-1).astype(BF16)
    gate_b = jnp.concatenate(
        [rg_b.reshape(nchunk, LANE_CHUNK), ig_b.reshape(nchunk, LANE_CHUNK)], axis=-1).reshape(1, 2 * D_LRU)
    rt = jnp.concatenate(
        [rgw.T, jnp.zeros((SUBLANES - N_GROUPS, D_MODEL), F32),
         jnp.transpose(rew, (0, 2, 1)).reshape(N_EXPERTS, D_MODEL)], axis=0)
    rt_hi = rt.astype(BF16)
    rt_lo = (rt - rt_hi.astype(F32)).astype(BF16)
    r_bias = jnp.concatenate([rgb, jnp.zeros((SUBLANES - N_GROUPS,), F32), reb.reshape(N_EXPERTS)]).reshape(ROUTER_ROWS, 1)
    wts = (
        w_in.astype(BF16), pool_w.astype(BF16), pool_b.reshape(1, D_POOL), pool_scale.reshape(1, D_POOL),
        conv_w, conv_b.reshape(1, D_LRU), gate_w, gate_b, lru_lambda.reshape(1, D_LRU), w_out.astype(BF16),
        ln1_g.reshape(1, D_MODEL), ln1_b.reshape(1, D_MODEL),
        jnp.pad(jnp.stack([rt_hi, rt_lo]), ((0, 0), (0, ROUTER_PAD - ROUTER_ROWS), (0, 0))).reshape(2 * ROUTER_PAD, D_MODEL),
    )

    x1, lg, pool_p, conv_p, h_p = _mixer_call(
        yp, jnp.zeros((bp, POOL_STATE, D_POOL), F32), jnp.zeros((bp, CONV_STATE, D_LRU), F32),
        jnp.zeros((bp, 1, D_LRU), F32), wts, None, None,
        ns=1, l=SEQ_ROWS, pos0=0, alpha=alpha, n_total=n, blk0=0)
    x1, lg, pool_n, conv_n, h_n = _mixer_call(
        ys, pool_s, conv_s, h_s.reshape(bs, 1, D_LRU), wts, x1, lg,
        ns=ROWS // ts, l=ts, pos0=past_len, alpha=alpha, n_total=n, blk0=n_p // ROWS)

    dest, tw, meta = _route_call(lg, r_bias, n)
    src = _inverse_call(dest.reshape(2 * n), n, max_tiles * TM)
    ysort = _moe_call(meta[0], meta[2], meta[1, 0:1], src.reshape(max_tiles, 1, TM), x1, w1, w3, w2, max_tiles)

    dest3 = jnp.transpose(dest.reshape(2, n // ROWS, ROWS), (1, 0, 2))
    tw_t = tw.T
    g2, b2 = ln2_g.reshape(1, D_MODEL), ln2_b.reshape(1, D_MODEL)
    out_p = _combine_call(dest3, x1, tw_t, ysort, g2, b2, (bp, tp, D_MODEL), (1, ROWS, D_MODEL), 0, alpha)
    out_s = _combine_call(dest3, x1, tw_t, ysort, g2, b2, (bs, ts, D_MODEL), (ROWS // ts, ts, D_MODEL),
                          n_p // ROWS, alpha)
    states = (pool_p, conv_p, h_p.reshape(bp, D_LRU), pool_n, conv_n, h_n.reshape(bs, D_LRU))
    return out_p, out_s, states


def kernel(x_prompt, x_sample, state_pool, state_conv, state_h, w_in, pool_w, pool_b, pool_scale, conv_w, conv_b, rg_w, rg_b, ig_w, ig_b, lru_lambda, w_out, ln1_g, ln1_b, router_group_w, router_group_b, router_expert_w, router_expert_b, expert_w1, expert_w3, expert_w2, ln2_g, ln2_b):
    depth = w_in.shape[0]
    alpha = (2.0 * depth) ** 0.25
    past_len = 16384
    layer_weights = (w_in, pool_w, pool_b, pool_scale, conv_w, conv_b, rg_w, rg_b, ig_w, ig_b, lru_lambda, w_out,
                     ln1_g, ln1_b, router_group_w, router_group_b, router_expert_w, router_expert_b,
                     expert_w1, expert_w3, expert_w2, ln2_g, ln2_b)
    yp, ys = x_prompt, x_sample
    outs = [[] for _ in range(6)]
    for layer in range(depth):
        lw = tuple(w[layer] for w in layer_weights)
        yp, ys, states = _layer(yp, ys, state_pool[layer], state_conv[layer], state_h[layer], lw, alpha, past_len)
        for acc, s in zip(outs, states):
            acc.append(s)
    return (yp, ys) + tuple(jnp.stack(o, axis=0) for o in outs)
```

```python
import functools

import jax
import jax.numpy as jnp
from jax import lax
from jax.experimental import pallas as pl
from jax.experimental.pallas import tpu as pltpu

F32 = jnp.float32
BF16 = jnp.bfloat16
I32 = jnp.int32

D_MODEL = 2048
D_POOL = 1024
D_LRU = 1024
POOL_WINDOWS = (2, 4, 8, 16)
POOL_GROUP = D_POOL // len(POOL_WINDOWS)
POOL_STATE = max(POOL_WINDOWS) - 1
CONV_WIDTH = 4
CONV_STATE = CONV_WIDTH - 1
LRU_HEADS = 16
LRU_HEAD_DIM = D_LRU // LRU_HEADS
LRU_C = 8.0
N_GROUPS = 4
PER_GROUP = 8
N_EXPERTS = N_GROUPS * PER_GROUP
D_EXPERT = D_MODEL // 4
LN_EPS = 1e-5

SUBLANES = 8
LANES = 128
LANE_CHUNK = 256
POOL_HALO = 16
CONV_HALO = 8
ROWS = 256
SEQ_ROWS = 512
TM = 256
GATHER_DEPTH = 5
COMBINE_DEPTH = 4
ROUTER_ROWS = 8 + N_EXPERTS
ROUTER_PAD = 48
META_LANES = LANES
VMEM_LIMIT = 56 * 1024 * 1024
BULK_DMA_PRIORITY = 1

_NT = (((1,), (1,)), ((), ()))


def _const_spec(shape):
    nd = len(shape)
    return pl.BlockSpec(shape, lambda *_: (0,) * nd, pipeline_mode=pl.Buffered(1))


def _mixer_kernel(x_ref, pool_in, conv_in, h_in, w_in, pool_w, pool_b, pool_scale, conv_w, conv_b,
                  gate_w, gate_b, lam, w_out, ln_g, ln_b, rt_hl, *rest, ns, l, pos0, alpha, aliased):
    if aliased:
        rest = rest[2:]
    (x1_ref, lg_ref, pool_o, conv_o, h_o,
     pool_ext, conv_ext, h_prev, gate_s, a_s, u_s, y_s) = rest
    t = pl.program_id(1)
    rows = ns * l

    @pl.when(t == 0)
    def _():
        pool_ext[:, 0:POOL_HALO - POOL_STATE, :] = jnp.zeros((ns, POOL_HALO - POOL_STATE, D_POOL), F32)
        pool_ext[:, POOL_HALO - POOL_STATE:POOL_HALO, :] = pool_in[...]
        conv_ext[:, 0:CONV_HALO - CONV_STATE, :] = jnp.zeros((ns, CONV_HALO - CONV_STATE, D_LRU), F32)
        conv_ext[:, CONV_HALO - CONV_STATE:CONV_HALO, :] = conv_in[...]
        h_prev[...] = h_in[...]

    xb = x_ref[...].reshape(rows, D_MODEL).astype(BF16)
    pool_ext[:, POOL_HALO:, :] = jnp.dot(
        xb, w_in[:, 0:D_POOL], preferred_element_type=F32).reshape(ns, l, D_POOL)
    conv_ext[:, CONV_HALO:, :] = jnp.dot(
        xb, w_in[:, D_POOL:D_POOL + D_LRU], preferred_element_type=F32).reshape(ns, l, D_LRU)
    gate_s[...] = jnp.dot(xb, w_in[:, D_POOL + D_LRU:], preferred_element_type=F32)

    pos = pos0 + t * l + lax.broadcasted_iota(I32, (ns, l, LANE_CHUNK), 1).reshape(rows, LANE_CHUNK)

    for g, w in enumerate(POOL_WINDOWS):
        cols = slice(g * POOL_GROUP, (g + 1) * POOL_GROUP)
        s = pool_ext[:, :, cols].reshape(ns * (POOL_HALO + l), POOL_GROUP)
        shift = 1
        while shift < w:
            s = s + pltpu.roll(s, shift, 0)
            shift *= 2
        win = s.reshape(ns, POOL_HALO + l, POOL_GROUP)[:, POOL_HALO:, :].reshape(rows, POOL_GROUP)
        u = pool_ext[:, POOL_HALO:, cols].reshape(rows, POOL_GROUP)
        if pos0 >= w - 1:
            inv = 1.0 / w
        else:
            inv = 1.0 / jnp.minimum(pos + 1, w).astype(F32)
        d = win * inv - u
        z = jnp.dot(d.astype(BF16), pool_w[g], preferred_element_type=F32) + pool_b[:, cols]
        y_s[:, cols] = (z * pool_scale[:, cols]).astype(BF16)

    lam_v = lam[...]
    softplus_neg = jnp.maximum(-lam_v, 0.0) + jnp.log1p(jnp.exp(-jnp.abs(lam_v)))
    log_a_scale = -LRU_C * softplus_neg
    sub = lax.broadcasted_iota(I32, (rows // SUBLANES, SUBLANES, LANE_CHUNK), 1)
    for c in range(D_LRU // LANE_CHUNK):
        cols = slice(c * LANE_CHUNK, (c + 1) * LANE_CHUNK)
        ce = conv_ext[:, :, cols].reshape(ns * (CONV_HALO + l), LANE_CHUNK)
        conv = conv_b[:, cols] + conv_w[CONV_WIDTH - 1:CONV_WIDTH, cols] * ce
        for k in range(1, CONV_WIDTH):
            conv = conv + conv_w[CONV_WIDTH - 1 - k:CONV_WIDTH - k, cols] * pltpu.roll(ce, k, 0)
        conv = conv.reshape(ns, CONV_HALO + l, LANE_CHUNK)[:, CONV_HALO:, :].reshape(rows, LANE_CHUNK)
        gz = jnp.dot(conv.astype(BF16), gate_w[c], preferred_element_type=F32)
        gz = gz + gate_b[:, 2 * c * LANE_CHUNK:2 * (c + 1) * LANE_CHUNK]
        r = 0.5 * jnp.tanh(0.5 * gz[:, :LANE_CHUNK]) + 0.5
        i = 0.5 * jnp.tanh(0.5 * gz[:, LANE_CHUNK:]) + 0.5
        log_a = r * log_a_scale[:, cols]
        a = jnp.exp(log_a)
        th = jnp.tanh(log_a)
        q = (th + th) / (th - 1.0)
        mult = jnp.where(q > 0.0, q * lax.rsqrt(q), 0.0)
        if pos0 == 0:
            mult = jnp.where(pos == 0, 1.0, mult)
        u = mult * (i * conv)
        a3 = a.reshape(rows // SUBLANES, SUBLANES, LANE_CHUNK)
        u3 = u.reshape(rows // SUBLANES, SUBLANES, LANE_CHUNK)
        for sh in (1, 2, 4):
            keep = sub >= sh
            a_sh = jnp.where(keep, pltpu.roll(a3, sh, 1), 1.0)
            u_sh = jnp.where(keep, pltpu.roll(u3, sh, 1), 0.0)
            u3 = a3 * u_sh + u3
            a3 = a3 * a_sh
        a_s[:, :, cols] = a3.reshape(ns, l, LANE_CHUNK)
        u_s[:, :, cols] = u3.reshape(ns, l, LANE_CHUNK)

    hp = h_prev[...]
    for gi in range(l // SUBLANES):
        sl = slice(gi * SUBLANES, (gi + 1) * SUBLANES)
        h = a_s[:, sl, :] * hp + u_s[:, sl, :]
        u_s[:, sl, :] = h
        hp = h[:, SUBLANES - 1:SUBLANES, :]
    h_prev[...] = hp

    for c in range(D_LRU // LANE_CHUNK):
        cols = slice(c * LANE_CHUNK, (c + 1) * LANE_CHUNK)
        h = u_s[:, :, cols].reshape(rows, LANE_CHUNK)
        y_s[:, D_POOL + c * LANE_CHUNK:D_POOL + (c + 1) * LANE_CHUNK] = (
            h * jax.nn.gelu(gate_s[:, cols])).astype(BF16)

    mix = jnp.dot(y_s[...], w_out[...], preferred_element_type=F32)
    xin = alpha * x_ref[...].reshape(rows, D_MODEL) + mix
    mu = jnp.mean(xin, axis=-1, keepdims=True)
    xc = xin - mu
    var = jnp.mean(xc * xc, axis=-1, keepdims=True)
    x1 = xc * lax.rsqrt(var + LN_EPS) * ln_g[...] + ln_b[...]
    x1_ref[...] = x1

    hi = x1.astype(BF16)
    lo = (x1 - hi.astype(F32)).astype(BF16)
    both = lax.dot_general(rt_hl[...], hi, _NT, preferred_element_type=F32)
    lg = both[0:ROUTER_ROWS] + both[ROUTER_PAD:ROUTER_PAD + ROUTER_ROWS]
    lg = lg + lax.dot_general(rt_hl[0:ROUTER_PAD, :], lo, _NT, preferred_element_type=F32)[0:ROUTER_ROWS]
    lg_ref[...] = lg

    pool_o[...] = pool_ext[:, POOL_HALO + l - POOL_STATE:, :]
    conv_o[...] = conv_ext[:, CONV_HALO + l - CONV_STATE:, :]
    h_o[...] = hp
    pool_ext[:, 0:POOL_HALO, :] = pool_ext[:, l:l + POOL_HALO, :]
    conv_ext[:, 0:CONV_HALO, :] = conv_ext[:, l:l + CONV_HALO, :]


def _mixer_call(x, pool_in, conv_in, h_in, wts, x1_buf, lg_buf, *, ns, l, pos0, alpha, n_total, blk0):
    bsz, tlen, _ = x.shape
    nb, nt = bsz // ns, tlen // l
    rows = ns * l
    assert rows % ROWS == 0 and bsz % ns == 0 and tlen % l == 0 and l % SUBLANES == 0
    aliased = x1_buf is not None

    def row_blk(b, t):
        return (blk0 + b * nt + t, 0)

    in_specs = [
        pl.BlockSpec((ns, l, D_MODEL), lambda b, t: (b, t, 0)),
        pl.BlockSpec((ns, POOL_STATE, D_POOL), lambda b, t: (b, 0, 0)),
        pl.BlockSpec((ns, CONV_STATE, D_LRU), lambda b, t: (b, 0, 0)),
        pl.BlockSpec((ns, 1, D_LRU), lambda b, t: (b, 0, 0)),
    ] + [_const_spec(w.shape) for w in wts]
    args = [x, pool_in, conv_in, h_in, *wts]
    aliases = {}
    if aliased:
        in_specs += [pl.BlockSpec(memory_space=pl.ANY), pl.BlockSpec(memory_space=pl.ANY)]
        aliases = {len(args): 0, len(args) + 1: 1}
        args += [x1_buf, lg_buf]
    out_shape = (
        jax.ShapeDtypeStruct((n_total, D_MODEL), F32),
        jax.ShapeDtypeStruct((ROUTER_ROWS, n_total), F32),
        jax.ShapeDtypeStruct((bsz, POOL_STATE, D_POOL), F32),
        jax.ShapeDtypeStruct((bsz, CONV_STATE, D_LRU), F32),
        jax.ShapeDtypeStruct((bsz, 1, D_LRU), F32),
    )
    out_specs = (
        pl.BlockSpec((rows, D_MODEL), row_blk),
        pl.BlockSpec((ROUTER_ROWS, rows), lambda b, t: (0, blk0 + b * nt + t)),
        pl.BlockSpec((ns, POOL_STATE, D_POOL), lambda b, t: (b, 0, 0)),
        pl.BlockSpec((ns, CONV_STATE, D_LRU), lambda b, t: (b, 0, 0)),
        pl.BlockSpec((ns, 1, D_LRU), lambda b, t: (b, 0, 0)),
    )
    scratch = [
        pltpu.VMEM((ns, POOL_HALO + l, D_POOL), F32),
        pltpu.VMEM((ns, CONV_HALO + l, D_LRU), F32),
        pltpu.VMEM((ns, 1, D_LRU), F32),
        pltpu.VMEM((rows, D_LRU), F32),
        pltpu.VMEM((ns, l, D_LRU), F32),
        pltpu.VMEM((ns, l, D_LRU), F32),
        pltpu.VMEM((rows, D_MODEL), BF16),
    ]
    return pl.pallas_call(
        functools.partial(_mixer_kernel, ns=ns, l=l, pos0=pos0, alpha=alpha, aliased=aliased),
        grid=(nb, nt),
        in_specs=in_specs,
        out_specs=out_specs,
        out_shape=out_shape,
        scratch_shapes=scratch,
        input_output_aliases=aliases,
        compiler_params=pltpu.CompilerParams(
            dimension_semantics=("arbitrary", "arbitrary"), vmem_limit_bytes=VMEM_LIMIT),
        name="mixer_seq" if nt > 1 else "mixer_step",
    )(*args)


def _route_kernel(lg_ref, bias_ref, dest_ref, tw_ref, meta_ref, e_s, r_s, *, n):
    blk = 256
    nblk = n // blk
    neg_inf = -jnp.inf
    ridx8 = lax.broadcasted_iota(I32, (SUBLANES, blk), 0).astype(F32)
    eidx = lax.broadcasted_iota(I32, (N_EXPERTS, blk), 0).astype(F32)
    tri = (lax.broadcasted_iota(I32, (blk, blk), 0) < lax.broadcasted_iota(I32, (blk, blk), 1))
    tri = jnp.where(tri, 1.0, 0.0).astype(BF16)

    def first_idx(vals, m):
        return jnp.min(jnp.where(vals == m, ridx8, float(SUBLANES)), axis=0, keepdims=True)

    def pass1(j, base):
        ls = pl.ds(pl.multiple_of(j * blk, blk), blk)
        lg = lg_ref[0:SUBLANES, ls] + bias_ref[0:SUBLANES, :]
        lg = jnp.where(ridx8 < N_GROUPS, lg, neg_inf)
        m = jnp.max(lg, axis=0, keepdims=True)
        gi = first_idx(lg, m)
        pg_sel = 1.0 / jnp.sum(jnp.exp(lg - m), axis=0, keepdims=True)
        le = jnp.zeros((PER_GROUP, blk), F32)
        for g in range(N_GROUPS):
            rows = slice(SUBLANES + g * PER_GROUP, SUBLANES + (g + 1) * PER_GROUP)
            le = jnp.where(gi == float(g), lg_ref[rows, ls] + bias_ref[rows, :], le)
        m1 = jnp.max(le, axis=0, keepdims=True)
        i1 = first_idx(le, m1)
        le2 = jnp.where(ridx8 == i1, neg_inf, le)
        m2 = jnp.max(le2, axis=0, keepdims=True)
        i2 = first_idx(le2, m2)
        e21 = jnp.exp(m2 - m1)
        denom = 1.0 / (1.0 + e21)
        tw_ref[0:1, ls] = pg_sel * denom
        tw_ref[1:2, ls] = pg_sel * (e21 * denom)
        e1 = gi * float(PER_GROUP) + i1
        e2 = gi * float(PER_GROUP) + i2
        e_s[0:1, ls] = e1
        e_s[1:2, ls] = e2
        oh1 = jnp.where(eidx == e1, 1.0, 0.0)
        oh2 = jnp.where(eidx == e2, 1.0, 0.0)
        oh = oh1 + oh2
        before = base + jnp.dot(oh.astype(BF16), tri, preferred_element_type=F32)
        r_s[0:1, ls] = jnp.sum(oh1 * before, axis=0, keepdims=True)
        r_s[1:2, ls] = jnp.sum(oh2 * before, axis=0, keepdims=True)
        return base + jnp.sum(oh, axis=1, keepdims=True)

    cnt = lax.fori_loop(0, nblk, pass1, jnp.zeros((N_EXPERTS, blk), F32))

    ntile = jnp.floor((cnt + float(TM - 1)) * (1.0 / TM))
    lt = (lax.broadcasted_iota(I32, (N_EXPERTS, N_EXPERTS), 1) < lax.broadcasted_iota(I32, (N_EXPERTS, N_EXPERTS), 0))
    lt = jnp.where(lt, 1.0, 0.0).astype(BF16)
    tile0 = jnp.dot(lt, ntile.astype(BF16), preferred_element_type=F32)
    row0 = tile0 * float(TM)

    def pass2(j, carry):
        ls = pl.ds(pl.multiple_of(j * blk, blk), blk)
        for k in range(2):
            ohk = jnp.where(eidx == e_s[k:k + 1, ls], 1.0, 0.0)
            dest = r_s[k:k + 1, ls] + jnp.sum(ohk * row0, axis=0, keepdims=True)
            dest_ref[k:k + 1, ls] = dest.astype(I32)
        return carry

    lax.fori_loop(0, nblk, pass2, 0)

    ml = slice(0, META_LANES)
    total = jnp.sum(ntile[:, ml], axis=0, keepdims=True)
    tile_i = lax.broadcasted_iota(I32, (N_EXPERTS, META_LANES), 1).astype(F32)
    tile_c = jnp.minimum(tile_i, total - 1.0)
    tile_end = tile0[:, ml] + ntile[:, ml]
    te = jnp.sum(jnp.where(tile_end <= tile_c, 1.0, 0.0), axis=0, keepdims=True)
    meta_ref[...] = jnp.zeros((SUBLANES, META_LANES), I32)
    meta_ref[0:1, :] = te.astype(I32)
    meta_ref[1:2, :] = total.astype(I32)
    later = (eidx[:, ml] > te) & (ntile[:, ml] > 0.0)
    nxt = jnp.min(jnp.where(later, eidx[:, ml], float(N_EXPERTS)), axis=0, keepdims=True)
    meta_ref[2:3, :] = nxt.astype(I32)


def _route_call(lg, bias, n):
    return pl.pallas_call(
        functools.partial(_route_kernel, n=n),
        out_shape=(
            jax.ShapeDtypeStruct((2, n), I32),
            jax.ShapeDtypeStruct((2, n), F32),
            jax.ShapeDtypeStruct((SUBLANES, META_LANES), I32),
        ),
        scratch_shapes=[pltpu.VMEM((2, n), F32), pltpu.VMEM((2, n), F32)],
        compiler_params=pltpu.CompilerParams(vmem_limit_bytes=VMEM_LIMIT),
        name="route",
    )(lg, bias)


def _inverse_kernel(dest_ref, init_hbm, src_ref, sem, *, n):
    init = pltpu.make_async_copy(init_hbm, src_ref, sem)
    init.start()
    init.wait()

    def fill(t, c):
        src_ref[dest_ref[t]] = t
        src_ref[dest_ref[n + t]] = t
        return c

    lax.fori_loop(0, n, fill, 0, unroll=8)


def _inverse_call(dest_flat, n, n_slots):
    return pl.pallas_call(
        functools.partial(_inverse_kernel, n=n),
        in_specs=[pl.BlockSpec(memory_space=pltpu.SMEM), pl.BlockSpec(memory_space=pl.ANY)],
        out_specs=pl.BlockSpec(memory_space=pltpu.SMEM),
        out_shape=jax.ShapeDtypeStruct((n_slots,), I32),
        scratch_shapes=[pltpu.SemaphoreType.DMA(())],
        name="inverse",
    )(dest_flat, jnp.arange(n_slots, dtype=I32) % n)


def _row_copy(src_hbm, row, dst_vmem, slot, sem):
    return pltpu.make_async_copy(src_hbm.at[pl.ds(row, 1), :], dst_vmem.at[pl.ds(slot, 1), :], sem)


def _tile_copy(src_hbm, dst_vmem, sem):
    return pltpu.make_async_copy(src_hbm.at[pl.ds(0, dst_vmem.shape[0]), :], dst_vmem, sem)


def _moe_kernel(te_ref, ne_ref, nt_ref, *refs):
    src_refs = refs[:GATHER_DEPTH]
    (x1_hbm, w1_hbm, w3_hbm, w2_hbm, o_hbm,
     xbuf, obuf, wf1, wf3, wf2, w1b, w3b, w2b, wslot, gsem, osem, wsem) = refs[GATHER_DEPTH:]
    ahead = GATHER_DEPTH - 1
    i = pl.program_id(0)
    nt = nt_ref[0]

    def gather_start(src_ref, slot):
        for r in range(TM):
            _row_copy(x1_hbm, src_ref[0, 0, r], xbuf.at[slot], r, gsem.at[slot]).start()

    def out_copy(tile, slot):
        return pltpu.make_async_copy(obuf.at[slot], o_hbm.at[pl.ds(tile * TM, TM), :], osem.at[slot])

    def weight_copies(e, slot):
        return (pltpu.make_async_copy(w1_hbm.at[e], wf1.at[slot], wsem.at[slot, 0]),
                pltpu.make_async_copy(w3_hbm.at[e], wf3.at[slot], wsem.at[slot, 1]),
                pltpu.make_async_copy(w2_hbm.at[e], wf2.at[slot], wsem.at[slot, 2]))

    @pl.when(i == 0)
    def _():
        for cp in weight_copies(te_ref[0], 0):
            cp.start(priority=BULK_DMA_PRIORITY)
        wslot[0] = 0
        for k in range(ahead):
            gather_start(src_refs[k], k)

    @pl.when(i < nt)
    def _():
        slot = lax.rem(i, 2)
        gslot = lax.rem(i, GATHER_DEPTH)

        @pl.when(i >= 2)
        def _():
            out_copy(0, slot).wait()

        _tile_copy(x1_hbm, xbuf.at[gslot], gsem.at[gslot]).wait()

        @pl.when((i == 0) | (te_ref[i] != te_ref[jnp.maximum(i - 1, 0)]))
        def _():
            cur = wslot[0]
            for cp in weight_copies(te_ref[i], cur):
                cp.wait()
            w1b[...] = wf1[cur].astype(BF16)
            w3b[...] = wf3[cur].astype(BF16)
            w2b[...] = wf2[cur].astype(BF16)

            @pl.when(ne_ref[i] < N_EXPERTS)
            def _():
                for cp in weight_copies(ne_ref[i], 1 - cur):
                    cp.start(priority=BULK_DMA_PRIORITY)

            wslot[0] = 1 - cur

        xb = xbuf[gslot].astype(BF16)
        gather_start(src_refs[ahead], lax.rem(i + ahead, GATHER_DEPTH))
        h1 = jnp.dot(xb, w1b[...], preferred_element_type=F32)
        h3 = jnp.dot(xb, w3b[...], preferred_element_type=F32)
        h = (jax.nn.silu(h1) * h3).astype(BF16)
        obuf[slot] = jnp.dot(h, w2b[...], preferred_element_type=F32)
        out_copy(i, slot).start(priority=BULK_DMA_PRIORITY)

        @pl.when(i + 1 >= nt)
        def _():
            for k in range(1, GATHER_DEPTH):
                s = lax.rem(i + k, GATHER_DEPTH)
                _tile_copy(x1_hbm, xbuf.at[s], gsem.at[s]).wait()
            out_copy(0, slot).wait()

            @pl.when(i >= 1)
            def _():
                out_copy(0, 1 - slot).wait()


def _moe_call(te, ne, ntiles, src3, x1, w1, w3, w2, max_tiles):
    grid_spec = pltpu.PrefetchScalarGridSpec(
        num_scalar_prefetch=3,
        grid=(max_tiles,),
        in_specs=[
            pl.BlockSpec((1, 1, TM), lambda i, te, ne, nt, k=k: (jnp.minimum(i + k, max_tiles - 1), 0, 0),
                         memory_space=pltpu.SMEM)
            for k in range(GATHER_DEPTH)
        ] + [
            pl.BlockSpec(memory_space=pl.ANY),
            pl.BlockSpec(memory_space=pl.ANY),
            pl.BlockSpec(memory_space=pl.ANY),
            pl.BlockSpec(memory_space=pl.ANY),
        ],
        out_specs=pl.BlockSpec(memory_space=pl.ANY),
        scratch_shapes=[
            pltpu.VMEM((GATHER_DEPTH, TM, D_MODEL), F32),
            pltpu.VMEM((2, TM, D_MODEL), F32),
            pltpu.VMEM((2, D_MODEL, D_EXPERT), F32),
            pltpu.VMEM((2, D_MODEL, D_EXPERT), F32),
            pltpu.VMEM((2, D_EXPERT, D_MODEL), F32),
            pltpu.VMEM((D_MODEL, D_EXPERT), BF16),
            pltpu.VMEM((D_MODEL, D_EXPERT), BF16),
            pltpu.VMEM((D_EXPERT, D_MODEL), BF16),
            pltpu.SMEM((1,), I32),
            pltpu.SemaphoreType.DMA((GATHER_DEPTH,)),
            pltpu.SemaphoreType.DMA((2,)),
            pltpu.SemaphoreType.DMA((2, 3)),
        ],
    )
    return pl.pallas_call(
        _moe_kernel,
        grid_spec=grid_spec,
        out_shape=jax.ShapeDtypeStruct((max_tiles * TM, D_MODEL), F32),
        compiler_params=pltpu.CompilerParams(dimension_semantics=("arbitrary",), vmem_limit_bytes=VMEM_LIMIT),
        name="moe",
    )(te, ne, ntiles, *([src3] * GATHER_DEPTH), x1, w1, w3, w2)


def _combine_kernel(*refs, alpha, nsteps):
    dest_refs = refs[:COMBINE_DEPTH]
    x1_ref, tw_ref, ys_hbm, ln_g, ln_b, o_ref, buf, sem = refs[COMBINE_DEPTH:]
    ahead = COMBINE_DEPTH - 1
    i = pl.program_id(0)
    slot = lax.rem(i, COMBINE_DEPTH)

    def gather_start(dest_ref, s):
        for r in range(ROWS):
            for k in range(2):
                _row_copy(ys_hbm, dest_ref[0, k, r], buf.at[s, k], r, sem.at[s, k]).start(priority=k)

    def gather_wait(s):
        for k in range(2):
            _tile_copy(ys_hbm, buf.at[s, k], sem.at[s, k]).wait()

    @pl.when(i == 0)
    def _():
        for j in range(ahead):
            gather_start(dest_refs[j], j)

    gather_wait(slot)
    tw = tw_ref[...]
    moe = tw[:, 0:1] * buf[slot, 0] + tw[:, 1:2] * buf[slot, 1]
    gather_start(dest_refs[ahead], lax.rem(i + ahead, COMBINE_DEPTH))
    xin = alpha * x1_ref[...] + moe
    mu = jnp.mean(xin, axis=-1, keepdims=True)
    xc = xin - mu
    var = jnp.mean(xc * xc, axis=-1, keepdims=True)
    y = xc * lax.rsqrt(var + LN_EPS) * ln_g[...] + ln_b[...]
    o_ref[...] = y.reshape(o_ref.shape)

    @pl.when(i == nsteps - 1)
    def _():
        for j in range(1, COMBINE_DEPTH):
            gather_wait(lax.rem(i + j, COMBINE_DEPTH))


def _combine_call(dest3, x1, tw, ys, ln_g, ln_b, out_shape, blk, blk0, alpha):
    bsz, tlen, _ = out_shape
    nt = tlen // blk[1]
    nsteps = (bsz // blk[0]) * nt
    return pl.pallas_call(
        functools.partial(_combine_kernel, alpha=alpha, nsteps=nsteps),
        grid=(nsteps,),
        in_specs=[
            pl.BlockSpec((1, 2, ROWS), lambda i, j=j: (blk0 + jnp.minimum(i + j, nsteps - 1), 0, 0),
                         memory_space=pltpu.SMEM)
            for j in range(COMBINE_DEPTH)
        ] + [
            pl.BlockSpec((ROWS, D_MODEL), lambda i: (blk0 + i, 0)),
            pl.BlockSpec((ROWS, 2), lambda i: (blk0 + i, 0)),
            pl.BlockSpec(memory_space=pl.ANY),
            _const_spec(ln_g.shape),
            _const_spec(ln_b.shape),
        ],
        out_specs=pl.BlockSpec(blk, lambda i: (i // nt, i % nt, 0)),
        out_shape=jax.ShapeDtypeStruct(out_shape, F32),
        scratch_shapes=[
            pltpu.VMEM((COMBINE_DEPTH, 2, ROWS, D_MODEL), F32),
            pltpu.SemaphoreType.DMA((COMBINE_DEPTH, 2)),
        ],
        compiler_params=pltpu.CompilerParams(dimension_semantics=("arbitrary",), vmem_limit_bytes=VMEM_LIMIT),
        name="combine",
    )(*([dest3] * COMBINE_DEPTH), x1, tw, ys, ln_g, ln_b)


def _block_diag_chunks(w):
    per = LANE_CHUNK // LRU_HEAD_DIM
    nchunk = LRU_HEADS // per
    w4 = w.reshape(nchunk, per, LRU_HEAD_DIM, LRU_HEAD_DIM)
    bd = jnp.einsum("cjio,jk->cjiko", w4, jnp.eye(per, dtype=w.dtype))
    return bd.reshape(nchunk, LANE_CHUNK, LANE_CHUNK)


def _layer(yp, ys, pool_s, conv_s, h_s, lw, alpha, past_len):
    (w_in, pool_w, pool_b, pool_scale, conv_w, conv_b, rg_w, rg_b, ig_w, ig_b, lru_lambda, w_out,
     ln1_g, ln1_b, rgw, rgb, rew, reb, w1, w3, w2, ln2_g, ln2_b) = lw
    bp, tp, _ = yp.shape
    bs, ts, _ = ys.shape
    n_p, n_s = bp * tp, bs * ts
    n = n_p + n_s
    assert n % ROWS == 0 and n_p % ROWS == 0
    max_tiles = (2 * n) // TM + N_EXPERTS
    assert max_tiles <= META_LANES

    nchunk = D_LRU // LANE_CHUNK
    gate_w = jnp.concatenate([_block_diag_chunks(rg_w), _block_diag_chunks(ig_w)], axis=-1).astype(BF16)
    gate_b = jnp.concatenate(
        [rg_b.reshape(nchunk, LANE_CHUNK), ig_b.reshape(nchunk, LANE_CHUNK)], axis=-1).reshape(1, 2 * D_LRU)
    rt = jnp.concatenate(
        [rgw.T, jnp.zeros((SUBLANES - N_GROUPS, D_MODEL), F32),
         jnp.transpose(rew, (0, 2, 1)).reshape(N_EXPERTS, D_MODEL)], axis=0)
    rt_hi = rt.astype(BF16)
    rt_lo = (rt - rt_hi.astype(F32)).astype(BF16)
    r_bias = jnp.concatenate([rgb, jnp.zeros((SUBLANES - N_GROUPS,), F32), reb.reshape(N_EXPERTS)]).reshape(ROUTER_ROWS, 1)
    wts = (
        w_in.astype(BF16), pool_w.astype(BF16), pool_b.reshape(1, D_POOL), pool_scale.reshape(1, D_POOL),
        conv_w, conv_b.reshape(1, D_LRU), gate_w, gate_b, lru_lambda.reshape(1, D_LRU), w_out.astype(BF16),
        ln1_g.reshape(1, D_MODEL), ln1_b.reshape(1, D_MODEL),
        jnp.pad(jnp.stack([rt_hi, rt_lo]), ((0, 0), (0, ROUTER_PAD - ROUTER_ROWS), (0, 0))).reshape(2 * ROUTER_PAD, D_MODEL),
    )

    x1, lg, pool_p, conv_p, h_p = _mixer_call(
        yp, jnp.zeros((bp, POOL_STATE, D_POOL), F32), jnp.zeros((bp, CONV_STATE, D_LRU), F32),
        jnp.zeros((bp, 1, D_LRU), F32), wts, None, None,
        ns=1, l=SEQ_ROWS, pos0=0, alpha=alpha, n_total=n, blk0=0)
    x1, lg, pool_n, conv_n, h_n = _mixer_call(
        ys, pool_s, conv_s, h_s.reshape(bs, 1, D_LRU), wts, x1, lg,
        ns=ROWS // ts, l=ts, pos0=past_len, alpha=alpha, n_total=n, blk0=n_p // ROWS)

    dest, tw, meta = _route_call(lg, r_bias, n)
    src = _inverse_call(dest.reshape(2 * n), n, max_tiles * TM)
    ysort = _moe_call(meta[0], meta[2], meta[1, 0:1], src.reshape(max_tiles, 1, TM), x1, w1, w3, w2, max_tiles)

    dest3 = jnp.transpose(dest.reshape(2, n // ROWS, ROWS), (1, 0, 2))
    tw_t = tw.T
    g2, b2 = ln2_g.reshape(1, D_MODEL), ln2_b.reshape(1, D_MODEL)
    out_p = _combine_call(dest3, x1, tw_t, ysort, g2, b2, (bp, tp, D_MODEL), (1, ROWS, D_MODEL), 0, alpha)
    out_s = _combine_call(dest3, x1, tw_t, ysort, g2, b2, (bs, ts, D_MODEL), (ROWS // ts, ts, D_MODEL),
                          n_p // ROWS, alpha)
    states = (pool_p, conv_p, h_p.reshape(bp, D_LRU), pool_n, conv_n, h_n.reshape(bs, D_LRU))
    return out_p, out_s, states


def kernel(x_prompt, x_sample, state_pool, state_conv, state_h, w_in, pool_w, pool_b, pool_scale, conv_w, conv_b, rg_w, rg_b, ig_w, ig_b, lru_lambda, w_out, ln1_g, ln1_b, router_group_w, router_group_b, router_expert_w, router_expert_b, expert_w1, expert_w3, expert_w2, ln2_g, ln2_b):
    depth = w_in.shape[0]
    alpha = (2.0 * depth) ** 0.25
    past_len = 16384
    layer_weights = (w_in, pool_w, pool_b, pool_scale, conv_w, conv_b, rg_w, rg_b, ig_w, ig_b, lru_lambda, w_out,
                     ln1_g, ln1_b, router_group_w, router_group_b, router_expert_w, router_expert_b,
                     expert_w1, expert_w3, expert_w2, ln2_g, ln2_b)
    yp, ys = x_prompt, x_sample
    outs = [[] for _ in range(6)]
    for layer in range(depth):
        lw = tuple(w[layer] for w in layer_weights)
        yp, ys, states = _layer(yp, ys, state_pool[layer], state_conv[layer], state_h[layer], lw, alpha, past_len)
        for acc, s in zip(outs, states):
            acc.append(s)
    return (yp, ys) + tuple(jnp.stack(o, axis=0) for o in outs)
```

```python
import functools

import jax
import jax.numpy as jnp
from jax import lax
from jax.experimental import pallas as pl
from jax.experimental.pallas import tpu as pltpu

F32 = jnp.float32
BF16 = jnp.bfloat16
I32 = jnp.int32

D_MODEL = 2048
D_POOL = 1024
D_LRU = 1024
POOL_WINDOWS = (2, 4, 8, 16)
POOL_GROUP = D_POOL // len(POOL_WINDOWS)
POOL_STATE = max(POOL_WINDOWS) - 1
CONV_WIDTH = 4
CONV_STATE = CONV_WIDTH - 1
LRU_HEADS = 16
LRU_HEAD_DIM = D_LRU // LRU_HEADS
LRU_C = 8.0
N_GROUPS = 4
PER_GROUP = 8
N_EXPERTS = N_GROUPS * PER_GROUP
D_EXPERT = D_MODEL // 4
LN_EPS = 1e-5

SUBLANES = 8
LANES = 128
LANE_CHUNK = 256
POOL_HALO = 16
CONV_HALO = 8
ROWS = 256
SEQ_ROWS = 512
TM = 256
GATHER_DEPTH = 4
COMBINE_DEPTH = 3
ROUTER_ROWS = 8 + N_EXPERTS
META_LANES = LANES
VMEM_LIMIT = 56 * 1024 * 1024
BULK_DMA_PRIORITY = 1

_NT = (((1,), (1,)), ((), ()))


def _const_spec(shape):
    nd = len(shape)
    return pl.BlockSpec(shape, lambda *_: (0,) * nd, pipeline_mode=pl.Buffered(1))


def _mixer_kernel(x_ref, pool_in, conv_in, h_in, w_in, pool_w, pool_b, pool_scale, conv_w, conv_b,
                  gate_w, gate_b, lam, w_out, ln_g, ln_b, rt_t, *rest, ns, l, pos0, alpha, aliased):
    if aliased:
        rest = rest[2:]
    (x1_ref, lg_ref, pool_o, conv_o, h_o,
     pool_ext, conv_ext, h_prev, gate_s, a_s, u_s, y_s) = rest
    t = pl.program_id(1)
    rows = ns * l

    @pl.when(t == 0)
    def _():
        pool_ext[:, 0:POOL_HALO - POOL_STATE, :] = jnp.zeros((ns, POOL_HALO - POOL_STATE, D_POOL), F32)
        pool_ext[:, POOL_HALO - POOL_STATE:POOL_HALO, :] = pool_in[...]
        conv_ext[:, 0:CONV_HALO - CONV_STATE, :] = jnp.zeros((ns, CONV_HALO - CONV_STATE, D_LRU), F32)
        conv_ext[:, CONV_HALO - CONV_STATE:CONV_HALO, :] = conv_in[...]
        h_prev[...] = h_in[...]

    xb = x_ref[...].reshape(rows, D_MODEL).astype(BF16)
    pool_ext[:, POOL_HALO:, :] = jnp.dot(
        xb, w_in[:, 0:D_POOL], preferred_element_type=F32).reshape(ns, l, D_POOL)
    conv_ext[:, CONV_HALO:, :] = jnp.dot(
        xb, w_in[:, D_POOL:D_POOL + D_LRU], preferred_element_type=F32).reshape(ns, l, D_LRU)
    gate_s[...] = jnp.dot(xb, w_in[:, D_POOL + D_LRU:], preferred_element_type=F32)

    pos = pos0 + t * l + lax.broadcasted_iota(I32, (ns, l, LANE_CHUNK), 1).reshape(rows, LANE_CHUNK)

    for g, w in enumerate(POOL_WINDOWS):
        cols = slice(g * POOL_GROUP, (g + 1) * POOL_GROUP)
        s = pool_ext[:, :, cols].reshape(ns * (POOL_HALO + l), POOL_GROUP)
        shift = 1
        while shift < w:
            s = s + pltpu.roll(s, shift, 0)
            shift *= 2
        win = s.reshape(ns, POOL_HALO + l, POOL_GROUP)[:, POOL_HALO:, :].reshape(rows, POOL_GROUP)
        u = pool_ext[:, POOL_HALO:, cols].reshape(rows, POOL_GROUP)
        if pos0 >= w - 1:
            inv = 1.0 / w
        else:
            inv = 1.0 / jnp.minimum(pos + 1, w).astype(F32)
        d = win * inv - u
        z = jnp.dot(d.astype(BF16), pool_w[g], preferred_element_type=F32) + pool_b[:, cols]
        y_s[:, cols] = (z * pool_scale[:, cols]).astype(BF16)

    lam_v = lam[...]
    softplus_neg = jnp.maximum(-lam_v, 0.0) + jnp.log1p(jnp.exp(-jnp.abs(lam_v)))
    log_a_scale = -LRU_C * softplus_neg
    sub = lax.broadcasted_iota(I32, (rows // SUBLANES, SUBLANES, LANE_CHUNK), 1)
    for c in range(D_LRU // LANE_CHUNK):
        cols = slice(c * LANE_CHUNK, (c + 1) * LANE_CHUNK)
        ce = conv_ext[:, :, cols].reshape(ns * (CONV_HALO + l), LANE_CHUNK)
        conv = conv_b[:, cols] + conv_w[CONV_WIDTH - 1:CONV_WIDTH, cols] * ce
        for k in range(1, CONV_WIDTH):
            conv = conv + conv_w[CONV_WIDTH - 1 - k:CONV_WIDTH - k, cols] * pltpu.roll(ce, k, 0)
        conv = conv.reshape(ns, CONV_HALO + l, LANE_CHUNK)[:, CONV_HALO:, :].reshape(rows, LANE_CHUNK)
        gz = jnp.dot(conv.astype(BF16), gate_w[c], preferred_element_type=F32)
        gz = gz + gate_b[:, 2 * c * LANE_CHUNK:2 * (c + 1) * LANE_CHUNK]
        r = 0.5 * jnp.tanh(0.5 * gz[:, :LANE_CHUNK]) + 0.5
        i = 0.5 * jnp.tanh(0.5 * gz[:, LANE_CHUNK:]) + 0.5
        log_a = r * log_a_scale[:, cols]
        a = jnp.exp(log_a)
        th = jnp.tanh(log_a)
        q = (th + th) / (th - 1.0)
        mult = jnp.where(q > 0.0, q * lax.rsqrt(q), 0.0)
        if pos0 == 0:
            mult = jnp.where(pos == 0, 1.0, mult)
        u = mult * (i * conv)
        a3 = a.reshape(rows // SUBLANES, SUBLANES, LANE_CHUNK)
        u3 = u.reshape(rows // SUBLANES, SUBLANES, LANE_CHUNK)
        for sh in (1, 2, 4):
            keep = sub >= sh
            a_sh = jnp.where(keep, pltpu.roll(a3, sh, 1), 1.0)
            u_sh = jnp.where(keep, pltpu.roll(u3, sh, 1), 0.0)
            u3 = a3 * u_sh + u3
            a3 = a3 * a_sh
        a_s[:, :, cols] = a3.reshape(ns, l, LANE_CHUNK)
        u_s[:, :, cols] = u3.reshape(ns, l, LANE_CHUNK)

    hp = h_prev[...]
    for gi in range(l // SUBLANES):
        sl = slice(gi * SUBLANES, (gi + 1) * SUBLANES)
        h = a_s[:, sl, :] * hp + u_s[:, sl, :]
        u_s[:, sl, :] = h
        hp = h[:, SUBLANES - 1:SUBLANES, :]
    h_prev[...] = hp

    for c in range(D_LRU // LANE_CHUNK):
        cols = slice(c * LANE_CHUNK, (c + 1) * LANE_CHUNK)
        h = u_s[:, :, cols].reshape(rows, LANE_CHUNK)
        y_s[:, D_POOL + c * LANE_CHUNK:D_POOL + (c + 1) * LANE_CHUNK] = (
            h * jax.nn.gelu(gate_s[:, cols])).astype(BF16)

    mix = jnp.dot(y_s[...], w_out[...], preferred_element_type=F32)
    xin = alpha * x_ref[...].reshape(rows, D_MODEL) + mix
    mu = jnp.mean(xin, axis=-1, keepdims=True)
    xc = xin - mu
    var = jnp.mean(xc * xc, axis=-1, keepdims=True)
    x1 = xc * lax.rsqrt(var + LN_EPS) * ln_g[...] + ln_b[...]
    x1_ref[...] = x1

    hi = x1.astype(BF16)
    lo = (x1 - hi.astype(F32)).astype(BF16)
    both = jnp.dot(hi, rt_t[...], preferred_element_type=F32)
    lg = both[:, 0:LANES] + both[:, LANES:2 * LANES]
    lg = lg + jnp.dot(lo, rt_t[:, 0:LANES], preferred_element_type=F32)
    lg_ref[...] = lg.T[0:ROUTER_ROWS]

    pool_o[...] = pool_ext[:, POOL_HALO + l - POOL_STATE:, :]
    conv_o[...] = conv_ext[:, CONV_HALO + l - CONV_STATE:, :]
    h_o[...] = hp
    pool_ext[:, 0:POOL_HALO, :] = pool_ext[:, l:l + POOL_HALO, :]
    conv_ext[:, 0:CONV_HALO, :] = conv_ext[:, l:l + CONV_HALO, :]


def _mixer_call(x, pool_in, conv_in, h_in, wts, x1_buf, lg_buf, *, ns, l, pos0, alpha, n_total, blk0):
    bsz, tlen, _ = x.shape
    nb, nt = bsz // ns, tlen // l
    rows = ns * l
    assert rows % ROWS == 0 and bsz % ns == 0 and tlen % l == 0 and l % SUBLANES == 0
    aliased = x1_buf is not None

    def row_blk(b, t):
        return (blk0 + b * nt + t, 0)

    in_specs = [
        pl.BlockSpec((ns, l, D_MODEL), lambda b, t: (b, t, 0)),
        pl.BlockSpec((ns, POOL_STATE, D_POOL), lambda b, t: (b, 0, 0)),
        pl.BlockSpec((ns, CONV_STATE, D_LRU), lambda b, t: (b, 0, 0)),
        pl.BlockSpec((ns, 1, D_LRU), lambda b, t: (b, 0, 0)),
    ] + [_const_spec(w.shape) for w in wts]
    args = [x, pool_in, conv_in, h_in, *wts]
    aliases = {}
    if aliased:
        in_specs += [pl.BlockSpec(memory_space=pl.ANY), pl.BlockSpec(memory_space=pl.ANY)]
        aliases = {len(args): 0, len(args) + 1: 1}
        args += [x1_buf, lg_buf]
    out_shape = (
        jax.ShapeDtypeStruct((n_total, D_MODEL), F32),
        jax.ShapeDtypeStruct((ROUTER_ROWS, n_total), F32),
        jax.ShapeDtypeStruct((bsz, POOL_STATE, D_POOL), F32),
        jax.ShapeDtypeStruct((bsz, CONV_STATE, D_LRU), F32),
        jax.ShapeDtypeStruct((bsz, 1, D_LRU), F32),
    )
    out_specs = (
        pl.BlockSpec((rows, D_MODEL), row_blk),
        pl.BlockSpec((ROUTER_ROWS, rows), lambda b, t: (0, blk0 + b * nt + t)),
        pl.BlockSpec((ns, POOL_STATE, D_POOL), lambda b, t: (b, 0, 0)),
        pl.BlockSpec((ns, CONV_STATE, D_LRU), lambda b, t: (b, 0, 0)),
        pl.BlockSpec((ns, 1, D_LRU), lambda b, t: (b, 0, 0)),
    )
    scratch = [
        pltpu.VMEM((ns, POOL_HALO + l, D_POOL), F32),
        pltpu.VMEM((ns, CONV_HALO + l, D_LRU), F32),
        pltpu.VMEM((ns, 1, D_LRU), F32),
        pltpu.VMEM((rows, D_LRU), F32),
        pltpu.VMEM((ns, l, D_LRU), F32),
        pltpu.VMEM((ns, l, D_LRU), F32),
        pltpu.VMEM((rows, D_MODEL), BF16),
    ]
    return pl.pallas_call(
        functools.partial(_mixer_kernel, ns=ns, l=l, pos0=pos0, alpha=alpha, aliased=aliased),
        grid=(nb, nt),
        in_specs=in_specs,
        out_specs=out_specs,
        out_shape=out_shape,
        scratch_shapes=scratch,
        input_output_aliases=aliases,
        compiler_params=pltpu.CompilerParams(
            dimension_semantics=("arbitrary", "arbitrary"), vmem_limit_bytes=VMEM_LIMIT),
        name="mixer_seq" if nt > 1 else "mixer_step",
    )(*args)


def _route_kernel(lg_ref, bias_ref, dest_ref, tw_ref, meta_ref, e_s, r_s, *, n):
    blk = 256
    nblk = n // blk
    neg_inf = -jnp.inf
    ridx8 = lax.broadcasted_iota(I32, (SUBLANES, blk), 0).astype(F32)
    eidx = lax.broadcasted_iota(I32, (N_EXPERTS, blk), 0).astype(F32)
    tri = (lax.broadcasted_iota(I32, (blk, blk), 0) < lax.broadcasted_iota(I32, (blk, blk), 1))
    tri = jnp.where(tri, 1.0, 0.0).astype(BF16)

    def first_idx(vals, m):
        return jnp.min(jnp.where(vals == m, ridx8, float(SUBLANES)), axis=0, keepdims=True)

    def pass1(j, base):
        ls = pl.ds(pl.multiple_of(j * blk, blk), blk)
        lg = lg_ref[0:SUBLANES, ls] + bias_ref[0:SUBLANES, :]
        lg = jnp.where(ridx8 < N_GROUPS, lg, neg_inf)
        m = jnp.max(lg, axis=0, keepdims=True)
        gi = first_idx(lg, m)
        pg_sel = 1.0 / jnp.sum(jnp.exp(lg - m), axis=0, keepdims=True)
        le = jnp.zeros((PER_GROUP, blk), F32)
        for g in range(N_GROUPS):
            rows = slice(SUBLANES + g * PER_GROUP, SUBLANES + (g + 1) * PER_GROUP)
            le = jnp.where(gi == float(g), lg_ref[rows, ls] + bias_ref[rows, :], le)
        m1 = jnp.max(le, axis=0, keepdims=True)
        i1 = first_idx(le, m1)
        le2 = jnp.where(ridx8 == i1, neg_inf, le)
        m2 = jnp.max(le2, axis=0, keepdims=True)
        i2 = first_idx(le2, m2)
        e21 = jnp.exp(m2 - m1)
        denom = 1.0 / (1.0 + e21)
        tw_ref[0:1, ls] = pg_sel * denom
        tw_ref[1:2, ls] = pg_sel * (e21 * denom)
        e1 = gi * float(PER_GROUP) + i1
        e2 = gi * float(PER_GROUP) + i2
        e_s[0:1, ls] = e1
        e_s[1:2, ls] = e2
        oh1 = jnp.where(eidx == e1, 1.0, 0.0)
        oh2 = jnp.where(eidx == e2, 1.0, 0.0)
        oh = oh1 + oh2
        before = base + jnp.dot(oh.astype(BF16), tri, preferred_element_type=F32)
        r_s[0:1, ls] = jnp.sum(oh1 * before, axis=0, keepdims=True)
        r_s[1:2, ls] = jnp.sum(oh2 * before, axis=0, keepdims=True)
        return base + jnp.sum(oh, axis=1, keepdims=True)

    cnt = lax.fori_loop(0, nblk, pass1, jnp.zeros((N_EXPERTS, blk), F32))

    ntile = jnp.floor((cnt + float(TM - 1)) * (1.0 / TM))
    lt = (lax.broadcasted_iota(I32, (N_EXPERTS, N_EXPERTS), 1) < lax.broadcasted_iota(I32, (N_EXPERTS, N_EXPERTS), 0))
    lt = jnp.where(lt, 1.0, 0.0).astype(BF16)
    tile0 = jnp.dot(lt, ntile.astype(BF16), preferred_element_type=F32)
    row0 = tile0 * float(TM)

    def pass2(j, carry):
        ls = pl.ds(pl.multiple_of(j * blk, blk), blk)
        for k in range(2):
            ohk = jnp.where(eidx == e_s[k:k + 1, ls], 1.0, 0.0)
            dest = r_s[k:k + 1, ls] + jnp.sum(ohk * row0, axis=0, keepdims=True)
            dest_ref[k:k + 1, ls] = dest.astype(I32)
        return carry

    lax.fori_loop(0, nblk, pass2, 0)

    ml = slice(0, META_LANES)
    total = jnp.sum(ntile[:, ml], axis=0, keepdims=True)
    tile_i = lax.broadcasted_iota(I32, (N_EXPERTS, META_LANES), 1).astype(F32)
    tile_c = jnp.minimum(tile_i, total - 1.0)
    tile_end = tile0[:, ml] + ntile[:, ml]
    te = jnp.sum(jnp.where(tile_end <= tile_c, 1.0, 0.0), axis=0, keepdims=True)
    meta_ref[...] = jnp.zeros((SUBLANES, META_LANES), I32)
    meta_ref[0:1, :] = te.astype(I32)
    meta_ref[1:2, :] = total.astype(I32)
    later = (eidx[:, ml] > te) & (ntile[:, ml] > 0.0)
    nxt = jnp.min(jnp.where(later, eidx[:, ml], float(N_EXPERTS)), axis=0, keepdims=True)
    meta_ref[2:3, :] = nxt.astype(I32)


def _route_call(lg, bias, n):
    return pl.pallas_call(
        functools.partial(_route_kernel, n=n),
        out_shape=(
            jax.ShapeDtypeStruct((2, n), I32),
            jax.ShapeDtypeStruct((2, n), F32),
            jax.ShapeDtypeStruct((SUBLANES, META_LANES), I32),
        ),
        scratch_shapes=[pltpu.VMEM((2, n), F32), pltpu.VMEM((2, n), F32)],
        compiler_params=pltpu.CompilerParams(vmem_limit_bytes=VMEM_LIMIT),
        name="route",
    )(lg, bias)


def _inverse_kernel(dest_ref, init_hbm, src_ref, sem, *, n):
    init = pltpu.make_async_copy(init_hbm, src_ref, sem)
    init.start()
    init.wait()

    def fill(t, c):
        src_ref[dest_ref[t]] = t
        src_ref[dest_ref[n + t]] = t
        return c

    lax.fori_loop(0, n, fill, 0, unroll=8)


def _inverse_call(dest_flat, n, n_slots):
    return pl.pallas_call(
        functools.partial(_inverse_kernel, n=n),
        in_specs=[pl.BlockSpec(memory_space=pltpu.SMEM), pl.BlockSpec(memory_space=pl.ANY)],
        out_specs=pl.BlockSpec(memory_space=pltpu.SMEM),
        out_shape=jax.ShapeDtypeStruct((n_slots,), I32),
        scratch_shapes=[pltpu.SemaphoreType.DMA(())],
        name="inverse",
    )(dest_flat, jnp.arange(n_slots, dtype=I32) % n)


def _row_copy(src_hbm, row, dst_vmem, slot, sem):
    return pltpu.make_async_copy(src_hbm.at[pl.ds(row, 1), :], dst_vmem.at[pl.ds(slot, 1), :], sem)


def _tile_copy(src_hbm, dst_vmem, sem):
    return pltpu.make_async_copy(src_hbm.at[pl.ds(0, dst_vmem.shape[0]), :], dst_vmem, sem)


def _moe_kernel(te_ref, ne_ref, nt_ref, *refs):
    src_refs = refs[:GATHER_DEPTH]
    (x1_hbm, w1_hbm, w3_hbm, w2_hbm, o_hbm,
     xbuf, obuf, wf1, wf3, wf2, w1b, w3b, w2b, wslot, gsem, osem, wsem) = refs[GATHER_DEPTH:]
    ahead = GATHER_DEPTH - 1
    i = pl.program_id(0)
    nt = nt_ref[0]

    def gather_start(src_ref, slot):
        for r in range(TM):
            _row_copy(x1_hbm, src_ref[0, 0, r], xbuf.at[slot], r, gsem.at[slot]).start()

    def out_copy(tile, slot):
        return pltpu.make_async_copy(obuf.at[slot], o_hbm.at[pl.ds(tile * TM, TM), :], osem.at[slot])

    def weight_copies(e, slot):
        return (pltpu.make_async_copy(w1_hbm.at[e], wf1.at[slot], wsem.at[slot, 0]),
                pltpu.make_async_copy(w3_hbm.at[e], wf3.at[slot], wsem.at[slot, 1]),
                pltpu.make_async_copy(w2_hbm.at[e], wf2.at[slot], wsem.at[slot, 2]))

    @pl.when(i == 0)
    def _():
        for cp in weight_copies(te_ref[0], 0):
            cp.start(priority=BULK_DMA_PRIORITY)
        wslot[0] = 0
        for k in range(ahead):
            gather_start(src_refs[k], k)

    @pl.when(i < nt)
    def _():
        slot = lax.rem(i, 2)
        gslot = lax.rem(i, GATHER_DEPTH)

        @pl.when(i >= 2)
        def _():
            out_copy(0, slot).wait()

        _tile_copy(x1_hbm, xbuf.at[gslot], gsem.at[gslot]).wait()

        @pl.when((i == 0) | (te_ref[i] != te_ref[jnp.maximum(i - 1, 0)]))
        def _():
            cur = wslot[0]
            for cp in weight_copies(te_ref[i], cur):
                cp.wait()
            w1b[...] = wf1[cur].astype(BF16)
            w3b[...] = wf3[cur].astype(BF16)
            w2b[...] = wf2[cur].astype(BF16)

            @pl.when(ne_ref[i] < N_EXPERTS)
            def _():
                for cp in weight_copies(ne_ref[i], 1 - cur):
                    cp.start(priority=BULK_DMA_PRIORITY)

            wslot[0] = 1 - cur

        xb = xbuf[gslot].astype(BF16)
        gather_start(src_refs[ahead], lax.rem(i + ahead, GATHER_DEPTH))
        h1 = jnp.dot(xb, w1b[...], preferred_element_type=F32)
        h3 = jnp.dot(xb, w3b[...], preferred_element_type=F32)
        h = (jax.nn.silu(h1) * h3).astype(BF16)
        obuf[slot] = jnp.dot(h, w2b[...], preferred_element_type=F32)
        out_copy(i, slot).start(priority=BULK_DMA_PRIORITY)

        @pl.when(i + 1 >= nt)
        def _():
            for k in range(1, GATHER_DEPTH):
                s = lax.rem(i + k, GATHER_DEPTH)
                _tile_copy(x1_hbm, xbuf.at[s], gsem.at[s]).wait()
            out_copy(0, slot).wait()

            @pl.when(i >= 1)
            def _():
                out_copy(0, 1 - slot).wait()


def _moe_call(te, ne, ntiles, src3, x1, w1, w3, w2, max_tiles):
    grid_spec = pltpu.PrefetchScalarGridSpec(
        num_scalar_prefetch=3,
        grid=(max_tiles,),
        in_specs=[
            pl.BlockSpec((1, 1, TM), lambda i, te, ne, nt, k=k: (jnp.minimum(i + k, max_tiles - 1), 0, 0),
                         memory_space=pltpu.SMEM)
            for k in range(GATHER_DEPTH)
        ] + [
            pl.BlockSpec(memory_space=pl.ANY),
            pl.BlockSpec(memory_space=pl.ANY),
            pl.BlockSpec(memory_space=pl.ANY),
            pl.BlockSpec(memory_space=pl.ANY),
        ],
        out_specs=pl.BlockSpec(memory_space=pl.ANY),
        scratch_shapes=[
            pltpu.VMEM((GATHER_DEPTH, TM, D_MODEL), F32),
            pltpu.VMEM((2, TM, D_MODEL), F32),
            pltpu.VMEM((2, D_MODEL, D_EXPERT), F32),
            pltpu.VMEM((2, D_MODEL, D_EXPERT), F32),
            pltpu.VMEM((2, D_EXPERT, D_MODEL), F32),
            pltpu.VMEM((D_MODEL, D_EXPERT), BF16),
            pltpu.VMEM((D_MODEL, D_EXPERT), BF16),
            pltpu.VMEM((D_EXPERT, D_MODEL), BF16),
            pltpu.SMEM((1,), I32),
            pltpu.SemaphoreType.DMA((GATHER_DEPTH,)),
            pltpu.SemaphoreType.DMA((2,)),
            pltpu.SemaphoreType.DMA((2, 3)),
        ],
    )
    return pl.pallas_call(
        _moe_kernel,
        grid_spec=grid_spec,
        out_shape=jax.ShapeDtypeStruct((max_tiles * TM, D_MODEL), F32),
        compiler_params=pltpu.CompilerParams(dimension_semantics=("arbitrary",), vmem_limit_bytes=VMEM_LIMIT),
        name="moe",
    )(te, ne, ntiles, *([src3] * GATHER_DEPTH), x1, w1, w3, w2)


def _combine_kernel(*refs, alpha, nsteps):
    dest_refs = refs[:COMBINE_DEPTH]
    x1_ref, tw_ref, ys_hbm, ln_g, ln_b, o_ref, buf, sem = refs[COMBINE_DEPTH:]
    ahead = COMBINE_DEPTH - 1
    i = pl.program_id(0)
    slot = lax.rem(i, COMBINE_DEPTH)

    def gather_start(dest_ref, s):
        for r in range(ROWS):
            for k in range(2):
                _row_copy(ys_hbm, dest_ref[0, k, r], buf.at[s, k], r, sem.at[s, k]).start(priority=k)

    def gather_wait(s):
        for k in range(2):
            _tile_copy(ys_hbm, buf.at[s, k], sem.at[s, k]).wait()

    @pl.when(i == 0)
    def _():
        for j in range(ahead):
            gather_start(dest_refs[j], j)

    gather_wait(slot)
    tw = tw_ref[...]
    moe = tw[:, 0:1] * buf[slot, 0] + tw[:, 1:2] * buf[slot, 1]
    gather_start(dest_refs[ahead], lax.rem(i + ahead, COMBINE_DEPTH))
    xin = alpha * x1_ref[...] + moe
    mu = jnp.mean(xin, axis=-1, keepdims=True)
    xc = xin - mu
    var = jnp.mean(xc * xc, axis=-1, keepdims=True)
    y = xc * lax.rsqrt(var + LN_EPS) * ln_g[...] + ln_b[...]
    o_ref[...] = y.reshape(o_ref.shape)

    @pl.when(i == nsteps - 1)
    def _():
        for j in range(1, COMBINE_DEPTH):
            gather_wait(lax.rem(i + j, COMBINE_DEPTH))


def _combine_call(dest3, x1, tw, ys, ln_g, ln_b, out_shape, blk, blk0, alpha):
    bsz, tlen, _ = out_shape
    nt = tlen // blk[1]
    nsteps = (bsz // blk[0]) * nt
    return pl.pallas_call(
        functools.partial(_combine_kernel, alpha=alpha, nsteps=nsteps),
        grid=(nsteps,),
        in_specs=[
            pl.BlockSpec((1, 2, ROWS), lambda i, j=j: (blk0 + jnp.minimum(i + j, nsteps - 1), 0, 0),
                         memory_space=pltpu.SMEM)
            for j in range(COMBINE_DEPTH)
        ] + [
            pl.BlockSpec((ROWS, D_MODEL), lambda i: (blk0 + i, 0)),
            pl.BlockSpec((ROWS, 2), lambda i: (blk0 + i, 0)),
            pl.BlockSpec(memory_space=pl.ANY),
            _const_spec(ln_g.shape),
            _const_spec(ln_b.shape),
        ],
        out_specs=pl.BlockSpec(blk, lambda i: (i // nt, i % nt, 0)),
        out_shape=jax.ShapeDtypeStruct(out_shape, F32),
        scratch_shapes=[
            pltpu.VMEM((COMBINE_DEPTH, 2, ROWS, D_MODEL), F32),
            pltpu.SemaphoreType.DMA((COMBINE_DEPTH, 2)),
        ],
        compiler_params=pltpu.CompilerParams(dimension_semantics=("arbitrary",), vmem_limit_bytes=VMEM_LIMIT),
        name="combine",
    )(*([dest3] * COMBINE_DEPTH), x1, tw, ys, ln_g, ln_b)


def _block_diag_chunks(w):
    per = LANE_CHUNK // LRU_HEAD_DIM
    nchunk = LRU_HEADS // per
    w4 = w.reshape(nchunk, per, LRU_HEAD_DIM, LRU_HEAD_DIM)
    bd = jnp.einsum("cjio,jk->cjiko", w4, jnp.eye(per, dtype=w.dtype))
    return bd.reshape(nchunk, LANE_CHUNK, LANE_CHUNK)


def _layer(yp, ys, pool_s, conv_s, h_s, lw, alpha, past_len):
    (w_in, pool_w, pool_b, pool_scale, conv_w, conv_b, rg_w, rg_b, ig_w, ig_b, lru_lambda, w_out,
     ln1_g, ln1_b, rgw, rgb, rew, reb, w1, w3, w2, ln2_g, ln2_b) = lw
    bp, tp, _ = yp.shape
    bs, ts, _ = ys.shape
    n_p, n_s = bp * tp, bs * ts
    n = n_p + n_s
    assert n % ROWS == 0 and n_p % ROWS == 0
    max_tiles = (2 * n) // TM + N_EXPERTS
    assert max_tiles <= META_LANES

    nchunk = D_LRU // LANE_CHUNK
    gate_w = jnp.concatenate([_block_diag_chunks(rg_w), _block_diag_chunks(ig_w)], axis=-1).astype(BF16)
    gate_b = jnp.concatenate(
        [rg_b.reshape(nchunk, LANE_CHUNK), ig_b.reshape(nchunk, LANE_CHUNK)], axis=-1).reshape(1, 2 * D_LRU)
    rt = jnp.concatenate(
        [rgw.T, jnp.zeros((SUBLANES - N_GROUPS, D_MODEL), F32),
         jnp.transpose(rew, (0, 2, 1)).reshape(N_EXPERTS, D_MODEL)], axis=0)
    rt_hi = rt.astype(BF16)
    rt_lo = (rt - rt_hi.astype(F32)).astype(BF16)
    r_bias = jnp.concatenate([rgb, jnp.zeros((SUBLANES - N_GROUPS,), F32), reb.reshape(N_EXPERTS)]).reshape(ROUTER_ROWS, 1)
    wts = (
        w_in.astype(BF16), pool_w.astype(BF16), pool_b.reshape(1, D_POOL), pool_scale.reshape(1, D_POOL),
        conv_w, conv_b.reshape(1, D_LRU), gate_w, gate_b, lru_lambda.reshape(1, D_LRU), w_out.astype(BF16),
        ln1_g.reshape(1, D_MODEL), ln1_b.reshape(1, D_MODEL),
        jnp.pad(jnp.stack([rt_hi, rt_lo]), ((0, 0), (0, LANES - ROUTER_ROWS), (0, 0))).reshape(2 * LANES, D_MODEL).T,
    )

    x1, lg, pool_p, conv_p, h_p = _mixer_call(
        yp, jnp.zeros((bp, POOL_STATE, D_POOL), F32), jnp.zeros((bp, CONV_STATE, D_LRU), F32),
        jnp.zeros((bp, 1, D_LRU), F32), wts, None, None,
        ns=1, l=SEQ_ROWS, pos0=0, alpha=alpha, n_total=n, blk0=0)
    x1, lg, pool_n, conv_n, h_n = _mixer_call(
        ys, pool_s, conv_s, h_s.reshape(bs, 1, D_LRU), wts, x1, lg,
        ns=ROWS // ts, l=ts, pos0=past_len, alpha=alpha, n_total=n, blk0=n_p // ROWS)

    dest, tw, meta = _route_call(lg, r_bias, n)
    src = _inverse_call(dest.reshape(2 * n), n, max_tiles * TM)
    ysort = _moe_call(meta[0], meta[2], meta[1, 0:1], src.reshape(max_tiles, 1, TM), x1, w1, w3, w2, max_tiles)

    dest3 = jnp.transpose(dest.reshape(2, n // ROWS, ROWS), (1, 0, 2))
    tw_t = tw.T
    g2, b2 = ln2_g.reshape(1, D_MODEL), ln2_b.reshape(1, D_MODEL)
    out_p = _combine_call(dest3, x1, tw_t, ysort, g2, b2, (bp, tp, D_MODEL), (1, ROWS, D_MODEL), 0, alpha)
    out_s = _combine_call(dest3, x1, tw_t, ysort, g2, b2, (bs, ts, D_MODEL), (ROWS // ts, ts, D_MODEL),
                          n_p // ROWS, alpha)
    states = (pool_p, conv_p, h_p.reshape(bp, D_LRU), pool_n, conv_n, h_n.reshape(bs, D_LRU))
    return out_p, out_s, states


def kernel(x_prompt, x_sample, state_pool, state_conv, state_h, w_in, pool_w, pool_b, pool_scale, conv_w, conv_b, rg_w, rg_b, ig_w, ig_b, lru_lambda, w_out, ln1_g, ln1_b, router_group_w, router_group_b, router_expert_w, router_expert_b, expert_w1, expert_w3, expert_w2, ln2_g, ln2_b):
    depth = w_in.shape[0]
    alpha = (2.0 * depth) ** 0.25
    past_len = 16384
    layer_weights = (w_in, pool_w, pool_b, pool_scale, conv_w, conv_b, rg_w, rg_b, ig_w, ig_b, lru_lambda, w_out,
                     ln1_g, ln1_b, router_group_w, router_group_b, router_expert_w, router_expert_b,
                     expert_w1, expert_w3, expert_w2, ln2_g, ln2_b)
    yp, ys = x_prompt, x_sample
    outs = [[] for _ in range(6)]
    for layer in range(depth):
        lw = tuple(w[layer] for w in layer_weights)
        yp, ys, states = _layer(yp, ys, state_pool[layer], state_conv[layer], state_h[layer], lw, alpha, past_len)
        for acc, s in zip(outs, states):
            acc.append(s)
    return (yp, ys) + tuple(jnp.stack(o, axis=0) for o in outs)
```

```python
import functools

import jax
import jax.numpy as jnp
from jax import lax
from jax.experimental import pallas as pl
from jax.experimental.pallas import tpu as pltpu

F32 = jnp.float32
BF16 = jnp.bfloat16
I32 = jnp.int32

D_MODEL = 2048
D_POOL = 1024
D_LRU = 1024
POOL_WINDOWS = (2, 4, 8, 16)
POOL_GROUP = D_POOL // len(POOL_WINDOWS)
POOL_STATE = max(POOL_WINDOWS) - 1
CONV_WIDTH = 4
CONV_STATE = CONV_WIDTH - 1
LRU_HEADS = 16
LRU_HEAD_DIM = D_LRU // LRU_HEADS
LRU_C = 8.0
N_GROUPS = 4
PER_GROUP = 8
N_EXPERTS = N_GROUPS * PER_GROUP
D_EXPERT = D_MODEL // 4
LN_EPS = 1e-5

SUBLANES = 8
LANES = 128
LANE_CHUNK = 256
POOL_HALO = 16
CONV_HALO = 8
ROWS = 256
SEQ_ROWS = 512
TM = 256
GATHER_DEPTH = 4
COMBINE_DEPTH = 3
ROUTER_ROWS = 8 + N_EXPERTS
ROUTER_PAD = 48
META_LANES = LANES
VMEM_LIMIT = 56 * 1024 * 1024
BULK_DMA_PRIORITY = 1

_NT = (((1,), (1,)), ((), ()))


def _const_spec(shape):
    nd = len(shape)
    return pl.BlockSpec(shape, lambda *_: (0,) * nd, pipeline_mode=pl.Buffered(1))


def _mixer_kernel(x_ref, pool_in, conv_in, h_in, w_in, pool_w, pool_b, pool_scale, conv_w, conv_b,
                  gate_w, gate_b, lam, w_out, ln_g, ln_b, rt_hl, *rest, ns, l, pos0, alpha, aliased):
    if aliased:
        rest = rest[2:]
    (x1_ref, lg_ref, pool_o, conv_o, h_o,
     pool_ext, conv_ext, h_prev, gate_s, a_s, u_s, y_s) = rest
    t = pl.program_id(1)
    rows = ns * l

    @pl.when(t == 0)
    def _():
        pool_ext[:, 0:POOL_HALO - POOL_STATE, :] = jnp.zeros((ns, POOL_HALO - POOL_STATE, D_POOL), F32)
        pool_ext[:, POOL_HALO - POOL_STATE:POOL_HALO, :] = pool_in[...]
        conv_ext[:, 0:CONV_HALO - CONV_STATE, :] = jnp.zeros((ns, CONV_HALO - CONV_STATE, D_LRU), F32)
        conv_ext[:, CONV_HALO - CONV_STATE:CONV_HALO, :] = conv_in[...]
        h_prev[...] = h_in[...]

    xb = x_ref[...].reshape(rows, D_MODEL).astype(BF16)
    pool_ext[:, POOL_HALO:, :] = jnp.dot(
        xb, w_in[:, 0:D_POOL], preferred_element_type=F32).reshape(ns, l, D_POOL)
    conv_ext[:, CONV_HALO:, :] = jnp.dot(
        xb, w_in[:, D_POOL:D_POOL + D_LRU], preferred_element_type=F32).reshape(ns, l, D_LRU)
    gate_s[...] = jnp.dot(xb, w_in[:, D_POOL + D_LRU:], preferred_element_type=F32)

    pos = pos0 + t * l + lax.broadcasted_iota(I32, (ns, l, LANE_CHUNK), 1).reshape(rows, LANE_CHUNK)

    for g, w in enumerate(POOL_WINDOWS):
        cols = slice(g * POOL_GROUP, (g + 1) * POOL_GROUP)
        s = pool_ext[:, :, cols].reshape(ns * (POOL_HALO + l), POOL_GROUP)
        shift = 1
        while shift < w:
            s = s + pltpu.roll(s, shift, 0)
            shift *= 2
        win = s.reshape(ns, POOL_HALO + l, POOL_GROUP)[:, POOL_HALO:, :].reshape(rows, POOL_GROUP)
        u = pool_ext[:, POOL_HALO:, cols].reshape(rows, POOL_GROUP)
        if pos0 >= w - 1:
            inv = 1.0 / w
        else:
            inv = 1.0 / jnp.minimum(pos + 1, w).astype(F32)
        d = win * inv - u
        z = jnp.dot(d.astype(BF16), pool_w[g], preferred_element_type=F32) + pool_b[:, cols]
        y_s[:, cols] = (z * pool_scale[:, cols]).astype(BF16)

    lam_v = lam[...]
    softplus_neg = jnp.maximum(-lam_v, 0.0) + jnp.log1p(jnp.exp(-jnp.abs(lam_v)))
    log_a_scale = -LRU_C * softplus_neg
    sub = lax.broadcasted_iota(I32, (rows // SUBLANES, SUBLANES, LANE_CHUNK), 1)
    for c in range(D_LRU // LANE_CHUNK):
        cols = slice(c * LANE_CHUNK, (c + 1) * LANE_CHUNK)
        ce = conv_ext[:, :, cols].reshape(ns * (CONV_HALO + l), LANE_CHUNK)
        conv = conv_b[:, cols] + conv_w[CONV_WIDTH - 1:CONV_WIDTH, cols] * ce
        for k in range(1, CONV_WIDTH):
            conv = conv + conv_w[CONV_WIDTH - 1 - k:CONV_WIDTH - k, cols] * pltpu.roll(ce, k, 0)
        conv = conv.reshape(ns, CONV_HALO + l, LANE_CHUNK)[:, CONV_HALO:, :].reshape(rows, LANE_CHUNK)
        gz = jnp.dot(conv.astype(BF16), gate_w[c], preferred_element_type=F32)
        gz = gz + gate_b[:, 2 * c * LANE_CHUNK:2 * (c + 1) * LANE_CHUNK]
        r = 0.5 * jnp.tanh(0.5 * gz[:, :LANE_CHUNK]) + 0.5
        i = 0.5 * jnp.tanh(0.5 * gz[:, LANE_CHUNK:]) + 0.5
        log_a = r * log_a_scale[:, cols]
        a = jnp.exp(log_a)
        th = jnp.tanh(log_a)
        q = (th + th) / (th - 1.0)
        mult = jnp.where(q > 0.0, q * lax.rsqrt(q), 0.0)
        if pos0 == 0:
            mult = jnp.where(pos == 0, 1.0, mult)
        u = mult * (i * conv)
        a3 = a.reshape(rows // SUBLANES, SUBLANES, LANE_CHUNK)
        u3 = u.reshape(rows // SUBLANES, SUBLANES, LANE_CHUNK)
        for sh in (1, 2, 4):
            keep = sub >= sh
            a_sh = jnp.where(keep, pltpu.roll(a3, sh, 1), 1.0)
            u_sh = jnp.where(keep, pltpu.roll(u3, sh, 1), 0.0)
            u3 = a3 * u_sh + u3
            a3 = a3 * a_sh
        a_s[:, :, cols] = a3.reshape(ns, l, LANE_CHUNK)
        u_s[:, :, cols] = u3.reshape(ns, l, LANE_CHUNK)

    hp = h_prev[...]
    for gi in range(l // SUBLANES):
        sl = slice(gi * SUBLANES, (gi + 1) * SUBLANES)
        h = a_s[:, sl, :] * hp + u_s[:, sl, :]
        u_s[:, sl, :] = h
        hp = h[:, SUBLANES - 1:SUBLANES, :]
    h_prev[...] = hp

    for c in range(D_LRU // LANE_CHUNK):
        cols = slice(c * LANE_CHUNK, (c + 1) * LANE_CHUNK)
        h = u_s[:, :, cols].reshape(rows, LANE_CHUNK)
        y_s[:, D_POOL + c * LANE_CHUNK:D_POOL + (c + 1) * LANE_CHUNK] = (
            h * jax.nn.gelu(gate_s[:, cols])).astype(BF16)

    mix = jnp.dot(y_s[...], w_out[...], preferred_element_type=F32)
    xin = alpha * x_ref[...].reshape(rows, D_MODEL) + mix
    mu = jnp.mean(xin, axis=-1, keepdims=True)
    xc = xin - mu
    var = jnp.mean(xc * xc, axis=-1, keepdims=True)
    x1 = xc * lax.rsqrt(var + LN_EPS) * ln_g[...] + ln_b[...]
    x1_ref[...] = x1

    hi = x1.astype(BF16)
    lo = (x1 - hi.astype(F32)).astype(BF16)
    both = lax.dot_general(rt_hl[...], hi, _NT, preferred_element_type=F32)
    lg = both[0:ROUTER_ROWS] + both[ROUTER_PAD:ROUTER_PAD + ROUTER_ROWS]
    lg = lg + lax.dot_general(rt_hl[0:ROUTER_PAD, :], lo, _NT, preferred_element_type=F32)[0:ROUTER_ROWS]
    lg_ref[...] = lg

    pool_o[...] = pool_ext[:, POOL_HALO + l - POOL_STATE:, :]
    conv_o[...] = conv_ext[:, CONV_HALO + l - CONV_STATE:, :]
    h_o[...] = hp
    pool_ext[:, 0:POOL_HALO, :] = pool_ext[:, l:l + POOL_HALO, :]
    conv_ext[:, 0:CONV_HALO, :] = conv_ext[:, l:l + CONV_HALO, :]


def _mixer_call(x, pool_in, conv_in, h_in, wts, x1_buf, lg_buf, *, ns, l, pos0, alpha, n_total, blk0):
    bsz, tlen, _ = x.shape
    nb, nt = bsz // ns, tlen // l
    rows = ns * l
    assert rows % ROWS == 0 and bsz % ns == 0 and tlen % l == 0 and l % SUBLANES == 0
    aliased = x1_buf is not None

    def row_blk(b, t):
        return (blk0 + b * nt + t, 0)

    in_specs = [
        pl.BlockSpec((ns, l, D_MODEL), lambda b, t: (b, t, 0)),
        pl.BlockSpec((ns, POOL_STATE, D_POOL), lambda b, t: (b, 0, 0)),
        pl.BlockSpec((ns, CONV_STATE, D_LRU), lambda b, t: (b, 0, 0)),
        pl.BlockSpec((ns, 1, D_LRU), lambda b, t: (b, 0, 0)),
    ] + [_const_spec(w.shape) for w in wts]
    args = [x, pool_in, conv_in, h_in, *wts]
    aliases = {}
    if aliased:
        in_specs += [pl.BlockSpec(memory_space=pl.ANY), pl.BlockSpec(memory_space=pl.ANY)]
        aliases = {len(args): 0, len(args) + 1: 1}
        args += [x1_buf, lg_buf]
    out_shape = (
        jax.ShapeDtypeStruct((n_total, D_MODEL), F32),
        jax.ShapeDtypeStruct((ROUTER_ROWS, n_total), F32),
        jax.ShapeDtypeStruct((bsz, POOL_STATE, D_POOL), F32),
        jax.ShapeDtypeStruct((bsz, CONV_STATE, D_LRU), F32),
        jax.ShapeDtypeStruct((bsz, 1, D_LRU), F32),
    )
    out_specs = (
        pl.BlockSpec((rows, D_MODEL), row_blk),
        pl.BlockSpec((ROUTER_ROWS, rows), lambda b, t: (0, blk0 + b * nt + t)),
        pl.BlockSpec((ns, POOL_STATE, D_POOL), lambda b, t: (b, 0, 0)),
        pl.BlockSpec((ns, CONV_STATE, D_LRU), lambda b, t: (b, 0, 0)),
        pl.BlockSpec((ns, 1, D_LRU), lambda b, t: (b, 0, 0)),
    )
    scratch = [
        pltpu.VMEM((ns, POOL_HALO + l, D_POOL), F32),
        pltpu.VMEM((ns, CONV_HALO + l, D_LRU), F32),
        pltpu.VMEM((ns, 1, D_LRU), F32),
        pltpu.VMEM((rows, D_LRU), F32),
        pltpu.VMEM((ns, l, D_LRU), F32),
        pltpu.VMEM((ns, l, D_LRU), F32),
        pltpu.VMEM((rows, D_MODEL), BF16),
    ]
    return pl.pallas_call(
        functools.partial(_mixer_kernel, ns=ns, l=l, pos0=pos0, alpha=alpha, aliased=aliased),
        grid=(nb, nt),
        in_specs=in_specs,
        out_specs=out_specs,
        out_shape=out_shape,
        scratch_shapes=scratch,
        input_output_aliases=aliases,
        compiler_params=pltpu.CompilerParams(
            dimension_semantics=("arbitrary", "arbitrary"), vmem_limit_bytes=VMEM_LIMIT),
        name="mixer_seq" if nt > 1 else "mixer_step",
    )(*args)


def _route_kernel(lg_ref, bias_ref, dest_ref, tw_ref, meta_ref, e_s, r_s, *, n):
    blk = 256
    nblk = n // blk
    neg_inf = -jnp.inf
    ridx8 = lax.broadcasted_iota(I32, (SUBLANES, blk), 0).astype(F32)
    eidx = lax.broadcasted_iota(I32, (N_EXPERTS, blk), 0).astype(F32)
    tri = (lax.broadcasted_iota(I32, (blk, blk), 0) < lax.broadcasted_iota(I32, (blk, blk), 1))
    tri = jnp.where(tri, 1.0, 0.0).astype(BF16)

    def first_idx(vals, m):
        return jnp.min(jnp.where(vals == m, ridx8, float(SUBLANES)), axis=0, keepdims=True)

    def pass1(j, base):
        ls = pl.ds(pl.multiple_of(j * blk, blk), blk)
        lg = lg_ref[0:SUBLANES, ls] + bias_ref[0:SUBLANES, :]
        lg = jnp.where(ridx8 < N_GROUPS, lg, neg_inf)
        m = jnp.max(lg, axis=0, keepdims=True)
        gi = first_idx(lg, m)
        pg_sel = 1.0 / jnp.sum(jnp.exp(lg - m), axis=0, keepdims=True)
        le = jnp.zeros((PER_GROUP, blk), F32)
        for g in range(N_GROUPS):
            rows = slice(SUBLANES + g * PER_GROUP, SUBLANES + (g + 1) * PER_GROUP)
            le = jnp.where(gi == float(g), lg_ref[rows, ls] + bias_ref[rows, :], le)
        m1 = jnp.max(le, axis=0, keepdims=True)
        i1 = first_idx(le, m1)
        le2 = jnp.where(ridx8 == i1, neg_inf, le)
        m2 = jnp.max(le2, axis=0, keepdims=True)
        i2 = first_idx(le2, m2)
        e21 = jnp.exp(m2 - m1)
        denom = 1.0 / (1.0 + e21)
        tw_ref[0:1, ls] = pg_sel * denom
        tw_ref[1:2, ls] = pg_sel * (e21 * denom)
        e1 = gi * float(PER_GROUP) + i1
        e2 = gi * float(PER_GROUP) + i2
        e_s[0:1, ls] = e1
        e_s[1:2, ls] = e2
        oh1 = jnp.where(eidx == e1, 1.0, 0.0)
        oh2 = jnp.where(eidx == e2, 1.0, 0.0)
        oh = oh1 + oh2
        before = base + jnp.dot(oh.astype(BF16), tri, preferred_element_type=F32)
        r_s[0:1, ls] = jnp.sum(oh1 * before, axis=0, keepdims=True)
        r_s[1:2, ls] = jnp.sum(oh2 * before, axis=0, keepdims=True)
        return base + jnp.sum(oh, axis=1, keepdims=True)

    cnt = lax.fori_loop(0, nblk, pass1, jnp.zeros((N_EXPERTS, blk), F32))

    ntile = jnp.floor((cnt + float(TM - 1)) * (1.0 / TM))
    lt = (lax.broadcasted_iota(I32, (N_EXPERTS, N_EXPERTS), 1) < lax.broadcasted_iota(I32, (N_EXPERTS, N_EXPERTS), 0))
    lt = jnp.where(lt, 1.0, 0.0).astype(BF16)
    tile0 = jnp.dot(lt, ntile.astype(BF16), preferred_element_type=F32)
    row0 = tile0 * float(TM)

    def pass2(j, carry):
        ls = pl.ds(pl.multiple_of(j * blk, blk), blk)
        for k in range(2):
            ohk = jnp.where(eidx == e_s[k:k + 1, ls], 1.0, 0.0)
            dest = r_s[k:k + 1, ls] + jnp.sum(ohk * row0, axis=0, keepdims=True)
            dest_ref[k:k + 1, ls] = dest.astype(I32)
        return carry

    lax.fori_loop(0, nblk, pass2, 0)

    ml = slice(0, META_LANES)
    total = jnp.sum(ntile[:, ml], axis=0, keepdims=True)
    tile_i = lax.broadcasted_iota(I32, (N_EXPERTS, META_LANES), 1).astype(F32)
    tile_c = jnp.minimum(tile_i, total - 1.0)
    tile_end = tile0[:, ml] + ntile[:, ml]
    te = jnp.sum(jnp.where(tile_end <= tile_c, 1.0, 0.0), axis=0, keepdims=True)
    meta_ref[...] = jnp.zeros((SUBLANES, META_LANES), I32)
    meta_ref[0:1, :] = te.astype(I32)
    meta_ref[1:2, :] = total.astype(I32)
    later = (eidx[:, ml] > te) & (ntile[:, ml] > 0.0)
    nxt = jnp.min(jnp.where(later, eidx[:, ml], float(N_EXPERTS)), axis=0, keepdims=True)
    meta_ref[2:3, :] = nxt.astype(I32)


def _route_call(lg, bias, n):
    return pl.pallas_call(
        functools.partial(_route_kernel, n=n),
        out_shape=(
            jax.ShapeDtypeStruct((2, n), I32),
            jax.ShapeDtypeStruct((2, n), F32),
            jax.ShapeDtypeStruct((SUBLANES, META_LANES), I32),
        ),
        scratch_shapes=[pltpu.VMEM((2, n), F32), pltpu.VMEM((2, n), F32)],
        compiler_params=pltpu.CompilerParams(vmem_limit_bytes=VMEM_LIMIT),
        name="route",
    )(lg, bias)


def _inverse_kernel(dest_ref, init_hbm, src_ref, sem, *, n):
    init = pltpu.make_async_copy(init_hbm, src_ref, sem)
    init.start()
    init.wait()

    def fill(t, c):
        src_ref[dest_ref[t]] = t
        src_ref[dest_ref[n + t]] = t
        return c

    lax.fori_loop(0, n, fill, 0, unroll=8)


def _inverse_call(dest_flat, n, n_slots):
    return pl.pallas_call(
        functools.partial(_inverse_kernel, n=n),
        in_specs=[pl.BlockSpec(memory_space=pltpu.SMEM), pl.BlockSpec(memory_space=pl.ANY)],
        out_specs=pl.BlockSpec(memory_space=pltpu.SMEM),
        out_shape=jax.ShapeDtypeStruct((n_slots,), I32),
        scratch_shapes=[pltpu.SemaphoreType.DMA(())],
        name="inverse",
    )(dest_flat, jnp.arange(n_slots, dtype=I32) % n)


def _row_copy(src_hbm, row, dst_vmem, slot, sem):
    return pltpu.make_async_copy(src_hbm.at[pl.ds(row, 1), :], dst_vmem.at[pl.ds(slot, 1), :], sem)


def _tile_copy(src_hbm, dst_vmem, sem):
    return pltpu.make_async_copy(src_hbm.at[pl.ds(0, dst_vmem.shape[0]), :], dst_vmem, sem)


def _moe_kernel(te_ref, ne_ref, nt_ref, *refs):
    src_refs = refs[:GATHER_DEPTH]
    (x1_hbm, w1_hbm, w3_hbm, w2_hbm, o_hbm,
     xbuf, obuf, wf1, wf3, wf2, w1b, w3b, w2b, wslot, gsem, osem, wsem) = refs[GATHER_DEPTH:]
    ahead = GATHER_DEPTH - 1
    i = pl.program_id(0)
    nt = nt_ref[0]

    def gather_start(src_ref, slot):
        for r in range(TM):
            _row_copy(x1_hbm, src_ref[0, 0, r], xbuf.at[slot], r, gsem.at[slot]).start()

    def out_copy(tile, slot):
        return pltpu.make_async_copy(obuf.at[slot], o_hbm.at[pl.ds(tile * TM, TM), :], osem.at[slot])

    def weight_copies(e, slot):
        return (pltpu.make_async_copy(w1_hbm.at[e], wf1.at[slot], wsem.at[slot, 0]),
                pltpu.make_async_copy(w3_hbm.at[e], wf3.at[slot], wsem.at[slot, 1]),
                pltpu.make_async_copy(w2_hbm.at[e], wf2.at[slot], wsem.at[slot, 2]))

    @pl.when(i == 0)
    def _():
        for cp in weight_copies(te_ref[0], 0):
            cp.start(priority=BULK_DMA_PRIORITY)
        wslot[0] = 0
        for k in range(ahead):
            gather_start(src_refs[k], k)

    @pl.when(i < nt)
    def _():
        slot = lax.rem(i, 2)
        gslot = lax.rem(i, GATHER_DEPTH)

        @pl.when(i >= 2)
        def _():
            out_copy(0, slot).wait()

        _tile_copy(x1_hbm, xbuf.at[gslot], gsem.at[gslot]).wait()

        @pl.when((i == 0) | (te_ref[i] != te_ref[jnp.maximum(i - 1, 0)]))
        def _():
            cur = wslot[0]
            for cp in weight_copies(te_ref[i], cur):
                cp.wait()
            w1b[...] = wf1[cur].astype(BF16)
            w3b[...] = wf3[cur].astype(BF16)
            w2b[...] = wf2[cur].astype(BF16)

            @pl.when(ne_ref[i] < N_EXPERTS)
            def _():
                for cp in weight_copies(ne_ref[i], 1 - cur):
                    cp.start(priority=BULK_DMA_PRIORITY)

            wslot[0] = 1 - cur

        xb = xbuf[gslot].astype(BF16)
        gather_start(src_refs[ahead], lax.rem(i + ahead, GATHER_DEPTH))
        h1 = jnp.dot(xb, w1b[...], preferred_element_type=F32)
        h3 = jnp.dot(xb, w3b[...], preferred_element_type=F32)
        h = (jax.nn.silu(h1) * h3).astype(BF16)
        obuf[slot] = jnp.dot(h, w2b[...], preferred_element_type=F32)
        out_copy(i, slot).start(priority=BULK_DMA_PRIORITY)

        @pl.when(i + 1 >= nt)
        def _():
            for k in range(1, GATHER_DEPTH):
                s = lax.rem(i + k, GATHER_DEPTH)
                _tile_copy(x1_hbm, xbuf.at[s], gsem.at[s]).wait()
            out_copy(0, slot).wait()

            @pl.when(i >= 1)
            def _():
                out_copy(0, 1 - slot).wait()


def _moe_call(te, ne, ntiles, src3, x1, w1, w3, w2, max_tiles):
    grid_spec = pltpu.PrefetchScalarGridSpec(
        num_scalar_prefetch=3,
        grid=(max_tiles,),
        in_specs=[
            pl.BlockSpec((1, 1, TM), lambda i, te, ne, nt, k=k: (jnp.minimum(i + k, max_tiles - 1), 0, 0),
                         memory_space=pltpu.SMEM)
            for k in range(GATHER_DEPTH)
        ] + [
            pl.BlockSpec(memory_space=pl.ANY),
            pl.BlockSpec(memory_space=pl.ANY),
            pl.BlockSpec(memory_space=pl.ANY),
            pl.BlockSpec(memory_space=pl.ANY),
        ],
        out_specs=pl.BlockSpec(memory_space=pl.ANY),
        scratch_shapes=[
            pltpu.VMEM((GATHER_DEPTH, TM, D_MODEL), F32),
            pltpu.VMEM((2, TM, D_MODEL), F32),
            pltpu.VMEM((2, D_MODEL, D_EXPERT), F32),
            pltpu.VMEM((2, D_MODEL, D_EXPERT), F32),
            pltpu.VMEM((2, D_EXPERT, D_MODEL), F32),
            pltpu.VMEM((D_MODEL, D_EXPERT), BF16),
            pltpu.VMEM((D_MODEL, D_EXPERT), BF16),
            pltpu.VMEM((D_EXPERT, D_MODEL), BF16),
            pltpu.SMEM((1,), I32),
            pltpu.SemaphoreType.DMA((GATHER_DEPTH,)),
            pltpu.SemaphoreType.DMA((2,)),
            pltpu.SemaphoreType.DMA((2, 3)),
        ],
    )
    return pl.pallas_call(
        _moe_kernel,
        grid_spec=grid_spec,
        out_shape=jax.ShapeDtypeStruct((max_tiles * TM, D_MODEL), F32),
        compiler_params=pltpu.CompilerParams(dimension_semantics=("arbitrary",), vmem_limit_bytes=VMEM_LIMIT),
        name="moe",
    )(te, ne, ntiles, *([src3] * GATHER_DEPTH), x1, w1, w3, w2)


def _combine_kernel(*refs, alpha, nsteps, blk0):
    dest_refs = refs[:COMBINE_DEPTH]
    x1_hbm, tw_ref, ys_hbm, ln_g, ln_b, o_ref, buf, xbuf, sem, xsem = refs[COMBINE_DEPTH:]
    ahead = COMBINE_DEPTH - 1
    i = pl.program_id(0)
    slot = lax.rem(i, COMBINE_DEPTH)

    def x_copy(tile, s):
        return pltpu.make_async_copy(x1_hbm.at[pl.ds((blk0 + tile) * ROWS, ROWS), :], xbuf.at[s], xsem.at[s])

    def gather_start(dest_ref, s):
        for r in range(ROWS):
            for k in range(2):
                _row_copy(ys_hbm, dest_ref[0, k, r], buf.at[s, k], r, sem.at[s, k]).start(priority=k)

    def gather_wait(s):
        for k in range(2):
            _tile_copy(ys_hbm, buf.at[s, k], sem.at[s, k]).wait()

    @pl.when(i == 0)
    def _():
        for j in range(ahead):
            gather_start(dest_refs[j], j)
            x_copy(j, j).start()

    gather_wait(slot)
    x_copy(0, slot).wait()
    tw = tw_ref[...]
    moe = tw[:, 0:1] * buf[slot, 0] + tw[:, 1:2] * buf[slot, 1]
    x_now = xbuf[slot]
    gather_start(dest_refs[ahead], lax.rem(i + ahead, COMBINE_DEPTH))
    x_copy(jnp.minimum(i + ahead, nsteps - 1), lax.rem(i + ahead, COMBINE_DEPTH)).start()
    xin = alpha * x_now + moe
    mu = jnp.mean(xin, axis=-1, keepdims=True)
    xc = xin - mu
    var = jnp.mean(xc * xc, axis=-1, keepdims=True)
    y = xc * lax.rsqrt(var + LN_EPS) * ln_g[...] + ln_b[...]
    o_ref[...] = y.reshape(o_ref.shape)

    @pl.when(i == nsteps - 1)
    def _():
        for j in range(1, COMBINE_DEPTH):
            gather_wait(lax.rem(i + j, COMBINE_DEPTH))
            x_copy(0, lax.rem(i + j, COMBINE_DEPTH)).wait()


def _combine_call(dest3, x1, tw, ys, ln_g, ln_b, out_shape, blk, blk0, alpha):
    bsz, tlen, _ = out_shape
    nt = tlen // blk[1]
    nsteps = (bsz // blk[0]) * nt
    return pl.pallas_call(
        functools.partial(_combine_kernel, alpha=alpha, nsteps=nsteps, blk0=blk0),
        grid=(nsteps,),
        in_specs=[
            pl.BlockSpec((1, 2, ROWS), lambda i, j=j: (blk0 + jnp.minimum(i + j, nsteps - 1), 0, 0),
                         memory_space=pltpu.SMEM)
            for j in range(COMBINE_DEPTH)
        ] + [
            pl.BlockSpec(memory_space=pl.ANY),
            pl.BlockSpec((ROWS, 2), lambda i: (blk0 + i, 0)),
            pl.BlockSpec(memory_space=pl.ANY),
            _const_spec(ln_g.shape),
            _const_spec(ln_b.shape),
        ],
        out_specs=pl.BlockSpec(blk, lambda i: (i // nt, i % nt, 0)),
        out_shape=jax.ShapeDtypeStruct(out_shape, F32),
        scratch_shapes=[
            pltpu.VMEM((COMBINE_DEPTH, 2, ROWS, D_MODEL), F32),
            pltpu.VMEM((COMBINE_DEPTH, ROWS, D_MODEL), F32),
            pltpu.SemaphoreType.DMA((COMBINE_DEPTH, 2)),
            pltpu.SemaphoreType.DMA((COMBINE_DEPTH,)),
        ],
        compiler_params=pltpu.CompilerParams(dimension_semantics=("arbitrary",), vmem_limit_bytes=VMEM_LIMIT),
        name="combine",
    )(*([dest3] * COMBINE_DEPTH), x1, tw, ys, ln_g, ln_b)


def _block_diag_chunks(w):
    per = LANE_CHUNK // LRU_HEAD_DIM
    nchunk = LRU_HEADS // per
    w4 = w.reshape(nchunk, per, LRU_HEAD_DIM, LRU_HEAD_DIM)
    bd = jnp.einsum("cjio,jk->cjiko", w4, jnp.eye(per, dtype=w.dtype))
    return bd.reshape(nchunk, LANE_CHUNK, LANE_CHUNK)


def _layer(yp, ys, pool_s, conv_s, h_s, lw, alpha, past_len):
    (w_in, pool_w, pool_b, pool_scale, conv_w, conv_b, rg_w, rg_b, ig_w, ig_b, lru_lambda, w_out,
     ln1_g, ln1_b, rgw, rgb, rew, reb, w1, w3, w2, ln2_g, ln2_b) = lw
    bp, tp, _ = yp.shape
    bs, ts, _ = ys.shape
    n_p, n_s = bp * tp, bs * ts
    n = n_p + n_s
    assert n % ROWS == 0 and n_p % ROWS == 0
    max_tiles = (2 * n) // TM + N_EXPERTS
    assert max_tiles <= META_LANES

    nchunk = D_LRU // LANE_CHUNK
    gate_w = jnp.concatenate([_block_diag_chunks(rg_w), _block_diag_chunks(ig_w)], axis=-1).astype(BF16)
    gate_b = jnp.concatenate(
        [rg_b.reshape(nchunk, LANE_CHUNK), ig_b.reshape(nchunk, LANE_CHUNK)], axis=-1).reshape(1, 2 * D_LRU)
    rt = jnp.concatenate(
        [rgw.T, jnp.zeros((SUBLANES - N_GROUPS, D_MODEL), F32),
         jnp.transpose(rew, (0, 2, 1)).reshape(N_EXPERTS, D_MODEL)], axis=0)
    rt_hi = rt.astype(BF16)
    rt_lo = (rt - rt_hi.astype(F32)).astype(BF16)
    r_bias = jnp.concatenate([rgb, jnp.zeros((SUBLANES - N_GROUPS,), F32), reb.reshape(N_EXPERTS)]).reshape(ROUTER_ROWS, 1)
    wts = (
        w_in.astype(BF16), pool_w.astype(BF16), pool_b.reshape(1, D_POOL), pool_scale.reshape(1, D_POOL),
        conv_w, conv_b.reshape(1, D_LRU), gate_w, gate_b, lru_lambda.reshape(1, D_LRU), w_out.astype(BF16),
        ln1_g.reshape(1, D_MODEL), ln1_b.reshape(1, D_MODEL),
        jnp.pad(jnp.stack([rt_hi, rt_lo]), ((0, 0), (0, ROUTER_PAD - ROUTER_ROWS), (0, 0))).reshape(2 * ROUTER_PAD, D_MODEL),
    )

    x1, lg, pool_p, conv_p, h_p = _mixer_call(
        yp, jnp.zeros((bp, POOL_STATE, D_POOL), F32), jnp.zeros((bp, CONV_STATE, D_LRU), F32),
        jnp.zeros((bp, 1, D_LRU), F32), wts, None, None,
        ns=1, l=SEQ_ROWS, pos0=0, alpha=alpha, n_total=n, blk0=0)
    x1, lg, pool_n, conv_n, h_n = _mixer_call(
        ys, pool_s, conv_s, h_s.reshape(bs, 1, D_LRU), wts, x1, lg,
        ns=ROWS // ts, l=ts, pos0=past_len, alpha=alpha, n_total=n, blk0=n_p // ROWS)

    dest, tw, meta = _route_call(lg, r_bias, n)
    src = _inverse_call(dest.reshape(2 * n), n, max_tiles * TM)
    ysort = _moe_call(meta[0], meta[2], meta[1, 0:1], src.reshape(max_tiles, 1, TM), x1, w1, w3, w2, max_tiles)

    dest3 = jnp.transpose(dest.reshape(2, n // ROWS, ROWS), (1, 0, 2))
    tw_t = tw.T
    g2, b2 = ln2_g.reshape(1, D_MODEL), ln2_b.reshape(1, D_MODEL)
    out_p = _combine_call(dest3, x1, tw_t, ysort, g2, b2, (bp, tp, D_MODEL), (1, ROWS, D_MODEL), 0, alpha)
    out_s = _combine_call(dest3, x1, tw_t, ysort, g2, b2, (bs, ts, D_MODEL), (ROWS // ts, ts, D_MODEL),
                          n_p // ROWS, alpha)
    states = (pool_p, conv_p, h_p.reshape(bp, D_LRU), pool_n, conv_n, h_n.reshape(bs, D_LRU))
    return out_p, out_s, states


def kernel(x_prompt, x_sample, state_pool, state_conv, state_h, w_in, pool_w, pool_b, pool_scale, conv_w, conv_b, rg_w, rg_b, ig_w, ig_b, lru_lambda, w_out, ln1_g, ln1_b, router_group_w, router_group_b, router_expert_w, router_expert_b, expert_w1, expert_w3, expert_w2, ln2_g, ln2_b):
    depth = w_in.shape[0]
    alpha = (2.0 * depth) ** 0.25
    past_len = 16384
    layer_weights = (w_in, pool_w, pool_b, pool_scale, conv_w, conv_b, rg_w, rg_b, ig_w, ig_b, lru_lambda, w_out,
                     ln1_g, ln1_b, router_group_w, router_group_b, router_expert_w, router_expert_b,
                     expert_w1, expert_w3, expert_w2, ln2_g, ln2_b)
    yp, ys = x_prompt, x_sample
    outs = [[] for _ in range(6)]
    for layer in range(depth):
        lw = tuple(w[layer] for w in layer_weights)
        yp, ys, states = _layer(yp, ys, state_pool[layer], state_conv[layer], state_h[layer], lw, alpha, past_len)
        for acc, s in zip(outs, states):
            acc.append(s)
    return (yp, ys) + tuple(jnp.stack(o, axis=0) for o in outs)
```
